```python
import jax, jax.numpy as jnp
from jax import lax
import numpy as np

D_MODEL = 1024
BATCH = 4
SEQ = 4096
DEPTH = 2
DEC_BATCH = 8
DEC_SEQ = 64
PAST_LEN = 1024

CHUNK = 64
H_M = 4
HD_M = 128
D_M = H_M * HD_M
CONV_W = 4
H_FOX = 8
HD_FOX = 64
D_FOX = H_FOX * HD_FOX
FOX_BLOCK = 128
H_MEM = 4
HD_MEM = 128
D_MEM = H_MEM * HD_MEM
N_MEM = 256
N_BRANCH = 3
SPLIT_SIZES = (D_M, D_M, D_M, D_M, H_M, H_M, D_FOX, D_FOX, D_FOX, H_FOX, D_MEM, N_BRANCH * D_MODEL)
N_IN = 4 * D_M + 2 * H_M + 3 * D_FOX + H_FOX + D_MEM + N_BRANCH * D_MODEL
PEER_HEADS = 8
PEER_DQ = 256
N_KEYS = 128
N_EXPERTS = N_KEYS * N_KEYS
PEER_TOPK = 16
PEER_BLOCK = 256
EPS = 1e-6

kernel_name = 'hybrid_mlstm_fox_peer_stream'


def rmsnorm(x, g):
    xf = x.astype(jnp.float32)
    y = xf * lax.rsqrt(jnp.mean(xf * xf, axis=-1, keepdims=True) + EPS)
    return (y * g.astype(jnp.float32)).astype(x.dtype)


def split_cols(z):
    idx = np.cumsum(np.array(SPLIT_SIZES))[:-1].tolist()
    return jnp.split(z, idx, axis=-1)


def causal_conv(prev, x, w, b):
    s = x.shape[1]
    xp = jnp.concatenate([prev.astype(x.dtype), x], axis=1)
    y = b.astype(x.dtype)
    for j in range(CONV_W):
        y = y + w[j].astype(x.dtype) * xp[:, j:j + s]
    return jax.nn.silu(y), xp[:, xp.shape[1] - (CONV_W - 1):]


def mlstm_chunk(carry, xs):
    c, n, m = carry
    q, k, v, ig, lf = xs
    L = q.shape[1]
    bcum = jnp.cumsum(lf, axis=1)
    causal = jnp.tril(jnp.ones((L, L), dtype=bool))
    dmat = bcum[:, :, None, :] - bcum[:, None, :, :] + ig[:, None, :, :]
    dmat = jnp.where(causal[None, :, :, None], dmat, -jnp.inf)
    inter = bcum + m[:, None, :]
    m_t = jnp.maximum(inter, jnp.max(dmat, axis=2))
    w_intra = jnp.exp(dmat - m_t[:, :, None, :])
    w_state = jnp.exp(inter - m_t)
    sw = jnp.einsum('bthd,bshd->btsh', q, k) * w_intra
    num = jnp.einsum('btsh,bshd->bthd', sw, v) + w_state[..., None] * jnp.einsum('bthk,bhkv->bthv', q, c)
    den = jnp.sum(sw, axis=2) + w_state * jnp.einsum('bthk,bhk->bth', q, n)
    h = num / jnp.maximum(jnp.abs(den), jnp.exp(-m_t))[..., None]
    bl = bcum[:, -1]
    m_new = jnp.maximum(bl + m, jnp.max(bl[:, None] - bcum + ig, axis=1))
    decay_state = jnp.exp(bl + m - m_new)
    ws = jnp.exp(bl[:, None] - bcum + ig - m_new[:, None])
    c_new = decay_state[..., None, None] * c + jnp.einsum('bsh,bshk,bshv->bhkv', ws, k, v)
    n_new = decay_state[..., None] * n + jnp.einsum('bsh,bshk->bhk', ws, k)
    return (c_new, n_new, m_new), h


def mlstm_sequence(c, n, m, q, k, v, ig, lf):
    b, s = q.shape[:2]
    if s <= CHUNK:
        return mlstm_chunk((c, n, m), (q, k, v, ig, lf))
    nc = s // CHUNK

    def to_chunks(a):
        return jnp.moveaxis(a.reshape((b, nc, CHUNK) + a.shape[2:]), 1, 0)

    state, h = lax.scan(mlstm_chunk, (c, n, m),
                        (to_chunks(q), to_chunks(k), to_chunks(v), to_chunks(ig), to_chunks(lf)))
    return state, jnp.moveaxis(h, 0, 1).reshape((b, s) + h.shape[3:])


def fox_attend(q, k, v, fq, fk, q_pos, k_pos):
    sc = jnp.einsum('bqhd,bkhd->bhqk', q, k).astype(jnp.float32) * (HD_FOX ** -0.5)
    sc = sc + (jnp.transpose(fq, (0, 2, 1))[:, :, :, None] - jnp.transpose(fk, (0, 2, 1))[:, :, None, :])
    mask = k_pos[None, :] <= q_pos[:, None]
    sc = jnp.where(mask[None, None], sc, -jnp.inf)
    prob = jax.nn.softmax(sc, axis=-1)
    return jnp.einsum('bhqk,bkhd->bqhd', prob.astype(v.dtype), v)


def fox_attention(q, k, v, fq, fk, q_pos, k_pos):
    b, nq, h, d = q.shape
    if nq > FOX_BLOCK and nq % FOX_BLOCK == 0:
        nb = nq // FOX_BLOCK
        qb = jnp.moveaxis(q.reshape(b, nb, FOX_BLOCK, h, d), 1, 0)
        fqb = jnp.moveaxis(fq.reshape(b, nb, FOX_BLOCK, h), 1, 0)
        pb = q_pos.reshape(nb, FOX_BLOCK)
        out = lax.map(lambda a: fox_attend(a[0], k, v, a[1], fk, a[2], k_pos), (qb, fqb, pb))
        return jnp.moveaxis(out, 0, 1).reshape(b, nq, h, d)
    return fox_attend(q, k, v, fq, fk, q_pos, k_pos)


def memory_kv(mem, g, w):
    b, nm, _ = mem.shape
    k, v = jnp.split(rmsnorm(mem, g) @ w, 2, axis=-1)
    return k.reshape(b, nm, H_MEM, HD_MEM), v.reshape(b, nm, H_MEM, HD_MEM)


def peer_block(tok, wq, keys, u, v):
    p = tok.shape[0]
    qr = (tok @ wq).reshape(p, PEER_HEADS, 2, PEER_DQ // 2).astype(jnp.float32)
    s = jnp.einsum('phcd,hckd->phck', qr, keys.astype(jnp.float32))
    top_s, top_i = lax.top_k(s, PEER_TOPK)
    cand = top_s[:, :, 0, :, None] + top_s[:, :, 1, None, :]
    ids = top_i[:, :, 0, :, None] * N_KEYS + top_i[:, :, 1, None, :]
    best, j = lax.top_k(cand.reshape(p, PEER_HEADS, PEER_TOPK * PEER_TOPK), PEER_TOPK)
    eidx = jnp.take_along_axis(ids.reshape(p, PEER_HEADS, PEER_TOPK * PEER_TOPK), j, axis=-1)
    g = jax.nn.softmax(best, axis=-1)
    act = jax.nn.gelu(jnp.einsum('phkd,pd->phk', u[eidx], tok))
    return jnp.einsum('phk,phkd->pd', (g * act).astype(tok.dtype), v[eidx])


def peer_ffn(h, wq, keys, u, v):
    b, s, d = h.shape
    t = b * s
    nblk = -(-t // PEER_BLOCK)
    tok = jnp.pad(h.reshape(t, d), ((0, nblk * PEER_BLOCK - t), (0, 0))).reshape(nblk, PEER_BLOCK, d)
    out = lax.map(lambda blk: peer_block(blk, wq, keys, u, v), tok)
    return out.reshape(nblk * PEER_BLOCK, d)[:t].reshape(b, s, d)


def layer(x, mem_k, mem_v, conv_prev, c0, n0, m0, fk_past, fv_past, flf_past, p):
    f32 = jnp.float32
    b, s, _ = x.shape
    p0 = fk_past.shape[1]
    h = rmsnorm(x, p['norm_mix'])
    mq, mk, mv, mo, mi, mf, fq, fk, fv, ff, cq, gates = split_cols(h @ p['w_in'])
    qk_c, conv_new = causal_conv(conv_prev, jnp.concatenate([mq, mk], axis=-1), p['conv_w'], p['conv_b'])
    q_m, k_m = jnp.split(qk_c.astype(f32), 2, axis=-1)
    q_m = q_m.reshape(b, s, H_M, HD_M)
    k_m = k_m.reshape(b, s, H_M, HD_M) * (HD_M ** -0.5)
    v_m = mv.astype(f32).reshape(b, s, H_M, HD_M)
    ig = (mi + p['b_m_i']).astype(f32)
    lf = jax.nn.log_sigmoid((mf + p['b_m_f']).astype(f32))
    (c1, n1, m1), h_m = mlstm_sequence(c0.astype(f32), n0.astype(f32), m0.astype(f32), q_m, k_m, v_m, ig, lf)
    h_m = rmsnorm(h_m, p['norm_m_head'].reshape(H_M, HD_M)).reshape(b, s, D_M).astype(x.dtype)
    a_m = (h_m * jax.nn.sigmoid(mo)) @ p['w_up_m']
    k_f = fk.reshape(b, s, H_FOX, HD_FOX)
    v_f = fv.reshape(b, s, H_FOX, HD_FOX)
    lf_f = jax.nn.log_sigmoid((ff + p['b_fox_f']).astype(f32))
    k_all = jnp.concatenate([fk_past.astype(x.dtype), k_f], axis=1)
    v_all = jnp.concatenate([fv_past.astype(x.dtype), v_f], axis=1)
    cum = jnp.cumsum(jnp.concatenate([flf_past.astype(f32), lf_f], axis=1), axis=1)
    o_f = fox_attention(fq.reshape(b, s, H_FOX, HD_FOX), k_all, v_all, cum[:, p0:], cum,
                        p0 + jnp.arange(s), jnp.arange(p0 + s))
    a_f = o_f.reshape(b, s, D_FOX) @ p['w_up_f']
    sc = jnp.einsum('bqhd,bkhd->bhqk', cq.reshape(b, s, H_MEM, HD_MEM), mem_k.astype(x.dtype)).astype(f32) * (HD_MEM ** -0.5)
    o_c = jnp.einsum('bhqk,bkhd->bqhd', jax.nn.softmax(sc, axis=-1).astype(x.dtype), mem_v.astype(x.dtype))
    a_c = o_c.reshape(b, s, D_MEM) @ p['w_up_c']
    g_m, g_f, g_c = jnp.split(jax.nn.sigmoid(gates), N_BRANCH, axis=-1)
    x = x + (g_m * a_m + g_f * a_f + g_c * a_c) @ p['w_out']
    x = x + peer_ffn(rmsnorm(x, p['norm_ffn']), p['peer_wq'], p['peer_keys'], p['peer_u'], p['peer_v'])
    return x, conv_new, c1, n1, m1, k_f, v_f, lf_f


def setup_inputs(seed: int = 0) -> dict:
    key = jax.random.key(seed)
    k = jax.random.split(key, 40)
    f32 = jnp.float32

    def nrm(i, shape, scale):
        return jax.random.normal(k[i], shape, f32) * scale

    return {
        'x_prompt': nrm(0, (BATCH, SEQ, D_MODEL), 1.0),
        'x_sample': nrm(1, (DEC_BATCH, DEC_SEQ, D_MODEL), 1.0),
        'mem_prompt': nrm(2, (BATCH, N_MEM, D_MODEL), 1.0),
        'cache_fox_k': nrm(3, (DEPTH, DEC_BATCH, PAST_LEN, H_FOX, HD_FOX), 1.0),
        'cache_fox_v': nrm(4, (DEPTH, DEC_BATCH, PAST_LEN, H_FOX, HD_FOX), 1.0),
        'cache_fox_lf': jax.nn.log_sigmoid(3.0 + nrm(5, (DEPTH, DEC_BATCH, PAST_LEN, H_FOX), 1.0)),
        'state_mlstm_c': nrm(6, (DEPTH, DEC_BATCH, H_M, HD_M, HD_M), 0.1),
        'state_mlstm_n': nrm(7, (DEPTH, DEC_BATCH, H_M, HD_M), 0.1),
        'state_mlstm_m': nrm(8, (DEPTH, DEC_BATCH, H_M), 0.5),
        'state_conv': nrm(9, (DEPTH, DEC_BATCH, CONV_W - 1, 2 * D_M), 1.0),
        'cache_mem_k': nrm(10, (DEPTH, DEC_BATCH, N_MEM, H_MEM, HD_MEM), 1.0),
        'cache_mem_v': nrm(11, (DEPTH, DEC_BATCH, N_MEM, H_MEM, HD_MEM), 1.0),
        'norm_mix': 1.0 + nrm(12, (DEPTH, D_MODEL), 0.02),
        'w_in': nrm(13, (DEPTH, D_MODEL, N_IN), D_MODEL ** -0.5),
        'conv_w': nrm(14, (DEPTH, CONV_W, 2 * D_M), CONV_W ** -0.5),
        'conv_b': nrm(15, (DEPTH, 2 * D_M), 0.01),
        'b_m_i': nrm(16, (DEPTH, H_M), 0.1),
        'b_m_f': jnp.linspace(3.0, 6.0, H_M, dtype=f32)[None, :] + nrm(17, (DEPTH, H_M), 0.1),
        'norm_m_head': 1.0 + nrm(18, (DEPTH, D_M), 0.02),
        'b_fox_f': jnp.linspace(1.0, 5.0, H_FOX, dtype=f32)[None, :] + nrm(19, (DEPTH, H_FOX), 0.1),
        'norm_mem': 1.0 + nrm(20, (DEPTH, D_MODEL), 0.02),
        'w_mem_kv': nrm(21, (DEPTH, D_MODEL, 2 * D_MEM), D_MODEL ** -0.5),
        'w_up_m': nrm(22, (DEPTH, D_M, D_MODEL), D_M ** -0.5),
        'w_up_f': nrm(23, (DEPTH, D_FOX, D_MODEL), D_FOX ** -0.5),
        'w_up_c': nrm(24, (DEPTH, D_MEM, D_MODEL), D_MEM ** -0.5),
        'w_out': nrm(25, (DEPTH, D_MODEL, D_MODEL), D_MODEL ** -0.5),
        'norm_ffn': 1.0 + nrm(26, (DEPTH, D_MODEL), 0.02),
        'peer_wq': nrm(27, (DEPTH, D_MODEL, PEER_HEADS * PEER_DQ), D_MODEL ** -0.5),
        'peer_keys': nrm(28, (DEPTH, PEER_HEADS, 2, N_KEYS, PEER_DQ // 2), (PEER_DQ // 2) ** -0.5),
        'peer_u': nrm(29, (DEPTH, N_EXPERTS, D_MODEL), D_MODEL ** -0.5),
        'peer_v': nrm(30, (DEPTH, N_EXPERTS, D_MODEL), 0.25),
        'norm_final': 1.0 + nrm(31, (D_MODEL,), 0.02),
    }


def reference(x_prompt, x_sample, mem_prompt, cache_fox_k, cache_fox_v, cache_fox_lf,
              state_mlstm_c, state_mlstm_n, state_mlstm_m, state_conv, cache_mem_k, cache_mem_v,
              norm_mix, w_in, conv_w, conv_b, b_m_i, b_m_f, norm_m_head, b_fox_f, norm_mem, w_mem_kv,
              w_up_m, w_up_f, w_up_c, w_out, norm_ffn, peer_wq, peer_keys, peer_u, peer_v, norm_final):
    f32 = jnp.float32

    def params(l):
        return {'norm_mix': norm_mix[l], 'w_in': w_in[l], 'conv_w': conv_w[l], 'conv_b': conv_b[l],
                'b_m_i': b_m_i[l], 'b_m_f': b_m_f[l], 'norm_m_head': norm_m_head[l], 'b_fox_f': b_fox_f[l],
                'w_up_m': w_up_m[l], 'w_up_f': w_up_f[l], 'w_up_c': w_up_c[l], 'w_out': w_out[l],
                'norm_ffn': norm_ffn[l], 'peer_wq': peer_wq[l], 'peer_keys': peer_keys[l],
                'peer_u': peer_u[l], 'peer_v': peer_v[l]}

    xp = x_prompt
    bp = x_prompt.shape[0]
    new_p = [[] for _ in range(9)]
    for l in range(DEPTH):
        mk, mv = memory_kv(mem_prompt, norm_mem[l], w_mem_kv[l])
        xp, conv1, c1, n1, m1, kf, vf, lff = layer(
            xp, mk, mv,
            jnp.zeros((bp, CONV_W - 1, 2 * D_M), xp.dtype),
            jnp.zeros((bp, H_M, HD_M, HD_M), f32), jnp.zeros((bp, H_M, HD_M), f32), jnp.zeros((bp, H_M), f32),
            jnp.zeros((bp, 0, H_FOX, HD_FOX), xp.dtype), jnp.zeros((bp, 0, H_FOX, HD_FOX), xp.dtype),
            jnp.zeros((bp, 0, H_FOX), f32), params(l))
        for lst, a in zip(new_p, (kf, vf, lff, c1, n1, m1, conv1, mk, mv)):
            lst.append(a)
    y_prompt = rmsnorm(xp, norm_final)
    fox_k_p, fox_v_p, fox_lf_p, mlstm_c_p, mlstm_n_p, mlstm_m_p, conv_p, mem_k_p, mem_v_p = [jnp.stack(a) for a in new_p]

    xs = x_sample
    new_s = [[] for _ in range(7)]
    for l in range(DEPTH):
        xs, conv1, c1, n1, m1, kf, vf, lff = layer(
            xs, cache_mem_k[l], cache_mem_v[l], state_conv[l],
            state_mlstm_c[l], state_mlstm_n[l], state_mlstm_m[l],
            cache_fox_k[l], cache_fox_v[l], cache_fox_lf[l], params(l))
        for lst, a in zip(new_s, (kf, vf, lff, c1, n1, m1, conv1)):
            lst.append(a)
    y_sample = rmsnorm(xs, norm_final)
    fox_k_s, fox_v_s, fox_lf_s, mlstm_c_s, mlstm_n_s, mlstm_m_s, conv_s = [jnp.stack(a) for a in new_s]

    return (y_prompt, y_sample, fox_k_p, fox_v_p, fox_lf_p, mlstm_c_p, mlstm_n_p, mlstm_m_p, conv_p, mem_k_p, mem_v_p,
            fox_k_s, fox_v_s, fox_lf_s, mlstm_c_s, mlstm_n_s, mlstm_m_s, conv_s)
```

```python
import functools

import jax
import jax.numpy as jnp
from jax import lax
from jax.experimental import pallas as pl
from jax.experimental.pallas import tpu as pltpu

F32 = jnp.float32
BF16 = jnp.bfloat16
EPS = 1e-6
NEG_BIG = -1e30

D_MODEL = 1024
H_M, HD_M = 4, 128
D_M = H_M * HD_M
CONV_W = 4
H_FOX, HD_FOX = 8, 64
D_FOX = H_FOX * HD_FOX
H_MEM, HD_MEM = 4, 128
D_MEM = H_MEM * HD_MEM
N_BRANCH = 3
PEER_HEADS = 8
N_KEYS = 128
PEER_TOPK = 16
LANES = 128
MLSTM_CHUNK = 256
PROJ_TILE = 512
VMEM_LIMIT = 56 * 1024 * 1024


def _cparams(sem):
    return pltpu.CompilerParams(dimension_semantics=sem, vmem_limit_bytes=VMEM_LIMIT)


def _nt_dot(a, b):
    return lax.dot_general(a, b, (((1,), (1,)), ((), ())), preferred_element_type=F32)


def _rms(x, g):
    return x * lax.rsqrt(jnp.mean(x * x, axis=-1, keepdims=True) + EPS) * g


def _log_sigmoid(x):
    return jnp.minimum(x, 0.0) - jnp.log(1.0 + jnp.exp(-jnp.abs(x)))


def _sigmoid(x):
    return 1.0 / (1.0 + jnp.exp(-x))


def _norm_proj_kernel(x_ref, g_ref, w_ref, ws_ref, qk_ref, v_ref, o_ref, fq_ref, fk_ref, fv_ref,
                      cq_ref, gates_ref, small_ref, h_scr):
    j = pl.program_id(1)

    @pl.when(j == 0)
    def _():
        h = _rms(x_ref[...], g_ref[...]).astype(BF16)
        h_scr[...] = h
        small_ref[...] = jnp.dot(h, ws_ref[...], preferred_element_type=F32)

    acc = jnp.dot(h_scr[...], w_ref[...], preferred_element_type=F32)

    @pl.when(j == 0)
    def _():
        qk_ref[:, 0:PROJ_TILE] = acc

    @pl.when(j == 1)
    def _():
        qk_ref[:, PROJ_TILE:2 * PROJ_TILE] = acc

    @pl.when(j == 2)
    def _():
        v_ref[...] = acc.astype(BF16)

    @pl.when(j == 3)
    def _():
        o_ref[...] = acc

    @pl.when(j == 4)
    def _():
        fq_ref[...] = acc.astype(BF16)

    @pl.when(j == 5)
    def _():
        fk_ref[...] = acc

    @pl.when(j == 6)
    def _():
        fv_ref[...] = acc

    @pl.when(j == 7)
    def _():
        cq_ref[...] = acc.astype(BF16)

    @pl.when(j >= 8)
    def _():
        gates_ref[...] = acc.astype(BF16)


def _norm_proj(x2d, g, w_big, w_small):
    t = x2d.shape[0]
    tm = 512
    n_tiles = w_big.shape[1] // PROJ_TILE
    n_gate_tiles = n_tiles - 8
    row = lambda i, j: (i, 0)
    out_shape = (
        jax.ShapeDtypeStruct((t, 2 * D_M), F32),
        jax.ShapeDtypeStruct((t, D_M), BF16),
        jax.ShapeDtypeStruct((t, D_M), F32),
        jax.ShapeDtypeStruct((t, D_FOX), BF16),
        jax.ShapeDtypeStruct((t, D_FOX), F32),
        jax.ShapeDtypeStruct((t, D_FOX), F32),
        jax.ShapeDtypeStruct((t, D_MEM), BF16),
        jax.ShapeDtypeStruct((t, N_BRANCH * D_MODEL), BF16),
        jax.ShapeDtypeStruct((t, LANES), F32),
    )
    out_specs = (
        pl.BlockSpec((tm, 2 * D_M), row),
        pl.BlockSpec((tm, D_M), row),
        pl.BlockSpec((tm, D_M), row),
        pl.BlockSpec((tm, D_FOX), row),
        pl.BlockSpec((tm, D_FOX), row),
        pl.BlockSpec((tm, D_FOX), row),
        pl.BlockSpec((tm, D_MEM), row),
        pl.BlockSpec((tm, PROJ_TILE), lambda i, j: (i, jnp.clip(j - 8, 0, n_gate_tiles - 1))),
        pl.BlockSpec((tm, LANES), row),
    )
    return pl.pallas_call(
        _norm_proj_kernel,
        out_shape=out_shape,
        grid=(t // tm, n_tiles),
        in_specs=[
            pl.BlockSpec((tm, D_MODEL), row),
            pl.BlockSpec((1, D_MODEL), lambda i, j: (0, 0)),
            pl.BlockSpec((D_MODEL, PROJ_TILE), lambda i, j: (0, j)),
            pl.BlockSpec((D_MODEL, LANES), lambda i, j: (0, 0)),
        ],
        out_specs=out_specs,
        scratch_shapes=[pltpu.VMEM((tm, D_MODEL), BF16)],
        compiler_params=_cparams(("parallel", "arbitrary")),
        name="norm_proj",
    )(x2d, g, w_big, w_small)


def _norm_matmul_kernel(x_ref, g_ref, w_ref, o_ref):
    h = _rms(x_ref[...], g_ref[...]).astype(BF16)
    o_ref[...] = jnp.dot(h, w_ref[...], preferred_element_type=F32)


def _norm_matmul(x2d, g, w):
    t, n = x2d.shape[0], w.shape[1]
    tm, tn = 512, 512
    return pl.pallas_call(
        _norm_matmul_kernel,
        out_shape=jax.ShapeDtypeStruct((t, n), F32),
        grid=(t // tm, n // tn),
        in_specs=[
            pl.BlockSpec((tm, D_MODEL), lambda i, j: (i, 0)),
            pl.BlockSpec((1, D_MODEL), lambda i, j: (0, 0)),
            pl.BlockSpec((D_MODEL, tn), lambda i, j: (0, j)),
        ],
        out_specs=pl.BlockSpec((tm, tn), lambda i, j: (i, j)),
        compiler_params=_cparams(("parallel", "parallel")),
        name="norm_matmul",
    )(x2d, g, w)


def _mlstm_kernel(qk_ref, v_ref, o_ref, small_ref, smallt_ref, convw_ref, convb_ref, brow_ref,
                  bcol_ref, ghead_ref, c0_ref, m0_ref, conv0_ref,
                  hm_ref, c_out_ref, m_out_ref, conv_out_ref,
                  c_scr, m_scr, xp_scr, *, chunk):
    L = chunk
    si = pl.program_id(1)

    @pl.when(si == 0)
    def _():
        c_scr[...] = c0_ref[0]
        m_scr[...] = m0_ref[0]
        xp_scr[5:8, :] = conv0_ref[0]

    xp_scr[8:8 + L, :] = qk_ref[0]
    y = convb_ref[...] + convw_ref[0:1, :] * xp_scr[5:5 + L, :]
    for j in range(1, CONV_W):
        y = y + convw_ref[j:j + 1, :] * xp_scr[5 + j:5 + j + L, :]
    y = y * _sigmoid(y)
    tail = xp_scr[5 + L:8 + L, :]
    xp_scr[5:8, :] = tail
    conv_out_ref[0] = tail

    t_io = lax.broadcasted_iota(jnp.int32, (L, L), 0)
    s_io = lax.broadcasted_iota(jnp.int32, (L, L), 1)
    causal = s_io <= t_io
    ones_col = (lax.broadcasted_iota(jnp.int32, (L, HD_M), 1) == 0).astype(BF16)

    small = small_ref[0]
    smallt = smallt_ref[0]
    brow = brow_ref[...]
    bcol = bcol_ref[...]
    outs = []
    for h in range(H_M):
        q = y[:, h * HD_M:(h + 1) * HD_M]
        k = y[:, D_M + h * HD_M:D_M + (h + 1) * HD_M] * (HD_M ** -0.5)
        v_aug = jnp.concatenate([v_ref[0, :, h * HD_M:(h + 1) * HD_M], ones_col], axis=1)
        ig_col = small[:, h:h + 1] + brow[:, h:h + 1]
        ig_row = smallt[h:h + 1, :] + bcol[h:h + 1, :]
        lf_col = _log_sigmoid(small[:, H_M + h:H_M + h + 1] + brow[:, H_M + h:H_M + h + 1])
        lf_row = _log_sigmoid(smallt[H_M + h:H_M + h + 1, :] + bcol[H_M + h:H_M + h + 1, :])
        bcum_col = jnp.sum(jnp.where(causal, lf_row, 0.0), axis=1, keepdims=True)
        bcum_row = jnp.sum(jnp.where(t_io <= s_io, lf_col, 0.0), axis=0, keepdims=True)
        m_prev = m_scr[h][:, 0:1]
        dmat = jnp.where(causal, bcum_col - bcum_row + ig_row, -jnp.inf)
        inter = bcum_col + m_prev
        m_t = jnp.maximum(inter, jnp.max(dmat, axis=1, keepdims=True))
        w_intra = jnp.exp(dmat - m_t)
        w_state = jnp.exp(inter - m_t)
        qb = q.astype(BF16)
        sw = _nt_dot(qb, k.astype(BF16)) * w_intra
        c_aug = c_scr[h]
        num_aug = (jnp.dot(sw.astype(BF16), v_aug, preferred_element_type=F32)
                   + w_state * jnp.dot(qb, c_aug.astype(BF16), preferred_element_type=F32))
        den = num_aug[:, HD_M:HD_M + 1]
        hh = num_aug[:, 0:HD_M] / jnp.maximum(jnp.abs(den), jnp.exp(-m_t))
        hh = _rms(hh, ghead_ref[:, h * HD_M:(h + 1) * HD_M])
        outs.append(hh * _sigmoid(o_ref[0, :, h * HD_M:(h + 1) * HD_M]))
        bl = bcum_row[:, L - 1:L]
        m_new = jnp.maximum(bl + m_prev, jnp.max(bl - bcum_row + ig_row, axis=1, keepdims=True))
        decay = jnp.exp(bl + m_prev - m_new)
        ws_col = jnp.exp(bl - bcum_col + ig_col - m_new)
        kwt = (k * ws_col).T.astype(BF16)
        c_scr[h] = decay * c_aug + jnp.dot(kwt, v_aug, preferred_element_type=F32)
        m_scr[h] = jnp.broadcast_to(m_new, (1, LANES))
    hm_ref[0] = jnp.concatenate(outs, axis=1).astype(BF16)
    c_out_ref[0] = c_scr[...]
    m_out_ref[0] = m_scr[...]


def _mlstm(qk, v, o, small, smallt, convw, convb, brow, bcol, ghead, c_aug0, m0b, conv0):
    b, s, _ = qk.shape
    chunk = min(s, MLSTM_CHUNK)
    per_b = lambda bi, si: (bi, 0, 0)
    per_b4 = lambda bi, si: (bi, 0, 0, 0)
    const2 = lambda bi, si: (0, 0)
    return pl.pallas_call(
        functools.partial(_mlstm_kernel, chunk=chunk),
        out_shape=(
            jax.ShapeDtypeStruct((b, s, D_M), BF16),
            jax.ShapeDtypeStruct((b, H_M, HD_M, 2 * HD_M), F32),
            jax.ShapeDtypeStruct((b, H_M, 1, LANES), F32),
            jax.ShapeDtypeStruct((b, CONV_W - 1, 2 * D_M), F32),
        ),
        grid=(b, s // chunk),
        in_specs=[
            pl.BlockSpec((1, chunk, 2 * D_M), lambda bi, si: (bi, si, 0)),
            pl.BlockSpec((1, chunk, D_M), lambda bi, si: (bi, si, 0)),
            pl.BlockSpec((1, chunk, D_M), lambda bi, si: (bi, si, 0)),
            pl.BlockSpec((1, chunk, LANES), lambda bi, si: (bi, si, 0)),
            pl.BlockSpec((1, 16, chunk), lambda bi, si: (bi, 0, si)),
            pl.BlockSpec((CONV_W, 2 * D_M), const2),
            pl.BlockSpec((1, 2 * D_M), const2),
            pl.BlockSpec((1, LANES), const2),
            pl.BlockSpec((16, 1), const2),
            pl.BlockSpec((1, D_M), const2),
            pl.BlockSpec((1, H_M, HD_M, 2 * HD_M), per_b4),
            pl.BlockSpec((1, H_M, 1, LANES), per_b4),
            pl.BlockSpec((1, CONV_W - 1, 2 * D_M), per_b),
        ],
        out_specs=(
            pl.BlockSpec((1, chunk, D_M), lambda bi, si: (bi, si, 0)),
            pl.BlockSpec((1, H_M, HD_M, 2 * HD_M), per_b4),
            pl.BlockSpec((1, H_M, 1, LANES), per_b4),
            pl.BlockSpec((1, CONV_W - 1, 2 * D_M), per_b),
        ),
        scratch_shapes=[
            pltpu.VMEM((H_M, HD_M, 2 * HD_M), F32),
            pltpu.VMEM((H_M, 1, LANES), F32),
            pltpu.VMEM((chunk + 8, 2 * D_M), F32),
        ],
        compiler_params=_cparams(("parallel", "arbitrary")),
        name="mlstm",
    )(qk, v, o, small, smallt, convw, convb, brow, bcol, ghead, c_aug0, m0b, conv0)


def _lane_cumsum(v):
    lane = lax.broadcasted_iota(jnp.int32, v.shape, 1)
    for sh in (1, 2, 4, 8, 16, 32, 64):
        v = v + jnp.where(lane >= sh, pltpu.roll(v, sh, axis=1), 0.0)
    return v


def _fox_prep_kernel(*refs, n_past, n_new, s_valid):
    if n_past:
        past_ref, pre_ref, bias_ref, lf_ref, hi_ref, mid_ref, lo_ref = refs
    else:
        pre_ref, bias_ref, lf_ref, hi_ref, mid_ref, lo_ref = refs
    lane = lax.broadcasted_iota(jnp.int32, (H_FOX, LANES), 1)
    carry = jnp.zeros((H_FOX, 1), F32)
    for blk in range(n_past + n_new):
        if blk < n_past:
            lf = past_ref[0, blk]
        else:
            nb = blk - n_past
            lf = _log_sigmoid(pre_ref[0, nb] + bias_ref[...])
            lf = jnp.where(lane + nb * LANES < s_valid, lf, 0.0)
            lf_ref[0, nb] = lf
        cum = _lane_cumsum(lf) + carry
        carry = cum[:, LANES - 1:LANES]
        hi = cum.astype(BF16).astype(F32)
        r1 = cum - hi
        mid = r1.astype(BF16).astype(F32)
        lo = (r1 - mid).astype(BF16).astype(F32)
        hi_ref[0, blk] = hi
        mid_ref[0, blk] = mid
        lo_ref[0, blk] = lo


def _fox_prep(past_blocks, pre_blocks, bias_col, s_valid):
    b, n_new = pre_blocks.shape[:2]
    n_past = 0 if past_blocks is None else past_blocks.shape[1]
    nb = n_past + n_new
    blk = lambda n: pl.BlockSpec((1, n, H_FOX, LANES), lambda bi: (bi, 0, 0, 0))
    in_specs = ([blk(n_past)] if n_past else []) + [blk(n_new), pl.BlockSpec((H_FOX, 1), lambda bi: (0, 0))]
    args = ([past_blocks] if n_past else []) + [pre_blocks, bias_col]
    cum_shape = jax.ShapeDtypeStruct((b, nb, H_FOX, LANES), F32)
    return pl.pallas_call(
        functools.partial(_fox_prep_kernel, n_past=n_past, n_new=n_new, s_valid=s_valid),
        out_shape=(jax.ShapeDtypeStruct((b, n_new, H_FOX, LANES), F32), cum_shape, cum_shape, cum_shape),
        grid=(b,),
        in_specs=in_specs,
        out_specs=(blk(n_new), blk(nb), blk(nb), blk(nb)),
        compiler_params=_cparams(("parallel",)),
        name="fox_prep",
    )(*args)


def _fox_attn_kernel(qa_ref, ka_ref, v_ref, out_ref, m_scr, l_scr, acc_scr, *, p0, tq, tk, nk):
    qi, kj = pl.program_id(1), pl.program_id(2)

    @pl.when(kj == 0)
    def _():
        m_scr[...] = jnp.full(m_scr.shape, NEG_BIG, F32)
        l_scr[...] = jnp.zeros(l_scr.shape, F32)
        acc_scr[...] = jnp.zeros(acc_scr.shape, F32)

    @pl.when(kj * tk <= p0 + (qi + 1) * tq - 1)
    def _():
        qpos = p0 + qi * tq + lax.broadcasted_iota(jnp.int32, (tq, tk), 0)
        kpos = kj * tk + lax.broadcasted_iota(jnp.int32, (tq, tk), 1)
        mask = kpos <= qpos
        col = lax.broadcasted_iota(jnp.int32, (1, 2 * HD_FOX), 1)
        scale = jnp.where(col < HD_FOX, HD_FOX ** -0.5, 1.0).astype(BF16)
        for h in range(H_FOX):
            s = _nt_dot(qa_ref[0, h] * scale, ka_ref[0, h])
            s = jnp.where(mask, s, NEG_BIG)
            m_prev = m_scr[h]
            m_new = jnp.maximum(m_prev, jnp.max(s, axis=1, keepdims=True))
            p = jnp.exp(s - m_new)
            alpha = jnp.exp(m_prev - m_new)
            l_scr[h] = alpha * l_scr[h] + jnp.sum(p, axis=1, keepdims=True)
            acc_scr[h] = alpha * acc_scr[h] + jnp.dot(p.astype(BF16), v_ref[0, h], preferred_element_type=F32)
            m_scr[h] = m_new

    @pl.when(kj == nk - 1)
    def _():
        out_ref[0] = jnp.concatenate([acc_scr[h] / l_scr[h] for h in range(H_FOX)], axis=1).astype(BF16)


def _fox_attn(qa, ka, v, p0):
    b, _, sq, _ = qa.shape
    sk = ka.shape[2]
    tq = min(sq, 512)
    tk = 512 if sk % 512 == 0 else 384
    nq, nk = sq // tq, sk // tk
    kmap = lambda bi, qi, kj: (bi, 0, jnp.minimum(kj, (p0 + (qi + 1) * tq - 1) // tk), 0)
    return pl.pallas_call(
        functools.partial(_fox_attn_kernel, p0=p0, tq=tq, tk=tk, nk=nk),
        out_shape=jax.ShapeDtypeStruct((b, sq, D_FOX), BF16),
        grid=(b, nq, nk),
        in_specs=[
            pl.BlockSpec((1, H_FOX, tq, 2 * HD_FOX), lambda bi, qi, kj: (bi, 0, qi, 0)),
            pl.BlockSpec((1, H_FOX, tk, 2 * HD_FOX), kmap),
            pl.BlockSpec((1, H_FOX, tk, HD_FOX), kmap),
        ],
        out_specs=pl.BlockSpec((1, tq, D_FOX), lambda bi, qi, kj: (bi, qi, 0)),
        scratch_shapes=[
            pltpu.VMEM((H_FOX, tq, 1), F32),
            pltpu.VMEM((H_FOX, tq, 1), F32),
            pltpu.VMEM((H_FOX, tq, HD_FOX), F32),
        ],
        compiler_params=_cparams(("parallel", "parallel", "arbitrary")),
        name="fox_attn",
    )(qa, ka, v)


def _merge_kernel(x_ref, hm_ref, of_ref, cq_ref, gates_ref, mk_ref, mv_ref, wm_ref, wf_ref, wc_ref,
                  wo_ref, gffn_ref, xnew_ref, hn_ref):
    cq = cq_ref[0]
    heads = []
    for h in range(H_MEM):
        sl = slice(h * HD_MEM, (h + 1) * HD_MEM)
        s = _nt_dot(cq[:, sl], mk_ref[0, :, sl]) * (HD_MEM ** -0.5)
        p = jnp.exp(s - jnp.max(s, axis=1, keepdims=True))
        o = jnp.dot(p.astype(BF16), mv_ref[0, :, sl], preferred_element_type=F32)
        heads.append(o / jnp.sum(p, axis=1, keepdims=True))
    oc = jnp.concatenate(heads, axis=1).astype(BF16)
    a_m = jnp.dot(hm_ref[0], wm_ref[...], preferred_element_type=F32)
    a_f = jnp.dot(of_ref[0], wf_ref[...], preferred_element_type=F32)
    a_c = jnp.dot(oc, wc_ref[...], preferred_element_type=F32)
    g = gates_ref[0].astype(F32)
    merged = (_sigmoid(g[:, 0:D_MODEL]) * a_m + _sigmoid(g[:, D_MODEL:2 * D_MODEL]) * a_f
              + _sigmoid(g[:, 2 * D_MODEL:3 * D_MODEL]) * a_c)
    xn = x_ref[0] + jnp.dot(merged.astype(BF16), wo_ref[...], preferred_element_type=F32)
    xnew_ref[0] = xn
    hn_ref[0] = _rms(xn, gffn_ref[...]).astype(BF16)


def _merge(x, hm, of, cq, gates, mk, mv, wm, wf, wc, wo, gffn):
    b, s, _ = x.shape
    ts = min(s, 256)
    tile = lambda w: pl.BlockSpec((1, ts, w), lambda bi, si: (bi, si, 0))
    const = lambda shape: pl.BlockSpec(shape, lambda bi, si: (0, 0))
    mem = pl.BlockSpec((1, mk.shape[1], D_MEM), lambda bi, si: (bi, 0, 0))
    return pl.pallas_call(
        _merge_kernel,
        out_shape=(jax.ShapeDtypeStruct((b, s, D_MODEL), F32), jax.ShapeDtypeStruct((b, s, D_MODEL), BF16)),
        grid=(b, s // ts),
        in_specs=[tile(D_MODEL), tile(D_M), tile(D_FOX), tile(D_MEM), tile(N_BRANCH * D_MODEL), mem, mem,
                  const((D_M, D_MODEL)), const((D_FOX, D_MODEL)), const((D_MEM, D_MODEL)),
                  const((D_MODEL, D_MODEL)), const((1, D_MODEL))],
        out_specs=(tile(D_MODEL), tile(D_MODEL)),
        compiler_params=_cparams(("parallel", "parallel")),
        name="merge",
    )(x, hm, of, cq, gates, mk, mv, wm, wf, wc, wo, gffn)


def _top16(s, key_io):
    rank = jnp.full(s.shape, float(PEER_TOPK), F32)
    row_io = lax.broadcasted_iota(jnp.int32, (PEER_TOPK, s.shape[1]), 0)
    vals = jnp.zeros((PEER_TOPK, s.shape[1]), F32)
    for r in range(PEER_TOPK):
        m = jnp.max(s, axis=0, keepdims=True)
        first = jnp.min(jnp.where(s == m, key_io, float(N_KEYS)), axis=0, keepdims=True)
        hit = key_io == first
        s = jnp.where(hit, -jnp.inf, s)
        rank = jnp.where(hit, float(r), rank)
        vals = jnp.where(row_io == r, m, vals)
    return rank, vals


def _router_kernel(hn_ref, wqt_ref, keys_ref, a0_ref, cnt_ref, r1_ref, b1_ref, qr_scr):
    tp = hn_ref.shape[0]
    qr_scr[...] = _nt_dot(wqt_ref[...], hn_ref[...])
    key_io = lax.broadcasted_iota(jnp.int32, (N_KEYS, tp), 0).astype(F32)
    a_io = lax.broadcasted_iota(jnp.int32, (PEER_TOPK, tp), 0).astype(F32)

    def head(h, carry):
        base = pl.multiple_of(h * 2 * N_KEYS, 2 * N_KEYS)
        s0 = jnp.dot(keys_ref[2 * h], qr_scr[pl.ds(base, N_KEYS), :].astype(BF16),
                     preferred_element_type=F32)
        s1 = jnp.dot(keys_ref[2 * h + 1], qr_scr[pl.ds(base + N_KEYS, N_KEYS), :].astype(BF16),
                     preferred_element_type=F32)
        rank0, top0 = _top16(s0, key_io)
        rank1, top1 = _top16(s1, key_io)
        cnt = jnp.zeros((PEER_TOPK, tp), F32)
        for _ in range(PEER_TOPK):
            nxt = jnp.zeros((PEER_TOPK, tp), F32)
            for bb in range(PEER_TOPK):
                nxt = jnp.where(cnt == float(bb), top1[bb:bb + 1, :], nxt)
            f = top0 + nxt
            mx = jnp.max(f, axis=0, keepdims=True)
            first = jnp.min(jnp.where(f == mx, a_io, float(PEER_TOPK)), axis=0, keepdims=True)
            cnt = cnt + jnp.where(a_io == first, 1.0, 0.0)
        e0 = jnp.exp(top0 - top0[0:1, :])
        e1 = jnp.exp(top1 - top1[0:1, :])
        zrow = jnp.zeros((PEER_TOPK, tp), F32)
        for bb in range(PEER_TOPK):
            zrow = zrow + jnp.where(cnt > float(bb), e1[bb:bb + 1, :], 0.0)
        inv_z = 1.0 / jnp.sum(e0 * zrow, axis=0, keepdims=True)
        cnt_i = jnp.zeros((N_KEYS, tp), F32)
        for a in range(PEER_TOPK):
            cnt_i = jnp.where(rank0 == float(a), cnt[a:a + 1, :], cnt_i)
        a0_ref[h] = jnp.exp(s0 - top0[0:1, :])
        cnt_ref[h] = cnt_i
        r1_ref[h] = rank1
        b1_ref[h] = jnp.exp(s1 - top1[0:1, :]) * inv_z
        return carry

    lax.fori_loop(0, PEER_HEADS, head, 0)


def _router(hn2d, wqt, keys):
    t = hn2d.shape[0]
    tp = LANES
    shape = jax.ShapeDtypeStruct((PEER_HEADS, N_KEYS, t), F32)
    spec = pl.BlockSpec((PEER_HEADS, N_KEYS, tp), lambda i: (0, 0, i))
    return pl.pallas_call(
        _router_kernel,
        out_shape=(shape, shape, shape, shape),
        grid=(t // tp,),
        in_specs=[
            pl.BlockSpec((tp, D_MODEL), lambda i: (i, 0)),
            pl.BlockSpec(wqt.shape, lambda i: (0, 0)),
            pl.BlockSpec(keys.shape, lambda i: (0, 0, 0)),
        ],
        out_specs=(spec, spec, spec, spec),
        scratch_shapes=[pltpu.VMEM((wqt.shape[0], tp), F32)],
        compiler_params=_cparams(("parallel",)),
        name="peer_router",
    )(hn2d, wqt, keys)


def _gelu_tanh(x):
    return 0.5 * x * (1.0 + jnp.tanh(0.7978845608028654 * (x + 0.044715 * (x * x * x))))


def _experts_kernel(hn_ref, u_ref, vt_ref, a0_ref, cnt_ref, r1_ref, b1_ref, x_ref, gfin_ref, out_ref,
                    acc_scr, *, rows_per_tile, n_tiles, final_norm):
    e = pl.program_id(1)

    @pl.when(e == 0)
    def _():
        acc_scr[...] = jnp.zeros(acc_scr.shape, F32)

    act = _gelu_tanh(_nt_dot(u_ref[...], hn_ref[...]))
    pieces = []
    for ib in range(rows_per_tile):
        i = e * rows_per_tile + ib
        gate = jnp.zeros((N_KEYS, hn_ref.shape[0]), F32)
        for h in range(PEER_HEADS):
            a_row = a0_ref[h, pl.ds(i, 1), :]
            c_row = cnt_ref[h, pl.ds(i, 1), :]
            gate = gate + a_row * jnp.where(r1_ref[h] < c_row, b1_ref[h], 0.0)
        pieces.append((gate * act[ib * N_KEYS:(ib + 1) * N_KEYS, :]).astype(BF16))
    wt = jnp.concatenate(pieces, axis=0)
    acc_scr[...] += jnp.dot(vt_ref[...], wt, preferred_element_type=F32)

    @pl.when(e == n_tiles - 1)
    def _():
        xo = x_ref[...] + acc_scr[...].T
        if final_norm:
            xo = _rms(xo, gfin_ref[...])
        out_ref[...] = xo


def _experts(hn2d, u_bf, vt_bf, a0, cnt, r1, b1, x2d, gfin, final_norm):
    t = hn2d.shape[0]
    tp = 512
    rows_per_tile = 4
    te = rows_per_tile * N_KEYS
    n_tiles = u_bf.shape[0] // te
    rspec = pl.BlockSpec((PEER_HEADS, N_KEYS, tp), lambda ti, e: (0, 0, ti))
    return pl.pallas_call(
        functools.partial(_experts_kernel, rows_per_tile=rows_per_tile, n_tiles=n_tiles, final_norm=final_norm),
        out_shape=jax.ShapeDtypeStruct((t, D_MODEL), F32),
        grid=(t // tp, n_tiles),
        in_specs=[
            pl.BlockSpec((tp, D_MODEL), lambda ti, e: (ti, 0)),
            pl.BlockSpec((te, D_MODEL), lambda ti, e: (e, 0)),
            pl.BlockSpec((D_MODEL, te), lambda ti, e: (0, e)),
            rspec, rspec, rspec, rspec,
            pl.BlockSpec((tp, D_MODEL), lambda ti, e: (ti, 0)),
            pl.BlockSpec((1, D_MODEL), lambda ti, e: (0, 0)),
        ],
        out_specs=pl.BlockSpec((tp, D_MODEL), lambda ti, e: (ti, 0)),
        scratch_shapes=[pltpu.VMEM((D_MODEL, tp), F32)],
        compiler_params=_cparams(("parallel", "arbitrary")),
        name="peer_experts",
    )(hn2d, u_bf, vt_bf, a0, cnt, r1, b1, x2d, gfin)


def _prep_layer_weights(p):
    w_in = p['w_in']
    o_mi = 4 * D_M
    o_fq = o_mi + 2 * H_M
    o_ff = o_fq + 3 * D_FOX
    o_cq = o_ff + H_FOX
    w_big = jnp.concatenate([w_in[:, :o_mi], w_in[:, o_fq:o_ff], w_in[:, o_cq:]], axis=1).astype(BF16)
    w_small = jnp.concatenate([w_in[:, o_mi:o_fq], w_in[:, o_ff:o_cq]], axis=1)
    w_small = jnp.pad(w_small, ((0, 0), (0, LANES - w_small.shape[1]))).astype(BF16)
    bias = jnp.concatenate([p['b_m_i'], p['b_m_f'], p['b_fox_f']]).astype(F32)
    return dict(
        norm_mix=p['norm_mix'].reshape(1, D_MODEL), w_big=w_big, w_small=w_small,
        conv_w=p['conv_w'], conv_b=p['conv_b'].reshape(1, 2 * D_M),
        bias_row=jnp.pad(bias, (0, LANES - 16)).reshape(1, LANES), bias_col=bias.reshape(16, 1),
        fox_bias_col=p['b_fox_f'].reshape(H_FOX, 1).astype(F32),
        norm_m_head=p['norm_m_head'].reshape(1, D_M),
        w_up_m=p['w_up_m'].astype(BF16), w_up_f=p['w_up_f'].astype(BF16), w_up_c=p['w_up_c'].astype(BF16),
        w_out=p['w_out'].astype(BF16), norm_ffn=p['norm_ffn'].reshape(1, D_MODEL),
        wqt=p['peer_wq'].T.astype(BF16),
        keys=p['peer_keys'].reshape(2 * PEER_HEADS, N_KEYS, N_KEYS).astype(BF16),
        u=p['peer_u'].astype(BF16), vt=p['peer_v'].T.astype(BF16),
    )


def _to_blocks(a):
    b, n, h = a.shape
    return a.reshape(b, n // LANES, LANES, h).transpose(0, 1, 3, 2)


def _from_blocks(a):
    b, nb, h, _ = a.shape
    return a.transpose(0, 2, 1, 3).reshape(b, h, nb * LANES)


def _layer(x, mem_k, mem_v, conv_prev, c0, n0, m0, fk_past, fv_past, flf_past, w, gfin, final_norm):
    b, s, _ = x.shape
    t = b * s
    p0 = 0 if fk_past is None else fk_past.shape[1]
    qk, v_m, o_m, fq, fk, fv, cq, gates, small = _norm_proj(x.reshape(t, D_MODEL), w['norm_mix'], w['w_big'], w['w_small'])

    small3 = small.reshape(b, s, LANES)
    smallt = small3[:, :, :16].transpose(0, 2, 1)
    c_aug0 = jnp.concatenate([c0, n0[..., None], jnp.zeros(c0.shape[:-1] + (HD_M - 1,), F32)], axis=-1)
    m0b = jnp.broadcast_to(m0[:, :, None, None], (b, H_M, 1, LANES))
    hm, c_aug1, m1b, conv_new = _mlstm(
        qk.reshape(b, s, 2 * D_M), v_m.reshape(b, s, D_M), o_m.reshape(b, s, D_M), small3, smallt,
        w['conv_w'], w['conv_b'], w['bias_row'], w['bias_col'], w['norm_m_head'], c_aug0, m0b, conv_prev)
    c1, n1, m1 = c_aug1[..., :HD_M], c_aug1[..., HD_M], m1b[:, :, 0, 0]

    k_f = fk.reshape(b, s, H_FOX, HD_FOX)
    v_f = fv.reshape(b, s, H_FOX, HD_FOX)
    s_pad = -(-s // LANES) * LANES
    pre = jnp.pad(small3[:, :, 16 - H_FOX:16], ((0, 0), (0, s_pad - s), (0, 0)))
    past_blocks = None if p0 == 0 else _to_blocks(flf_past.astype(F32))
    lf_b, hi, mid, lo = _fox_prep(past_blocks, _to_blocks(pre), w['fox_bias_col'], s)
    lf_f = _from_blocks(lf_b)[:, :, :s].transpose(0, 2, 1)
    hi, mid, lo = (_from_blocks(a).astype(BF16)[..., None] for a in (hi, mid, lo))
    sk = p0 + s_pad
    one_q = jnp.ones((b, H_FOX, s, 3), BF16)
    qa = jnp.concatenate([fq.reshape(b, s, H_FOX, HD_FOX).transpose(0, 2, 1, 3),
                          hi[:, :, p0:p0 + s], mid[:, :, p0:p0 + s], lo[:, :, p0:p0 + s], -one_q,
                          jnp.zeros((b, H_FOX, s, HD_FOX - 6), BF16)], axis=-1)
    k_all, v_all = k_f, v_f
    if p0:
        k_all = jnp.concatenate([fk_past.astype(F32), k_f], axis=1)
        v_all = jnp.concatenate([fv_past.astype(F32), v_f], axis=1)
    k_all = jnp.pad(k_all.astype(BF16), ((0, 0), (0, sk - p0 - s), (0, 0), (0, 0))).transpose(0, 2, 1, 3)
    v_all = jnp.pad(v_all.astype(BF16), ((0, 0), (0, sk - p0 - s), (0, 0), (0, 0))).transpose(0, 2, 1, 3)
    ka = jnp.concatenate([k_all, jnp.ones((b, H_FOX, sk, 3), BF16), hi, mid, lo,
                          jnp.zeros((b, H_FOX, sk, HD_FOX - 6), BF16)], axis=-1)
    o_f = _fox_attn(qa, ka, v_all, p0)

    x_new, hn = _merge(x, hm, o_f, cq.reshape(b, s, D_MEM), gates.reshape(b, s, N_BRANCH * D_MODEL),
                       mem_k.reshape(b, -1, D_MEM).astype(BF16), mem_v.reshape(b, -1, D_MEM).astype(BF16),
                       w['w_up_m'], w['w_up_f'], w['w_up_c'], w['w_out'], w['norm_ffn'])

    hn2d = hn.reshape(t, D_MODEL)
    a0, cnt, r1, b1 = _router(hn2d, w['wqt'], w['keys'])
    x_out = _experts(hn2d, w['u'], w['vt'], a0, cnt, r1, b1, x_new.reshape(t, D_MODEL), gfin, final_norm)
    return x_out.reshape(b, s, D_MODEL), conv_new, c1, n1, m1, k_f, v_f, lf_f


def kernel(x_prompt, x_sample, mem_prompt, cache_fox_k, cache_fox_v, cache_fox_lf, state_mlstm_c, state_mlstm_n, state_mlstm_m, state_conv, cache_mem_k, cache_mem_v, norm_mix, w_in, conv_w, conv_b, b_m_i, b_m_f, norm_m_head, b_fox_f, norm_mem, w_mem_kv, w_up_m, w_up_f, w_up_c, w_out, norm_ffn, peer_wq, peer_keys, peer_u, peer_v, norm_final):
    depth = w_in.shape[0]
    names = dict(norm_mix=norm_mix, w_in=w_in, conv_w=conv_w, conv_b=conv_b, b_m_i=b_m_i, b_m_f=b_m_f,
                 norm_m_head=norm_m_head, b_fox_f=b_fox_f, w_up_m=w_up_m, w_up_f=w_up_f, w_up_c=w_up_c,
                 w_out=w_out, norm_ffn=norm_ffn, peer_wq=peer_wq, peer_keys=peer_keys, peer_u=peer_u,
                 peer_v=peer_v)
    weights = [_prep_layer_weights({k: a[l] for k, a in names.items()}) for l in range(depth)]
    gfin = norm_final.reshape(1, D_MODEL)
    bp, n_mem = mem_prompt.shape[0], mem_prompt.shape[1]

    xp = x_prompt
    new_p = [[] for _ in range(9)]
    for l in range(depth):
        kv = _norm_matmul(mem_prompt.reshape(bp * n_mem, D_MODEL), norm_mem[l].reshape(1, D_MODEL),
                          w_mem_kv[l].astype(BF16))
        mk = kv[:, :D_MEM].reshape(bp, n_mem, H_MEM, HD_MEM)
        mv = kv[:, D_MEM:].reshape(bp, n_mem, H_MEM, HD_MEM)
        xp, conv1, c1, n1, m1, kf, vf, lff = _layer(
            xp, mk, mv, jnp.zeros((bp, CONV_W - 1, 2 * D_M), F32),
            jnp.zeros((bp, H_M, HD_M, HD_M), F32), jnp.zeros((bp, H_M, HD_M), F32), jnp.zeros((bp, H_M), F32),
            None, None, None, weights[l], gfin, l == depth - 1)
        for lst, a in zip(new_p, (kf, vf, lff, c1, n1, m1, conv1, mk, mv)):
            lst.append(a)
    outs_p = [jnp.stack(a) for a in new_p]

    xs = x_sample
    new_s = [[] for _ in range(7)]
    for l in range(depth):
        xs, conv1, c1, n1, m1, kf, vf, lff = _layer(
            xs, cache_mem_k[l], cache_mem_v[l], state_conv[l], state_mlstm_c[l], state_mlstm_n[l],
            state_mlstm_m[l], cache_fox_k[l], cache_fox_v[l], cache_fox_lf[l], weights[l], gfin, l == depth - 1)
        for lst, a in zip(new_s, (kf, vf, lff, c1, n1, m1, conv1)):
            lst.append(a)
    outs_s = [jnp.stack(a) for a in new_s]

    return (xp, xs, *outs_p, *outs_s)
```

```python
import functools

import jax
import jax.numpy as jnp
from jax import lax
from jax.experimental import pallas as pl
from jax.experimental.pallas import tpu as pltpu

F32 = jnp.float32
BF16 = jnp.bfloat16
EPS = 1e-6
NEG_BIG = -1e30

D_MODEL = 1024
H_M, HD_M = 4, 128
D_M = H_M * HD_M
CONV_W = 4
H_FOX, HD_FOX = 8, 64
D_FOX = H_FOX * HD_FOX
H_MEM, HD_MEM = 4, 128
D_MEM = H_MEM * HD_MEM
N_BRANCH = 3
PEER_HEADS = 8
N_KEYS = 128
PEER_TOPK = 16
LANES = 128
MLSTM_CHUNK = 256
PROJ_TILE = 512
VMEM_LIMIT = 56 * 1024 * 1024


def _cparams(sem):
    return pltpu.CompilerParams(dimension_semantics=sem, vmem_limit_bytes=VMEM_LIMIT)


def _nt_dot(a, b):
    return lax.dot_general(a, b, (((1,), (1,)), ((), ())), preferred_element_type=F32)


def _rms(x, g):
    return x * lax.rsqrt(jnp.mean(x * x, axis=-1, keepdims=True) + EPS) * g


def _log_sigmoid(x):
    return jnp.minimum(x, 0.0) - jnp.log(1.0 + jnp.exp(-jnp.abs(x)))


def _sigmoid(x):
    return 1.0 / (1.0 + jnp.exp(-x))


def _norm_proj_kernel(x_ref, g_ref, w_ref, ws_ref, qk_ref, v_ref, o_ref, fq_ref, fk_ref, fv_ref,
                      cq_ref, gates_ref, small_ref, h_scr):
    j = pl.program_id(1)

    @pl.when(j == 0)
    def _():
        h = _rms(x_ref[...], g_ref[...]).astype(BF16)
        h_scr[...] = h
        small_ref[...] = jnp.dot(h, ws_ref[...], preferred_element_type=F32)

    acc = jnp.dot(h_scr[...], w_ref[...], preferred_element_type=F32)

    @pl.when(j == 0)
    def _():
        qk_ref[:, 0:PROJ_TILE] = acc

    @pl.when(j == 1)
    def _():
        qk_ref[:, PROJ_TILE:2 * PROJ_TILE] = acc

    @pl.when(j == 2)
    def _():
        v_ref[...] = acc.astype(BF16)

    @pl.when(j == 3)
    def _():
        o_ref[...] = acc

    @pl.when(j == 4)
    def _():
        fq_ref[...] = acc.astype(BF16)

    @pl.when(j == 5)
    def _():
        fk_ref[...] = acc

    @pl.when(j == 6)
    def _():
        fv_ref[...] = acc

    @pl.when(j == 7)
    def _():
        cq_ref[...] = acc.astype(BF16)

    @pl.when(j >= 8)
    def _():
        gates_ref[...] = acc.astype(BF16)


def _norm_proj(x2d, g, w_big, w_small):
    t = x2d.shape[0]
    tm = 512
    n_tiles = w_big.shape[1] // PROJ_TILE
    n_gate_tiles = n_tiles - 8
    row = lambda i, j: (i, 0)
    out_shape = (
        jax.ShapeDtypeStruct((t, 2 * D_M), F32),
        jax.ShapeDtypeStruct((t, D_M), BF16),
        jax.ShapeDtypeStruct((t, D_M), F32),
        jax.ShapeDtypeStruct((t, D_FOX), BF16),
        jax.ShapeDtypeStruct((t, D_FOX), F32),
        jax.ShapeDtypeStruct((t, D_FOX), F32),
        jax.ShapeDtypeStruct((t, D_MEM), BF16),
        jax.ShapeDtypeStruct((t, N_BRANCH * D_MODEL), BF16),
        jax.ShapeDtypeStruct((t, LANES), F32),
    )
    out_specs = (
        pl.BlockSpec((tm, 2 * D_M), row),
        pl.BlockSpec((tm, D_M), row),
        pl.BlockSpec((tm, D_M), row),
        pl.BlockSpec((tm, D_FOX), row),
        pl.BlockSpec((tm, D_FOX), row),
        pl.BlockSpec((tm, D_FOX), row),
        pl.BlockSpec((tm, D_MEM), row),
        pl.BlockSpec((tm, PROJ_TILE), lambda i, j: (i, jnp.clip(j - 8, 0, n_gate_tiles - 1))),
        pl.BlockSpec((tm, LANES), row),
    )
    return pl.pallas_call(
        _norm_proj_kernel,
        out_shape=out_shape,
        grid=(t // tm, n_tiles),
        in_specs=[
            pl.BlockSpec((tm, D_MODEL), row),
            pl.BlockSpec((1, D_MODEL), lambda i, j: (0, 0)),
            pl.BlockSpec((D_MODEL, PROJ_TILE), lambda i, j: (0, j)),
            pl.BlockSpec((D_MODEL, LANES), lambda i, j: (0, 0)),
        ],
        out_specs=out_specs,
        scratch_shapes=[pltpu.VMEM((tm, D_MODEL), BF16)],
        compiler_params=_cparams(("parallel", "arbitrary")),
        name="norm_proj",
    )(x2d, g, w_big, w_small)


def _norm_matmul_kernel(x_ref, g_ref, w_ref, o_ref):
    h = _rms(x_ref[...], g_ref[...]).astype(BF16)
    o_ref[...] = jnp.dot(h, w_ref[...], preferred_element_type=F32)


def _norm_matmul(x2d, g, w):
    t, n = x2d.shape[0], w.shape[1]
    tm, tn = 512, 512
    return pl.pallas_call(
        _norm_matmul_kernel,
        out_shape=jax.ShapeDtypeStruct((t, n), F32),
        grid=(t // tm, n // tn),
        in_specs=[
            pl.BlockSpec((tm, D_MODEL), lambda i, j: (i, 0)),
            pl.BlockSpec((1, D_MODEL), lambda i, j: (0, 0)),
            pl.BlockSpec((D_MODEL, tn), lambda i, j: (0, j)),
        ],
        out_specs=pl.BlockSpec((tm, tn), lambda i, j: (i, j)),
        compiler_params=_cparams(("parallel", "parallel")),
        name="norm_matmul",
    )(x2d, g, w)


def _mlstm_kernel(qk_ref, v_ref, o_ref, small_ref, smallt_ref, convw_ref, convb_ref, brow_ref,
                  bcol_ref, ghead_ref, c0_ref, n0_ref, m0_ref, conv0_ref,
                  hm_ref, c_out_ref, n_out_ref, m_out_ref, conv_out_ref,
                  c_scr, n_scr, m_scr, xp_scr, *, chunk):
    L = chunk
    si = pl.program_id(1)

    @pl.when(si == 0)
    def _():
        c_scr[...] = c0_ref[0]
        n_scr[...] = n0_ref[0]
        m_scr[...] = m0_ref[0]
        xp_scr[5:8, :] = conv0_ref[0]

    xp_scr[8:8 + L, :] = qk_ref[0]
    y = convb_ref[...] + convw_ref[0:1, :] * xp_scr[5:5 + L, :]
    for j in range(1, CONV_W):
        y = y + convw_ref[j:j + 1, :] * xp_scr[5 + j:5 + j + L, :]
    y = y * _sigmoid(y)
    tail = xp_scr[5 + L:8 + L, :]
    xp_scr[5:8, :] = tail
    conv_out_ref[0] = tail

    t_io = lax.broadcasted_iota(jnp.int32, (L, L), 0)
    s_io = lax.broadcasted_iota(jnp.int32, (L, L), 1)
    causal = s_io <= t_io
    ones_col = (lax.broadcasted_iota(jnp.int32, (L, HD_M), 1) == 0).astype(BF16)

    small = small_ref[0]
    smallt = smallt_ref[0]
    brow = brow_ref[...]
    bcol = bcol_ref[...]
    outs = []
    for h in range(H_M):
        q = y[:, h * HD_M:(h + 1) * HD_M]
        k = y[:, D_M + h * HD_M:D_M + (h + 1) * HD_M] * (HD_M ** -0.5)
        v_aug = jnp.concatenate([v_ref[0, :, h * HD_M:(h + 1) * HD_M], ones_col], axis=1)
        ig_col = small[:, h:h + 1] + brow[:, h:h + 1]
        ig_row = smallt[h:h + 1, :] + bcol[h:h + 1, :]
        lf_col = _log_sigmoid(small[:, H_M + h:H_M + h + 1] + brow[:, H_M + h:H_M + h + 1])
        lf_row = _log_sigmoid(smallt[H_M + h:H_M + h + 1, :] + bcol[H_M + h:H_M + h + 1, :])
        bcum_col = jnp.sum(jnp.where(causal, lf_row, 0.0), axis=1, keepdims=True)
        bcum_row = jnp.sum(jnp.where(t_io <= s_io, lf_col, 0.0), axis=0, keepdims=True)
        m_prev = m_scr[h][:, 0:1]
        dmat = jnp.where(causal, bcum_col - bcum_row + ig_row, -jnp.inf)
        inter = bcum_col + m_prev
        m_t = jnp.maximum(inter, jnp.max(dmat, axis=1, keepdims=True))
        w_intra = jnp.exp(dmat - m_t)
        w_state = jnp.exp(inter - m_t)
        qb = q.astype(BF16)
        sw = _nt_dot(qb, k.astype(BF16)) * w_intra
        c_prev = c_scr[h]
        n_prev = n_scr[h]
        intra = jnp.dot(sw.astype(BF16), v_aug, preferred_element_type=F32)
        num = intra[:, 0:HD_M] + w_state * jnp.dot(qb, c_prev.astype(BF16), preferred_element_type=F32)
        den = intra[:, HD_M:HD_M + 1] + w_state * jnp.sum(q * n_prev, axis=1, keepdims=True)
        hh = num / jnp.maximum(jnp.abs(den), jnp.exp(-m_t))
        hh = _rms(hh, ghead_ref[:, h * HD_M:(h + 1) * HD_M])
        outs.append(hh * _sigmoid(o_ref[0, :, h * HD_M:(h + 1) * HD_M]))
        bl = bcum_row[:, L - 1:L]
        m_new = jnp.maximum(bl + m_prev, jnp.max(bl - bcum_row + ig_row, axis=1, keepdims=True))
        decay = jnp.exp(bl + m_prev - m_new)
        ws_col = jnp.exp(bl - bcum_col + ig_col - m_new)
        kw = k * ws_col
        c_scr[h] = decay * c_prev + jnp.dot(kw.T.astype(BF16), v_ref[0, :, h * HD_M:(h + 1) * HD_M],
                                            preferred_element_type=F32)
        n_scr[h] = decay * n_prev + jnp.sum(kw, axis=0, keepdims=True)
        m_scr[h] = jnp.broadcast_to(m_new, (1, LANES))
    hm_ref[0] = jnp.concatenate(outs, axis=1).astype(BF16)
    c_out_ref[0] = c_scr[...]
    n_out_ref[0] = n_scr[...]
    m_out_ref[0] = m_scr[...]


def _mlstm(qk, v, o, small, smallt, convw, convb, brow, bcol, ghead, c0, n0, m0b, conv0):
    b, s, _ = qk.shape
    chunk = min(s, MLSTM_CHUNK)
    tile = lambda w: pl.BlockSpec((1, chunk, w), lambda bi, si: (bi, si, 0))
    per_b = lambda shape: pl.BlockSpec((1,) + shape, lambda bi, si: (bi,) + (0,) * len(shape))
    const = lambda shape: pl.BlockSpec(shape, lambda bi, si: (0, 0))
    state_shapes = ((H_M, HD_M, HD_M), (H_M, 1, HD_M), (H_M, 1, LANES), (CONV_W - 1, 2 * D_M))
    return pl.pallas_call(
        functools.partial(_mlstm_kernel, chunk=chunk),
        out_shape=(jax.ShapeDtypeStruct((b, s, D_M), BF16),)
        + tuple(jax.ShapeDtypeStruct((b,) + sh, F32) for sh in state_shapes),
        grid=(b, s // chunk),
        in_specs=[
            tile(2 * D_M), tile(D_M), tile(D_M), tile(LANES),
            pl.BlockSpec((1, 16, chunk), lambda bi, si: (bi, 0, si)),
            const((CONV_W, 2 * D_M)), const((1, 2 * D_M)), const((1, LANES)), const((16, 1)), const((1, D_M)),
        ] + [per_b(sh) for sh in state_shapes],
        out_specs=(tile(D_M),) + tuple(per_b(sh) for sh in state_shapes),
        scratch_shapes=[
            pltpu.VMEM((H_M, HD_M, HD_M), F32),
            pltpu.VMEM((H_M, 1, HD_M), F32),
            pltpu.VMEM((H_M, 1, LANES), F32),
            pltpu.VMEM((chunk + 8, 2 * D_M), F32),
        ],
        compiler_params=_cparams(("parallel", "arbitrary")),
        name="mlstm",
    )(qk, v, o, small, smallt, convw, convb, brow, bcol, ghead, c0, n0, m0b, conv0)


def _lane_cumsum(v):
    lane = lax.broadcasted_iota(jnp.int32, v.shape, 1)
    for sh in (1, 2, 4, 8, 16, 32, 64):
        v = v + jnp.where(lane >= sh, pltpu.roll(v, sh, axis=1), 0.0)
    return v


def _fox_prep_kernel(*refs, n_past, n_new, s_valid):
    if n_past:
        past_ref, pre_ref, bias_ref, lf_ref, hi_ref, mid_ref, lo_ref = refs
    else:
        pre_ref, bias_ref, lf_ref, hi_ref, mid_ref, lo_ref = refs
    lane = lax.broadcasted_iota(jnp.int32, (H_FOX, LANES), 1)
    carry = jnp.zeros((H_FOX, 1), F32)
    for blk in range(n_past + n_new):
        if blk < n_past:
            lf = past_ref[0, blk]
        else:
            nb = blk - n_past
            lf = _log_sigmoid(pre_ref[0, nb] + bias_ref[...])
            lf = jnp.where(lane + nb * LANES < s_valid, lf, 0.0)
            lf_ref[0, nb] = lf
        cum = _lane_cumsum(lf) + carry
        carry = cum[:, LANES - 1:LANES]
        hi = cum.astype(BF16).astype(F32)
        r1 = cum - hi
        mid = r1.astype(BF16).astype(F32)
        lo = (r1 - mid).astype(BF16).astype(F32)
        hi_ref[0, blk] = hi
        mid_ref[0, blk] = mid
        lo_ref[0, blk] = lo


def _fox_prep(past_blocks, pre_blocks, bias_col, s_valid):
    b, n_new = pre_blocks.shape[:2]
    n_past = 0 if past_blocks is None else past_blocks.shape[1]
    nb = n_past + n_new
    blk = lambda n: pl.BlockSpec((1, n, H_FOX, LANES), lambda bi: (bi, 0, 0, 0))
    in_specs = ([blk(n_past)] if n_past else []) + [blk(n_new), pl.BlockSpec((H_FOX, 1), lambda bi: (0, 0))]
    args = ([past_blocks] if n_past else []) + [pre_blocks, bias_col]
    cum_shape = jax.ShapeDtypeStruct((b, nb, H_FOX, LANES), F32)
    return pl.pallas_call(
        functools.partial(_fox_prep_kernel, n_past=n_past, n_new=n_new, s_valid=s_valid),
        out_shape=(jax.ShapeDtypeStruct((b, n_new, H_FOX, LANES), F32), cum_shape, cum_shape, cum_shape),
        grid=(b,),
        in_specs=in_specs,
        out_specs=(blk(n_new), blk(nb), blk(nb), blk(nb)),
        compiler_params=_cparams(("parallel",)),
        name="fox_prep",
    )(*args)


def _fox_attn_kernel(qa_ref, ka_ref, vt_ref, out_ref, m_scr, l_scr, acc_scr, *, p0, tq, tk, nk):
    qi, kj = pl.program_id(1), pl.program_id(2)

    @pl.when(kj == 0)
    def _():
        m_scr[...] = jnp.full(m_scr.shape, NEG_BIG, F32)
        l_scr[...] = jnp.zeros(l_scr.shape, F32)
        acc_scr[...] = jnp.zeros(acc_scr.shape, F32)

    def step(masked):
        col = lax.broadcasted_iota(jnp.int32, (1, 2 * HD_FOX), 1)
        scale = jnp.where(col < HD_FOX, HD_FOX ** -0.5, 1.0).astype(BF16)
        if masked:
            kpos = kj * tk + lax.broadcasted_iota(jnp.int32, (tk, tq), 0)
            qpos = p0 + qi * tq + lax.broadcasted_iota(jnp.int32, (tk, tq), 1)
            mask = kpos <= qpos
        for h in range(H_FOX):
            s = _nt_dot(ka_ref[0, h], qa_ref[0, h] * scale)
            if masked:
                s = jnp.where(mask, s, NEG_BIG)
            m_prev = m_scr[h]
            m_new = jnp.maximum(m_prev, jnp.max(s, axis=0, keepdims=True))
            p = jnp.exp(s - m_new)
            alpha = jnp.exp(m_prev - m_new)
            l_scr[h] = alpha * l_scr[h] + jnp.sum(p, axis=0, keepdims=True)
            acc_scr[h] = alpha * acc_scr[h] + jnp.dot(vt_ref[0, h], p.astype(BF16), preferred_element_type=F32)
            m_scr[h] = m_new

    first_q = p0 + qi * tq
    last_q = first_q + tq - 1
    unmasked = (kj + 1) * tk - 1 <= first_q

    @pl.when(unmasked)
    def _():
        step(False)

    @pl.when(jnp.logical_and(jnp.logical_not(unmasked), kj * tk <= last_q))
    def _():
        step(True)

    @pl.when(kj == nk - 1)
    def _():
        o_t = jnp.concatenate([acc_scr[h] / l_scr[h] for h in range(H_FOX)], axis=0)
        out_ref[0] = o_t.T.astype(BF16)


def _fox_attn(qa, ka, vt, p0):
    b, _, sq, _ = qa.shape
    sk = ka.shape[2]
    tq = min(sq, 512)
    tk = 512 if sk % 512 == 0 else 384
    nq, nk = sq // tq, sk // tk
    last_tile = lambda qi: (p0 + (qi + 1) * tq - 1) // tk
    return pl.pallas_call(
        functools.partial(_fox_attn_kernel, p0=p0, tq=tq, tk=tk, nk=nk),
        out_shape=jax.ShapeDtypeStruct((b, sq, D_FOX), BF16),
        grid=(b, nq, nk),
        in_specs=[
            pl.BlockSpec((1, H_FOX, tq, 2 * HD_FOX), lambda bi, qi, kj: (bi, 0, qi, 0)),
            pl.BlockSpec((1, H_FOX, tk, 2 * HD_FOX), lambda bi, qi, kj: (bi, 0, jnp.minimum(kj, last_tile(qi)), 0)),
            pl.BlockSpec((1, H_FOX, HD_FOX, tk), lambda bi, qi, kj: (bi, 0, 0, jnp.minimum(kj, last_tile(qi)))),
        ],
        out_specs=pl.BlockSpec((1, tq, D_FOX), lambda bi, qi, kj: (bi, qi, 0)),
        scratch_shapes=[
            pltpu.VMEM((H_FOX, 1, tq), F32),
            pltpu.VMEM((H_FOX, 1, tq), F32),
            pltpu.VMEM((H_FOX, HD_FOX, tq), F32),
        ],
        compiler_params=_cparams(("parallel", "parallel", "arbitrary")),
        name="fox_attn",
    )(qa, ka, vt)


def _merge_kernel(x_ref, hm_ref, of_ref, cq_ref, gates_ref, mk_ref, mv_ref, wm_ref, wf_ref, wc_ref,
                  wo_ref, gffn_ref, xnew_ref, hn_ref):
    cq = cq_ref[0]
    heads = []
    for h in range(H_MEM):
        sl = slice(h * HD_MEM, (h + 1) * HD_MEM)
        s = _nt_dot(cq[:, sl], mk_ref[0, :, sl]) * (HD_MEM ** -0.5)
        p = jnp.exp(s - jnp.max(s, axis=1, keepdims=True))
        o = jnp.dot(p.astype(BF16), mv_ref[0, :, sl], preferred_element_type=F32)
        heads.append(o / jnp.sum(p, axis=1, keepdims=True))
    oc = jnp.concatenate(heads, axis=1).astype(BF16)
    a_m = jnp.dot(hm_ref[0], wm_ref[...], preferred_element_type=F32)
    a_f = jnp.dot(of_ref[0], wf_ref[...], preferred_element_type=F32)
    a_c = jnp.dot(oc, wc_ref[...], preferred_element_type=F32)
    g = gates_ref[0].astype(F32)
    merged = (_sigmoid(g[:, 0:D_MODEL]) * a_m + _sigmoid(g[:, D_MODEL:2 * D_MODEL]) * a_f
              + _sigmoid(g[:, 2 * D_MODEL:3 * D_MODEL]) * a_c)
    xn = x_ref[0] + jnp.dot(merged.astype(BF16), wo_ref[...], preferred_element_type=F32)
    xnew_ref[0] = xn
    hn_ref[0] = _rms(xn, gffn_ref[...]).astype(BF16)


def _merge(x, hm, of, cq, gates, mk, mv, wm, wf, wc, wo, gffn):
    b, s, _ = x.shape
    ts = min(s, 256)
    tile = lambda w: pl.BlockSpec((1, ts, w), lambda bi, si: (bi, si, 0))
    const = lambda shape: pl.BlockSpec(shape, lambda bi, si: (0, 0))
    mem = pl.BlockSpec((1, mk.shape[1], D_MEM), lambda bi, si: (bi, 0, 0))
    return pl.pallas_call(
        _merge_kernel,
        out_shape=(jax.ShapeDtypeStruct((b, s, D_MODEL), F32), jax.ShapeDtypeStruct((b, s, D_MODEL), BF16)),
        grid=(b, s // ts),
        in_specs=[tile(D_MODEL), tile(D_M), tile(D_FOX), tile(D_MEM), tile(N_BRANCH * D_MODEL), mem, mem,
                  const((D_M, D_MODEL)), const((D_FOX, D_MODEL)), const((D_MEM, D_MODEL)),
                  const((D_MODEL, D_MODEL)), const((1, D_MODEL))],
        out_specs=(tile(D_MODEL), tile(D_MODEL)),
        compiler_params=_cparams(("parallel", "parallel")),
        name="merge",
    )(x, hm, of, cq, gates, mk, mv, wm, wf, wc, wo, gffn)


def _top16(s, key_io):
    rank = jnp.full(s.shape, float(PEER_TOPK), F32)
    row_io = lax.broadcasted_iota(jnp.int32, (PEER_TOPK, s.shape[1]), 0)
    vals = jnp.zeros((PEER_TOPK, s.shape[1]), F32)
    for r in range(PEER_TOPK):
        m = jnp.max(s, axis=0, keepdims=True)
        first = jnp.min(jnp.where(s == m, key_io, float(N_KEYS)), axis=0, keepdims=True)
        hit = key_io == first
        s = jnp.where(hit, -jnp.inf, s)
        rank = jnp.where(hit, float(r), rank)
        vals = jnp.where(row_io == r, m, vals)
    return rank, vals


def _router_kernel(hn_ref, wqt_ref, keys_ref, a0_ref, cnt_ref, r1_ref, b1_ref, qr_scr):
    tp = hn_ref.shape[0]
    qr_scr[...] = _nt_dot(wqt_ref[...], hn_ref[...])
    key_io = lax.broadcasted_iota(jnp.int32, (N_KEYS, tp), 0).astype(F32)
    a_io = lax.broadcasted_iota(jnp.int32, (PEER_TOPK, tp), 0).astype(F32)

    def head(h, carry):
        base = pl.multiple_of(h * 2 * N_KEYS, 2 * N_KEYS)
        s0 = jnp.dot(keys_ref[2 * h], qr_scr[pl.ds(base, N_KEYS), :].astype(BF16),
                     preferred_element_type=F32)
        s1 = jnp.dot(keys_ref[2 * h + 1], qr_scr[pl.ds(base + N_KEYS, N_KEYS), :].astype(BF16),
                     preferred_element_type=F32)
        rank0, top0 = _top16(s0, key_io)
        rank1, top1 = _top16(s1, key_io)
        cnt = jnp.zeros((PEER_TOPK, tp), F32)
        for _ in range(PEER_TOPK):
            nxt = jnp.zeros((PEER_TOPK, tp), F32)
            for bb in range(PEER_TOPK):
                nxt = jnp.where(cnt == float(bb), top1[bb:bb + 1, :], nxt)
            f = top0 + nxt
            mx = jnp.max(f, axis=0, keepdims=True)
            first = jnp.min(jnp.where(f == mx, a_io, float(PEER_TOPK)), axis=0, keepdims=True)
            cnt = cnt + jnp.where(a_io == first, 1.0, 0.0)
        e0 = jnp.exp(top0 - top0[0:1, :])
        e1 = jnp.exp(top1 - top1[0:1, :])
        zrow = jnp.zeros((PEER_TOPK, tp), F32)
        for bb in range(PEER_TOPK):
            zrow = zrow + jnp.where(cnt > float(bb), e1[bb:bb + 1, :], 0.0)
        inv_z = 1.0 / jnp.sum(e0 * zrow, axis=0, keepdims=True)
        cnt_i = jnp.zeros((N_KEYS, tp), F32)
        for a in range(PEER_TOPK):
            cnt_i = jnp.where(rank0 == float(a), cnt[a:a + 1, :], cnt_i)
        a0_ref[h] = jnp.exp(s0 - top0[0:1, :])
        cnt_ref[h] = cnt_i
        r1_ref[h] = rank1.astype(BF16)
        b1_ref[h] = (jnp.exp(s1 - top1[0:1, :]) * inv_z).astype(BF16)
        return carry

    lax.fori_loop(0, PEER_HEADS, head, 0)


def _router(hn2d, wqt, keys):
    t = hn2d.shape[0]
    tp = LANES
    shape = lambda dt: jax.ShapeDtypeStruct((PEER_HEADS, N_KEYS, t), dt)
    spec = pl.BlockSpec((PEER_HEADS, N_KEYS, tp), lambda i: (0, 0, i))
    return pl.pallas_call(
        _router_kernel,
        out_shape=(shape(F32), shape(F32), shape(BF16), shape(BF16)),
        grid=(t // tp,),
        in_specs=[
            pl.BlockSpec((tp, D_MODEL), lambda i: (i, 0)),
            pl.BlockSpec(wqt.shape, lambda i: (0, 0)),
            pl.BlockSpec(keys.shape, lambda i: (0, 0, 0)),
        ],
        out_specs=(spec, spec, spec, spec),
        scratch_shapes=[pltpu.VMEM((wqt.shape[0], tp), F32)],
        compiler_params=_cparams(("parallel",)),
        name="peer_router",
    )(hn2d, wqt, keys)


def _gelu_tanh(x):
    return 0.5 * x * (1.0 + jnp.tanh(0.7978845608028654 * (x + 0.044715 * (x * x * x))))


def _experts_kernel(hn_ref, u_ref, vt_ref, a0_ref, cnt_ref, r1_ref, b1_ref, x_ref, gfin_ref, out_ref,
                    acc_scr, *, rows_per_tile, n_tiles, final_norm):
    e = pl.program_id(1)

    @pl.when(e == 0)
    def _():
        acc_scr[...] = jnp.zeros(acc_scr.shape, F32)

    act = _gelu_tanh(_nt_dot(u_ref[...], hn_ref[...]))
    pieces = []
    for ib in range(rows_per_tile):
        i = e * rows_per_tile + ib
        gate = jnp.zeros((N_KEYS, hn_ref.shape[0]), BF16)
        for h in range(PEER_HEADS):
            a_row = a0_ref[h, pl.ds(i, 1), :].astype(BF16)
            c_row = cnt_ref[h, pl.ds(i, 1), :].astype(BF16)
            gate = gate + a_row * jnp.where(r1_ref[h] < c_row, b1_ref[h], jnp.zeros((), BF16))
        pieces.append(gate * act[ib * N_KEYS:(ib + 1) * N_KEYS, :].astype(BF16))
    wt = jnp.concatenate(pieces, axis=0)
    acc_scr[...] += jnp.dot(vt_ref[...], wt, preferred_element_type=F32)

    @pl.when(e == n_tiles - 1)
    def _():
        xo = x_ref[...] + acc_scr[...].T
        if final_norm:
            xo = _rms(xo, gfin_ref[...])
        out_ref[...] = xo


def _experts(hn2d, u_bf, vt_bf, a0, cnt, r1, b1, x2d, gfin, final_norm):
    t = hn2d.shape[0]
    tp = 512
    rows_per_tile = 4
    te = rows_per_tile * N_KEYS
    n_tiles = u_bf.shape[0] // te
    rspec = pl.BlockSpec((PEER_HEADS, N_KEYS, tp), lambda ti, e: (0, 0, ti))
    return pl.pallas_call(
        functools.partial(_experts_kernel, rows_per_tile=rows_per_tile, n_tiles=n_tiles, final_norm=final_norm),
        out_shape=jax.ShapeDtypeStruct((t, D_MODEL), F32),
        grid=(t // tp, n_tiles),
        in_specs=[
            pl.BlockSpec((tp, D_MODEL), lambda ti, e: (ti, 0)),
            pl.BlockSpec((te, D_MODEL), lambda ti, e: (e, 0)),
            pl.BlockSpec((D_MODEL, te), lambda ti, e: (0, e)),
            rspec, rspec, rspec, rspec,
            pl.BlockSpec((tp, D_MODEL), lambda ti, e: (ti, 0)),
            pl.BlockSpec((1, D_MODEL), lambda ti, e: (0, 0)),
        ],
        out_specs=pl.BlockSpec((tp, D_MODEL), lambda ti, e: (ti, 0)),
        scratch_shapes=[pltpu.VMEM((D_MODEL, tp), F32)],
        compiler_params=_cparams(("parallel", "arbitrary")),
        name="peer_experts",
    )(hn2d, u_bf, vt_bf, a0, cnt, r1, b1, x2d, gfin)


def _prep_layer_weights(p):
    w_in = p['w_in']
    o_mi = 4 * D_M
    o_fq = o_mi + 2 * H_M
    o_ff = o_fq + 3 * D_FOX
    o_cq = o_ff + H_FOX
    w_big = jnp.concatenate([w_in[:, :o_mi], w_in[:, o_fq:o_ff], w_in[:, o_cq:]], axis=1).astype(BF16)
    w_small = jnp.concatenate([w_in[:, o_mi:o_fq], w_in[:, o_ff:o_cq]], axis=1)
    w_small = jnp.pad(w_small, ((0, 0), (0, LANES - w_small.shape[1]))).astype(BF16)
    bias = jnp.concatenate([p['b_m_i'], p['b_m_f'], p['b_fox_f']]).astype(F32)
    return dict(
        norm_mix=p['norm_mix'].reshape(1, D_MODEL), w_big=w_big, w_small=w_small,
        conv_w=p['conv_w'], conv_b=p['conv_b'].reshape(1, 2 * D_M),
        bias_row=jnp.pad(bias, (0, LANES - 16)).reshape(1, LANES), bias_col=bias.reshape(16, 1),
        fox_bias_col=p['b_fox_f'].reshape(H_FOX, 1).astype(F32),
        norm_m_head=p['norm_m_head'].reshape(1, D_M),
        w_up_m=p['w_up_m'].astype(BF16), w_up_f=p['w_up_f'].astype(BF16), w_up_c=p['w_up_c'].astype(BF16),
        w_out=p['w_out'].astype(BF16), norm_ffn=p['norm_ffn'].reshape(1, D_MODEL),
        wqt=p['peer_wq'].T.astype(BF16),
        keys=p['peer_keys'].reshape(2 * PEER_HEADS, N_KEYS, N_KEYS).astype(BF16),
        u=p['peer_u'].astype(BF16), vt=p['peer_v'].T.astype(BF16),
    )


def _to_blocks(a):
    b, n, h = a.shape
    return a.reshape(b, n // LANES, LANES, h).transpose(0, 1, 3, 2)


def _from_blocks(a):
    b, nb, h, _ = a.shape
    return a.transpose(0, 2, 1, 3).reshape(b, h, nb * LANES)


def _layer(x, mem_k, mem_v, conv_prev, c0, n0, m0, fk_past, fv_past, flf_past, w, gfin, final_norm):
    b, s, _ = x.shape
    t = b * s
    p0 = 0 if fk_past is None else fk_past.shape[1]
    qk, v_m, o_m, fq, fk, fv, cq, gates, small = _norm_proj(x.reshape(t, D_MODEL), w['norm_mix'], w['w_big'], w['w_small'])

    small3 = small.reshape(b, s, LANES)
    smallt = small3[:, :, :16].transpose(0, 2, 1)
    m0b = jnp.broadcast_to(m0[:, :, None, None], (b, H_M, 1, LANES))
    hm, c1, n1, m1b, conv_new = _mlstm(
        qk.reshape(b, s, 2 * D_M), v_m.reshape(b, s, D_M), o_m.reshape(b, s, D_M), small3, smallt,
        w['conv_w'], w['conv_b'], w['bias_row'], w['bias_col'], w['norm_m_head'],
        c0, n0[:, :, None, :], m0b, conv_prev)
    n1, m1 = n1[:, :, 0, :], m1b[:, :, 0, 0]

    k_f = fk.reshape(b, s, H_FOX, HD_FOX)
    v_f = fv.reshape(b, s, H_FOX, HD_FOX)
    s_pad = -(-s // LANES) * LANES
    pre = jnp.pad(small3[:, :, 16 - H_FOX:16], ((0, 0), (0, s_pad - s), (0, 0)))
    past_blocks = None if p0 == 0 else _to_blocks(flf_past.astype(F32))
    lf_b, hi, mid, lo = _fox_prep(past_blocks, _to_blocks(pre), w['fox_bias_col'], s)
    lf_f = _from_blocks(lf_b)[:, :, :s].transpose(0, 2, 1)
    hi, mid, lo = (_from_blocks(a).astype(BF16)[..., None] for a in (hi, mid, lo))
    sk = p0 + s_pad
    one_q = jnp.ones((b, H_FOX, s_pad, 3), BF16)
    fq_pad = jnp.pad(fq.reshape(b, s, H_FOX, HD_FOX), ((0, 0), (0, s_pad - s), (0, 0), (0, 0)))
    qa = jnp.concatenate([fq_pad.transpose(0, 2, 1, 3), hi[:, :, p0:], mid[:, :, p0:], lo[:, :, p0:], -one_q,
                          jnp.zeros((b, H_FOX, s_pad, HD_FOX - 6), BF16)], axis=-1)
    k_all, v_all = k_f, v_f
    if p0:
        k_all = jnp.concatenate([fk_past.astype(F32), k_f], axis=1)
        v_all = jnp.concatenate([fv_past.astype(F32), v_f], axis=1)
    k_all = jnp.pad(k_all.astype(BF16), ((0, 0), (0, sk - p0 - s), (0, 0), (0, 0))).transpose(0, 2, 1, 3)
    vt_all = jnp.pad(v_all.astype(BF16), ((0, 0), (0, sk - p0 - s), (0, 0), (0, 0))).transpose(0, 2, 3, 1)
    ka = jnp.concatenate([k_all, jnp.ones((b, H_FOX, sk, 3), BF16), hi, mid, lo,
                          jnp.zeros((b, H_FOX, sk, HD_FOX - 6), BF16)], axis=-1)
    o_f = _fox_attn(qa, ka, vt_all, p0)[:, :s]

    x_new, hn = _merge(x, hm, o_f, cq.reshape(b, s, D_MEM), gates.reshape(b, s, N_BRANCH * D_MODEL),
                       mem_k.reshape(b, -1, D_MEM).astype(BF16), mem_v.reshape(b, -1, D_MEM).astype(BF16),
                       w['w_up_m'], w['w_up_f'], w['w_up_c'], w['w_out'], w['norm_ffn'])

    hn2d = hn.reshape(t, D_MODEL)
    a0, cnt, r1, b1 = _router(hn2d, w['wqt'], w['keys'])
    x_out = _experts(hn2d, w['u'], w['vt'], a0, cnt, r1, b1, x_new.reshape(t, D_MODEL), gfin, final_norm)
    return x_out.reshape(b, s, D_MODEL), conv_new, c1, n1, m1, k_f, v_f, lf_f


def kernel(x_prompt, x_sample, mem_prompt, cache_fox_k, cache_fox_v, cache_fox_lf, state_mlstm_c, state_mlstm_n, state_mlstm_m, state_conv, cache_mem_k, cache_mem_v, norm_mix, w_in, conv_w, conv_b, b_m_i, b_m_f, norm_m_head, b_fox_f, norm_mem, w_mem_kv, w_up_m, w_up_f, w_up_c, w_out, norm_ffn, peer_wq, peer_keys, peer_u, peer_v, norm_final):
    depth = w_in.shape[0]
    names = dict(norm_mix=norm_mix, w_in=w_in, conv_w=conv_w, conv_b=conv_b, b_m_i=b_m_i, b_m_f=b_m_f,
                 norm_m_head=norm_m_head, b_fox_f=b_fox_f, w_up_m=w_up_m, w_up_f=w_up_f, w_up_c=w_up_c,
                 w_out=w_out, norm_ffn=norm_ffn, peer_wq=peer_wq, peer_keys=peer_keys, peer_u=peer_u,
                 peer_v=peer_v)
    weights = [_prep_layer_weights({k: a[l] for k, a in names.items()}) for l in range(depth)]
    gfin = norm_final.reshape(1, D_MODEL)
    bp, n_mem = mem_prompt.shape[0], mem_prompt.shape[1]

    xp = x_prompt
    new_p = [[] for _ in range(9)]
    for l in range(depth):
        kv = _norm_matmul(mem_prompt.reshape(bp * n_mem, D_MODEL), norm_mem[l].reshape(1, D_MODEL),
                          w_mem_kv[l].astype(BF16))
        mk = kv[:, :D_MEM].reshape(bp, n_mem, H_MEM, HD_MEM)
        mv = kv[:, D_MEM:].reshape(bp, n_mem, H_MEM, HD_MEM)
        xp, conv1, c1, n1, m1, kf, vf, lff = _layer(
            xp, mk, mv, jnp.zeros((bp, CONV_W - 1, 2 * D_M), F32),
            jnp.zeros((bp, H_M, HD_M, HD_M), F32), jnp.zeros((bp, H_M, HD_M), F32), jnp.zeros((bp, H_M), F32),
            None, None, None, weights[l], gfin, l == depth - 1)
        for lst, a in zip(new_p, (kf, vf, lff, c1, n1, m1, conv1, mk, mv)):
            lst.append(a)
    outs_p = [jnp.stack(a) for a in new_p]

    xs = x_sample
    new_s = [[] for _ in range(7)]
    for l in range(depth):
        xs, conv1, c1, n1, m1, kf, vf, lff = _layer(
            xs, cache_mem_k[l], cache_mem_v[l], state_conv[l], state_mlstm_c[l], state_mlstm_n[l],
            state_mlstm_m[l], cache_fox_k[l], cache_fox_v[l], cache_fox_lf[l], weights[l], gfin, l == depth - 1)
        for lst, a in zip(new_s, (kf, vf, lff, c1, n1, m1, conv1)):
            lst.append(a)
    outs_s = [jnp.stack(a) for a in new_s]

    return (xp, xs, *outs_p, *outs_s)
```

```python
import functools

import jax
import jax.numpy as jnp
from jax import lax
from jax.experimental import pallas as pl
from jax.experimental.pallas import tpu as pltpu

F32 = jnp.float32
BF16 = jnp.bfloat16
EPS = 1e-6
NEG_BIG = -1e30

D_MODEL = 1024
H_M, HD_M = 4, 128
D_M = H_M * HD_M
CONV_W = 4
H_FOX, HD_FOX = 8, 64
D_FOX = H_FOX * HD_FOX
H_MEM, HD_MEM = 4, 128
D_MEM = H_MEM * HD_MEM
N_BRANCH = 3
PEER_HEADS = 8
N_KEYS = 128
PEER_TOPK = 16
LANES = 128
FOX_GATE_LANE = 2 * H_M
MLSTM_CHUNK = 256
PROJ_TILE = 512
VMEM_LIMIT = 56 * 1024 * 1024
EXPERT_FLAGS = None


def _cparams(sem, flags=None):
    return pltpu.CompilerParams(dimension_semantics=sem, vmem_limit_bytes=VMEM_LIMIT, flags=flags)


def _nt_dot(a, b):
    return lax.dot_general(a, b, (((1,), (1,)), ((), ())), preferred_element_type=F32)


def _rms(x, g):
    return x * lax.rsqrt(jnp.mean(x * x, axis=-1, keepdims=True) + EPS) * g


def _log_sigmoid(x):
    return jnp.minimum(x, 0.0) - jnp.log(1.0 + jnp.exp(-jnp.abs(x)))


def _sigmoid(x):
    return 1.0 / (1.0 + jnp.exp(-x))


def _norm_proj_kernel(x_ref, g_ref, w_ref, ws_ref, qk_ref, v_ref, o_ref, fq_ref, fk_ref, fv_ref,
                      cq_ref, gates_ref, small_ref, h_scr):
    j = pl.program_id(1)

    @pl.when(j == 0)
    def _():
        h = _rms(x_ref[...], g_ref[...]).astype(BF16)
        h_scr[...] = h
        small_ref[...] = jnp.dot(h, ws_ref[...], preferred_element_type=F32)

    acc = jnp.dot(h_scr[...], w_ref[...], preferred_element_type=F32)

    @pl.when(j == 0)
    def _():
        qk_ref[:, 0:PROJ_TILE] = acc

    @pl.when(j == 1)
    def _():
        qk_ref[:, PROJ_TILE:2 * PROJ_TILE] = acc

    @pl.when(j == 2)
    def _():
        v_ref[...] = acc.astype(BF16)

    @pl.when(j == 3)
    def _():
        o_ref[...] = acc

    @pl.when(j == 4)
    def _():
        fq_ref[...] = acc.astype(BF16)

    @pl.when(j == 5)
    def _():
        fk_ref[...] = acc

    @pl.when(j == 6)
    def _():
        fv_ref[...] = acc

    @pl.when(j == 7)
    def _():
        cq_ref[...] = acc.astype(BF16)

    @pl.when(j >= 8)
    def _():
        gates_ref[...] = acc.astype(BF16)


def _norm_proj(x2d, g, w_big, w_small):
    t = x2d.shape[0]
    tm = 512
    n_tiles = w_big.shape[1] // PROJ_TILE
    n_gate_tiles = n_tiles - 8
    row = lambda i, j: (i, 0)
    out_shape = (
        jax.ShapeDtypeStruct((t, 2 * D_M), F32),
        jax.ShapeDtypeStruct((t, D_M), BF16),
        jax.ShapeDtypeStruct((t, D_M), F32),
        jax.ShapeDtypeStruct((t, D_FOX), BF16),
        jax.ShapeDtypeStruct((t, D_FOX), F32),
        jax.ShapeDtypeStruct((t, D_FOX), F32),
        jax.ShapeDtypeStruct((t, D_MEM), BF16),
        jax.ShapeDtypeStruct((t, N_BRANCH * D_MODEL), BF16),
        jax.ShapeDtypeStruct((t, LANES), F32),
    )
    out_specs = (
        pl.BlockSpec((tm, 2 * D_M), row),
        pl.BlockSpec((tm, D_M), row),
        pl.BlockSpec((tm, D_M), row),
        pl.BlockSpec((tm, D_FOX), row),
        pl.BlockSpec((tm, D_FOX), row),
        pl.BlockSpec((tm, D_FOX), row),
        pl.BlockSpec((tm, D_MEM), row),
        pl.BlockSpec((tm, PROJ_TILE), lambda i, j: (i, jnp.clip(j - 8, 0, n_gate_tiles - 1))),
        pl.BlockSpec((tm, LANES), row),
    )
    return pl.pallas_call(
        _norm_proj_kernel,
        out_shape=out_shape,
        grid=(t // tm, n_tiles),
        in_specs=[
            pl.BlockSpec((tm, D_MODEL), row),
            pl.BlockSpec((1, D_MODEL), lambda i, j: (0, 0)),
            pl.BlockSpec((D_MODEL, PROJ_TILE), lambda i, j: (0, j)),
            pl.BlockSpec((D_MODEL, LANES), lambda i, j: (0, 0)),
        ],
        out_specs=out_specs,
        scratch_shapes=[pltpu.VMEM((tm, D_MODEL), BF16)],
        compiler_params=_cparams(("parallel", "arbitrary")),
        name="norm_proj",
    )(x2d, g, w_big, w_small)


def _norm_matmul_kernel(x_ref, g_ref, w_ref, o_ref):
    h = _rms(x_ref[...], g_ref[...]).astype(BF16)
    o_ref[...] = jnp.dot(h, w_ref[...], preferred_element_type=F32)


def _norm_matmul(x2d, g, w):
    t, n = x2d.shape[0], w.shape[1]
    tm, tn = 512, 512
    return pl.pallas_call(
        _norm_matmul_kernel,
        out_shape=jax.ShapeDtypeStruct((t, n), F32),
        grid=(t // tm, n // tn),
        in_specs=[
            pl.BlockSpec((tm, D_MODEL), lambda i, j: (i, 0)),
            pl.BlockSpec((1, D_MODEL), lambda i, j: (0, 0)),
            pl.BlockSpec((D_MODEL, tn), lambda i, j: (0, j)),
        ],
        out_specs=pl.BlockSpec((tm, tn), lambda i, j: (i, j)),
        compiler_params=_cparams(("parallel", "parallel")),
        name="norm_matmul",
    )(x2d, g, w)


def _mlstm_kernel(qk_ref, v_ref, o_ref, small_ref, smallt_ref, convw_ref, convb_ref, brow_ref,
                  bcol_ref, ghead_ref, c0_ref, n0_ref, m0_ref, conv0_ref,
                  hm_ref, c_out_ref, n_out_ref, m_out_ref, conv_out_ref,
                  c_scr, n_scr, m_scr, xp_scr, *, chunk):
    L = chunk
    si = pl.program_id(1)

    @pl.when(si == 0)
    def _():
        c_scr[...] = c0_ref[0]
        n_scr[...] = n0_ref[0]
        m_scr[...] = m0_ref[0]
        xp_scr[5:8, :] = conv0_ref[0]

    xp_scr[8:8 + L, :] = qk_ref[0]
    y = convb_ref[...] + convw_ref[0:1, :] * xp_scr[5:5 + L, :]
    for j in range(1, CONV_W):
        y = y + convw_ref[j:j + 1, :] * xp_scr[5 + j:5 + j + L, :]
    y = y * _sigmoid(y)
    tail = xp_scr[5 + L:8 + L, :]
    xp_scr[5:8, :] = tail
    conv_out_ref[0] = tail

    t_io = lax.broadcasted_iota(jnp.int32, (L, L), 0)
    s_io = lax.broadcasted_iota(jnp.int32, (L, L), 1)
    causal = s_io <= t_io
    ones_col = (lax.broadcasted_iota(jnp.int32, (L, HD_M), 1) == 0).astype(BF16)

    small = small_ref[0]
    smallt = smallt_ref[0]
    brow = brow_ref[...]
    bcol = bcol_ref[...]
    outs = []
    for h in range(H_M):
        q = y[:, h * HD_M:(h + 1) * HD_M]
        k = y[:, D_M + h * HD_M:D_M + (h + 1) * HD_M] * (HD_M ** -0.5)
        v_aug = jnp.concatenate([v_ref[0, :, h * HD_M:(h + 1) * HD_M], ones_col], axis=1)
        ig_col = small[:, h:h + 1] + brow[:, h:h + 1]
        ig_row = smallt[h:h + 1, :] + bcol[h:h + 1, :]
        lf_col = _log_sigmoid(small[:, H_M + h:H_M + h + 1] + brow[:, H_M + h:H_M + h + 1])
        lf_row = _log_sigmoid(smallt[H_M + h:H_M + h + 1, :] + bcol[H_M + h:H_M + h + 1, :])
        bcum_col = jnp.sum(jnp.where(causal, lf_row, 0.0), axis=1, keepdims=True)
        bcum_row = jnp.sum(jnp.where(t_io <= s_io, lf_col, 0.0), axis=0, keepdims=True)
        m_prev = m_scr[h][:, 0:1]
        dmat = jnp.where(causal, bcum_col - bcum_row + ig_row, -jnp.inf)
        inter = bcum_col + m_prev
        m_t = jnp.maximum(inter, jnp.max(dmat, axis=1, keepdims=True))
        w_intra = jnp.exp(dmat - m_t)
        w_state = jnp.exp(inter - m_t)
        qb = q.astype(BF16)
        sw = _nt_dot(qb, k.astype(BF16)) * w_intra
        c_prev = c_scr[h]
        n_prev = n_scr[h]
        intra = jnp.dot(sw.astype(BF16), v_aug, preferred_element_type=F32)
        num = intra[:, 0:HD_M] + w_state * jnp.dot(qb, c_prev.astype(BF16), preferred_element_type=F32)
        den = intra[:, HD_M:HD_M + 1] + w_state * jnp.sum(q * n_prev, axis=1, keepdims=True)
        hh = num / jnp.maximum(jnp.abs(den), jnp.exp(-m_t))
        hh = _rms(hh, ghead_ref[:, h * HD_M:(h + 1) * HD_M])
        outs.append(hh * _sigmoid(o_ref[0, :, h * HD_M:(h + 1) * HD_M]))
        bl = bcum_row[:, L - 1:L]
        m_new = jnp.maximum(bl + m_prev, jnp.max(bl - bcum_row + ig_row, axis=1, keepdims=True))
        decay = jnp.exp(bl + m_prev - m_new)
        ws_col = jnp.exp(bl - bcum_col + ig_col - m_new)
        kw = k * ws_col
        c_scr[h] = decay * c_prev + jnp.dot(kw.T.astype(BF16), v_ref[0, :, h * HD_M:(h + 1) * HD_M],
                                            preferred_element_type=F32)
        n_scr[h] = decay * n_prev + jnp.sum(kw, axis=0, keepdims=True)
        m_scr[h] = jnp.broadcast_to(m_new, (1, LANES))
    hm_ref[0] = jnp.concatenate(outs, axis=1).astype(BF16)
    c_out_ref[0] = c_scr[...]
    n_out_ref[0] = n_scr[...]
    m_out_ref[0] = m_scr[...]


def _mlstm(qk, v, o, small, smallt, convw, convb, brow, bcol, ghead, c0, n0, m0b, conv0):
    b, s, _ = qk.shape
    chunk = min(s, MLSTM_CHUNK)
    tile = lambda w: pl.BlockSpec((1, chunk, w), lambda bi, si: (bi, si, 0))
    per_b = lambda shape: pl.BlockSpec((1,) + shape, lambda bi, si: (bi,) + (0,) * len(shape))
    const = lambda shape: pl.BlockSpec(shape, lambda bi, si: (0, 0))
    state_shapes = ((H_M, HD_M, HD_M), (H_M, 1, HD_M), (H_M, 1, LANES), (CONV_W - 1, 2 * D_M))
    return pl.pallas_call(
        functools.partial(_mlstm_kernel, chunk=chunk),
        out_shape=(jax.ShapeDtypeStruct((b, s, D_M), BF16),)
        + tuple(jax.ShapeDtypeStruct((b,) + sh, F32) for sh in state_shapes),
        grid=(b, s // chunk),
        in_specs=[
            tile(2 * D_M), tile(D_M), tile(D_M), tile(LANES),
            pl.BlockSpec((1, 16, chunk), lambda bi, si: (bi, 0, si)),
            const((CONV_W, 2 * D_M)), const((1, 2 * D_M)), const((1, LANES)), const((16, 1)), const((1, D_M)),
        ] + [per_b(sh) for sh in state_shapes],
        out_specs=(tile(D_M),) + tuple(per_b(sh) for sh in state_shapes),
        scratch_shapes=[
            pltpu.VMEM((H_M, HD_M, HD_M), F32),
            pltpu.VMEM((H_M, 1, HD_M), F32),
            pltpu.VMEM((H_M, 1, LANES), F32),
            pltpu.VMEM((chunk + 8, 2 * D_M), F32),
        ],
        compiler_params=_cparams(("parallel", "arbitrary")),
        name="mlstm",
    )(qk, v, o, small, smallt, convw, convb, brow, bcol, ghead, c0, n0, m0b, conv0)


def _split3(x):
    hi = x.astype(BF16).astype(F32)
    r = x - hi
    mid = r.astype(BF16).astype(F32)
    lo = (r - mid).astype(BF16).astype(F32)
    return hi, mid, lo


def _fox_prep_kernel(*refs, n_past, n_new, s_valid, blk):
    if n_past:
        past_ref, pre_ref, bias_ref, lf_ref, hi_ref, mid_ref, lo_ref = refs
    else:
        pre_ref, bias_ref, lf_ref, hi_ref, mid_ref, lo_ref = refs
    tri = (lax.broadcasted_iota(jnp.int32, (blk, blk), 1) <= lax.broadcasted_iota(jnp.int32, (blk, blk), 0)).astype(BF16)
    carry = jnp.zeros((1, LANES), F32)
    start = 0
    while start < n_past + n_new:
        if start < n_past:
            r = min(blk, n_past - start)
            lf = past_ref[0, start:start + r, :]
        else:
            r = min(blk, n_past + n_new - start)
            ns = start - n_past
            row = ns + lax.broadcasted_iota(jnp.int32, (r, LANES), 0)
            lf = jnp.where(row < s_valid, _log_sigmoid(pre_ref[0, ns:ns + r, :] + bias_ref[...]), 0.0)
            lf_ref[0, ns:ns + r, :] = lf
        cum = carry
        for piece in _split3(lf):
            cum = cum + jnp.dot(tri[0:r, 0:r], piece.astype(BF16), preferred_element_type=F32)
        carry = cum[r - 1:r, :]
        hi, mid, lo = _split3(cum)
        hi_ref[0, start:start + r, :] = hi
        mid_ref[0, start:start + r, :] = mid
        lo_ref[0, start:start + r, :] = lo
        start += r


def _fox_prep(past, pre, bias_row, s_valid):
    b, n_new = pre.shape[:2]
    n_past = 0 if past is None else past.shape[1]
    n = n_past + n_new
    full = lambda rows: pl.BlockSpec((1, rows, LANES), lambda bi: (bi, 0, 0))
    in_specs = ([full(n_past)] if n_past else []) + [full(n_new), pl.BlockSpec((1, LANES), lambda bi: (0, 0))]
    args = ([past] if n_past else []) + [pre, bias_row]
    cum_shape = jax.ShapeDtypeStruct((b, n, LANES), F32)
    return pl.pallas_call(
        functools.partial(_fox_prep_kernel, n_past=n_past, n_new=n_new, s_valid=s_valid, blk=256),
        out_shape=(jax.ShapeDtypeStruct((b, n_new, LANES), F32), cum_shape, cum_shape, cum_shape),
        grid=(b,),
        in_specs=in_specs,
        out_specs=(full(n_new), full(n), full(n), full(n)),
        compiler_params=_cparams(("parallel",)),
        name="fox_prep",
    )(*args)


def _fox_pack_kernel(*refs, q_side, with_v):
    if with_v:
        x_ref, hi_ref, mid_ref, lo_ref, v_ref, out_ref, vt_ref = refs
        vt_ref[0] = v_ref[0].T.astype(BF16)
    else:
        x_ref, hi_ref, mid_ref, lo_ref, out_ref = refs
    tm = x_ref.shape[1]
    lane = lax.broadcasted_iota(jnp.int32, (tm, HD_FOX), 1)
    hi, mid, lo = hi_ref[0], mid_ref[0], lo_ref[0]
    for h in range(H_FOX):
        c = FOX_GATE_LANE + h
        pieces = (hi[:, c:c + 1], mid[:, c:c + 1], lo[:, c:c + 1])
        first = 0 if q_side else 3
        bias = jnp.where(jnp.logical_and(lane >= 3 - first, lane < 6 - first), -1.0 if q_side else 1.0, 0.0)
        for j, p in enumerate(pieces):
            bias = jnp.where(lane == first + j, p, bias)
        out_ref[0, h] = jnp.concatenate(
            [x_ref[0, :, h * HD_FOX:(h + 1) * HD_FOX].astype(BF16), bias.astype(BF16)], axis=1)


def _fox_pack(x, hi, mid, lo, row_offset, q_side, v=None):
    b, n, _ = x.shape
    tm = 512 if n % 512 == 0 else LANES
    off = row_offset // tm
    tile = lambda w: pl.BlockSpec((1, tm, w), lambda bi, i: (bi, i, 0))
    piece = pl.BlockSpec((1, tm, LANES), lambda bi, i: (bi, off + i, 0))
    out_shape = [jax.ShapeDtypeStruct((b, H_FOX, n, 2 * HD_FOX), BF16)]
    out_specs = [pl.BlockSpec((1, H_FOX, tm, 2 * HD_FOX), lambda bi, i: (bi, 0, i, 0))]
    in_specs = [tile(D_FOX), piece, piece, piece]
    args = [x, hi, mid, lo]
    if v is not None:
        in_specs.append(tile(D_FOX))
        args.append(v)
        out_shape.append(jax.ShapeDtypeStruct((b, D_FOX, n), BF16))
        out_specs.append(pl.BlockSpec((1, D_FOX, tm), lambda bi, i: (bi, 0, i)))
    return pl.pallas_call(
        functools.partial(_fox_pack_kernel, q_side=q_side, with_v=v is not None),
        out_shape=tuple(out_shape),
        grid=(b, n // tm),
        in_specs=in_specs,
        out_specs=tuple(out_specs),
        compiler_params=_cparams(("parallel", "parallel")),
        name="fox_pack_q" if q_side else "fox_pack_kv",
    )(*args)


def _fox_attn_kernel(qa_ref, ka_ref, vt_ref, out_ref, m_scr, l_scr, acc_scr, *, p0, tq, tk, nk):
    qi, kj = pl.program_id(1), pl.program_id(2)

    @pl.when(kj == 0)
    def _():
        m_scr[...] = jnp.full(m_scr.shape, NEG_BIG, F32)
        l_scr[...] = jnp.zeros(l_scr.shape, F32)
        acc_scr[...] = jnp.zeros(acc_scr.shape, F32)

    def step(masked):
        col = lax.broadcasted_iota(jnp.int32, (1, 2 * HD_FOX), 1)
        scale = jnp.where(col < HD_FOX, HD_FOX ** -0.5, 1.0).astype(BF16)
        if masked:
            kpos = kj * tk + lax.broadcasted_iota(jnp.int32, (tk, tq), 0)
            qpos = p0 + qi * tq + lax.broadcasted_iota(jnp.int32, (tk, tq), 1)
            mask = kpos <= qpos
        for h in range(H_FOX):
            s = _nt_dot(ka_ref[0, h], qa_ref[0, h] * scale)
            if masked:
                s = jnp.where(mask, s, NEG_BIG)
            m_prev = m_scr[h]
            m_new = jnp.maximum(m_prev, jnp.max(s, axis=0, keepdims=True))
            p = jnp.exp(s - m_new)
            alpha = jnp.exp(m_prev - m_new)
            l_scr[h] = alpha * l_scr[h] + jnp.sum(p, axis=0, keepdims=True)
            acc_scr[h] = alpha * acc_scr[h] + jnp.dot(vt_ref[0, h], p.astype(BF16), preferred_element_type=F32)
            m_scr[h] = m_new

    first_q = p0 + qi * tq
    last_q = first_q + tq - 1
    unmasked = (kj + 1) * tk - 1 <= first_q

    @pl.when(unmasked)
    def _():
        step(False)

    @pl.when(jnp.logical_and(jnp.logical_not(unmasked), kj * tk <= last_q))
    def _():
        step(True)

    @pl.when(kj == nk - 1)
    def _():
        o_t = jnp.concatenate([acc_scr[h] / l_scr[h] for h in range(H_FOX)], axis=0)
        out_ref[0] = o_t.T.astype(BF16)


def _fox_attn(qa, ka, vt, p0):
    b, _, sq, _ = qa.shape
    sk = ka.shape[2]
    tq = min(sq, 512)
    tk = 512 if sk % 512 == 0 else 384
    nq, nk = sq // tq, sk // tk
    last_tile = lambda qi: (p0 + (qi + 1) * tq - 1) // tk
    return pl.pallas_call(
        functools.partial(_fox_attn_kernel, p0=p0, tq=tq, tk=tk, nk=nk),
        out_shape=jax.ShapeDtypeStruct((b, sq, D_FOX), BF16),
        grid=(b, nq, nk),
        in_specs=[
            pl.BlockSpec((1, H_FOX, tq, 2 * HD_FOX), lambda bi, qi, kj: (bi, 0, qi, 0)),
            pl.BlockSpec((1, H_FOX, tk, 2 * HD_FOX), lambda bi, qi, kj: (bi, 0, jnp.minimum(kj, last_tile(qi)), 0)),
            pl.BlockSpec((1, H_FOX, HD_FOX, tk), lambda bi, qi, kj: (bi, 0, 0, jnp.minimum(kj, last_tile(qi)))),
        ],
        out_specs=pl.BlockSpec((1, tq, D_FOX), lambda bi, qi, kj: (bi, qi, 0)),
        scratch_shapes=[
            pltpu.VMEM((H_FOX, 1, tq), F32),
            pltpu.VMEM((H_FOX, 1, tq), F32),
            pltpu.VMEM((H_FOX, HD_FOX, tq), F32),
        ],
        compiler_params=_cparams(("parallel", "parallel", "arbitrary")),
        name="fox_attn",
    )(qa, ka, vt)


def _merge_kernel(x_ref, hm_ref, of_ref, cq_ref, gates_ref, mk_ref, mv_ref, wm_ref, wf_ref, wc_ref,
                  wo_ref, gffn_ref, xnew_ref, hn_ref):
    cq = cq_ref[0]
    heads = []
    for h in range(H_MEM):
        sl = slice(h * HD_MEM, (h + 1) * HD_MEM)
        s = _nt_dot(cq[:, sl], mk_ref[0, :, sl]) * (HD_MEM ** -0.5)
        p = jnp.exp(s - jnp.max(s, axis=1, keepdims=True))
        o = jnp.dot(p.astype(BF16), mv_ref[0, :, sl], preferred_element_type=F32)
        heads.append(o / jnp.sum(p, axis=1, keepdims=True))
    oc = jnp.concatenate(heads, axis=1).astype(BF16)
    a_m = jnp.dot(hm_ref[0], wm_ref[...], preferred_element_type=F32)
    a_f = jnp.dot(of_ref[0], wf_ref[...], preferred_element_type=F32)
    a_c = jnp.dot(oc, wc_ref[...], preferred_element_type=F32)
    g = gates_ref[0].astype(F32)
    merged = (_sigmoid(g[:, 0:D_MODEL]) * a_m + _sigmoid(g[:, D_MODEL:2 * D_MODEL]) * a_f
              + _sigmoid(g[:, 2 * D_MODEL:3 * D_MODEL]) * a_c)
    xn = x_ref[0] + jnp.dot(merged.astype(BF16), wo_ref[...], preferred_element_type=F32)
    xnew_ref[0] = xn
    hn_ref[0] = _rms(xn, gffn_ref[...]).astype(BF16)


def _merge(x, hm, of, cq, gates, mk, mv, wm, wf, wc, wo, gffn):
    b, s, _ = x.shape
    ts = min(s, 256)
    tile = lambda w: pl.BlockSpec((1, ts, w), lambda bi, si: (bi, si, 0))
    const = lambda shape: pl.BlockSpec(shape, lambda bi, si: (0, 0))
    mem = pl.BlockSpec((1, mk.shape[1], D_MEM), lambda bi, si: (bi, 0, 0))
    return pl.pallas_call(
        _merge_kernel,
        out_shape=(jax.ShapeDtypeStruct((b, s, D_MODEL), F32), jax.ShapeDtypeStruct((b, s, D_MODEL), BF16)),
        grid=(b, s // ts),
        in_specs=[tile(D_MODEL), tile(D_M), tile(D_FOX), tile(D_MEM), tile(N_BRANCH * D_MODEL), mem, mem,
                  const((D_M, D_MODEL)), const((D_FOX, D_MODEL)), const((D_MEM, D_MODEL)),
                  const((D_MODEL, D_MODEL)), const((1, D_MODEL))],
        out_specs=(tile(D_MODEL), tile(D_MODEL)),
        compiler_params=_cparams(("parallel", "parallel")),
        name="merge",
    )(x, hm, of, cq, gates, mk, mv, wm, wf, wc, wo, gffn)


def _top16(s, key_io):
    rank = jnp.full(s.shape, float(PEER_TOPK), F32)
    row_io = lax.broadcasted_iota(jnp.int32, (PEER_TOPK, s.shape[1]), 0)
    vals = jnp.zeros((PEER_TOPK, s.shape[1]), F32)
    for r in range(PEER_TOPK):
        m = jnp.max(s, axis=0, keepdims=True)
        first = jnp.min(jnp.where(s == m, key_io, float(N_KEYS)), axis=0, keepdims=True)
        hit = key_io == first
        s = jnp.where(hit, -jnp.inf, s)
        rank = jnp.where(hit, float(r), rank)
        vals = jnp.where(row_io == r, m, vals)
    return rank, vals


def _router_kernel(hn_ref, wqt_ref, keys_ref, a0_ref, cnt_ref, r1_ref, b1_ref, qr_scr):
    tp = hn_ref.shape[0]
    qr_scr[...] = _nt_dot(wqt_ref[...], hn_ref[...])
    key_io = lax.broadcasted_iota(jnp.int32, (N_KEYS, tp), 0).astype(F32)
    a_io = lax.broadcasted_iota(jnp.int32, (PEER_TOPK, tp), 0).astype(F32)

    def head(h, carry):
        base = pl.multiple_of(h * 2 * N_KEYS, 2 * N_KEYS)
        s0 = jnp.dot(keys_ref[2 * h], qr_scr[pl.ds(base, N_KEYS), :].astype(BF16),
                     preferred_element_type=F32)
        s1 = jnp.dot(keys_ref[2 * h + 1], qr_scr[pl.ds(base + N_KEYS, N_KEYS), :].astype(BF16),
                     preferred_element_type=F32)
        rank0, top0 = _top16(s0, key_io)
        rank1, top1 = _top16(s1, key_io)
        cnt = jnp.zeros((PEER_TOPK, tp), F32)
        for _ in range(PEER_TOPK):
            nxt = jnp.zeros((PEER_TOPK, tp), F32)
            for bb in range(PEER_TOPK):
                nxt = jnp.where(cnt == float(bb), top1[bb:bb + 1, :], nxt)
            f = top0 + nxt
            mx = jnp.max(f, axis=0, keepdims=True)
            first = jnp.min(jnp.where(f == mx, a_io, float(PEER_TOPK)), axis=0, keepdims=True)
            cnt = cnt + jnp.where(a_io == first, 1.0, 0.0)
        e0 = jnp.exp(top0 - top0[0:1, :])
        e1 = jnp.exp(top1 - top1[0:1, :])
        zrow = jnp.zeros((PEER_TOPK, tp), F32)
        for bb in range(PEER_TOPK):
            zrow = zrow + jnp.where(cnt > float(bb), e1[bb:bb + 1, :], 0.0)
        inv_z = 1.0 / jnp.sum(e0 * zrow, axis=0, keepdims=True)
        cnt_i = jnp.zeros((N_KEYS, tp), F32)
        for a in range(PEER_TOPK):
            cnt_i = jnp.where(rank0 == float(a), cnt[a:a + 1, :], cnt_i)
        a0_ref[h] = jnp.exp(s0 - top0[0:1, :])
        cnt_ref[h] = cnt_i
        b1 = jnp.exp(s1 - top1[0:1, :]) * inv_z
        for g in range(N_KEYS // 16):
            r1_ref[h, g] = rank1[g * 16:(g + 1) * 16, :].astype(BF16)
            b1_ref[h, g] = b1[g * 16:(g + 1) * 16, :].astype(BF16)
        return carry

    lax.fori_loop(0, PEER_HEADS, head, 0)


def _router(hn2d, wqt, keys):
    t = hn2d.shape[0]
    tp = LANES
    shape = jax.ShapeDtypeStruct((PEER_HEADS, N_KEYS, t), F32)
    spec = pl.BlockSpec((PEER_HEADS, N_KEYS, tp), lambda i: (0, 0, i))
    slab_shape = jax.ShapeDtypeStruct((PEER_HEADS, N_KEYS // 16, 16, t), BF16)
    slab_spec = pl.BlockSpec((PEER_HEADS, N_KEYS // 16, 16, tp), lambda i: (0, 0, 0, i))
    return pl.pallas_call(
        _router_kernel,
        out_shape=(shape, shape, slab_shape, slab_shape),
        grid=(t // tp,),
        in_specs=[
            pl.BlockSpec((tp, D_MODEL), lambda i: (i, 0)),
            pl.BlockSpec(wqt.shape, lambda i: (0, 0)),
            pl.BlockSpec(keys.shape, lambda i: (0, 0, 0)),
        ],
        out_specs=(spec, spec, slab_spec, slab_spec),
        scratch_shapes=[pltpu.VMEM((wqt.shape[0], tp), F32)],
        compiler_params=_cparams(("parallel",)),
        name="peer_router",
    )(hn2d, wqt, keys)


def _gelu_tanh(x):
    return 0.5 * x * (1.0 + jnp.tanh(0.7978845608028654 * (x + 0.044715 * (x * x * x))))


def _experts_kernel(hn_ref, u_ref, vt_ref, a0_ref, cnt_ref, r1_ref, b1_ref, x_ref, gfin_ref, out_ref,
                    acc_scr, pre0_scr, pre1_scr, w0_scr, w1_scr, *, rows_per_tile, n_tiles, final_norm):
    s = pl.program_id(1)
    tp = hn_ref.shape[0]
    te = u_ref.shape[0] // 2
    slab = 16
    slabs_per_row = N_KEYS // slab
    d_half = acc_scr.shape[0] // 2
    zero = jnp.zeros((), BF16)

    @pl.when(s == 0)
    def _():
        acc_scr[...] = jnp.zeros(acc_scr.shape, F32)
        pre1_scr[...] = jnp.zeros(pre1_scr.shape, F32)
        w0_scr[...] = jnp.zeros(w0_scr.shape, BF16)

    def pipeline_step(parity, pre_w, pre_r, w_w, w_r):
        e = 2 * s + parity
        cols = slice(parity * te, (parity + 1) * te)

        def stage_c(half):
            rows = slice(half * d_half, (half + 1) * d_half)
            acc_scr[rows, :] += jnp.dot(vt_ref[rows, cols], w_r[...], preferred_element_type=F32)

        def stage_a(half):
            rows = slice(half * (te // 2), (half + 1) * (te // 2))
            pre_w[rows, :] = _nt_dot(u_ref[parity * te + half * (te // 2):parity * te + (half + 1) * (te // 2), :],
                                     hn_ref[...])

        mxu_work = (lambda: stage_c(0), lambda: stage_a(0), lambda: stage_c(1), lambda: stage_a(1))

        valid = jnp.logical_and(e >= 1, e <= n_tiles)
        tile_b = jnp.clip(e - 1, 0, n_tiles - 1)
        for ib in range(rows_per_tile):
            mxu_work[ib]()
            i = tile_b * rows_per_tile + ib
            for lanes in (slice(0, tp // 2), slice(tp // 2, tp)):
                gates = [jnp.zeros((slab, tp // 2), BF16) for _ in range(slabs_per_row)]
                for h in range(PEER_HEADS):
                    a_b = jnp.broadcast_to(a0_ref[h, pl.ds(i, 1), lanes], (slab, tp // 2)).astype(BF16)
                    c_b = jnp.broadcast_to(cnt_ref[h, pl.ds(i, 1), lanes], (slab, tp // 2)).astype(BF16)
                    for g in range(slabs_per_row):
                        gates[g] = gates[g] + a_b * jnp.where(r1_ref[h, g, :, lanes] < c_b, b1_ref[h, g, :, lanes], zero)
                for g in range(slabs_per_row):
                    rows = slice(ib * N_KEYS + g * slab, ib * N_KEYS + (g + 1) * slab)
                    w = gates[g] * _gelu_tanh(pre_r[rows, lanes]).astype(BF16)
                    w_w[rows, lanes] = jnp.where(valid, w, zero)

    pipeline_step(0, pre0_scr, pre1_scr, w1_scr, w0_scr)
    pipeline_step(1, pre1_scr, pre0_scr, w0_scr, w1_scr)

    @pl.when(s == n_tiles // 2)
    def _():
        xo = x_ref[...] + acc_scr[...].T
        if final_norm:
            xo = _rms(xo, gfin_ref[...])
        out_ref[...] = xo


def _experts(hn2d, u_bf, vt_bf, a0, cnt, r1, b1, x2d, gfin, final_norm):
    t = hn2d.shape[0]
    tp = 512
    rows_per_tile = 4
    te = rows_per_tile * N_KEYS
    n_tiles = u_bf.shape[0] // te
    rspec = pl.BlockSpec((PEER_HEADS, N_KEYS, tp), lambda ti, e: (0, 0, ti))
    sspec = pl.BlockSpec((PEER_HEADS, N_KEYS // 16, 16, tp), lambda ti, e: (0, 0, 0, ti))
    return pl.pallas_call(
        functools.partial(_experts_kernel, rows_per_tile=rows_per_tile, n_tiles=n_tiles, final_norm=final_norm),
        out_shape=jax.ShapeDtypeStruct((t, D_MODEL), F32),
        grid=(t // tp, n_tiles // 2 + 1),
        in_specs=[
            pl.BlockSpec((tp, D_MODEL), lambda ti, s: (ti, 0)),
            pl.BlockSpec((2 * te, D_MODEL), lambda ti, s: (jnp.minimum(s, n_tiles // 2 - 1), 0)),
            pl.BlockSpec((D_MODEL, 2 * te), lambda ti, s: (0, jnp.maximum(s - 1, 0))),
            rspec, rspec, sspec, sspec,
            pl.BlockSpec((tp, D_MODEL), lambda ti, s: (ti, 0)),
            pl.BlockSpec((1, D_MODEL), lambda ti, s: (0, 0)),
        ],
        out_specs=pl.BlockSpec((tp, D_MODEL), lambda ti, s: (ti, 0)),
        scratch_shapes=[
            pltpu.VMEM((D_MODEL, tp), F32),
            pltpu.VMEM((te, tp), F32), pltpu.VMEM((te, tp), F32),
            pltpu.VMEM((te, tp), BF16), pltpu.VMEM((te, tp), BF16),
        ],
        compiler_params=_cparams(("parallel", "arbitrary"), EXPERT_FLAGS),
        name="peer_experts",
    )(hn2d, u_bf, vt_bf, a0, cnt, r1, b1, x2d, gfin)


def _prep_layer_weights(p):
    w_in = p['w_in']
    o_mi = 4 * D_M
    o_fq = o_mi + 2 * H_M
    o_ff = o_fq + 3 * D_FOX
    o_cq = o_ff + H_FOX
    w_big = jnp.concatenate([w_in[:, :o_mi], w_in[:, o_fq:o_ff], w_in[:, o_cq:]], axis=1).astype(BF16)
    w_small = jnp.concatenate([w_in[:, o_mi:o_fq], w_in[:, o_ff:o_cq]], axis=1)
    w_small = jnp.pad(w_small, ((0, 0), (0, LANES - w_small.shape[1]))).astype(BF16)
    bias = jnp.concatenate([p['b_m_i'], p['b_m_f'], p['b_fox_f']]).astype(F32)
    return dict(
        norm_mix=p['norm_mix'].reshape(1, D_MODEL), w_big=w_big, w_small=w_small,
        conv_w=p['conv_w'], conv_b=p['conv_b'].reshape(1, 2 * D_M),
        bias_row=jnp.pad(bias, (0, LANES - 16)).reshape(1, LANES), bias_col=bias.reshape(16, 1),
        norm_m_head=p['norm_m_head'].reshape(1, D_M),
        w_up_m=p['w_up_m'].astype(BF16), w_up_f=p['w_up_f'].astype(BF16), w_up_c=p['w_up_c'].astype(BF16),
        w_out=p['w_out'].astype(BF16), norm_ffn=p['norm_ffn'].reshape(1, D_MODEL),
        wqt=p['peer_wq'].T.astype(BF16),
        keys=p['peer_keys'].reshape(2 * PEER_HEADS, N_KEYS, N_KEYS).astype(BF16),
        u=p['peer_u'].astype(BF16), vt=p['peer_v'].T.astype(BF16),
    )


def _layer(x, mem_k, mem_v, conv_prev, c0, n0, m0, fk_past, fv_past, flf_past, w, gfin, final_norm):
    b, s, _ = x.shape
    t = b * s
    p0 = 0 if fk_past is None else fk_past.shape[1]
    qk, v_m, o_m, fq, fk, fv, cq, gates, small = _norm_proj(x.reshape(t, D_MODEL), w['norm_mix'], w['w_big'], w['w_small'])

    small3 = small.reshape(b, s, LANES)
    smallt = small3[:, :, :16].transpose(0, 2, 1)
    m0b = jnp.broadcast_to(m0[:, :, None, None], (b, H_M, 1, LANES))
    hm, c1, n1, m1b, conv_new = _mlstm(
        qk.reshape(b, s, 2 * D_M), v_m.reshape(b, s, D_M), o_m.reshape(b, s, D_M), small3, smallt,
        w['conv_w'], w['conv_b'], w['bias_row'], w['bias_col'], w['norm_m_head'],
        c0, n0[:, :, None, :], m0b, conv_prev)
    n1, m1 = n1[:, :, 0, :], m1b[:, :, 0, 0]

    k_f = fk.reshape(b, s, H_FOX, HD_FOX)
    v_f = fv.reshape(b, s, H_FOX, HD_FOX)
    s_pad = -(-s // LANES) * LANES
    pad_rows = lambda a: jnp.pad(a, ((0, 0), (0, s_pad - s), (0, 0)))
    past = None
    if p0:
        past = jnp.pad(flf_past.astype(F32), ((0, 0), (0, 0), (FOX_GATE_LANE, LANES - FOX_GATE_LANE - H_FOX)))
    lf_all, hi, mid, lo = _fox_prep(past, pad_rows(small3), w['bias_row'], s)
    lf_f = lf_all[:, :s, FOX_GATE_LANE:FOX_GATE_LANE + H_FOX]
    sk = p0 + s_pad
    k_all, v_all = pad_rows(fk.reshape(b, s, D_FOX)), pad_rows(fv.reshape(b, s, D_FOX))
    if p0:
        k_all = jnp.concatenate([fk_past.reshape(b, p0, D_FOX).astype(F32), k_all], axis=1)
        v_all = jnp.concatenate([fv_past.reshape(b, p0, D_FOX).astype(F32), v_all], axis=1)
    qa = _fox_pack(pad_rows(fq.reshape(b, s, D_FOX)), hi, mid, lo, p0, True)[0]
    ka, vt_all = _fox_pack(k_all, hi, mid, lo, 0, False, v=v_all)
    o_f = _fox_attn(qa, ka, vt_all.reshape(b, H_FOX, HD_FOX, sk), p0)[:, :s]

    x_new, hn = _merge(x, hm, o_f, cq.reshape(b, s, D_MEM), gates.reshape(b, s, N_BRANCH * D_MODEL),
                       mem_k.reshape(b, -1, D_MEM).astype(BF16), mem_v.reshape(b, -1, D_MEM).astype(BF16),
                       w['w_up_m'], w['w_up_f'], w['w_up_c'], w['w_out'], w['norm_ffn'])

    hn2d = hn.reshape(t, D_MODEL)
    a0, cnt, r1, b1 = _router(hn2d, w['wqt'], w['keys'])
    x_out = _experts(hn2d, w['u'], w['vt'], a0, cnt, r1, b1, x_new.reshape(t, D_MODEL), gfin, final_norm)
    return x_out.reshape(b, s, D_MODEL), conv_new, c1, n1, m1, k_f, v_f, lf_f


def kernel(x_prompt, x_sample, mem_prompt, cache_fox_k, cache_fox_v, cache_fox_lf, state_mlstm_c, state_mlstm_n, state_mlstm_m, state_conv, cache_mem_k, cache_mem_v, norm_mix, w_in, conv_w, conv_b, b_m_i, b_m_f, norm_m_head, b_fox_f, norm_mem, w_mem_kv, w_up_m, w_up_f, w_up_c, w_out, norm_ffn, peer_wq, peer_keys, peer_u, peer_v, norm_final):
    depth = w_in.shape[0]
    names = dict(norm_mix=norm_mix, w_in=w_in, conv_w=conv_w, conv_b=conv_b, b_m_i=b_m_i, b_m_f=b_m_f,
                 norm_m_head=norm_m_head, b_fox_f=b_fox_f, w_up_m=w_up_m, w_up_f=w_up_f, w_up_c=w_up_c,
                 w_out=w_out, norm_ffn=norm_ffn, peer_wq=peer_wq, peer_keys=peer_keys, peer_u=peer_u,
                 peer_v=peer_v)
    weights = [_prep_layer_weights({k: a[l] for k, a in names.items()}) for l in range(depth)]
    gfin = norm_final.reshape(1, D_MODEL)
    bp, n_mem = mem_prompt.shape[0], mem_prompt.shape[1]

    xp = x_prompt
    new_p = [[] for _ in range(9)]
    for l in range(depth):
        kv = _norm_matmul(mem_prompt.reshape(bp * n_mem, D_MODEL), norm_mem[l].reshape(1, D_MODEL),
                          w_mem_kv[l].astype(BF16))
        mk = kv[:, :D_MEM].reshape(bp, n_mem, H_MEM, HD_MEM)
        mv = kv[:, D_MEM:].reshape(bp, n_mem, H_MEM, HD_MEM)
        xp, conv1, c1, n1, m1, kf, vf, lff = _layer(
            xp, mk, mv, jnp.zeros((bp, CONV_W - 1, 2 * D_M), F32),
            jnp.zeros((bp, H_M, HD_M, HD_M), F32), jnp.zeros((bp, H_M, HD_M), F32), jnp.zeros((bp, H_M), F32),
            None, None, None, weights[l], gfin, l == depth - 1)
        for lst, a in zip(new_p, (kf, vf, lff, c1, n1, m1, conv1, mk, mv)):
            lst.append(a)
    outs_p = [jnp.stack(a) for a in new_p]

    xs = x_sample
    new_s = [[] for _ in range(7)]
    for l in range(depth):
        xs, conv1, c1, n1, m1, kf, vf, lff = _layer(
            xs, cache_mem_k[l], cache_mem_v[l], state_conv[l], state_mlstm_c[l], state_mlstm_n[l],
            state_mlstm_m[l], cache_fox_k[l], cache_fox_v[l], cache_fox_lf[l], weights[l], gfin, l == depth - 1)
        for lst, a in zip(new_s, (kf, vf, lff, c1, n1, m1, conv1)):
            lst.append(a)
    outs_s = [jnp.stack(a) for a in new_s]

    return (xp, xs, *outs_p, *outs_s)
```

```python
import functools

import jax
import jax.numpy as jnp
from jax import lax
from jax.experimental import pallas as pl
from jax.experimental.pallas import tpu as pltpu

F32 = jnp.float32
BF16 = jnp.bfloat16
EPS = 1e-6
NEG_BIG = -1e30

D_MODEL = 1024
H_M, HD_M = 4, 128
D_M = H_M * HD_M
CONV_W = 4
H_FOX, HD_FOX = 8, 64
D_FOX = H_FOX * HD_FOX
H_MEM, HD_MEM = 4, 128
D_MEM = H_MEM * HD_MEM
N_BRANCH = 3
PEER_HEADS = 8
N_KEYS = 128
PEER_TOPK = 16
LANES = 128
FOX_GATE_LANE = 2 * H_M
MLSTM_CHUNK = 256
PROJ_TILE = 512
VMEM_LIMIT = 56 * 1024 * 1024
EXPERT_FLAGS = None


def _cparams(sem, flags=None):
    return pltpu.CompilerParams(dimension_semantics=sem, vmem_limit_bytes=VMEM_LIMIT, flags=flags)


def _nt_dot(a, b):
    return lax.dot_general(a, b, (((1,), (1,)), ((), ())), preferred_element_type=F32)


def _rms(x, g):
    return x * lax.rsqrt(jnp.mean(x * x, axis=-1, keepdims=True) + EPS) * g


def _log_sigmoid(x):
    return jnp.minimum(x, 0.0) - jnp.log(1.0 + jnp.exp(-jnp.abs(x)))


def _sigmoid(x):
    return 1.0 / (1.0 + jnp.exp(-x))


def _norm_proj_kernel(x_ref, g_ref, w_ref, ws_ref, qk_ref, v_ref, o_ref, fq_ref, fk_ref, fv_ref,
                      cq_ref, gates_ref, small_ref, h_scr):
    j = pl.program_id(1)

    @pl.when(j == 0)
    def _():
        h = _rms(x_ref[...], g_ref[...]).astype(BF16)
        h_scr[...] = h
        small_ref[...] = jnp.dot(h, ws_ref[...], preferred_element_type=F32)

    acc = jnp.dot(h_scr[...], w_ref[...], preferred_element_type=F32)

    @pl.when(j == 0)
    def _():
        qk_ref[:, 0:PROJ_TILE] = acc

    @pl.when(j == 1)
    def _():
        qk_ref[:, PROJ_TILE:2 * PROJ_TILE] = acc

    @pl.when(j == 2)
    def _():
        v_ref[...] = acc.astype(BF16)

    @pl.when(j == 3)
    def _():
        o_ref[...] = acc

    @pl.when(j == 4)
    def _():
        fq_ref[...] = acc.astype(BF16)

    @pl.when(j == 5)
    def _():
        fk_ref[...] = acc

    @pl.when(j == 6)
    def _():
        fv_ref[...] = acc

    @pl.when(j == 7)
    def _():
        cq_ref[...] = acc.astype(BF16)

    @pl.when(j >= 8)
    def _():
        gates_ref[...] = acc.astype(BF16)


def _norm_proj(x2d, g, w_big, w_small):
    t = x2d.shape[0]
    tm = 512
    n_tiles = w_big.shape[1] // PROJ_TILE
    n_gate_tiles = n_tiles - 8
    row = lambda i, j: (i, 0)
    out_shape = (
        jax.ShapeDtypeStruct((t, 2 * D_M), F32),
        jax.ShapeDtypeStruct((t, D_M), BF16),
        jax.ShapeDtypeStruct((t, D_M), F32),
        jax.ShapeDtypeStruct((t, D_FOX), BF16),
        jax.ShapeDtypeStruct((t, D_FOX), F32),
        jax.ShapeDtypeStruct((t, D_FOX), F32),
        jax.ShapeDtypeStruct((t, D_MEM), BF16),
        jax.ShapeDtypeStruct((t, N_BRANCH * D_MODEL), BF16),
        jax.ShapeDtypeStruct((t, LANES), F32),
    )
    out_specs = (
        pl.BlockSpec((tm, 2 * D_M), row),
        pl.BlockSpec((tm, D_M), row),
        pl.BlockSpec((tm, D_M), row),
        pl.BlockSpec((tm, D_FOX), row),
        pl.BlockSpec((tm, D_FOX), row),
        pl.BlockSpec((tm, D_FOX), row),
        pl.BlockSpec((tm, D_MEM), row),
        pl.BlockSpec((tm, PROJ_TILE), lambda i, j: (i, jnp.clip(j - 8, 0, n_gate_tiles - 1))),
        pl.BlockSpec((tm, LANES), row),
    )
    return pl.pallas_call(
        _norm_proj_kernel,
        out_shape=out_shape,
        grid=(t // tm, n_tiles),
        in_specs=[
            pl.BlockSpec((tm, D_MODEL), row),
            pl.BlockSpec((1, D_MODEL), lambda i, j: (0, 0)),
            pl.BlockSpec((D_MODEL, PROJ_TILE), lambda i, j: (0, j)),
            pl.BlockSpec((D_MODEL, LANES), lambda i, j: (0, 0)),
        ],
        out_specs=out_specs,
        scratch_shapes=[pltpu.VMEM((tm, D_MODEL), BF16)],
        compiler_params=_cparams(("parallel", "arbitrary")),
        name="norm_proj",
    )(x2d, g, w_big, w_small)


def _norm_matmul_kernel(x_ref, g_ref, w_ref, o_ref):
    h = _rms(x_ref[...], g_ref[...]).astype(BF16)
    o_ref[...] = jnp.dot(h, w_ref[...], preferred_element_type=F32)


def _norm_matmul(x2d, g, w):
    t, n = x2d.shape[0], w.shape[1]
    tm, tn = 512, 512
    return pl.pallas_call(
        _norm_matmul_kernel,
        out_shape=jax.ShapeDtypeStruct((t, n), F32),
        grid=(t // tm, n // tn),
        in_specs=[
            pl.BlockSpec((tm, D_MODEL), lambda i, j: (i, 0)),
            pl.BlockSpec((1, D_MODEL), lambda i, j: (0, 0)),
            pl.BlockSpec((D_MODEL, tn), lambda i, j: (0, j)),
        ],
        out_specs=pl.BlockSpec((tm, tn), lambda i, j: (i, j)),
        compiler_params=_cparams(("parallel", "parallel")),
        name="norm_matmul",
    )(x2d, g, w)


def _mlstm_kernel(qk_ref, v_ref, o_ref, small_ref, smallt_ref, convw_ref, convb_ref, brow_ref,
                  bcol_ref, ghead_ref, c0_ref, n0_ref, m0_ref, conv0_ref,
                  hm_ref, c_out_ref, n_out_ref, m_out_ref, conv_out_ref,
                  c_scr, n_scr, m_scr, xp_scr, *, chunk):
    L = chunk
    si = pl.program_id(1)

    @pl.when(si == 0)
    def _():
        c_scr[...] = c0_ref[0]
        n_scr[...] = n0_ref[0]
        m_scr[...] = m0_ref[0]
        xp_scr[5:8, :] = conv0_ref[0]

    xp_scr[8:8 + L, :] = qk_ref[0]
    y = convb_ref[...] + convw_ref[0:1, :] * xp_scr[5:5 + L, :]
    for j in range(1, CONV_W):
        y = y + convw_ref[j:j + 1, :] * xp_scr[5 + j:5 + j + L, :]
    y = y * _sigmoid(y)
    tail = xp_scr[5 + L:8 + L, :]
    xp_scr[5:8, :] = tail
    conv_out_ref[0] = tail

    t_io = lax.broadcasted_iota(jnp.int32, (L, L), 0)
    s_io = lax.broadcasted_iota(jnp.int32, (L, L), 1)
    causal = s_io <= t_io
    ones_col = (lax.broadcasted_iota(jnp.int32, (L, HD_M), 1) == 0).astype(BF16)

    small = small_ref[0]
    smallt = smallt_ref[0]
    brow = brow_ref[...]
    bcol = bcol_ref[...]
    outs = []
    for h in range(H_M):
        q = y[:, h * HD_M:(h + 1) * HD_M]
        k = y[:, D_M + h * HD_M:D_M + (h + 1) * HD_M] * (HD_M ** -0.5)
        v_aug = jnp.concatenate([v_ref[0, :, h * HD_M:(h + 1) * HD_M], ones_col], axis=1)
        ig_col = small[:, h:h + 1] + brow[:, h:h + 1]
        ig_row = smallt[h:h + 1, :] + bcol[h:h + 1, :]
        lf_col = _log_sigmoid(small[:, H_M + h:H_M + h + 1] + brow[:, H_M + h:H_M + h + 1])
        lf_row = _log_sigmoid(smallt[H_M + h:H_M + h + 1, :] + bcol[H_M + h:H_M + h + 1, :])
        bcum_col = jnp.sum(jnp.where(causal, lf_row, 0.0), axis=1, keepdims=True)
        bcum_row = jnp.sum(jnp.where(t_io <= s_io, lf_col, 0.0), axis=0, keepdims=True)
        m_prev = m_scr[h][:, 0:1]
        dmat = jnp.where(causal, bcum_col - bcum_row + ig_row, -jnp.inf)
        inter = bcum_col + m_prev
        m_t = jnp.maximum(inter, jnp.max(dmat, axis=1, keepdims=True))
        w_intra = jnp.exp(dmat - m_t)
        w_state = jnp.exp(inter - m_t)
        qb = q.astype(BF16)
        sw = _nt_dot(qb, k.astype(BF16)) * w_intra
        c_prev = c_scr[h]
        n_prev = n_scr[h]
        intra = jnp.dot(sw.astype(BF16), v_aug, preferred_element_type=F32)
        num = intra[:, 0:HD_M] + w_state * jnp.dot(qb, c_prev.astype(BF16), preferred_element_type=F32)
        den = intra[:, HD_M:HD_M + 1] + w_state * jnp.sum(q * n_prev, axis=1, keepdims=True)
        hh = num / jnp.maximum(jnp.abs(den), jnp.exp(-m_t))
        hh = _rms(hh, ghead_ref[:, h * HD_M:(h + 1) * HD_M])
        outs.append(hh * _sigmoid(o_ref[0, :, h * HD_M:(h + 1) * HD_M]))
        bl = bcum_row[:, L - 1:L]
        m_new = jnp.maximum(bl + m_prev, jnp.max(bl - bcum_row + ig_row, axis=1, keepdims=True))
        decay = jnp.exp(bl + m_prev - m_new)
        ws_col = jnp.exp(bl - bcum_col + ig_col - m_new)
        kw = k * ws_col
        c_scr[h] = decay * c_prev + jnp.dot(kw.T.astype(BF16), v_ref[0, :, h * HD_M:(h + 1) * HD_M],
                                            preferred_element_type=F32)
        n_scr[h] = decay * n_prev + jnp.sum(kw, axis=0, keepdims=True)
        m_scr[h] = jnp.broadcast_to(m_new, (1, LANES))
    hm_ref[0] = jnp.concatenate(outs, axis=1).astype(BF16)
    c_out_ref[0] = c_scr[...]
    n_out_ref[0] = n_scr[...]
    m_out_ref[0] = m_scr[...]


def _mlstm(qk, v, o, small, smallt, convw, convb, brow, bcol, ghead, c0, n0, m0b, conv0):
    b, s, _ = qk.shape
    chunk = min(s, MLSTM_CHUNK)
    tile = lambda w: pl.BlockSpec((1, chunk, w), lambda bi, si: (bi, si, 0))
    per_b = lambda shape: pl.BlockSpec((1,) + shape, lambda bi, si: (bi,) + (0,) * len(shape))
    const = lambda shape: pl.BlockSpec(shape, lambda bi, si: (0, 0))
    state_shapes = ((H_M, HD_M, HD_M), (H_M, 1, HD_M), (H_M, 1, LANES), (CONV_W - 1, 2 * D_M))
    return pl.pallas_call(
        functools.partial(_mlstm_kernel, chunk=chunk),
        out_shape=(jax.ShapeDtypeStruct((b, s, D_M), BF16),)
        + tuple(jax.ShapeDtypeStruct((b,) + sh, F32) for sh in state_shapes),
        grid=(b, s // chunk),
        in_specs=[
            tile(2 * D_M), tile(D_M), tile(D_M), tile(LANES),
            pl.BlockSpec((1, 16, chunk), lambda bi, si: (bi, 0, si)),
            const((CONV_W, 2 * D_M)), const((1, 2 * D_M)), const((1, LANES)), const((16, 1)), const((1, D_M)),
        ] + [per_b(sh) for sh in state_shapes],
        out_specs=(tile(D_M),) + tuple(per_b(sh) for sh in state_shapes),
        scratch_shapes=[
            pltpu.VMEM((H_M, HD_M, HD_M), F32),
            pltpu.VMEM((H_M, 1, HD_M), F32),
            pltpu.VMEM((H_M, 1, LANES), F32),
            pltpu.VMEM((chunk + 8, 2 * D_M), F32),
        ],
        compiler_params=_cparams(("parallel", "arbitrary")),
        name="mlstm",
    )(qk, v, o, small, smallt, convw, convb, brow, bcol, ghead, c0, n0, m0b, conv0)


def _split3(x):
    hi = x.astype(BF16).astype(F32)
    r = x - hi
    mid = r.astype(BF16).astype(F32)
    lo = (r - mid).astype(BF16).astype(F32)
    return hi, mid, lo


def _fox_prep_kernel(*refs, n_past, n_new, s_valid, blk):
    if n_past:
        past_ref, pre_ref, bias_ref, lf_ref, hi_ref, mid_ref, lo_ref = refs
    else:
        pre_ref, bias_ref, lf_ref, hi_ref, mid_ref, lo_ref = refs
    tri = (lax.broadcasted_iota(jnp.int32, (blk, blk), 1) <= lax.broadcasted_iota(jnp.int32, (blk, blk), 0)).astype(BF16)
    carry = jnp.zeros((1, LANES), F32)
    start = 0
    while start < n_past + n_new:
        if start < n_past:
            r = min(blk, n_past - start)
            lf = past_ref[0, start:start + r, :]
        else:
            r = min(blk, n_past + n_new - start)
            ns = start - n_past
            row = ns + lax.broadcasted_iota(jnp.int32, (r, LANES), 0)
            lf = jnp.where(row < s_valid, _log_sigmoid(pre_ref[0, ns:ns + r, :] + bias_ref[...]), 0.0)
            lf_ref[0, ns:ns + r, :] = lf
        cum = carry
        for piece in _split3(lf):
            cum = cum + jnp.dot(tri[0:r, 0:r], piece.astype(BF16), preferred_element_type=F32)
        carry = cum[r - 1:r, :]
        hi, mid, lo = _split3(cum)
        hi_ref[0, start:start + r, :] = hi
        mid_ref[0, start:start + r, :] = mid
        lo_ref[0, start:start + r, :] = lo
        start += r


def _fox_prep(past, pre, bias_row, s_valid):
    b, n_new = pre.shape[:2]
    n_past = 0 if past is None else past.shape[1]
    n = n_past + n_new
    full = lambda rows: pl.BlockSpec((1, rows, LANES), lambda bi: (bi, 0, 0))
    in_specs = ([full(n_past)] if n_past else []) + [full(n_new), pl.BlockSpec((1, LANES), lambda bi: (0, 0))]
    args = ([past] if n_past else []) + [pre, bias_row]
    cum_shape = jax.ShapeDtypeStruct((b, n, LANES), F32)
    return pl.pallas_call(
        functools.partial(_fox_prep_kernel, n_past=n_past, n_new=n_new, s_valid=s_valid, blk=256),
        out_shape=(jax.ShapeDtypeStruct((b, n_new, LANES), F32), cum_shape, cum_shape, cum_shape),
        grid=(b,),
        in_specs=in_specs,
        out_specs=(full(n_new), full(n), full(n), full(n)),
        compiler_params=_cparams(("parallel",)),
        name="fox_prep",
    )(*args)


def _fox_pack_kernel(*refs, q_side, with_v):
    if with_v:
        x_ref, hi_ref, mid_ref, lo_ref, v_ref, out_ref, vt_ref = refs
        vt_ref[0] = v_ref[0].T.astype(BF16)
    else:
        x_ref, hi_ref, mid_ref, lo_ref, out_ref = refs
    tm = x_ref.shape[1]
    lane = lax.broadcasted_iota(jnp.int32, (tm, HD_FOX), 1)
    hi, mid, lo = hi_ref[0], mid_ref[0], lo_ref[0]
    for h in range(H_FOX):
        c = FOX_GATE_LANE + h
        pieces = (hi[:, c:c + 1], mid[:, c:c + 1], lo[:, c:c + 1])
        first = 0 if q_side else 3
        bias = jnp.where(jnp.logical_and(lane >= 3 - first, lane < 6 - first), -1.0 if q_side else 1.0, 0.0)
        for j, p in enumerate(pieces):
            bias = jnp.where(lane == first + j, p, bias)
        out_ref[0, h] = jnp.concatenate(
            [x_ref[0, :, h * HD_FOX:(h + 1) * HD_FOX].astype(BF16), bias.astype(BF16)], axis=1)


def _fox_pack(x, hi, mid, lo, row_offset, q_side, v=None):
    b, n, _ = x.shape
    tm = next(c for c in (512, 384, 256, LANES) if n % c == 0 and row_offset % c == 0)
    off = row_offset // tm
    tile = lambda w: pl.BlockSpec((1, tm, w), lambda bi, i: (bi, i, 0))
    piece = pl.BlockSpec((1, tm, LANES), lambda bi, i: (bi, off + i, 0))
    out_shape = [jax.ShapeDtypeStruct((b, H_FOX, n, 2 * HD_FOX), BF16)]
    out_specs = [pl.BlockSpec((1, H_FOX, tm, 2 * HD_FOX), lambda bi, i: (bi, 0, i, 0))]
    in_specs = [tile(D_FOX), piece, piece, piece]
    args = [x, hi, mid, lo]
    if v is not None:
        in_specs.append(tile(D_FOX))
        args.append(v)
        out_shape.append(jax.ShapeDtypeStruct((b, D_FOX, n), BF16))
        out_specs.append(pl.BlockSpec((1, D_FOX, tm), lambda bi, i: (bi, 0, i)))
    return pl.pallas_call(
        functools.partial(_fox_pack_kernel, q_side=q_side, with_v=v is not None),
        out_shape=tuple(out_shape),
        grid=(b, n // tm),
        in_specs=in_specs,
        out_specs=tuple(out_specs),
        compiler_params=_cparams(("parallel", "parallel")),
        name="fox_pack_q" if q_side else "fox_pack_kv",
    )(*args)


def _fox_attn_kernel(qa_ref, ka_ref, vt_ref, out_ref, m_scr, l_scr, acc_scr, *, p0, tq, tk, nk):
    qi, kj = pl.program_id(1), pl.program_id(2)

    @pl.when(kj == 0)
    def _():
        m_scr[...] = jnp.full(m_scr.shape, NEG_BIG, F32)
        l_scr[...] = jnp.zeros(l_scr.shape, F32)
        acc_scr[...] = jnp.zeros(acc_scr.shape, F32)

    def step(masked):
        col = lax.broadcasted_iota(jnp.int32, (1, 2 * HD_FOX), 1)
        scale = jnp.where(col < HD_FOX, HD_FOX ** -0.5, 1.0).astype(BF16)
        if masked:
            kpos = kj * tk + lax.broadcasted_iota(jnp.int32, (tk, tq), 0)
            qpos = p0 + qi * tq + lax.broadcasted_iota(jnp.int32, (tk, tq), 1)
            mask = kpos <= qpos
        for h in range(H_FOX):
            s = _nt_dot(ka_ref[0, h], qa_ref[0, h] * scale)
            if masked:
                s = jnp.where(mask, s, NEG_BIG)
            m_prev = m_scr[h]
            m_new = jnp.maximum(m_prev, jnp.max(s, axis=0, keepdims=True))
            p = jnp.exp(s - m_new)
            alpha = jnp.exp(m_prev - m_new)
            l_scr[h] = alpha * l_scr[h] + jnp.sum(p, axis=0, keepdims=True)
            acc_scr[h] = alpha * acc_scr[h] + jnp.dot(vt_ref[0, h], p.astype(BF16), preferred_element_type=F32)
            m_scr[h] = m_new

    first_q = p0 + qi * tq
    last_q = first_q + tq - 1
    unmasked = (kj + 1) * tk - 1 <= first_q

    @pl.when(unmasked)
    def _():
        step(False)

    @pl.when(jnp.logical_and(jnp.logical_not(unmasked), kj * tk <= last_q))
    def _():
        step(True)

    @pl.when(kj == nk - 1)
    def _():
        o_t = jnp.concatenate([acc_scr[h] / l_scr[h] for h in range(H_FOX)], axis=0)
        out_ref[0] = o_t.T.astype(BF16)


def _fox_attn(qa, ka, vt, p0):
    b, _, sq, _ = qa.shape
    sk = ka.shape[2]
    tq = min(sq, 512)
    tk = 512 if sk % 512 == 0 else 384
    nq, nk = sq // tq, sk // tk
    last_tile = lambda qi: (p0 + (qi + 1) * tq - 1) // tk
    return pl.pallas_call(
        functools.partial(_fox_attn_kernel, p0=p0, tq=tq, tk=tk, nk=nk),
        out_shape=jax.ShapeDtypeStruct((b, sq, D_FOX), BF16),
        grid=(b, nq, nk),
        in_specs=[
            pl.BlockSpec((1, H_FOX, tq, 2 * HD_FOX), lambda bi, qi, kj: (bi, 0, qi, 0)),
            pl.BlockSpec((1, H_FOX, tk, 2 * HD_FOX), lambda bi, qi, kj: (bi, 0, jnp.minimum(kj, last_tile(qi)), 0)),
            pl.BlockSpec((1, H_FOX, HD_FOX, tk), lambda bi, qi, kj: (bi, 0, 0, jnp.minimum(kj, last_tile(qi)))),
        ],
        out_specs=pl.BlockSpec((1, tq, D_FOX), lambda bi, qi, kj: (bi, qi, 0)),
        scratch_shapes=[
            pltpu.VMEM((H_FOX, 1, tq), F32),
            pltpu.VMEM((H_FOX, 1, tq), F32),
            pltpu.VMEM((H_FOX, HD_FOX, tq), F32),
        ],
        compiler_params=_cparams(("parallel", "parallel", "arbitrary")),
        name="fox_attn",
    )(qa, ka, vt)


def _merge_kernel(x_ref, hm_ref, of_ref, cq_ref, gates_ref, mk_ref, mv_ref, wm_ref, wf_ref, wc_ref,
                  wo_ref, gffn_ref, xnew_ref, hn_ref):
    cq = cq_ref[0]
    heads = []
    for h in range(H_MEM):
        sl = slice(h * HD_MEM, (h + 1) * HD_MEM)
        s = _nt_dot(cq[:, sl], mk_ref[0, :, sl]) * (HD_MEM ** -0.5)
        p = jnp.exp(s - jnp.max(s, axis=1, keepdims=True))
        o = jnp.dot(p.astype(BF16), mv_ref[0, :, sl], preferred_element_type=F32)
        heads.append(o / jnp.sum(p, axis=1, keepdims=True))
    oc = jnp.concatenate(heads, axis=1).astype(BF16)
    a_m = jnp.dot(hm_ref[0], wm_ref[...], preferred_element_type=F32)
    a_f = jnp.dot(of_ref[0], wf_ref[...], preferred_element_type=F32)
    a_c = jnp.dot(oc, wc_ref[...], preferred_element_type=F32)
    g = gates_ref[0].astype(F32)
    merged = (_sigmoid(g[:, 0:D_MODEL]) * a_m + _sigmoid(g[:, D_MODEL:2 * D_MODEL]) * a_f
              + _sigmoid(g[:, 2 * D_MODEL:3 * D_MODEL]) * a_c)
    xn = x_ref[0] + jnp.dot(merged.astype(BF16), wo_ref[...], preferred_element_type=F32)
    xnew_ref[0] = xn
    hn_ref[0] = _rms(xn, gffn_ref[...]).astype(BF16)


def _merge(x, hm, of, cq, gates, mk, mv, wm, wf, wc, wo, gffn):
    b, s, _ = x.shape
    ts = min(s, 256)
    tile = lambda w: pl.BlockSpec((1, ts, w), lambda bi, si: (bi, si, 0))
    const = lambda shape: pl.BlockSpec(shape, lambda bi, si: (0, 0))
    mem = pl.BlockSpec((1, mk.shape[1], D_MEM), lambda bi, si: (bi, 0, 0))
    return pl.pallas_call(
        _merge_kernel,
        out_shape=(jax.ShapeDtypeStruct((b, s, D_MODEL), F32), jax.ShapeDtypeStruct((b, s, D_MODEL), BF16)),
        grid=(b, s // ts),
        in_specs=[tile(D_MODEL), tile(D_M), tile(D_FOX), tile(D_MEM), tile(N_BRANCH * D_MODEL), mem, mem,
                  const((D_M, D_MODEL)), const((D_FOX, D_MODEL)), const((D_MEM, D_MODEL)),
                  const((D_MODEL, D_MODEL)), const((1, D_MODEL))],
        out_specs=(tile(D_MODEL), tile(D_MODEL)),
        compiler_params=_cparams(("parallel", "parallel")),
        name="merge",
    )(x, hm, of, cq, gates, mk, mv, wm, wf, wc, wo, gffn)


def _extract_top16(s, key_io, val_scr, idx_scr, h, want_rank):
    rank = jnp.full(s.shape, float(PEER_TOPK), F32) if want_rank else None
    for r in range(PEER_TOPK):
        m = jnp.max(s, axis=0, keepdims=True)
        first = jnp.min(jnp.where(s == m, key_io, float(N_KEYS)), axis=0, keepdims=True)
        hit = key_io == first
        s = jnp.where(hit, -jnp.inf, s)
        val_scr[r, pl.ds(h, 1), :] = m
        if want_rank:
            rank = jnp.where(hit, float(r), rank)
        else:
            idx_scr[r, pl.ds(h, 1), :] = first
    return rank


def _router_kernel(hn_ref, wqt_ref, keys_ref, a0_ref, cnt_ref, r1_ref, b1_ref,
                   qr_scr, e1_scr, top0_scr, top1_scr, idx0_scr, cnt_scr, invz_scr):
    tp = hn_ref.shape[0]
    qr_scr[...] = _nt_dot(wqt_ref[...], hn_ref[...])
    key_io = lax.broadcasted_iota(jnp.int32, (N_KEYS, tp), 0).astype(F32)

    def scores_and_top16(h, carry):
        base = pl.multiple_of(h * 2 * N_KEYS, 2 * N_KEYS)
        s0 = jnp.dot(keys_ref[2 * h], qr_scr[pl.ds(base, N_KEYS), :].astype(BF16),
                     preferred_element_type=F32)
        s1 = jnp.dot(keys_ref[2 * h + 1], qr_scr[pl.ds(base + N_KEYS, N_KEYS), :].astype(BF16),
                     preferred_element_type=F32)
        _extract_top16(s0, key_io, top0_scr, idx0_scr, h, False)
        rank1 = _extract_top16(s1, key_io, top1_scr, None, h, True)
        a0_ref[h] = jnp.exp(s0 - top0_scr[0, pl.ds(h, 1), :])
        e1_scr[h] = jnp.exp(s1 - top1_scr[0, pl.ds(h, 1), :])
        for g in range(N_KEYS // 16):
            r1_ref[h, g] = rank1[g * 16:(g + 1) * 16, :].astype(BF16)
        return carry

    lax.fori_loop(0, PEER_HEADS, scores_and_top16, 0)

    top0 = [top0_scr[a] for a in range(PEER_TOPK)]
    top1 = [top1_scr[b] for b in range(PEER_TOPK)]
    cnt = [jnp.zeros((PEER_HEADS, tp), F32) for _ in range(PEER_TOPK)]
    for _ in range(PEER_TOPK):
        front = []
        for a in range(PEER_TOPK):
            nxt = jnp.full((PEER_HEADS, tp), -jnp.inf, F32)
            for bb in range(PEER_TOPK // (a + 1)):
                nxt = jnp.where(cnt[a] == float(bb), top1[bb], nxt)
            front.append(top0[a] + nxt)
        mx = functools.reduce(jnp.maximum, front)
        first = functools.reduce(jnp.minimum,
                                 [jnp.where(front[a] == mx, float(a), float(PEER_TOPK)) for a in range(PEER_TOPK)])
        cnt = [cnt[a] + jnp.where(first == float(a), 1.0, 0.0) for a in range(PEER_TOPK)]
    z = jnp.zeros((PEER_HEADS, tp), F32)
    for a in range(PEER_TOPK):
        za = jnp.zeros((PEER_HEADS, tp), F32)
        for bb in range(PEER_TOPK // (a + 1)):
            za = za + jnp.where(cnt[a] > float(bb), jnp.exp(top1[bb] - top1[0]), 0.0)
        z = z + jnp.exp(top0[a] - top0[0]) * za
        cnt_scr[a] = cnt[a]
    invz_scr[...] = 1.0 / z

    def counts_and_gates(h, carry):
        cnt_i = jnp.zeros((N_KEYS, tp), F32)
        for a in range(PEER_TOPK):
            cnt_i = jnp.where(key_io == idx0_scr[a, pl.ds(h, 1), :], cnt_scr[a, pl.ds(h, 1), :], cnt_i)
        cnt_ref[h] = cnt_i
        b1 = e1_scr[h] * invz_scr[pl.ds(h, 1), :]
        for g in range(N_KEYS // 16):
            b1_ref[h, g] = b1[g * 16:(g + 1) * 16, :].astype(BF16)
        return carry

    lax.fori_loop(0, PEER_HEADS, counts_and_gates, 0)


def _router(hn2d, wqt, keys):
    t = hn2d.shape[0]
    tp = LANES
    shape = jax.ShapeDtypeStruct((PEER_HEADS, N_KEYS, t), F32)
    spec = pl.BlockSpec((PEER_HEADS, N_KEYS, tp), lambda i: (0, 0, i))
    slab_shape = jax.ShapeDtypeStruct((PEER_HEADS, N_KEYS // 16, 16, t), BF16)
    slab_spec = pl.BlockSpec((PEER_HEADS, N_KEYS // 16, 16, tp), lambda i: (0, 0, 0, i))
    return pl.pallas_call(
        _router_kernel,
        out_shape=(shape, shape, slab_shape, slab_shape),
        grid=(t // tp,),
        in_specs=[
            pl.BlockSpec((tp, D_MODEL), lambda i: (i, 0)),
            pl.BlockSpec(wqt.shape, lambda i: (0, 0)),
            pl.BlockSpec(keys.shape, lambda i: (0, 0, 0)),
        ],
        out_specs=(spec, spec, slab_spec, slab_spec),
        scratch_shapes=[
            pltpu.VMEM((wqt.shape[0], tp), F32),
            pltpu.VMEM((PEER_HEADS, N_KEYS, tp), F32),
        ] + [pltpu.VMEM((PEER_TOPK, PEER_HEADS, tp), F32) for _ in range(4)] + [pltpu.VMEM((PEER_HEADS, tp), F32)],
        compiler_params=_cparams(("parallel",)),
        name="peer_router",
    )(hn2d, wqt, keys)


def _gelu_tanh(x):
    return 0.5 * x * (1.0 + jnp.tanh(0.7978845608028654 * (x + 0.044715 * (x * x * x))))


def _experts_kernel(hn_ref, u_ref, vt_ref, a0_ref, cnt_ref, r1_ref, b1_ref, x_ref, gfin_ref, out_ref,
                    acc_scr, pre0_scr, pre1_scr, w0_scr, w1_scr, *, rows_per_tile, n_tiles, final_norm):
    s = pl.program_id(1)
    tp = hn_ref.shape[0]
    te = u_ref.shape[0] // 2
    slab = 16
    slabs_per_row = N_KEYS // slab
    d_half = acc_scr.shape[0] // 2
    zero = jnp.zeros((), BF16)

    @pl.when(s == 0)
    def _():
        acc_scr[...] = jnp.zeros(acc_scr.shape, F32)
        pre1_scr[...] = jnp.zeros(pre1_scr.shape, F32)
        w0_scr[...] = jnp.zeros(w0_scr.shape, BF16)

    def pipeline_step(parity, pre_w, pre_r, w_w, w_r):
        e = 2 * s + parity
        cols = slice(parity * te, (parity + 1) * te)

        def stage_c(half):
            rows = slice(half * d_half, (half + 1) * d_half)
            acc_scr[rows, :] += jnp.dot(vt_ref[rows, cols], w_r[...], preferred_element_type=F32)

        def stage_a(half):
            rows = slice(half * (te // 2), (half + 1) * (te // 2))
            pre_w[rows, :] = _nt_dot(u_ref[parity * te + half * (te // 2):parity * te + (half + 1) * (te // 2), :],
                                     hn_ref[...])

        mxu_work = (lambda: stage_c(0), lambda: stage_a(0), lambda: stage_c(1), lambda: stage_a(1))

        valid = jnp.logical_and(e >= 1, e <= n_tiles)
        tile_b = jnp.clip(e - 1, 0, n_tiles - 1)
        for ib in range(rows_per_tile):
            mxu_work[ib]()
            i = tile_b * rows_per_tile + ib
            for lanes in (slice(0, tp // 2), slice(tp // 2, tp)):
                gates = [jnp.zeros((slab, tp // 2), BF16) for _ in range(slabs_per_row)]
                for h in range(PEER_HEADS):
                    a_b = jnp.broadcast_to(a0_ref[h, pl.ds(i, 1), lanes], (slab, tp // 2)).astype(BF16)
                    c_b = jnp.broadcast_to(cnt_ref[h, pl.ds(i, 1), lanes], (slab, tp // 2)).astype(BF16)
                    for g in range(slabs_per_row):
                        gates[g] = gates[g] + a_b * jnp.where(r1_ref[h, g, :, lanes] < c_b, b1_ref[h, g, :, lanes], zero)
                for g in range(slabs_per_row):
                    rows = slice(ib * N_KEYS + g * slab, ib * N_KEYS + (g + 1) * slab)
                    w = gates[g] * _gelu_tanh(pre_r[rows, lanes]).astype(BF16)
                    w_w[rows, lanes] = jnp.where(valid, w, zero)

    pipeline_step(0, pre0_scr, pre1_scr, w1_scr, w0_scr)
    pipeline_step(1, pre1_scr, pre0_scr, w0_scr, w1_scr)

    @pl.when(s == n_tiles // 2)
    def _():
        xo = x_ref[...] + acc_scr[...].T
        if final_norm:
            xo = _rms(xo, gfin_ref[...])
        out_ref[...] = xo


def _experts(hn2d, u_bf, vt_bf, a0, cnt, r1, b1, x2d, gfin, final_norm):
    t = hn2d.shape[0]
    tp = 512
    rows_per_tile = 4
    te = rows_per_tile * N_KEYS
    n_tiles = u_bf.shape[0] // te
    rspec = pl.BlockSpec((PEER_HEADS, N_KEYS, tp), lambda ti, e: (0, 0, ti))
    sspec = pl.BlockSpec((PEER_HEADS, N_KEYS // 16, 16, tp), lambda ti, e: (0, 0, 0, ti))
    return pl.pallas_call(
        functools.partial(_experts_kernel, rows_per_tile=rows_per_tile, n_tiles=n_tiles, final_norm=final_norm),
        out_shape=jax.ShapeDtypeStruct((t, D_MODEL), F32),
        grid=(t // tp, n_tiles // 2 + 1),
        in_specs=[
            pl.BlockSpec((tp, D_MODEL), lambda ti, s: (ti, 0)),
            pl.BlockSpec((2 * te, D_MODEL), lambda ti, s: (jnp.minimum(s, n_tiles // 2 - 1), 0)),
            pl.BlockSpec((D_MODEL, 2 * te), lambda ti, s: (0, jnp.maximum(s - 1, 0))),
            rspec, rspec, sspec, sspec,
            pl.BlockSpec((tp, D_MODEL), lambda ti, s: (ti, 0)),
            pl.BlockSpec((1, D_MODEL), lambda ti, s: (0, 0)),
        ],
        out_specs=pl.BlockSpec((tp, D_MODEL), lambda ti, s: (ti, 0)),
        scratch_shapes=[
            pltpu.VMEM((D_MODEL, tp), F32),
            pltpu.VMEM((te, tp), F32), pltpu.VMEM((te, tp), F32),
            pltpu.VMEM((te, tp), BF16), pltpu.VMEM((te, tp), BF16),
        ],
        compiler_params=_cparams(("parallel", "arbitrary"), EXPERT_FLAGS),
        name="peer_experts",
    )(hn2d, u_bf, vt_bf, a0, cnt, r1, b1, x2d, gfin)


def _prep_layer_weights(p):
    w_in = p['w_in']
    o_mi = 4 * D_M
    o_fq = o_mi + 2 * H_M
    o_ff = o_fq + 3 * D_FOX
    o_cq = o_ff + H_FOX
    w_big = jnp.concatenate([w_in[:, :o_mi], w_in[:, o_fq:o_ff], w_in[:, o_cq:]], axis=1).astype(BF16)
    w_small = jnp.concatenate([w_in[:, o_mi:o_fq], w_in[:, o_ff:o_cq]], axis=1)
    w_small = jnp.pad(w_small, ((0, 0), (0, LANES - w_small.shape[1]))).astype(BF16)
    bias = jnp.concatenate([p['b_m_i'], p['b_m_f'], p['b_fox_f']]).astype(F32)
    return dict(
        norm_mix=p['norm_mix'].reshape(1, D_MODEL), w_big=w_big, w_small=w_small,
        conv_w=p['conv_w'], conv_b=p['conv_b'].reshape(1, 2 * D_M),
        bias_row=jnp.pad(bias, (0, LANES - 16)).reshape(1, LANES), bias_col=bias.reshape(16, 1),
        norm_m_head=p['norm_m_head'].reshape(1, D_M),
        w_up_m=p['w_up_m'].astype(BF16), w_up_f=p['w_up_f'].astype(BF16), w_up_c=p['w_up_c'].astype(BF16),
        w_out=p['w_out'].astype(BF16), norm_ffn=p['norm_ffn'].reshape(1, D_MODEL),
        wqt=p['peer_wq'].T.astype(BF16),
        keys=p['peer_keys'].reshape(2 * PEER_HEADS, N_KEYS, N_KEYS).astype(BF16),
        u=p['peer_u'].astype(BF16), vt=p['peer_v'].T.astype(BF16),
    )


def _layer(x, mem_k, mem_v, conv_prev, c0, n0, m0, fk_past, fv_past, flf_past, w, gfin, final_norm):
    b, s, _ = x.shape
    t = b * s
    p0 = 0 if fk_past is None else fk_past.shape[1]
    qk, v_m, o_m, fq, fk, fv, cq, gates, small = _norm_proj(x.reshape(t, D_MODEL), w['norm_mix'], w['w_big'], w['w_small'])

    small3 = small.reshape(b, s, LANES)
    smallt = small3[:, :, :16].transpose(0, 2, 1)
    m0b = jnp.broadcast_to(m0[:, :, None, None], (b, H_M, 1, LANES))
    hm, c1, n1, m1b, conv_new = _mlstm(
        qk.reshape(b, s, 2 * D_M), v_m.reshape(b, s, D_M), o_m.reshape(b, s, D_M), small3, smallt,
        w['conv_w'], w['conv_b'], w['bias_row'], w['bias_col'], w['norm_m_head'],
        c0, n0[:, :, None, :], m0b, conv_prev)
    n1, m1 = n1[:, :, 0, :], m1b[:, :, 0, 0]

    k_f = fk.reshape(b, s, H_FOX, HD_FOX)
    v_f = fv.reshape(b, s, H_FOX, HD_FOX)
    s_pad = -(-s // LANES) * LANES
    pad_rows = lambda a: jnp.pad(a, ((0, 0), (0, s_pad - s), (0, 0)))
    past = None
    if p0:
        past = jnp.pad(flf_past.astype(F32), ((0, 0), (0, 0), (FOX_GATE_LANE, LANES - FOX_GATE_LANE - H_FOX)))
    lf_all, hi, mid, lo = _fox_prep(past, pad_rows(small3), w['bias_row'], s)
    lf_f = lf_all[:, :s, FOX_GATE_LANE:FOX_GATE_LANE + H_FOX]
    sk = p0 + s_pad
    k_all, v_all = pad_rows(fk.reshape(b, s, D_FOX)), pad_rows(fv.reshape(b, s, D_FOX))
    if p0:
        k_all = jnp.concatenate([fk_past.reshape(b, p0, D_FOX).astype(F32), k_all], axis=1)
        v_all = jnp.concatenate([fv_past.reshape(b, p0, D_FOX).astype(F32), v_all], axis=1)
    qa = _fox_pack(pad_rows(fq.reshape(b, s, D_FOX)), hi, mid, lo, p0, True)[0]
    ka, vt_all = _fox_pack(k_all, hi, mid, lo, 0, False, v=v_all)
    o_f = _fox_attn(qa, ka, vt_all.reshape(b, H_FOX, HD_FOX, sk), p0)[:, :s]

    x_new, hn = _merge(x, hm, o_f, cq.reshape(b, s, D_MEM), gates.reshape(b, s, N_BRANCH * D_MODEL),
                       mem_k.reshape(b, -1, D_MEM).astype(BF16), mem_v.reshape(b, -1, D_MEM).astype(BF16),
                       w['w_up_m'], w['w_up_f'], w['w_up_c'], w['w_out'], w['norm_ffn'])

    hn2d = hn.reshape(t, D_MODEL)
    a0, cnt, r1, b1 = _router(hn2d, w['wqt'], w['keys'])
    x_out = _experts(hn2d, w['u'], w['vt'], a0, cnt, r1, b1, x_new.reshape(t, D_MODEL), gfin, final_norm)
    return x_out.reshape(b, s, D_MODEL), conv_new, c1, n1, m1, k_f, v_f, lf_f


def kernel(x_prompt, x_sample, mem_prompt, cache_fox_k, cache_fox_v, cache_fox_lf, state_mlstm_c, state_mlstm_n, state_mlstm_m, state_conv, cache_mem_k, cache_mem_v, norm_mix, w_in, conv_w, conv_b, b_m_i, b_m_f, norm_m_head, b_fox_f, norm_mem, w_mem_kv, w_up_m, w_up_f, w_up_c, w_out, norm_ffn, peer_wq, peer_keys, peer_u, peer_v, norm_final):
    depth = w_in.shape[0]
    names = dict(norm_mix=norm_mix, w_in=w_in, conv_w=conv_w, conv_b=conv_b, b_m_i=b_m_i, b_m_f=b_m_f,
                 norm_m_head=norm_m_head, b_fox_f=b_fox_f, w_up_m=w_up_m, w_up_f=w_up_f, w_up_c=w_up_c,
                 w_out=w_out, norm_ffn=norm_ffn, peer_wq=peer_wq, peer_keys=peer_keys, peer_u=peer_u,
                 peer_v=peer_v)
    weights = [_prep_layer_weights({k: a[l] for k, a in names.items()}) for l in range(depth)]
    gfin = norm_final.reshape(1, D_MODEL)
    bp, n_mem = mem_prompt.shape[0], mem_prompt.shape[1]

    xp = x_prompt
    new_p = [[] for _ in range(9)]
    for l in range(depth):
        kv = _norm_matmul(mem_prompt.reshape(bp * n_mem, D_MODEL), norm_mem[l].reshape(1, D_MODEL),
                          w_mem_kv[l].astype(BF16))
        mk = kv[:, :D_MEM].reshape(bp, n_mem, H_MEM, HD_MEM)
        mv = kv[:, D_MEM:].reshape(bp, n_mem, H_MEM, HD_MEM)
        xp, conv1, c1, n1, m1, kf, vf, lff = _layer(
            xp, mk, mv, jnp.zeros((bp, CONV_W - 1, 2 * D_M), F32),
            jnp.zeros((bp, H_M, HD_M, HD_M), F32), jnp.zeros((bp, H_M, HD_M), F32), jnp.zeros((bp, H_M), F32),
            None, None, None, weights[l], gfin, l == depth - 1)
        for lst, a in zip(new_p, (kf, vf, lff, c1, n1, m1, conv1, mk, mv)):
            lst.append(a)
    outs_p = [jnp.stack(a) for a in new_p]

    xs = x_sample
    new_s = [[] for _ in range(7)]
    for l in range(depth):
        xs, conv1, c1, n1, m1, kf, vf, lff = _layer(
            xs, cache_mem_k[l], cache_mem_v[l], state_conv[l], state_mlstm_c[l], state_mlstm_n[l],
            state_mlstm_m[l], cache_fox_k[l], cache_fox_v[l], cache_fox_lf[l], weights[l], gfin, l == depth - 1)
        for lst, a in zip(new_s, (kf, vf, lff, c1, n1, m1, conv1)):
            lst.append(a)
    outs_s = [jnp.stack(a) for a in new_s]

    return (xp, xs, *outs_p, *outs_s)
```

```python
import functools

import jax
import jax.numpy as jnp
from jax import lax
from jax.experimental import pallas as pl
from jax.experimental.pallas import tpu as pltpu

F32 = jnp.float32
BF16 = jnp.bfloat16
EPS = 1e-6
NEG_BIG = -1e30

D_MODEL = 1024
H_M, HD_M = 4, 128
D_M = H_M * HD_M
CONV_W = 4
H_FOX, HD_FOX = 8, 64
D_FOX = H_FOX * HD_FOX
H_MEM, HD_MEM = 4, 128
D_MEM = H_MEM * HD_MEM
N_BRANCH = 3
PEER_HEADS = 8
N_KEYS = 128
PEER_TOPK = 16
LANES = 128
FOX_GATE_LANE = 2 * H_M
MLSTM_CHUNK = 256
PROJ_TILE = 512
VMEM_LIMIT = 56 * 1024 * 1024
EXPERT_FLAGS = None


def _cparams(sem, flags=None):
    return pltpu.CompilerParams(dimension_semantics=sem, vmem_limit_bytes=VMEM_LIMIT, flags=flags)


def _nt_dot(a, b):
    return lax.dot_general(a, b, (((1,), (1,)), ((), ())), preferred_element_type=F32)


def _rms(x, g):
    return x * lax.rsqrt(jnp.mean(x * x, axis=-1, keepdims=True) + EPS) * g


def _log_sigmoid(x):
    return jnp.minimum(x, 0.0) - jnp.log(1.0 + jnp.exp(-jnp.abs(x)))


def _sigmoid(x):
    return 1.0 / (1.0 + jnp.exp(-x))


def _norm_proj_kernel(x_ref, g_ref, w_ref, ws_ref, qk_ref, v_ref, o_ref, fq_ref, fk_ref, fv_ref,
                      cq_ref, gates_ref, small_ref, h_scr):
    j = pl.program_id(1)

    @pl.when(j == 0)
    def _():
        h = _rms(x_ref[...], g_ref[...]).astype(BF16)
        h_scr[...] = h
        small_ref[...] = jnp.dot(h, ws_ref[...], preferred_element_type=F32)

    def tile():
        return jnp.dot(h_scr[...], w_ref[...], preferred_element_type=F32)

    @pl.when(j == 0)
    def _():
        qk_ref[:, 0:PROJ_TILE] = tile()

    @pl.when(j == 1)
    def _():
        qk_ref[:, PROJ_TILE:2 * PROJ_TILE] = tile()

    @pl.when(j == 2)
    def _():
        v_ref[...] = tile().astype(BF16)

    @pl.when(j == 3)
    def _():
        o_ref[...] = tile()

    @pl.when(j == 4)
    def _():
        fq_ref[...] = tile().astype(BF16)

    @pl.when(j == 5)
    def _():
        fk_ref[...] = tile()

    @pl.when(j == 6)
    def _():
        fv_ref[...] = tile()

    @pl.when(j == 7)
    def _():
        cq_ref[...] = tile().astype(BF16)

    @pl.when(j >= 8)
    def _():
        gates_ref[...] = tile().astype(BF16)


def _norm_proj(x2d, g, w_big, w_small):
    t = x2d.shape[0]
    tm = 512
    n_tiles = w_big.shape[1] // PROJ_TILE
    n_gate_tiles = n_tiles - 8
    row = lambda i, j: (i, 0)
    out_shape = (
        jax.ShapeDtypeStruct((t, 2 * D_M), F32),
        jax.ShapeDtypeStruct((t, D_M), BF16),
        jax.ShapeDtypeStruct((t, D_M), F32),
        jax.ShapeDtypeStruct((t, D_FOX), BF16),
        jax.ShapeDtypeStruct((t, D_FOX), F32),
        jax.ShapeDtypeStruct((t, D_FOX), F32),
        jax.ShapeDtypeStruct((t, D_MEM), BF16),
        jax.ShapeDtypeStruct((t, N_BRANCH * D_MODEL), BF16),
        jax.ShapeDtypeStruct((t, LANES), F32),
    )
    out_specs = (
        pl.BlockSpec((tm, 2 * D_M), row),
        pl.BlockSpec((tm, D_M), row),
        pl.BlockSpec((tm, D_M), row),
        pl.BlockSpec((tm, D_FOX), row),
        pl.BlockSpec((tm, D_FOX), row),
        pl.BlockSpec((tm, D_FOX), row),
        pl.BlockSpec((tm, D_MEM), row),
        pl.BlockSpec((tm, PROJ_TILE), lambda i, j: (i, jnp.clip(j - 8, 0, n_gate_tiles - 1))),
        pl.BlockSpec((tm, LANES), row),
    )
    return pl.pallas_call(
        _norm_proj_kernel,
        out_shape=out_shape,
        grid=(t // tm, n_tiles),
        in_specs=[
            pl.BlockSpec((tm, D_MODEL), row),
            pl.BlockSpec((1, D_MODEL), lambda i, j: (0, 0)),
            pl.BlockSpec((D_MODEL, PROJ_TILE), lambda i, j: (0, j)),
            pl.BlockSpec((D_MODEL, LANES), lambda i, j: (0, 0)),
        ],
        out_specs=out_specs,
        scratch_shapes=[pltpu.VMEM((tm, D_MODEL), BF16)],
        compiler_params=_cparams(("parallel", "arbitrary")),
        name="norm_proj",
    )(x2d, g, w_big, w_small)


def _norm_matmul_kernel(x_ref, g_ref, w_ref, o_ref):
    h = _rms(x_ref[...], g_ref[...]).astype(BF16)
    o_ref[...] = jnp.dot(h, w_ref[...], preferred_element_type=F32)


def _norm_matmul(x2d, g, w):
    t, n = x2d.shape[0], w.shape[1]
    tm, tn = 512, 512
    return pl.pallas_call(
        _norm_matmul_kernel,
        out_shape=jax.ShapeDtypeStruct((t, n), F32),
        grid=(t // tm, n // tn),
        in_specs=[
            pl.BlockSpec((tm, D_MODEL), lambda i, j: (i, 0)),
            pl.BlockSpec((1, D_MODEL), lambda i, j: (0, 0)),
            pl.BlockSpec((D_MODEL, tn), lambda i, j: (0, j)),
        ],
        out_specs=pl.BlockSpec((tm, tn), lambda i, j: (i, j)),
        compiler_params=_cparams(("parallel", "parallel")),
        name="norm_matmul",
    )(x2d, g, w)


def _mlstm_kernel(qk_ref, v_ref, o_ref, small_ref, smallt_ref, convw_ref, convb_ref, brow_ref,
                  bcol_ref, ghead_ref, c0_ref, n0_ref, m0_ref, conv0_ref,
                  hm_ref, c_out_ref, n_out_ref, m_out_ref, conv_out_ref,
                  c_scr, n_scr, m_scr, xp_scr, *, chunk):
    L = chunk
    si = pl.program_id(1)

    @pl.when(si == 0)
    def _():
        c_scr[...] = c0_ref[0]
        n_scr[...] = n0_ref[0]
        m_scr[...] = m0_ref[0]
        xp_scr[5:8, :] = conv0_ref[0]

    xp_scr[8:8 + L, :] = qk_ref[0]
    y = convb_ref[...] + convw_ref[0:1, :] * xp_scr[5:5 + L, :]
    for j in range(1, CONV_W):
        y = y + convw_ref[j:j + 1, :] * xp_scr[5 + j:5 + j + L, :]
    y = y * _sigmoid(y)
    tail = xp_scr[5 + L:8 + L, :]
    xp_scr[5:8, :] = tail
    conv_out_ref[0] = tail

    t_io = lax.broadcasted_iota(jnp.int32, (L, L), 0)
    s_io = lax.broadcasted_iota(jnp.int32, (L, L), 1)
    causal = s_io <= t_io
    ones_col = (lax.broadcasted_iota(jnp.int32, (L, HD_M), 1) == 0).astype(BF16)

    small = small_ref[0]
    smallt = smallt_ref[0]
    brow = brow_ref[...]
    bcol = bcol_ref[...]
    outs = []
    for h in range(H_M):
        q = y[:, h * HD_M:(h + 1) * HD_M]
        k = y[:, D_M + h * HD_M:D_M + (h + 1) * HD_M] * (HD_M ** -0.5)
        v_aug = jnp.concatenate([v_ref[0, :, h * HD_M:(h + 1) * HD_M], ones_col], axis=1)
        ig_col = small[:, h:h + 1] + brow[:, h:h + 1]
        ig_row = smallt[h:h + 1, :] + bcol[h:h + 1, :]
        lf_col = _log_sigmoid(small[:, H_M + h:H_M + h + 1] + brow[:, H_M + h:H_M + h + 1])
        lf_row = _log_sigmoid(smallt[H_M + h:H_M + h + 1, :] + bcol[H_M + h:H_M + h + 1, :])
        bcum_col = jnp.sum(jnp.where(causal, lf_row, 0.0), axis=1, keepdims=True)
        bcum_row = jnp.sum(jnp.where(t_io <= s_io, lf_col, 0.0), axis=0, keepdims=True)
        m_prev = m_scr[h][:, 0:1]
        dmat = jnp.where(causal, bcum_col - bcum_row + ig_row, -jnp.inf)
        inter = bcum_col + m_prev
        m_t = jnp.maximum(inter, jnp.max(dmat, axis=1, keepdims=True))
        w_intra = jnp.exp(dmat - m_t)
        w_state = jnp.exp(inter - m_t)
        qb = q.astype(BF16)
        sw = _nt_dot(qb, k.astype(BF16)) * w_intra
        c_prev = c_scr[h]
        n_prev = n_scr[h]
        intra = jnp.dot(sw.astype(BF16), v_aug, preferred_element_type=F32)
        num = intra[:, 0:HD_M] + w_state * jnp.dot(qb, c_prev.astype(BF16), preferred_element_type=F32)
        den = intra[:, HD_M:HD_M + 1] + w_state * jnp.sum(q * n_prev, axis=1, keepdims=True)
        hh = num / jnp.maximum(jnp.abs(den), jnp.exp(-m_t))
        hh = _rms(hh, ghead_ref[:, h * HD_M:(h + 1) * HD_M])
        outs.append(hh * _sigmoid(o_ref[0, :, h * HD_M:(h + 1) * HD_M]))
        bl = bcum_row[:, L - 1:L]
        m_new = jnp.maximum(bl + m_prev, jnp.max(bl - bcum_row + ig_row, axis=1, keepdims=True))
        decay = jnp.exp(bl + m_prev - m_new)
        ws_col = jnp.exp(bl - bcum_col + ig_col - m_new)
        kw = k * ws_col
        c_scr[h] = decay * c_prev + jnp.dot(kw.T.astype(BF16), v_ref[0, :, h * HD_M:(h + 1) * HD_M],
                                            preferred_element_type=F32)
        n_scr[h] = decay * n_prev + jnp.sum(kw, axis=0, keepdims=True)
        m_scr[h] = jnp.broadcast_to(m_new, (1, LANES))
    hm_ref[0] = jnp.concatenate(outs, axis=1).astype(BF16)
    c_out_ref[0] = c_scr[...]
    n_out_ref[0] = n_scr[...]
    m_out_ref[0] = m_scr[...]


def _mlstm(qk, v, o, small, smallt, convw, convb, brow, bcol, ghead, c0, n0, m0b, conv0):
    b, s, _ = qk.shape
    chunk = min(s, MLSTM_CHUNK)
    tile = lambda w: pl.BlockSpec((1, chunk, w), lambda bi, si: (bi, si, 0))
    per_b = lambda shape: pl.BlockSpec((1,) + shape, lambda bi, si: (bi,) + (0,) * len(shape))
    const = lambda shape: pl.BlockSpec(shape, lambda bi, si: (0, 0))
    state_shapes = ((H_M, HD_M, HD_M), (H_M, 1, HD_M), (H_M, 1, LANES), (CONV_W - 1, 2 * D_M))
    return pl.pallas_call(
        functools.partial(_mlstm_kernel, chunk=chunk),
        out_shape=(jax.ShapeDtypeStruct((b, s, D_M), BF16),)
        + tuple(jax.ShapeDtypeStruct((b,) + sh, F32) for sh in state_shapes),
        grid=(b, s // chunk),
        in_specs=[
            tile(2 * D_M), tile(D_M), tile(D_M), tile(LANES),
            pl.BlockSpec((1, 16, chunk), lambda bi, si: (bi, 0, si)),
            const((CONV_W, 2 * D_M)), const((1, 2 * D_M)), const((1, LANES)), const((16, 1)), const((1, D_M)),
        ] + [per_b(sh) for sh in state_shapes],
        out_specs=(tile(D_M),) + tuple(per_b(sh) for sh in state_shapes),
        scratch_shapes=[
            pltpu.VMEM((H_M, HD_M, HD_M), F32),
            pltpu.VMEM((H_M, 1, HD_M), F32),
            pltpu.VMEM((H_M, 1, LANES), F32),
            pltpu.VMEM((chunk + 8, 2 * D_M), F32),
        ],
        compiler_params=_cparams(("parallel", "arbitrary")),
        name="mlstm",
    )(qk, v, o, small, smallt, convw, convb, brow, bcol, ghead, c0, n0, m0b, conv0)


def _split3(x):
    hi = x.astype(BF16).astype(F32)
    r = x - hi
    mid = r.astype(BF16).astype(F32)
    lo = (r - mid).astype(BF16).astype(F32)
    return hi, mid, lo


def _fox_prep_kernel(*refs, n_past, n_new, s_valid, blk):
    if n_past:
        past_ref, pre_ref, bias_ref, lf_ref, hi_ref, mid_ref, lo_ref = refs
    else:
        pre_ref, bias_ref, lf_ref, hi_ref, mid_ref, lo_ref = refs
    tri = (lax.broadcasted_iota(jnp.int32, (blk, blk), 1) <= lax.broadcasted_iota(jnp.int32, (blk, blk), 0)).astype(BF16)
    carry = jnp.zeros((1, LANES), F32)
    start = 0
    while start < n_past + n_new:
        if start < n_past:
            r = min(blk, n_past - start)
            lf = past_ref[0, start:start + r, :]
        else:
            r = min(blk, n_past + n_new - start)
            ns = start - n_past
            row = ns + lax.broadcasted_iota(jnp.int32, (r, LANES), 0)
            lf = jnp.where(row < s_valid, _log_sigmoid(pre_ref[0, ns:ns + r, :] + bias_ref[...]), 0.0)
            lf_ref[0, ns:ns + r, :] = lf
        cum = carry
        for piece in _split3(lf):
            cum = cum + jnp.dot(tri[0:r, 0:r], piece.astype(BF16), preferred_element_type=F32)
        carry = cum[r - 1:r, :]
        hi, mid, lo = _split3(cum)
        hi_ref[0, start:start + r, :] = hi
        mid_ref[0, start:start + r, :] = mid
        lo_ref[0, start:start + r, :] = lo
        start += r


def _fox_prep(past, pre, bias_row, s_valid):
    b, n_new = pre.shape[:2]
    n_past = 0 if past is None else past.shape[1]
    n = n_past + n_new
    full = lambda rows: pl.BlockSpec((1, rows, LANES), lambda bi: (bi, 0, 0))
    in_specs = ([full(n_past)] if n_past else []) + [full(n_new), pl.BlockSpec((1, LANES), lambda bi: (0, 0))]
    args = ([past] if n_past else []) + [pre, bias_row]
    cum_shape = jax.ShapeDtypeStruct((b, n, LANES), F32)
    return pl.pallas_call(
        functools.partial(_fox_prep_kernel, n_past=n_past, n_new=n_new, s_valid=s_valid, blk=256),
        out_shape=(jax.ShapeDtypeStruct((b, n_new, LANES), F32), cum_shape, cum_shape, cum_shape),
        grid=(b,),
        in_specs=in_specs,
        out_specs=(full(n_new), full(n), full(n), full(n)),
        compiler_params=_cparams(("parallel",)),
        name="fox_prep",
    )(*args)


def _fox_pack_kernel(*refs, q_side, with_v):
    if with_v:
        x_ref, hi_ref, mid_ref, lo_ref, v_ref, out_ref, vt_ref = refs
        vt_ref[0] = v_ref[0].T.astype(BF16)
    else:
        x_ref, hi_ref, mid_ref, lo_ref, out_ref = refs
    tm = x_ref.shape[1]
    lane = lax.broadcasted_iota(jnp.int32, (tm, HD_FOX), 1)
    hi, mid, lo = hi_ref[0], mid_ref[0], lo_ref[0]
    for h in range(H_FOX):
        c = FOX_GATE_LANE + h
        pieces = (hi[:, c:c + 1], mid[:, c:c + 1], lo[:, c:c + 1])
        first = 0 if q_side else 3
        bias = jnp.where(jnp.logical_and(lane >= 3 - first, lane < 6 - first), -1.0 if q_side else 1.0, 0.0)
        for j, p in enumerate(pieces):
            bias = jnp.where(lane == first + j, p, bias)
        out_ref[0, h] = jnp.concatenate(
            [x_ref[0, :, h * HD_FOX:(h + 1) * HD_FOX].astype(BF16), bias.astype(BF16)], axis=1)


def _fox_pack(x, hi, mid, lo, row_offset, q_side, v=None):
    b, n, _ = x.shape
    tm = next(c for c in (512, 384, 256, LANES) if n % c == 0 and row_offset % c == 0)
    off = row_offset // tm
    tile = lambda w: pl.BlockSpec((1, tm, w), lambda bi, i: (bi, i, 0))
    piece = pl.BlockSpec((1, tm, LANES), lambda bi, i: (bi, off + i, 0))
    out_shape = [jax.ShapeDtypeStruct((b, H_FOX, n, 2 * HD_FOX), BF16)]
    out_specs = [pl.BlockSpec((1, H_FOX, tm, 2 * HD_FOX), lambda bi, i: (bi, 0, i, 0))]
    in_specs = [tile(D_FOX), piece, piece, piece]
    args = [x, hi, mid, lo]
    if v is not None:
        in_specs.append(tile(D_FOX))
        args.append(v)
        out_shape.append(jax.ShapeDtypeStruct((b, D_FOX, n), BF16))
        out_specs.append(pl.BlockSpec((1, D_FOX, tm), lambda bi, i: (bi, 0, i)))
    return pl.pallas_call(
        functools.partial(_fox_pack_kernel, q_side=q_side, with_v=v is not None),
        out_shape=tuple(out_shape),
        grid=(b, n // tm),
        in_specs=in_specs,
        out_specs=tuple(out_specs),
        compiler_params=_cparams(("parallel", "parallel")),
        name="fox_pack_q" if q_side else "fox_pack_kv",
    )(*args)


def _fox_attn_kernel(qa_ref, ka_ref, vt_ref, out_ref, m_scr, l_scr, acc_scr, *, p0, tq, tk, nk):
    qi, kj = pl.program_id(1), pl.program_id(2)

    @pl.when(kj == 0)
    def _():
        m_scr[...] = jnp.full(m_scr.shape, NEG_BIG, F32)
        l_scr[...] = jnp.zeros(l_scr.shape, F32)
        acc_scr[...] = jnp.zeros(acc_scr.shape, F32)

    def step(masked):
        col = lax.broadcasted_iota(jnp.int32, (1, 2 * HD_FOX), 1)
        scale = jnp.where(col < HD_FOX, HD_FOX ** -0.5, 1.0).astype(BF16)
        if masked:
            kpos = kj * tk + lax.broadcasted_iota(jnp.int32, (tk, tq), 0)
            qpos = p0 + qi * tq + lax.broadcasted_iota(jnp.int32, (tk, tq), 1)
            mask = kpos <= qpos
        for h in range(H_FOX):
            s = _nt_dot(ka_ref[0, h], qa_ref[0, h] * scale)
            if masked:
                s = jnp.where(mask, s, NEG_BIG)
            m_prev = m_scr[h]
            m_new = jnp.maximum(m_prev, jnp.max(s, axis=0, keepdims=True))
            p = jnp.exp(s - m_new)
            alpha = jnp.exp(m_prev - m_new)
            l_scr[h] = alpha * l_scr[h] + jnp.sum(p, axis=0, keepdims=True)
            acc_scr[h] = alpha * acc_scr[h] + jnp.dot(vt_ref[0, h], p.astype(BF16), preferred_element_type=F32)
            m_scr[h] = m_new

    first_q = p0 + qi * tq
    last_q = first_q + tq - 1
    unmasked = (kj + 1) * tk - 1 <= first_q

    @pl.when(unmasked)
    def _():
        step(False)

    @pl.when(jnp.logical_and(jnp.logical_not(unmasked), kj * tk <= last_q))
    def _():
        step(True)

    @pl.when(kj == nk - 1)
    def _():
        o_t = jnp.concatenate([acc_scr[h] / l_scr[h] for h in range(H_FOX)], axis=0)
        out_ref[0] = o_t.T.astype(BF16)


def _fox_attn(qa, ka, vt, p0):
    b, _, sq, _ = qa.shape
    sk = ka.shape[2]
    tq = min(sq, 512)
    tk = 512 if sk % 512 == 0 else 384
    nq, nk = sq // tq, sk // tk
    last_tile = lambda qi: (p0 + (qi + 1) * tq - 1) // tk
    return pl.pallas_call(
        functools.partial(_fox_attn_kernel, p0=p0, tq=tq, tk=tk, nk=nk),
        out_shape=jax.ShapeDtypeStruct((b, sq, D_FOX), BF16),
        grid=(b, nq, nk),
        in_specs=[
            pl.BlockSpec((1, H_FOX, tq, 2 * HD_FOX), lambda bi, qi, kj: (bi, 0, qi, 0)),
            pl.BlockSpec((1, H_FOX, tk, 2 * HD_FOX), lambda bi, qi, kj: (bi, 0, jnp.minimum(kj, last_tile(qi)), 0)),
            pl.BlockSpec((1, H_FOX, HD_FOX, tk), lambda bi, qi, kj: (bi, 0, 0, jnp.minimum(kj, last_tile(qi)))),
        ],
        out_specs=pl.BlockSpec((1, tq, D_FOX), lambda bi, qi, kj: (bi, qi, 0)),
        scratch_shapes=[
            pltpu.VMEM((H_FOX, 1, tq), F32),
            pltpu.VMEM((H_FOX, 1, tq), F32),
            pltpu.VMEM((H_FOX, HD_FOX, tq), F32),
        ],
        compiler_params=_cparams(("parallel", "parallel", "arbitrary")),
        name="fox_attn",
    )(qa, ka, vt)


def _merge_kernel(x_ref, hm_ref, of_ref, cq_ref, gates_ref, mk_ref, mv_ref, wm_ref, wf_ref, wc_ref,
                  wo_ref, gffn_ref, xnew_ref, hn_ref):
    cq = cq_ref[0]
    heads = []
    for h in range(H_MEM):
        sl = slice(h * HD_MEM, (h + 1) * HD_MEM)
        s = _nt_dot(cq[:, sl], mk_ref[0, :, sl]) * (HD_MEM ** -0.5)
        p = jnp.exp(s - jnp.max(s, axis=1, keepdims=True))
        o = jnp.dot(p.astype(BF16), mv_ref[0, :, sl], preferred_element_type=F32)
        heads.append(o / jnp.sum(p, axis=1, keepdims=True))
    oc = jnp.concatenate(heads, axis=1).astype(BF16)
    a_m = jnp.dot(hm_ref[0], wm_ref[...], preferred_element_type=F32)
    a_f = jnp.dot(of_ref[0], wf_ref[...], preferred_element_type=F32)
    a_c = jnp.dot(oc, wc_ref[...], preferred_element_type=F32)
    g = gates_ref[0].astype(F32)
    merged = (_sigmoid(g[:, 0:D_MODEL]) * a_m + _sigmoid(g[:, D_MODEL:2 * D_MODEL]) * a_f
              + _sigmoid(g[:, 2 * D_MODEL:3 * D_MODEL]) * a_c)
    xn = x_ref[0] + jnp.dot(merged.astype(BF16), wo_ref[...], preferred_element_type=F32)
    xnew_ref[0] = xn
    hn_ref[0] = _rms(xn, gffn_ref[...]).astype(BF16)


def _merge(x, hm, of, cq, gates, mk, mv, wm, wf, wc, wo, gffn):
    b, s, _ = x.shape
    ts = min(s, 256)
    tile = lambda w: pl.BlockSpec((1, ts, w), lambda bi, si: (bi, si, 0))
    const = lambda shape: pl.BlockSpec(shape, lambda bi, si: (0, 0))
    mem = pl.BlockSpec((1, mk.shape[1], D_MEM), lambda bi, si: (bi, 0, 0))
    return pl.pallas_call(
        _merge_kernel,
        out_shape=(jax.ShapeDtypeStruct((b, s, D_MODEL), F32), jax.ShapeDtypeStruct((b, s, D_MODEL), BF16)),
        grid=(b, s // ts),
        in_specs=[tile(D_MODEL), tile(D_M), tile(D_FOX), tile(D_MEM), tile(N_BRANCH * D_MODEL), mem, mem,
                  const((D_M, D_MODEL)), const((D_FOX, D_MODEL)), const((D_MEM, D_MODEL)),
                  const((D_MODEL, D_MODEL)), const((1, D_MODEL))],
        out_specs=(tile(D_MODEL), tile(D_MODEL)),
        compiler_params=_cparams(("parallel", "parallel")),
        name="merge",
    )(x, hm, of, cq, gates, mk, mv, wm, wf, wc, wo, gffn)


def _extract_top16(s, key_io, val_scr, idx_scr, h, want_rank):
    rank = jnp.full(s.shape, float(PEER_TOPK), F32) if want_rank else None
    for r in range(PEER_TOPK):
        m = jnp.max(s, axis=0, keepdims=True)
        first = jnp.min(jnp.where(s == m, key_io, float(N_KEYS)), axis=0, keepdims=True)
        hit = key_io == first
        s = jnp.where(hit, -jnp.inf, s)
        val_scr[r, pl.ds(h, 1), :] = m
        if want_rank:
            rank = jnp.where(hit, float(r), rank)
        else:
            idx_scr[r, pl.ds(h, 1), :] = first
    return rank


def _router_kernel(hn_ref, wqt_ref, keys_ref, a0_ref, cnt_ref, r1_ref, b1_ref,
                   qr_scr, e1_scr, top0_scr, top1_scr, idx0_scr, cnt_scr, invz_scr):
    tp = hn_ref.shape[0]
    qr_scr[...] = _nt_dot(wqt_ref[...], hn_ref[...])
    key_io = lax.broadcasted_iota(jnp.int32, (N_KEYS, tp), 0).astype(F32)

    heads_per_trip = 8

    def scores_and_top16_group(hg, carry):
        for k in range(heads_per_trip):
            scores_and_top16(heads_per_trip * hg + k)
        return carry

    def scores_and_top16(h):
        base = pl.multiple_of(h * 2 * N_KEYS, 2 * N_KEYS)
        s0 = jnp.dot(keys_ref[2 * h], qr_scr[pl.ds(base, N_KEYS), :].astype(BF16),
                     preferred_element_type=F32)
        s1 = jnp.dot(keys_ref[2 * h + 1], qr_scr[pl.ds(base + N_KEYS, N_KEYS), :].astype(BF16),
                     preferred_element_type=F32)
        _extract_top16(s0, key_io, top0_scr, idx0_scr, h, False)
        rank1 = _extract_top16(s1, key_io, top1_scr, None, h, True)
        a0_ref[h] = jnp.exp(s0 - top0_scr[0, pl.ds(h, 1), :])
        e1_scr[h] = jnp.exp(s1 - top1_scr[0, pl.ds(h, 1), :])
        r1_ref[h] = rank1.astype(BF16)

    lax.fori_loop(0, PEER_HEADS // heads_per_trip, scores_and_top16_group, 0)

    top0 = [top0_scr[a] for a in range(PEER_TOPK)]
    top1 = [top1_scr[b] for b in range(PEER_TOPK)]
    cnt = [jnp.zeros((PEER_HEADS, tp), F32) for _ in range(PEER_TOPK)]
    for _ in range(PEER_TOPK):
        front = []
        for a in range(PEER_TOPK):
            nxt = jnp.full((PEER_HEADS, tp), -jnp.inf, F32)
            for bb in range(PEER_TOPK // (a + 1)):
                nxt = jnp.where(cnt[a] == float(bb), top1[bb], nxt)
            front.append(top0[a] + nxt)
        mx = functools.reduce(jnp.maximum, front)
        first = functools.reduce(jnp.minimum,
                                 [jnp.where(front[a] == mx, float(a), float(PEER_TOPK)) for a in range(PEER_TOPK)])
        cnt = [cnt[a] + jnp.where(first == float(a), 1.0, 0.0) for a in range(PEER_TOPK)]
    z = jnp.zeros((PEER_HEADS, tp), F32)
    for a in range(PEER_TOPK):
        za = jnp.zeros((PEER_HEADS, tp), F32)
        for bb in range(PEER_TOPK // (a + 1)):
            za = za + jnp.where(cnt[a] > float(bb), jnp.exp(top1[bb] - top1[0]), 0.0)
        z = z + jnp.exp(top0[a] - top0[0]) * za
        cnt_scr[a] = cnt[a]
    invz_scr[...] = 1.0 / z

    def counts_and_gates(h, carry):
        cnt_i = jnp.zeros((N_KEYS, tp), F32)
        for a in range(PEER_TOPK):
            cnt_i = jnp.where(key_io == idx0_scr[a, pl.ds(h, 1), :], cnt_scr[a, pl.ds(h, 1), :], cnt_i)
        cnt_ref[h] = cnt_i
        b1 = e1_scr[h] * invz_scr[pl.ds(h, 1), :]
        b1_ref[h] = b1.astype(BF16)
        return carry

    lax.fori_loop(0, PEER_HEADS, counts_and_gates, 0)


def _router(hn2d, wqt, keys):
    t = hn2d.shape[0]
    tp = LANES
    shape = jax.ShapeDtypeStruct((PEER_HEADS, N_KEYS, t), F32)
    spec = pl.BlockSpec((PEER_HEADS, N_KEYS, tp), lambda i: (0, 0, i))
    slab_shape = jax.ShapeDtypeStruct((PEER_HEADS, N_KEYS, t), BF16)
    slab_spec = spec
    return pl.pallas_call(
        _router_kernel,
        out_shape=(shape, shape, slab_shape, slab_shape),
        grid=(t // tp,),
        in_specs=[
            pl.BlockSpec((tp, D_MODEL), lambda i: (i, 0)),
            pl.BlockSpec(wqt.shape, lambda i: (0, 0)),
            pl.BlockSpec(keys.shape, lambda i: (0, 0, 0)),
        ],
        out_specs=(spec, spec, slab_spec, slab_spec),
        scratch_shapes=[
            pltpu.VMEM((wqt.shape[0], tp), F32),
            pltpu.VMEM((PEER_HEADS, N_KEYS, tp), F32),
        ] + [pltpu.VMEM((PEER_TOPK, PEER_HEADS, tp), F32) for _ in range(4)] + [pltpu.VMEM((PEER_HEADS, tp), F32)],
        compiler_params=_cparams(("parallel",)),
        name="peer_router",
    )(hn2d, wqt, keys)


def _gelu_tanh(x):
    k = -2.0 * 0.7978845608028654 * 1.4426950408889634
    return x / (1.0 + jnp.exp2(x * (k + (k * 0.044715) * (x * x))))


def _experts_kernel(hn_ref, u_ref, vt_ref, a0_ref, cnt_ref, r1_ref, b1_ref, x_ref, gfin_ref, out_ref,
                    acc_scr, pre0_scr, pre1_scr, w0_scr, w1_scr, *, rows_per_tile, n_tiles, final_norm):
    s = pl.program_id(1)
    tp = hn_ref.shape[0]
    te = u_ref.shape[0] // 2
    slab = 16
    slabs_per_row = N_KEYS // slab
    d_half = acc_scr.shape[0] // 2
    zero = jnp.zeros((), BF16)

    @pl.when(s == 0)
    def _():
        acc_scr[...] = jnp.zeros(acc_scr.shape, F32)
        pre1_scr[...] = jnp.zeros(pre1_scr.shape, F32)
        w0_scr[...] = jnp.zeros(w0_scr.shape, BF16)

    def pipeline_step(parity, pre_w, pre_r, w_w, w_r):
        e = 2 * s + parity
        cols = slice(parity * te, (parity + 1) * te)

        def stage_c(half):
            rows = slice(half * d_half, (half + 1) * d_half)
            acc_scr[rows, :] += jnp.dot(vt_ref[rows, cols], w_r[...], preferred_element_type=F32)

        def stage_a(half):
            rows = slice(half * (te // 2), (half + 1) * (te // 2))
            pre_w[rows, :] = _nt_dot(u_ref[parity * te + half * (te // 2):parity * te + (half + 1) * (te // 2), :],
                                     hn_ref[...])

        mxu_work = (lambda: stage_c(0), lambda: stage_a(0), lambda: stage_c(1), lambda: stage_a(1))

        valid = jnp.logical_and(e >= 1, e <= n_tiles)
        tile_b = jnp.clip(e - 1, 0, n_tiles - 1)
        for ib in range(rows_per_tile):
            mxu_work[ib]()
            i = tile_b * rows_per_tile + ib
            for lanes in (slice(0, tp // 2), slice(tp // 2, tp)):
                gates = [jnp.zeros((slab, tp // 2), BF16) for _ in range(slabs_per_row)]
                for h in range(PEER_HEADS):
                    a_b = jnp.broadcast_to(a0_ref[h, pl.ds(i, 1), lanes], (slab, tp // 2)).astype(BF16)
                    c_b = jnp.broadcast_to(cnt_ref[h, pl.ds(i, 1), lanes], (slab, tp // 2)).astype(BF16)
                    for g in range(slabs_per_row):
                        key_rows = slice(g * slab, (g + 1) * slab)
                        gates[g] = gates[g] + a_b * jnp.where(r1_ref[h, key_rows, lanes] < c_b,
                                                              b1_ref[h, key_rows, lanes], zero)
                for g in range(slabs_per_row):
                    rows = slice(ib * N_KEYS + g * slab, ib * N_KEYS + (g + 1) * slab)
                    w = gates[g] * _gelu_tanh(pre_r[rows, lanes]).astype(BF16)
                    w_w[rows, lanes] = jnp.where(valid, w, zero)

    pipeline_step(0, pre0_scr, pre1_scr, w1_scr, w0_scr)
    pipeline_step(1, pre1_scr, pre0_scr, w0_scr, w1_scr)

    @pl.when(s == n_tiles // 2)
    def _():
        xo = x_ref[...] + acc_scr[...].T
        if final_norm:
            xo = _rms(xo, gfin_ref[...])
        out_ref[...] = xo


def _experts(hn2d, u_bf, vt_bf, a0, cnt, r1, b1, x2d, gfin, final_norm):
    t = hn2d.shape[0]
    tp = 512
    rows_per_tile = 4
    te = rows_per_tile * N_KEYS
    n_tiles = u_bf.shape[0] // te
    rspec = pl.BlockSpec((PEER_HEADS, N_KEYS, tp), lambda ti, e: (0, 0, ti))
    sspec = rspec
    return pl.pallas_call(
        functools.partial(_experts_kernel, rows_per_tile=rows_per_tile, n_tiles=n_tiles, final_norm=final_norm),
        out_shape=jax.ShapeDtypeStruct((t, D_MODEL), F32),
        grid=(t // tp, n_tiles // 2 + 1),
        in_specs=[
            pl.BlockSpec((tp, D_MODEL), lambda ti, s: (ti, 0)),
            pl.BlockSpec((2 * te, D_MODEL), lambda ti, s: (jnp.minimum(s, n_tiles // 2 - 1), 0)),
            pl.BlockSpec((D_MODEL, 2 * te), lambda ti, s: (0, jnp.maximum(s - 1, 0))),
            rspec, rspec, sspec, sspec,
            pl.BlockSpec((tp, D_MODEL), lambda ti, s: (ti, 0)),
            pl.BlockSpec((1, D_MODEL), lambda ti, s: (0, 0)),
        ],
        out_specs=pl.BlockSpec((tp, D_MODEL), lambda ti, s: (ti, 0)),
        scratch_shapes=[
            pltpu.VMEM((D_MODEL, tp), F32),
            pltpu.VMEM((te, tp), F32), pltpu.VMEM((te, tp), F32),
            pltpu.VMEM((te, tp), BF16), pltpu.VMEM((te, tp), BF16),
        ],
        compiler_params=_cparams(("parallel", "arbitrary"), EXPERT_FLAGS),
        name="peer_experts",
    )(hn2d, u_bf, vt_bf, a0, cnt, r1, b1, x2d, gfin)


def _prep_layer_weights(p):
    w_in = p['w_in']
    o_mi = 4 * D_M
    o_fq = o_mi + 2 * H_M
    o_ff = o_fq + 3 * D_FOX
    o_cq = o_ff + H_FOX
    w_big = jnp.concatenate([w_in[:, :o_mi], w_in[:, o_fq:o_ff], w_in[:, o_cq:]], axis=1).astype(BF16)
    w_small = jnp.concatenate([w_in[:, o_mi:o_fq], w_in[:, o_ff:o_cq]], axis=1)
    w_small = jnp.pad(w_small, ((0, 0), (0, LANES - w_small.shape[1]))).astype(BF16)
    bias = jnp.concatenate([p['b_m_i'], p['b_m_f'], p['b_fox_f']]).astype(F32)
    return dict(
        norm_mix=p['norm_mix'].reshape(1, D_MODEL), w_big=w_big, w_small=w_small,
        conv_w=p['conv_w'], conv_b=p['conv_b'].reshape(1, 2 * D_M),
        bias_row=jnp.pad(bias, (0, LANES - 16)).reshape(1, LANES), bias_col=bias.reshape(16, 1),
        norm_m_head=p['norm_m_head'].reshape(1, D_M),
        w_up_m=p['w_up_m'].astype(BF16), w_up_f=p['w_up_f'].astype(BF16), w_up_c=p['w_up_c'].astype(BF16),
        w_out=p['w_out'].astype(BF16), norm_ffn=p['norm_ffn'].reshape(1, D_MODEL),
        wqt=p['peer_wq'].T.astype(BF16),
        keys=p['peer_keys'].reshape(2 * PEER_HEADS, N_KEYS, N_KEYS).astype(BF16),
        u=p['peer_u'].astype(BF16), vt=p['peer_v'].T.astype(BF16),
    )


def _layer(x, mem_k, mem_v, conv_prev, c0, n0, m0, fk_past, fv_past, flf_past, w, gfin, final_norm):
    b, s, _ = x.shape
    t = b * s
    p0 = 0 if fk_past is None else fk_past.shape[1]
    qk, v_m, o_m, fq, fk, fv, cq, gates, small = _norm_proj(x.reshape(t, D_MODEL), w['norm_mix'], w['w_big'], w['w_small'])

    small3 = small.reshape(b, s, LANES)
    smallt = small3[:, :, :16].transpose(0, 2, 1)
    m0b = jnp.broadcast_to(m0[:, :, None, None], (b, H_M, 1, LANES))
    hm, c1, n1, m1b, conv_new = _mlstm(
        qk.reshape(b, s, 2 * D_M), v_m.reshape(b, s, D_M), o_m.reshape(b, s, D_M), small3, smallt,
        w['conv_w'], w['conv_b'], w['bias_row'], w['bias_col'], w['norm_m_head'],
        c0, n0[:, :, None, :], m0b, conv_prev)
    n1, m1 = n1[:, :, 0, :], m1b[:, :, 0, 0]

    k_f = fk.reshape(b, s, H_FOX, HD_FOX)
    v_f = fv.reshape(b, s, H_FOX, HD_FOX)
    s_pad = -(-s // LANES) * LANES
    pad_rows = lambda a: jnp.pad(a, ((0, 0), (0, s_pad - s), (0, 0)))
    past = None
    if p0:
        past = jnp.pad(flf_past.astype(F32), ((0, 0), (0, 0), (FOX_GATE_LANE, LANES - FOX_GATE_LANE - H_FOX)))
    lf_all, hi, mid, lo = _fox_prep(past, pad_rows(small3), w['bias_row'], s)
    lf_f = lf_all[:, :s, FOX_GATE_LANE:FOX_GATE_LANE + H_FOX]
    sk = p0 + s_pad
    k_all, v_all = pad_rows(fk.reshape(b, s, D_FOX)), pad_rows(fv.reshape(b, s, D_FOX))
    if p0:
        k_all = jnp.concatenate([fk_past.reshape(b, p0, D_FOX).astype(F32), k_all], axis=1)
        v_all = jnp.concatenate([fv_past.reshape(b, p0, D_FOX).astype(F32), v_all], axis=1)
    qa = _fox_pack(pad_rows(fq.reshape(b, s, D_FOX)), hi, mid, lo, p0, True)[0]
    ka, vt_all = _fox_pack(k_all, hi, mid, lo, 0, False, v=v_all)
    o_f = _fox_attn(qa, ka, vt_all.reshape(b, H_FOX, HD_FOX, sk), p0)[:, :s]

    x_new, hn = _merge(x, hm, o_f, cq.reshape(b, s, D_MEM), gates.reshape(b, s, N_BRANCH * D_MODEL),
                       mem_k.reshape(b, -1, D_MEM).astype(BF16), mem_v.reshape(b, -1, D_MEM).astype(BF16),
                       w['w_up_m'], w['w_up_f'], w['w_up_c'], w['w_out'], w['norm_ffn'])

    hn2d = hn.reshape(t, D_MODEL)
    a0, cnt, r1, b1 = _router(hn2d, w['wqt'], w['keys'])
    x_out = _experts(hn2d, w['u'], w['vt'], a0, cnt, r1, b1, x_new.reshape(t, D_MODEL), gfin, final_norm)
    return x_out.reshape(b, s, D_MODEL), conv_new, c1, n1, m1, k_f, v_f, lf_f


def kernel(x_prompt, x_sample, mem_prompt, cache_fox_k, cache_fox_v, cache_fox_lf, state_mlstm_c, state_mlstm_n, state_mlstm_m, state_conv, cache_mem_k, cache_mem_v, norm_mix, w_in, conv_w, conv_b, b_m_i, b_m_f, norm_m_head, b_fox_f, norm_mem, w_mem_kv, w_up_m, w_up_f, w_up_c, w_out, norm_ffn, peer_wq, peer_keys, peer_u, peer_v, norm_final):
    depth = w_in.shape[0]
    names = dict(norm_mix=norm_mix, w_in=w_in, conv_w=conv_w, conv_b=conv_b, b_m_i=b_m_i, b_m_f=b_m_f,
                 norm_m_head=norm_m_head, b_fox_f=b_fox_f, w_up_m=w_up_m, w_up_f=w_up_f, w_up_c=w_up_c,
                 w_out=w_out, norm_ffn=norm_ffn, peer_wq=peer_wq, peer_keys=peer_keys, peer_u=peer_u,
                 peer_v=peer_v)
    weights = [_prep_layer_weights({k: a[l] for k, a in names.items()}) for l in range(depth)]
    gfin = norm_final.reshape(1, D_MODEL)
    bp, n_mem = mem_prompt.shape[0], mem_prompt.shape[1]

    xp = x_prompt
    new_p = [[] for _ in range(9)]
    for l in range(depth):
        kv = _norm_matmul(mem_prompt.reshape(bp * n_mem, D_MODEL), norm_mem[l].reshape(1, D_MODEL),
                          w_mem_kv[l].astype(BF16))
        mk = kv[:, :D_MEM].reshape(bp, n_mem, H_MEM, HD_MEM)
        mv = kv[:, D_MEM:].reshape(bp, n_mem, H_MEM, HD_MEM)
        xp, conv1, c1, n1, m1, kf, vf, lff = _layer(
            xp, mk, mv, jnp.zeros((bp, CONV_W - 1, 2 * D_M), F32),
            jnp.zeros((bp, H_M, HD_M, HD_M), F32), jnp.zeros((bp, H_M, HD_M), F32), jnp.zeros((bp, H_M), F32),
            None, None, None, weights[l], gfin, l == depth - 1)
        for lst, a in zip(new_p, (kf, vf, lff, c1, n1, m1, conv1, mk, mv)):
            lst.append(a)
    outs_p = [jnp.stack(a) for a in new_p]

    xs = x_sample
    new_s = [[] for _ in range(7)]
    for l in range(depth):
        xs, conv1, c1, n1, m1, kf, vf, lff = _layer(
            xs, cache_mem_k[l], cache_mem_v[l], state_conv[l], state_mlstm_c[l], state_mlstm_n[l],
            state_mlstm_m[l], cache_fox_k[l], cache_fox_v[l], cache_fox_lf[l], weights[l], gfin, l == depth - 1)
        for lst, a in zip(new_s, (kf, vf, lff, c1, n1, m1, conv1)):
            lst.append(a)
    outs_s = [jnp.stack(a) for a in new_s]

    return (xp, xs, *outs_p, *outs_s)
```

```python
import functools

import jax
import jax.numpy as jnp
from jax import lax
from jax.experimental import pallas as pl
from jax.experimental.pallas import tpu as pltpu

F32 = jnp.float32
BF16 = jnp.bfloat16
EPS = 1e-6
NEG_BIG = -1e30

D_MODEL = 1024
H_M, HD_M = 4, 128
D_M = H_M * HD_M
CONV_W = 4
H_FOX, HD_FOX = 8, 64
D_FOX = H_FOX * HD_FOX
H_MEM, HD_MEM = 4, 128
D_MEM = H_MEM * HD_MEM
N_BRANCH = 3
PEER_HEADS = 8
N_KEYS = 128
PEER_TOPK = 16
LANES = 128
FOX_GATE_LANE = 2 * H_M
MLSTM_CHUNK = 256
PROJ_TILE = 512
VMEM_LIMIT = 56 * 1024 * 1024
EXPERT_FLAGS = None


def _cparams(sem, flags=None):
    return pltpu.CompilerParams(dimension_semantics=sem, vmem_limit_bytes=VMEM_LIMIT, flags=flags)


def _nt_dot(a, b):
    return lax.dot_general(a, b, (((1,), (1,)), ((), ())), preferred_element_type=F32)


def _rms(x, g):
    return x * lax.rsqrt(jnp.mean(x * x, axis=-1, keepdims=True) + EPS) * g


def _log_sigmoid(x):
    return jnp.minimum(x, 0.0) - jnp.log(1.0 + jnp.exp(-jnp.abs(x)))


def _sigmoid(x):
    return 1.0 / (1.0 + jnp.exp(-x))


def _norm_proj_kernel(x_ref, g_ref, w_ref, ws_ref, qk_ref, v_ref, o_ref, fq_ref, fk_ref, fv_ref,
                      cq_ref, gates_ref, small_ref, h_scr):
    j = pl.program_id(1)

    @pl.when(j == 0)
    def _():
        h = _rms(x_ref[...], g_ref[...]).astype(BF16)
        h_scr[...] = h
        small_ref[...] = jnp.dot(h, ws_ref[...], preferred_element_type=F32)

    def tile():
        return jnp.dot(h_scr[...], w_ref[0], preferred_element_type=F32)

    @pl.when(j == 0)
    def _():
        qk_ref[:, 0:PROJ_TILE] = tile()

    @pl.when(j == 1)
    def _():
        qk_ref[:, PROJ_TILE:2 * PROJ_TILE] = tile()

    @pl.when(j == 2)
    def _():
        v_ref[...] = tile().astype(BF16)

    @pl.when(j == 3)
    def _():
        o_ref[...] = tile()

    @pl.when(j == 4)
    def _():
        fq_ref[...] = tile().astype(BF16)

    @pl.when(j == 5)
    def _():
        fk_ref[...] = tile()

    @pl.when(j == 6)
    def _():
        fv_ref[...] = tile()

    @pl.when(j == 7)
    def _():
        cq_ref[...] = tile().astype(BF16)

    @pl.when(j >= 8)
    def _():
        gates_ref[...] = tile().astype(BF16)


def _norm_proj(x2d, g, w_big, w_small):
    t = x2d.shape[0]
    tm = 1024 if t % 1024 == 0 else 512
    n_tiles = w_big.shape[0]
    n_gate_tiles = n_tiles - 8
    row = lambda i, j: (i, 0)
    out_shape = (
        jax.ShapeDtypeStruct((t, 2 * D_M), F32),
        jax.ShapeDtypeStruct((t, D_M), BF16),
        jax.ShapeDtypeStruct((t, D_M), F32),
        jax.ShapeDtypeStruct((t, D_FOX), BF16),
        jax.ShapeDtypeStruct((t, D_FOX), F32),
        jax.ShapeDtypeStruct((t, D_FOX), F32),
        jax.ShapeDtypeStruct((t, D_MEM), BF16),
        jax.ShapeDtypeStruct((t, N_BRANCH * D_MODEL), BF16),
        jax.ShapeDtypeStruct((t, LANES), F32),
    )
    out_specs = (
        pl.BlockSpec((tm, 2 * D_M), row),
        pl.BlockSpec((tm, D_M), row),
        pl.BlockSpec((tm, D_M), row),
        pl.BlockSpec((tm, D_FOX), row),
        pl.BlockSpec((tm, D_FOX), row),
        pl.BlockSpec((tm, D_FOX), row),
        pl.BlockSpec((tm, D_MEM), row),
        pl.BlockSpec((tm, PROJ_TILE), lambda i, j: (i, jnp.clip(j - 8, 0, n_gate_tiles - 1))),
        pl.BlockSpec((tm, LANES), row),
    )
    return pl.pallas_call(
        _norm_proj_kernel,
        out_shape=out_shape,
        grid=(t // tm, n_tiles),
        in_specs=[
            pl.BlockSpec((tm, D_MODEL), row),
            pl.BlockSpec((1, D_MODEL), lambda i, j: (0, 0)),
            pl.BlockSpec((1, D_MODEL, PROJ_TILE), lambda i, j: (j, 0, 0)),
            pl.BlockSpec((D_MODEL, LANES), lambda i, j: (0, 0)),
        ],
        out_specs=out_specs,
        scratch_shapes=[pltpu.VMEM((tm, D_MODEL), BF16)],
        compiler_params=_cparams(("parallel", "arbitrary")),
        name="norm_proj",
    )(x2d, g, w_big, w_small)


def _norm_matmul_kernel(x_ref, g_ref, w_ref, k_ref, v_ref):
    h = _rms(x_ref[...], g_ref[...]).astype(BF16)
    k_ref[...] = jnp.dot(h, w_ref[:, 0:D_MEM], preferred_element_type=F32)
    v_ref[...] = jnp.dot(h, w_ref[:, D_MEM:2 * D_MEM], preferred_element_type=F32)


def _norm_matmul(x2d, g, w):
    t = x2d.shape[0]
    tm = 512
    out = jax.ShapeDtypeStruct((t, D_MEM), F32)
    return pl.pallas_call(
        _norm_matmul_kernel,
        out_shape=(out, out),
        grid=(t // tm,),
        in_specs=[
            pl.BlockSpec((tm, D_MODEL), lambda i: (i, 0)),
            pl.BlockSpec((1, D_MODEL), lambda i: (0, 0)),
            pl.BlockSpec((D_MODEL, 2 * D_MEM), lambda i: (0, 0)),
        ],
        out_specs=(pl.BlockSpec((tm, D_MEM), lambda i: (i, 0)), pl.BlockSpec((tm, D_MEM), lambda i: (i, 0))),
        compiler_params=_cparams(("parallel",)),
        name="norm_matmul",
    )(x2d, g, w)


def _mlstm_kernel(qk_ref, v_ref, o_ref, small_ref, smallt_ref, convw_ref, convb_ref, brow_ref,
                  bcol_ref, ghead_ref, c0_ref, n0_ref, m0_ref, conv0_ref,
                  hm_ref, c_out_ref, n_out_ref, m_out_ref, conv_out_ref,
                  c_scr, n_scr, m_scr, xp_scr, *, chunk):
    L = chunk
    si = pl.program_id(1)

    @pl.when(si == 0)
    def _():
        c_scr[...] = c0_ref[0]
        n_scr[...] = n0_ref[0]
        m_scr[...] = m0_ref[0]
        xp_scr[5:8, :] = conv0_ref[0]

    xp_scr[8:8 + L, :] = qk_ref[0]
    y = convb_ref[...] + convw_ref[0:1, :] * xp_scr[5:5 + L, :]
    for j in range(1, CONV_W):
        y = y + convw_ref[j:j + 1, :] * xp_scr[5 + j:5 + j + L, :]
    y = y * _sigmoid(y)
    tail = xp_scr[5 + L:8 + L, :]
    xp_scr[5:8, :] = tail
    conv_out_ref[0] = tail

    t_io = lax.broadcasted_iota(jnp.int32, (L, L), 0)
    s_io = lax.broadcasted_iota(jnp.int32, (L, L), 1)
    causal = s_io <= t_io
    ones_col = (lax.broadcasted_iota(jnp.int32, (L, HD_M), 1) == 0).astype(BF16)

    small = small_ref[0]
    smallt = smallt_ref[0]
    brow = brow_ref[...]
    bcol = bcol_ref[...]
    outs = []
    for h in range(H_M):
        q = y[:, h * HD_M:(h + 1) * HD_M]
        k = y[:, D_M + h * HD_M:D_M + (h + 1) * HD_M] * (HD_M ** -0.5)
        v_aug = jnp.concatenate([v_ref[0, :, h * HD_M:(h + 1) * HD_M], ones_col], axis=1)
        ig_col = small[:, h:h + 1] + brow[:, h:h + 1]
        ig_row = smallt[h:h + 1, :] + bcol[h:h + 1, :]
        lf_col = _log_sigmoid(small[:, H_M + h:H_M + h + 1] + brow[:, H_M + h:H_M + h + 1])
        lf_row = _log_sigmoid(smallt[H_M + h:H_M + h + 1, :] + bcol[H_M + h:H_M + h + 1, :])
        bcum_col = jnp.sum(jnp.where(causal, lf_row, 0.0), axis=1, keepdims=True)
        bcum_row = jnp.sum(jnp.where(t_io <= s_io, lf_col, 0.0), axis=0, keepdims=True)
        m_prev = m_scr[h][:, 0:1]
        dmat = jnp.where(causal, bcum_col - bcum_row + ig_row, -jnp.inf)
        inter = bcum_col + m_prev
        m_t = jnp.maximum(inter, jnp.max(dmat, axis=1, keepdims=True))
        w_intra = jnp.exp(dmat - m_t)
        w_state = jnp.exp(inter - m_t)
        qb = q.astype(BF16)
        sw = _nt_dot(qb, k.astype(BF16)) * w_intra
        c_prev = c_scr[h]
        n_prev = n_scr[h]
        intra = jnp.dot(sw.astype(BF16), v_aug, preferred_element_type=F32)
        num = intra[:, 0:HD_M] + w_state * jnp.dot(qb, c_prev.astype(BF16), preferred_element_type=F32)
        den = intra[:, HD_M:HD_M + 1] + w_state * jnp.sum(q * n_prev, axis=1, keepdims=True)
        hh = num / jnp.maximum(jnp.abs(den), jnp.exp(-m_t))
        hh = _rms(hh, ghead_ref[:, h * HD_M:(h + 1) * HD_M])
        outs.append(hh * _sigmoid(o_ref[0, :, h * HD_M:(h + 1) * HD_M]))
        bl = bcum_row[:, L - 1:L]
        m_new = jnp.maximum(bl + m_prev, jnp.max(bl - bcum_row + ig_row, axis=1, keepdims=True))
        decay = jnp.exp(bl + m_prev - m_new)
        ws_col = jnp.exp(bl - bcum_col + ig_col - m_new)
        kw = k * ws_col
        c_scr[h] = decay * c_prev + jnp.dot(kw.T.astype(BF16), v_ref[0, :, h * HD_M:(h + 1) * HD_M],
                                            preferred_element_type=F32)
        n_scr[h] = decay * n_prev + jnp.sum(kw, axis=0, keepdims=True)
        m_scr[h] = jnp.broadcast_to(m_new, (1, LANES))
    hm_ref[0] = jnp.concatenate(outs, axis=1).astype(BF16)
    c_out_ref[0] = c_scr[...]
    n_out_ref[0] = n_scr[...]
    m_out_ref[0] = m_scr[...]


def _mlstm(qk, v, o, small, smallt, convw, convb, brow, bcol, ghead, c0, n0, m0b, conv0):
    b, s, _ = qk.shape
    chunk = min(s, MLSTM_CHUNK)
    tile = lambda w: pl.BlockSpec((1, chunk, w), lambda bi, si: (bi, si, 0))
    per_b = lambda shape: pl.BlockSpec((1,) + shape, lambda bi, si: (bi,) + (0,) * len(shape))
    const = lambda shape: pl.BlockSpec(shape, lambda bi, si: (0, 0))
    state_shapes = ((H_M, HD_M, HD_M), (H_M, 1, HD_M), (H_M, 1, LANES), (CONV_W - 1, 2 * D_M))
    return pl.pallas_call(
        functools.partial(_mlstm_kernel, chunk=chunk),
        out_shape=(jax.ShapeDtypeStruct((b, s, D_M), BF16),)
        + tuple(jax.ShapeDtypeStruct((b,) + sh, F32) for sh in state_shapes),
        grid=(b, s // chunk),
        in_specs=[
            tile(2 * D_M), tile(D_M), tile(D_M), tile(LANES),
            pl.BlockSpec((1, 16, chunk), lambda bi, si: (bi, 0, si)),
            const((CONV_W, 2 * D_M)), const((1, 2 * D_M)), const((1, LANES)), const((16, 1)), const((1, D_M)),
        ] + [per_b(sh) for sh in state_shapes],
        out_specs=(tile(D_M),) + tuple(per_b(sh) for sh in state_shapes),
        scratch_shapes=[
            pltpu.VMEM((H_M, HD_M, HD_M), F32),
            pltpu.VMEM((H_M, 1, HD_M), F32),
            pltpu.VMEM((H_M, 1, LANES), F32),
            pltpu.VMEM((chunk + 8, 2 * D_M), F32),
        ],
        compiler_params=_cparams(("parallel", "arbitrary")),
        name="mlstm",
    )(qk, v, o, small, smallt, convw, convb, brow, bcol, ghead, c0, n0, m0b, conv0)


def _split3(x):
    hi = x.astype(BF16).astype(F32)
    r = x - hi
    mid = r.astype(BF16).astype(F32)
    lo = (r - mid).astype(BF16).astype(F32)
    return hi, mid, lo


def _fox_prep_kernel(*refs, n_past, n_new, s_valid, blk):
    if n_past:
        past_ref, pre_ref, bias_ref, lf_ref, hi_ref, mid_ref, lo_ref = refs
    else:
        pre_ref, bias_ref, lf_ref, hi_ref, mid_ref, lo_ref = refs
    tri = (lax.broadcasted_iota(jnp.int32, (blk, blk), 1) <= lax.broadcasted_iota(jnp.int32, (blk, blk), 0)).astype(BF16)
    carry = jnp.zeros((1, LANES), F32)
    start = 0
    while start < n_past + n_new:
        if start < n_past:
            r = min(blk, n_past - start)
            lf = jnp.concatenate([jnp.zeros((r, FOX_GATE_LANE), F32), past_ref[0, start:start + r, :],
                                  jnp.zeros((r, LANES - FOX_GATE_LANE - H_FOX), F32)], axis=1)
        else:
            r = min(blk, n_past + n_new - start)
            ns = start - n_past
            row = ns + lax.broadcasted_iota(jnp.int32, (r, LANES), 0)
            lf = jnp.where(row < s_valid, _log_sigmoid(pre_ref[0, ns:ns + r, :] + bias_ref[...]), 0.0)
            lf_ref[0, ns:ns + r, :] = lf
        cum = carry
        for piece in _split3(lf):
            cum = cum + jnp.dot(tri[0:r, 0:r], piece.astype(BF16), preferred_element_type=F32)
        carry = cum[r - 1:r, :]
        hi, mid, lo = _split3(cum)
        hi_ref[0, start:start + r, :] = hi
        mid_ref[0, start:start + r, :] = mid
        lo_ref[0, start:start + r, :] = lo
        start += r


def _fox_prep(past, pre, bias_row, s_valid):
    b, n_new = pre.shape[:2]
    n_past = 0 if past is None else past.shape[1]
    n = n_past + n_new
    full = lambda rows: pl.BlockSpec((1, rows, LANES), lambda bi: (bi, 0, 0))
    past_spec = pl.BlockSpec((1, n_past, H_FOX), lambda bi: (bi, 0, 0))
    in_specs = ([past_spec] if n_past else []) + [full(n_new), pl.BlockSpec((1, LANES), lambda bi: (0, 0))]
    args = ([past] if n_past else []) + [pre, bias_row]
    cum_shape = jax.ShapeDtypeStruct((b, n, LANES), F32)
    return pl.pallas_call(
        functools.partial(_fox_prep_kernel, n_past=n_past, n_new=n_new, s_valid=s_valid, blk=256),
        out_shape=(jax.ShapeDtypeStruct((b, n_new, LANES), F32), cum_shape, cum_shape, cum_shape),
        grid=(b,),
        in_specs=in_specs,
        out_specs=(full(n_new), full(n), full(n), full(n)),
        compiler_params=_cparams(("parallel",)),
        name="fox_prep",
    )(*args)


def _fox_pack_kernel(*refs, q_side, with_v):
    if with_v:
        x_ref, hi_ref, mid_ref, lo_ref, v_ref, out_ref, vt_ref = refs
        vt_ref[0] = v_ref[0].T.astype(BF16)
    else:
        x_ref, hi_ref, mid_ref, lo_ref, out_ref = refs
    tm = x_ref.shape[1]
    lane = lax.broadcasted_iota(jnp.int32, (tm, HD_FOX), 1)
    hi, mid, lo = hi_ref[0], mid_ref[0], lo_ref[0]
    for h in range(H_FOX):
        c = FOX_GATE_LANE + h
        pieces = (hi[:, c:c + 1], mid[:, c:c + 1], lo[:, c:c + 1])
        first = 0 if q_side else 3
        bias = jnp.where(jnp.logical_and(lane >= 3 - first, lane < 6 - first), -1.0 if q_side else 1.0, 0.0)
        for j, p in enumerate(pieces):
            bias = jnp.where(lane == first + j, p, bias)
        out_ref[0, h] = jnp.concatenate(
            [x_ref[0, :, h * HD_FOX:(h + 1) * HD_FOX].astype(BF16), bias.astype(BF16)], axis=1)


def _fox_pack(x, hi, mid, lo, row_offset, q_side, v=None):
    b, n, _ = x.shape
    tm = next(c for c in (512, 384, 256, LANES) if n % c == 0 and row_offset % c == 0)
    off = row_offset // tm
    tile = lambda w: pl.BlockSpec((1, tm, w), lambda bi, i: (bi, i, 0))
    piece = pl.BlockSpec((1, tm, LANES), lambda bi, i: (bi, off + i, 0))
    out_shape = [jax.ShapeDtypeStruct((b, H_FOX, n, 2 * HD_FOX), BF16)]
    out_specs = [pl.BlockSpec((1, H_FOX, tm, 2 * HD_FOX), lambda bi, i: (bi, 0, i, 0))]
    in_specs = [tile(D_FOX), piece, piece, piece]
    args = [x, hi, mid, lo]
    if v is not None:
        in_specs.append(tile(D_FOX))
        args.append(v)
        out_shape.append(jax.ShapeDtypeStruct((b, D_FOX, n), BF16))
        out_specs.append(pl.BlockSpec((1, D_FOX, tm), lambda bi, i: (bi, 0, i)))
    return pl.pallas_call(
        functools.partial(_fox_pack_kernel, q_side=q_side, with_v=v is not None),
        out_shape=tuple(out_shape),
        grid=(b, n // tm),
        in_specs=in_specs,
        out_specs=tuple(out_specs),
        compiler_params=_cparams(("parallel", "parallel")),
        name="fox_pack_q" if q_side else "fox_pack_kv",
    )(*args)


def _fox_attn_kernel(qa_ref, ka_ref, vt_ref, out_ref, m_scr, l_scr, acc_scr, *, p0, tq, tk, nk):
    qi, kj = pl.program_id(1), pl.program_id(2)

    @pl.when(kj == 0)
    def _():
        m_scr[...] = jnp.full(m_scr.shape, NEG_BIG, F32)
        l_scr[...] = jnp.zeros(l_scr.shape, F32)
        acc_scr[...] = jnp.zeros(acc_scr.shape, F32)

    def step(masked):
        col = lax.broadcasted_iota(jnp.int32, (1, 2 * HD_FOX), 1)
        scale = jnp.where(col < HD_FOX, HD_FOX ** -0.5, 1.0).astype(BF16)
        if masked:
            kpos = kj * tk + lax.broadcasted_iota(jnp.int32, (tk, tq), 0)
            qpos = p0 + qi * tq + lax.broadcasted_iota(jnp.int32, (tk, tq), 1)
            mask = kpos <= qpos
        for h in range(H_FOX):
            s = _nt_dot(ka_ref[0, h], qa_ref[0, h] * scale)
            if masked:
                s = jnp.where(mask, s, NEG_BIG)
            m_prev = m_scr[h]
            m_new = jnp.maximum(m_prev, jnp.max(s, axis=0, keepdims=True))
            p = jnp.exp(s - m_new)
            alpha = jnp.exp(m_prev - m_new)
            l_scr[h] = alpha * l_scr[h] + jnp.sum(p, axis=0, keepdims=True)
            acc_scr[h] = alpha * acc_scr[h] + jnp.dot(vt_ref[0, h], p.astype(BF16), preferred_element_type=F32)
            m_scr[h] = m_new

    first_q = p0 + qi * tq
    last_q = first_q + tq - 1
    unmasked = (kj + 1) * tk - 1 <= first_q

    @pl.when(unmasked)
    def _():
        step(False)

    @pl.when(jnp.logical_and(jnp.logical_not(unmasked), kj * tk <= last_q))
    def _():
        step(True)

    @pl.when(kj == nk - 1)
    def _():
        o_t = jnp.concatenate([acc_scr[h] / l_scr[h] for h in range(H_FOX)], axis=0)
        out_ref[0] = o_t.T.astype(BF16)


def _fox_attn(qa, ka, vt, p0):
    b, _, sq, _ = qa.shape
    sk = ka.shape[2]
    tq = min(sq, 512)
    tk = 512 if sk % 512 == 0 else 384
    nq, nk = sq // tq, sk // tk
    last_tile = lambda qi: (p0 + (qi + 1) * tq - 1) // tk
    return pl.pallas_call(
        functools.partial(_fox_attn_kernel, p0=p0, tq=tq, tk=tk, nk=nk),
        out_shape=jax.ShapeDtypeStruct((b, sq, D_FOX), BF16),
        grid=(b, nq, nk),
        in_specs=[
            pl.BlockSpec((1, H_FOX, tq, 2 * HD_FOX), lambda bi, qi, kj: (bi, 0, qi, 0)),
            pl.BlockSpec((1, H_FOX, tk, 2 * HD_FOX), lambda bi, qi, kj: (bi, 0, jnp.minimum(kj, last_tile(qi)), 0)),
            pl.BlockSpec((1, H_FOX, HD_FOX, tk), lambda bi, qi, kj: (bi, 0, 0, jnp.minimum(kj, last_tile(qi)))),
        ],
        out_specs=pl.BlockSpec((1, tq, D_FOX), lambda bi, qi, kj: (bi, qi, 0)),
        scratch_shapes=[
            pltpu.VMEM((H_FOX, 1, tq), F32),
            pltpu.VMEM((H_FOX, 1, tq), F32),
            pltpu.VMEM((H_FOX, HD_FOX, tq), F32),
        ],
        compiler_params=_cparams(("parallel", "parallel", "arbitrary")),
        name="fox_attn",
    )(qa, ka, vt)


def _merge_kernel(x_ref, hm_ref, of_ref, cq_ref, gates_ref, mk_ref, mv_ref, wm_ref, wf_ref, wc_ref,
                  wo_ref, gffn_ref, xnew_ref, hn_ref):
    cq = cq_ref[0]
    heads = []
    for h in range(H_MEM):
        sl = slice(h * HD_MEM, (h + 1) * HD_MEM)
        s = _nt_dot(cq[:, sl], mk_ref[0, :, sl]) * (HD_MEM ** -0.5)
        p = jnp.exp(s - jnp.max(s, axis=1, keepdims=True))
        o = jnp.dot(p.astype(BF16), mv_ref[0, :, sl], preferred_element_type=F32)
        heads.append(o / jnp.sum(p, axis=1, keepdims=True))
    oc = jnp.concatenate(heads, axis=1).astype(BF16)
    a_m = jnp.dot(hm_ref[0], wm_ref[...], preferred_element_type=F32)
    a_f = jnp.dot(of_ref[0], wf_ref[...], preferred_element_type=F32)
    a_c = jnp.dot(oc, wc_ref[...], preferred_element_type=F32)
    g = gates_ref[0].astype(F32)
    merged = (_sigmoid(g[:, 0:D_MODEL]) * a_m + _sigmoid(g[:, D_MODEL:2 * D_MODEL]) * a_f
              + _sigmoid(g[:, 2 * D_MODEL:3 * D_MODEL]) * a_c)
    xn = x_ref[0] + jnp.dot(merged.astype(BF16), wo_ref[...], preferred_element_type=F32)
    xnew_ref[0] = xn
    hn_ref[0] = _rms(xn, gffn_ref[...]).astype(BF16)


def _merge(x, hm, of, cq, gates, mk, mv, wm, wf, wc, wo, gffn):
    b, s, _ = x.shape
    ts = min(s, 256)
    tile = lambda w: pl.BlockSpec((1, ts, w), lambda bi, si: (bi, si, 0))
    const = lambda shape: pl.BlockSpec(shape, lambda bi, si: (0, 0))
    mem = pl.BlockSpec((1, mk.shape[1], D_MEM), lambda bi, si: (bi, 0, 0))
    return pl.pallas_call(
        _merge_kernel,
        out_shape=(jax.ShapeDtypeStruct((b, s, D_MODEL), F32), jax.ShapeDtypeStruct((b, s, D_MODEL), BF16)),
        grid=(b, s // ts),
        in_specs=[tile(D_MODEL), tile(D_M), tile(D_FOX), tile(D_MEM), tile(N_BRANCH * D_MODEL), mem, mem,
                  const((D_M, D_MODEL)), const((D_FOX, D_MODEL)), const((D_MEM, D_MODEL)),
                  const((D_MODEL, D_MODEL)), const((1, D_MODEL))],
        out_specs=(tile(D_MODEL), tile(D_MODEL)),
        compiler_params=_cparams(("parallel", "parallel")),
        name="merge",
    )(x, hm, of, cq, gates, mk, mv, wm, wf, wc, wo, gffn)


def _extract_top16(s, key_io, val_scr, idx_scr, h, want_rank):
    rank = jnp.full(s.shape, float(PEER_TOPK), F32) if want_rank else None
    for r in range(PEER_TOPK):
        m = jnp.max(s, axis=0, keepdims=True)
        first = jnp.min(jnp.where(s == m, key_io, float(N_KEYS)), axis=0, keepdims=True)
        hit = key_io == first
        s = jnp.where(hit, -jnp.inf, s)
        val_scr[r, pl.ds(h, 1), :] = m
        if want_rank:
            rank = jnp.where(hit, float(r), rank)
        else:
            idx_scr[r, pl.ds(h, 1), :] = first
    return rank


def _router_kernel(hn_ref, wqt_ref, keys_ref, a0_ref, cnt_ref, r1_ref, b1_ref,
                   qr_scr, e1_scr, top0_scr, top1_scr, idx0_scr, cnt_scr, invz_scr):
    tp = hn_ref.shape[0]
    qr_scr[...] = _nt_dot(wqt_ref[...], hn_ref[...])
    key_io = lax.broadcasted_iota(jnp.int32, (N_KEYS, tp), 0).astype(F32)

    heads_per_trip = 8

    def scores_and_top16_group(hg, carry):
        for k in range(heads_per_trip):
            scores_and_top16(heads_per_trip * hg + k)
        return carry

    def scores_and_top16(h):
        base = pl.multiple_of(h * 2 * N_KEYS, 2 * N_KEYS)
        s0 = jnp.dot(keys_ref[2 * h], qr_scr[pl.ds(base, N_KEYS), :].astype(BF16),
                     preferred_element_type=F32)
        s1 = jnp.dot(keys_ref[2 * h + 1], qr_scr[pl.ds(base + N_KEYS, N_KEYS), :].astype(BF16),
                     preferred_element_type=F32)
        _extract_top16(s0, key_io, top0_scr, idx0_scr, h, False)
        rank1 = _extract_top16(s1, key_io, top1_scr, None, h, True)
        a0_ref[h] = jnp.exp(s0 - top0_scr[0, pl.ds(h, 1), :])
        e1_scr[h] = jnp.exp(s1 - top1_scr[0, pl.ds(h, 1), :])
        r1_ref[h] = rank1.astype(BF16)

    lax.fori_loop(0, PEER_HEADS // heads_per_trip, scores_and_top16_group, 0)

    top0 = [top0_scr[a] for a in range(PEER_TOPK)]
    top1 = [top1_scr[b] for b in range(PEER_TOPK)]
    cnt = [jnp.zeros((PEER_HEADS, tp), F32) for _ in range(PEER_TOPK)]
    for _ in range(PEER_TOPK):
        front = []
        for a in range(PEER_TOPK):
            nxt = jnp.full((PEER_HEADS, tp), -jnp.inf, F32)
            for bb in range(PEER_TOPK // (a + 1)):
                nxt = jnp.where(cnt[a] == float(bb), top1[bb], nxt)
            front.append(top0[a] + nxt)
        mx = functools.reduce(jnp.maximum, front)
        first = functools.reduce(jnp.minimum,
                                 [jnp.where(front[a] == mx, float(a), float(PEER_TOPK)) for a in range(PEER_TOPK)])
        cnt = [cnt[a] + jnp.where(first == float(a), 1.0, 0.0) for a in range(PEER_TOPK)]
    z = jnp.zeros((PEER_HEADS, tp), F32)
    for a in range(PEER_TOPK):
        za = jnp.zeros((PEER_HEADS, tp), F32)
        for bb in range(PEER_TOPK // (a + 1)):
            za = za + jnp.where(cnt[a] > float(bb), jnp.exp(top1[bb] - top1[0]), 0.0)
        z = z + jnp.exp(top0[a] - top0[0]) * za
        cnt_scr[a] = cnt[a]
    invz_scr[...] = 1.0 / z

    def counts_and_gates(h, carry):
        cnt_i = jnp.zeros((N_KEYS, tp), F32)
        for a in range(PEER_TOPK):
            cnt_i = jnp.where(key_io == idx0_scr[a, pl.ds(h, 1), :], cnt_scr[a, pl.ds(h, 1), :], cnt_i)
        cnt_ref[h] = cnt_i
        b1 = e1_scr[h] * invz_scr[pl.ds(h, 1), :]
        b1_ref[h] = b1.astype(BF16)
        return carry

    lax.fori_loop(0, PEER_HEADS, counts_and_gates, 0)


def _router(hn2d, wqt, keys):
    t = hn2d.shape[0]
    tp = LANES
    shape = jax.ShapeDtypeStruct((PEER_HEADS, N_KEYS, t), F32)
    spec = pl.BlockSpec((PEER_HEADS, N_KEYS, tp), lambda i: (0, 0, i))
    slab_shape = jax.ShapeDtypeStruct((PEER_HEADS, N_KEYS, t), BF16)
    slab_spec = spec
    return pl.pallas_call(
        _router_kernel,
        out_shape=(shape, shape, slab_shape, slab_shape),
        grid=(t // tp,),
        in_specs=[
            pl.BlockSpec((tp, D_MODEL), lambda i: (i, 0)),
            pl.BlockSpec(wqt.shape, lambda i: (0, 0)),
            pl.BlockSpec(keys.shape, lambda i: (0, 0, 0)),
        ],
        out_specs=(spec, spec, slab_spec, slab_spec),
        scratch_shapes=[
            pltpu.VMEM((wqt.shape[0], tp), F32),
            pltpu.VMEM((PEER_HEADS, N_KEYS, tp), F32),
        ] + [pltpu.VMEM((PEER_TOPK, PEER_HEADS, tp), F32) for _ in range(4)] + [pltpu.VMEM((PEER_HEADS, tp), F32)],
        compiler_params=_cparams(("parallel",)),
        name="peer_router",
    )(hn2d, wqt, keys)


def _gelu_tanh(x):
    k = -2.0 * 0.7978845608028654 * 1.4426950408889634
    return x / (1.0 + jnp.exp2(x * (k + (k * 0.044715) * (x * x))))


def _experts_kernel(hn_ref, u_ref, vt_ref, a0_ref, cnt_ref, r1_ref, b1_ref, x_ref, gfin_ref, out_ref,
                    acc_scr, pre0_scr, pre1_scr, w0_scr, w1_scr, *, rows_per_tile, n_tiles, final_norm):
    s = pl.program_id(1)
    tp = hn_ref.shape[0]
    te = u_ref.shape[0] // 2
    slab = 16
    slabs_per_row = N_KEYS // slab
    d_half = acc_scr.shape[0] // 2
    zero = jnp.zeros((), BF16)

    @pl.when(s == 0)
    def _():
        acc_scr[...] = jnp.zeros(acc_scr.shape, F32)
        pre1_scr[...] = jnp.zeros(pre1_scr.shape, F32)
        w0_scr[...] = jnp.zeros(w0_scr.shape, BF16)

    def pipeline_step(parity, pre_w, pre_r, w_w, w_r):
        e = 2 * s + parity
        cols = slice(parity * te, (parity + 1) * te)

        def stage_c(half):
            rows = slice(half * d_half, (half + 1) * d_half)
            acc_scr[rows, :] += jnp.dot(vt_ref[rows, cols], w_r[...], preferred_element_type=F32)

        def stage_a(half):
            rows = slice(half * (te // 2), (half + 1) * (te // 2))
            pre_w[rows, :] = _nt_dot(u_ref[parity * te + half * (te // 2):parity * te + (half + 1) * (te // 2), :],
                                     hn_ref[...])

        mxu_work = (lambda: stage_c(0), lambda: stage_a(0), lambda: stage_c(1), lambda: stage_a(1))

        valid = jnp.logical_and(e >= 1, e <= n_tiles)
        tile_b = jnp.clip(e - 1, 0, n_tiles - 1)
        for ib in range(rows_per_tile):
            mxu_work[ib]()
            i = tile_b * rows_per_tile + ib
            for lanes in (slice(0, tp // 2), slice(tp // 2, tp)):
                gates = [jnp.zeros((slab, tp // 2), BF16) for _ in range(slabs_per_row)]
                for h in range(PEER_HEADS):
                    a_b = jnp.broadcast_to(a0_ref[h, pl.ds(i, 1), lanes], (slab, tp // 2)).astype(BF16)
                    c_b = jnp.broadcast_to(cnt_ref[h, pl.ds(i, 1), lanes], (slab, tp // 2)).astype(BF16)
                    for g in range(slabs_per_row):
                        key_rows = slice(g * slab, (g + 1) * slab)
                        gates[g] = gates[g] + a_b * jnp.where(r1_ref[h, key_rows, lanes] < c_b,
                                                              b1_ref[h, key_rows, lanes], zero)
                for g in range(slabs_per_row):
                    rows = slice(ib * N_KEYS + g * slab, ib * N_KEYS + (g + 1) * slab)
                    w = gates[g] * _gelu_tanh(pre_r[rows, lanes]).astype(BF16)
                    w_w[rows, lanes] = jnp.where(valid, w, zero)

    pipeline_step(0, pre0_scr, pre1_scr, w1_scr, w0_scr)
    pipeline_step(1, pre1_scr, pre0_scr, w0_scr, w1_scr)

    @pl.when(s == n_tiles // 2)
    def _():
        xo = x_ref[...] + acc_scr[...].T
        if final_norm:
            xo = _rms(xo, gfin_ref[...])
        out_ref[...] = xo


def _experts(hn2d, u_bf, vt_bf, layer, a0, cnt, r1, b1, x2d, gfin, final_norm):
    t = hn2d.shape[0]
    tp = 512
    rows_per_tile = 4
    te = rows_per_tile * N_KEYS
    n_tiles = u_bf.shape[1] // te
    rspec = pl.BlockSpec((PEER_HEADS, N_KEYS, tp), lambda ti, e: (0, 0, ti))
    sspec = rspec
    return pl.pallas_call(
        functools.partial(_experts_kernel, rows_per_tile=rows_per_tile, n_tiles=n_tiles, final_norm=final_norm),
        out_shape=jax.ShapeDtypeStruct((t, D_MODEL), F32),
        grid=(t // tp, n_tiles // 2 + 1),
        in_specs=[
            pl.BlockSpec((tp, D_MODEL), lambda ti, s: (ti, 0)),
            pl.BlockSpec((None, 2 * te, D_MODEL), lambda ti, s: (layer, jnp.minimum(s, n_tiles // 2 - 1), 0)),
            pl.BlockSpec((None, D_MODEL, 2 * te), lambda ti, s: (layer, 0, jnp.maximum(s - 1, 0))),
            rspec, rspec, sspec, sspec,
            pl.BlockSpec((tp, D_MODEL), lambda ti, s: (ti, 0)),
            pl.BlockSpec((1, D_MODEL), lambda ti, s: (0, 0)),
        ],
        out_specs=pl.BlockSpec((tp, D_MODEL), lambda ti, s: (ti, 0)),
        scratch_shapes=[
            pltpu.VMEM((D_MODEL, tp), F32),
            pltpu.VMEM((te, tp), F32), pltpu.VMEM((te, tp), F32),
            pltpu.VMEM((te, tp), BF16), pltpu.VMEM((te, tp), BF16),
        ],
        compiler_params=_cparams(("parallel", "arbitrary"), EXPERT_FLAGS),
        name="peer_experts",
    )(hn2d, u_bf, vt_bf, a0, cnt, r1, b1, x2d, gfin)


def _prep_layer_weights(p):
    w_in = p['w_in']
    o_mi = 4 * D_M
    o_fq = o_mi + 2 * H_M
    o_ff = o_fq + 3 * D_FOX
    o_cq = o_ff + H_FOX
    w_big = jnp.concatenate([w_in[:, :o_mi], w_in[:, o_fq:o_ff], w_in[:, o_cq:]], axis=1).astype(BF16)
    w_big = w_big.reshape(D_MODEL, -1, PROJ_TILE).transpose(1, 0, 2)
    w_small = jnp.concatenate([w_in[:, o_mi:o_fq], w_in[:, o_ff:o_cq]], axis=1)
    w_small = jnp.pad(w_small, ((0, 0), (0, LANES - w_small.shape[1]))).astype(BF16)
    bias = jnp.concatenate([p['b_m_i'], p['b_m_f'], p['b_fox_f']]).astype(F32)
    return dict(
        norm_mix=p['norm_mix'].reshape(1, D_MODEL), w_big=w_big, w_small=w_small,
        conv_w=p['conv_w'], conv_b=p['conv_b'].reshape(1, 2 * D_M),
        bias_row=jnp.pad(bias, (0, LANES - 16)).reshape(1, LANES), bias_col=bias.reshape(16, 1),
        norm_m_head=p['norm_m_head'].reshape(1, D_M),
        w_up_m=p['w_up_m'].astype(BF16), w_up_f=p['w_up_f'].astype(BF16), w_up_c=p['w_up_c'].astype(BF16),
        w_out=p['w_out'].astype(BF16), norm_ffn=p['norm_ffn'].reshape(1, D_MODEL),
        wqt=p['peer_wq'].T.astype(BF16),
        keys=p['peer_keys'].reshape(2 * PEER_HEADS, N_KEYS, N_KEYS).astype(BF16),
    )


def _layer(x, mem_k, mem_v, conv_prev, c0, n0, m0, fk_past, fv_past, flf_past, w, peer_tables, layer, gfin, final_norm):
    b, s, _ = x.shape
    t = b * s
    p0 = 0 if fk_past is None else fk_past.shape[1]
    qk, v_m, o_m, fq, fk, fv, cq, gates, small = _norm_proj(x.reshape(t, D_MODEL), w['norm_mix'], w['w_big'], w['w_small'])

    small3 = small.reshape(b, s, LANES)
    smallt = small3[:, :, :16].transpose(0, 2, 1)
    m0b = jnp.broadcast_to(m0[:, :, None, None], (b, H_M, 1, LANES))
    hm, c1, n1, m1b, conv_new = _mlstm(
        qk.reshape(b, s, 2 * D_M), v_m.reshape(b, s, D_M), o_m.reshape(b, s, D_M), small3, smallt,
        w['conv_w'], w['conv_b'], w['bias_row'], w['bias_col'], w['norm_m_head'],
        c0, n0[:, :, None, :], m0b, conv_prev)
    n1, m1 = n1[:, :, 0, :], m1b[:, :, 0, 0]

    k_f = fk.reshape(b, s, H_FOX, HD_FOX)
    v_f = fv.reshape(b, s, H_FOX, HD_FOX)
    s_pad = -(-s // LANES) * LANES
    pad_rows = lambda a: jnp.pad(a, ((0, 0), (0, s_pad - s), (0, 0)))
    past = flf_past.astype(F32) if p0 else None
    lf_all, hi, mid, lo = _fox_prep(past, pad_rows(small3), w['bias_row'], s)
    lf_f = lf_all[:, :s, FOX_GATE_LANE:FOX_GATE_LANE + H_FOX]
    sk = p0 + s_pad
    k_all, v_all = pad_rows(fk.reshape(b, s, D_FOX)), pad_rows(fv.reshape(b, s, D_FOX))
    if p0:
        k_all = jnp.concatenate([fk_past.reshape(b, p0, D_FOX).astype(F32), k_all], axis=1)
        v_all = jnp.concatenate([fv_past.reshape(b, p0, D_FOX).astype(F32), v_all], axis=1)
    qa = _fox_pack(pad_rows(fq.reshape(b, s, D_FOX)), hi, mid, lo, p0, True)[0]
    ka, vt_all = _fox_pack(k_all, hi, mid, lo, 0, False, v=v_all)
    o_f = _fox_attn(qa, ka, vt_all.reshape(b, H_FOX, HD_FOX, sk), p0)[:, :s]

    x_new, hn = _merge(x, hm, o_f, cq.reshape(b, s, D_MEM), gates.reshape(b, s, N_BRANCH * D_MODEL),
                       mem_k.reshape(b, -1, D_MEM).astype(BF16), mem_v.reshape(b, -1, D_MEM).astype(BF16),
                       w['w_up_m'], w['w_up_f'], w['w_up_c'], w['w_out'], w['norm_ffn'])

    hn2d = hn.reshape(t, D_MODEL)
    a0, cnt, r1, b1 = _router(hn2d, w['wqt'], w['keys'])
    x_out = _experts(hn2d, peer_tables[0], peer_tables[1], layer, a0, cnt, r1, b1, x_new.reshape(t, D_MODEL),
                     gfin, final_norm)
    return x_out.reshape(b, s, D_MODEL), conv_new, c1, n1, m1, k_f, v_f, lf_f


def kernel(x_prompt, x_sample, mem_prompt, cache_fox_k, cache_fox_v, cache_fox_lf, state_mlstm_c, state_mlstm_n, state_mlstm_m, state_conv, cache_mem_k, cache_mem_v, norm_mix, w_in, conv_w, conv_b, b_m_i, b_m_f, norm_m_head, b_fox_f, norm_mem, w_mem_kv, w_up_m, w_up_f, w_up_c, w_out, norm_ffn, peer_wq, peer_keys, peer_u, peer_v, norm_final):
    depth = w_in.shape[0]
    names = dict(norm_mix=norm_mix, w_in=w_in, conv_w=conv_w, conv_b=conv_b, b_m_i=b_m_i, b_m_f=b_m_f,
                 norm_m_head=norm_m_head, b_fox_f=b_fox_f, w_up_m=w_up_m, w_up_f=w_up_f, w_up_c=w_up_c,
                 w_out=w_out, norm_ffn=norm_ffn, peer_wq=peer_wq, peer_keys=peer_keys, peer_u=peer_u,
                 peer_v=peer_v)
    weights = [_prep_layer_weights({k: a[l] for k, a in names.items()}) for l in range(depth)]
    peer_tables = (peer_u.astype(BF16), jnp.swapaxes(peer_v, 1, 2).astype(BF16))
    gfin = norm_final.reshape(1, D_MODEL)
    bp, n_mem = mem_prompt.shape[0], mem_prompt.shape[1]

    xp = x_prompt
    new_p = [[] for _ in range(9)]
    for l in range(depth):
        mk, mv = _norm_matmul(mem_prompt.reshape(bp * n_mem, D_MODEL), norm_mem[l].reshape(1, D_MODEL),
                              w_mem_kv[l].astype(BF16))
        mk = mk.reshape(bp, n_mem, H_MEM, HD_MEM)
        mv = mv.reshape(bp, n_mem, H_MEM, HD_MEM)
        xp, conv1, c1, n1, m1, kf, vf, lff = _layer(
            xp, mk, mv, jnp.zeros((bp, CONV_W - 1, 2 * D_M), F32),
            jnp.zeros((bp, H_M, HD_M, HD_M), F32), jnp.zeros((bp, H_M, HD_M), F32), jnp.zeros((bp, H_M), F32),
            None, None, None, weights[l], peer_tables, l, gfin, l == depth - 1)
        for lst, a in zip(new_p, (kf, vf, lff, c1, n1, m1, conv1, mk, mv)):
            lst.append(a)
    outs_p = [jnp.stack(a) for a in new_p]

    xs = x_sample
    new_s = [[] for _ in range(7)]
    for l in range(depth):
        xs, conv1, c1, n1, m1, kf, vf, lff = _layer(
            xs, cache_mem_k[l], cache_mem_v[l], state_conv[l], state_mlstm_c[l], state_mlstm_n[l],
            state_mlstm_m[l], cache_fox_k[l], cache_fox_v[l], cache_fox_lf[l], weights[l], peer_tables, l, gfin,
            l == depth - 1)
        for lst, a in zip(new_s, (kf, vf, lff, c1, n1, m1, conv1)):
            lst.append(a)
    outs_s = [jnp.stack(a) for a in new_s]

    return (xp, xs, *outs_p, *outs_s)
```

```python
import functools

import jax
import jax.numpy as jnp
from jax import lax
from jax.experimental import pallas as pl
from jax.experimental.pallas import tpu as pltpu

F32 = jnp.float32
BF16 = jnp.bfloat16
EPS = 1e-6
NEG_BIG = -1e30

D_MODEL = 1024
H_M, HD_M = 4, 128
D_M = H_M * HD_M
CONV_W = 4
H_FOX, HD_FOX = 8, 64
D_FOX = H_FOX * HD_FOX
H_MEM, HD_MEM = 4, 128
D_MEM = H_MEM * HD_MEM
N_BRANCH = 3
PEER_HEADS = 8
N_KEYS = 128
PEER_TOPK = 16
LANES = 128
FOX_GATE_LANE = 2 * H_M
MLSTM_CHUNK = 256
PROJ_TILE = 512
VMEM_LIMIT = 56 * 1024 * 1024
EXPERT_FLAGS = None


def _cparams(sem, flags=None):
    return pltpu.CompilerParams(dimension_semantics=sem, vmem_limit_bytes=VMEM_LIMIT, flags=flags)


def _nt_dot(a, b):
    return lax.dot_general(a, b, (((1,), (1,)), ((), ())), preferred_element_type=F32)


def _rms(x, g):
    return x * lax.rsqrt(jnp.mean(x * x, axis=-1, keepdims=True) + EPS) * g


def _log_sigmoid(x):
    return jnp.minimum(x, 0.0) - jnp.log(1.0 + jnp.exp(-jnp.abs(x)))


def _sigmoid(x):
    return 1.0 / (1.0 + jnp.exp(-x))


def _norm_proj_kernel(x_ref, g_ref, w_ref, ws_ref, qk_ref, v_ref, o_ref, fq_ref, fk_ref, fv_ref,
                      cq_ref, gates_ref, small_ref, h_scr):
    j = pl.program_id(1)

    @pl.when(j == 0)
    def _():
        hf = _rms(x_ref[...], g_ref[...])
        h = hf.astype(BF16)
        h_scr[...] = h
        h_lo = (hf - h.astype(F32)).astype(BF16)
        small_ref[...] = (jnp.dot(h, ws_ref[0], preferred_element_type=F32)
                          + jnp.dot(h_lo, ws_ref[0], preferred_element_type=F32)
                          + jnp.dot(h, ws_ref[1], preferred_element_type=F32))

    def tile():
        return jnp.dot(h_scr[...], w_ref[0], preferred_element_type=F32)

    @pl.when(j == 0)
    def _():
        qk_ref[:, 0:PROJ_TILE] = tile()

    @pl.when(j == 1)
    def _():
        qk_ref[:, PROJ_TILE:2 * PROJ_TILE] = tile()

    @pl.when(j == 2)
    def _():
        v_ref[...] = tile().astype(BF16)

    @pl.when(j == 3)
    def _():
        o_ref[...] = tile()

    @pl.when(j == 4)
    def _():
        fq_ref[...] = tile().astype(BF16)

    @pl.when(j == 5)
    def _():
        fk_ref[...] = tile()

    @pl.when(j == 6)
    def _():
        fv_ref[...] = tile()

    @pl.when(j == 7)
    def _():
        cq_ref[...] = tile().astype(BF16)

    @pl.when(j >= 8)
    def _():
        gates_ref[...] = tile().astype(BF16)


def _norm_proj(x2d, g, w_big, w_small):
    t = x2d.shape[0]
    tm = 1024 if t % 1024 == 0 else 512
    n_tiles = w_big.shape[0]
    n_gate_tiles = n_tiles - 8
    row = lambda i, j: (i, 0)
    out_shape = (
        jax.ShapeDtypeStruct((t, 2 * D_M), F32),
        jax.ShapeDtypeStruct((t, D_M), BF16),
        jax.ShapeDtypeStruct((t, D_M), F32),
        jax.ShapeDtypeStruct((t, D_FOX), BF16),
        jax.ShapeDtypeStruct((t, D_FOX), F32),
        jax.ShapeDtypeStruct((t, D_FOX), F32),
        jax.ShapeDtypeStruct((t, D_MEM), BF16),
        jax.ShapeDtypeStruct((t, N_BRANCH * D_MODEL), BF16),
        jax.ShapeDtypeStruct((t, LANES), F32),
    )
    out_specs = (
        pl.BlockSpec((tm, 2 * D_M), row),
        pl.BlockSpec((tm, D_M), row),
        pl.BlockSpec((tm, D_M), row),
        pl.BlockSpec((tm, D_FOX), row),
        pl.BlockSpec((tm, D_FOX), row),
        pl.BlockSpec((tm, D_FOX), row),
        pl.BlockSpec((tm, D_MEM), row),
        pl.BlockSpec((tm, PROJ_TILE), lambda i, j: (i, jnp.clip(j - 8, 0, n_gate_tiles - 1))),
        pl.BlockSpec((tm, LANES), row),
    )
    return pl.pallas_call(
        _norm_proj_kernel,
        out_shape=out_shape,
        grid=(t // tm, n_tiles),
        in_specs=[
            pl.BlockSpec((tm, D_MODEL), row),
            pl.BlockSpec((1, D_MODEL), lambda i, j: (0, 0)),
            pl.BlockSpec((1, D_MODEL, PROJ_TILE), lambda i, j: (j, 0, 0)),
            pl.BlockSpec((2, D_MODEL, LANES), lambda i, j: (0, 0, 0)),
        ],
        out_specs=out_specs,
        scratch_shapes=[pltpu.VMEM((tm, D_MODEL), BF16)],
        compiler_params=_cparams(("parallel", "arbitrary")),
        name="norm_proj",
    )(x2d, g, w_big, w_small)


def _norm_matmul_kernel(x_ref, g_ref, w_ref, k_ref, v_ref):
    h = _rms(x_ref[...], g_ref[...]).astype(BF16)
    k_ref[...] = jnp.dot(h, w_ref[:, 0:D_MEM], preferred_element_type=F32)
    v_ref[...] = jnp.dot(h, w_ref[:, D_MEM:2 * D_MEM], preferred_element_type=F32)


def _norm_matmul(x2d, g, w):
    t = x2d.shape[0]
    tm = 512
    out = jax.ShapeDtypeStruct((t, D_MEM), F32)
    return pl.pallas_call(
        _norm_matmul_kernel,
        out_shape=(out, out),
        grid=(t // tm,),
        in_specs=[
            pl.BlockSpec((tm, D_MODEL), lambda i: (i, 0)),
            pl.BlockSpec((1, D_MODEL), lambda i: (0, 0)),
            pl.BlockSpec((D_MODEL, 2 * D_MEM), lambda i: (0, 0)),
        ],
        out_specs=(pl.BlockSpec((tm, D_MEM), lambda i: (i, 0)), pl.BlockSpec((tm, D_MEM), lambda i: (i, 0))),
        compiler_params=_cparams(("parallel",)),
        name="norm_matmul",
    )(x2d, g, w)


def _mlstm_kernel(qk_ref, v_ref, o_ref, small_ref, smallt_ref, convw_ref, convb_ref, brow_ref,
                  bcol_ref, ghead_ref, c0_ref, n0_ref, m0_ref, conv0_ref,
                  hm_ref, c_out_ref, n_out_ref, m_out_ref, conv_out_ref,
                  c_scr, n_scr, m_scr, xp_scr, *, chunk):
    L = chunk
    si = pl.program_id(1)

    @pl.when(si == 0)
    def _():
        c_scr[...] = c0_ref[0]
        n_scr[...] = n0_ref[0]
        m_scr[...] = m0_ref[0]
        xp_scr[5:8, :] = conv0_ref[0]

    xp_scr[8:8 + L, :] = qk_ref[0]
    y = convb_ref[...] + convw_ref[0:1, :] * xp_scr[5:5 + L, :]
    for j in range(1, CONV_W):
        y = y + convw_ref[j:j + 1, :] * xp_scr[5 + j:5 + j + L, :]
    y = y * _sigmoid(y)
    tail = xp_scr[5 + L:8 + L, :]
    xp_scr[5:8, :] = tail
    conv_out_ref[0] = tail

    t_io = lax.broadcasted_iota(jnp.int32, (L, L), 0)
    s_io = lax.broadcasted_iota(jnp.int32, (L, L), 1)
    causal = s_io <= t_io
    ones_col = (lax.broadcasted_iota(jnp.int32, (L, HD_M), 1) == 0).astype(BF16)

    small = small_ref[0]
    smallt = smallt_ref[0]
    brow = brow_ref[...]
    bcol = bcol_ref[...]
    outs = []
    for h in range(H_M):
        q = y[:, h * HD_M:(h + 1) * HD_M]
        k = y[:, D_M + h * HD_M:D_M + (h + 1) * HD_M] * (HD_M ** -0.5)
        v_aug = jnp.concatenate([v_ref[0, :, h * HD_M:(h + 1) * HD_M], ones_col], axis=1)
        ig_col = small[:, h:h + 1] + brow[:, h:h + 1]
        ig_row = smallt[h:h + 1, :] + bcol[h:h + 1, :]
        lf_col = _log_sigmoid(small[:, H_M + h:H_M + h + 1] + brow[:, H_M + h:H_M + h + 1])
        lf_row = _log_sigmoid(smallt[H_M + h:H_M + h + 1, :] + bcol[H_M + h:H_M + h + 1, :])
        bcum_col = jnp.sum(jnp.where(causal, lf_row, 0.0), axis=1, keepdims=True)
        bcum_row = jnp.sum(jnp.where(t_io <= s_io, lf_col, 0.0), axis=0, keepdims=True)
        m_prev = m_scr[h][:, 0:1]
        dmat = jnp.where(causal, bcum_col - bcum_row + ig_row, -jnp.inf)
        inter = bcum_col + m_prev
        m_t = jnp.maximum(inter, jnp.max(dmat, axis=1, keepdims=True))
        w_intra = jnp.exp(dmat - m_t)
        w_state = jnp.exp(inter - m_t)
        qb = q.astype(BF16)
        sw = _nt_dot(qb, k.astype(BF16)) * w_intra
        c_prev = c_scr[h]
        n_prev = n_scr[h]
        intra = jnp.dot(sw.astype(BF16), v_aug, preferred_element_type=F32)
        num = intra[:, 0:HD_M] + w_state * jnp.dot(qb, c_prev.astype(BF16), preferred_element_type=F32)
        den = intra[:, HD_M:HD_M + 1] + w_state * jnp.sum(q * n_prev, axis=1, keepdims=True)
        hh = num / jnp.maximum(jnp.abs(den), jnp.exp(-m_t))
        hh = _rms(hh, ghead_ref[:, h * HD_M:(h + 1) * HD_M])
        outs.append(hh * _sigmoid(o_ref[0, :, h * HD_M:(h + 1) * HD_M]))
        bl = bcum_row[:, L - 1:L]
        m_new = jnp.maximum(bl + m_prev, jnp.max(bl - bcum_row + ig_row, axis=1, keepdims=True))
        decay = jnp.exp(bl + m_prev - m_new)
        ws_col = jnp.exp(bl - bcum_col + ig_col - m_new)
        kw = k * ws_col
        c_scr[h] = decay * c_prev + jnp.dot(kw.T.astype(BF16), v_ref[0, :, h * HD_M:(h + 1) * HD_M],
                                            preferred_element_type=F32)
        n_scr[h] = decay * n_prev + jnp.sum(kw, axis=0, keepdims=True)
        m_scr[h] = jnp.broadcast_to(m_new, (1, LANES))
    hm_ref[0] = jnp.concatenate(outs, axis=1).astype(BF16)
    c_out_ref[0] = c_scr[...]
    n_out_ref[0] = n_scr[...]
    m_out_ref[0] = m_scr[...]


def _mlstm(qk, v, o, small, smallt, convw, convb, brow, bcol, ghead, c0, n0, m0b, conv0):
    b, s, _ = qk.shape
    chunk = min(s, MLSTM_CHUNK)
    tile = lambda w: pl.BlockSpec((1, chunk, w), lambda bi, si: (bi, si, 0))
    per_b = lambda shape: pl.BlockSpec((1,) + shape, lambda bi, si: (bi,) + (0,) * len(shape))
    const = lambda shape: pl.BlockSpec(shape, lambda bi, si: (0, 0))
    state_shapes = ((H_M, HD_M, HD_M), (H_M, 1, HD_M), (H_M, 1, LANES), (CONV_W - 1, 2 * D_M))
    return pl.pallas_call(
        functools.partial(_mlstm_kernel, chunk=chunk),
        out_shape=(jax.ShapeDtypeStruct((b, s, D_M), BF16),)
        + tuple(jax.ShapeDtypeStruct((b,) + sh, F32) for sh in state_shapes),
        grid=(b, s // chunk),
        in_specs=[
            tile(2 * D_M), tile(D_M), tile(D_M), tile(LANES),
            pl.BlockSpec((1, 16, chunk), lambda bi, si: (bi, 0, si)),
            const((CONV_W, 2 * D_M)), const((1, 2 * D_M)), const((1, LANES)), const((16, 1)), const((1, D_M)),
        ] + [per_b(sh) for sh in state_shapes],
        out_specs=(tile(D_M),) + tuple(per_b(sh) for sh in state_shapes),
        scratch_shapes=[
            pltpu.VMEM((H_M, HD_M, HD_M), F32),
            pltpu.VMEM((H_M, 1, HD_M), F32),
            pltpu.VMEM((H_M, 1, LANES), F32),
            pltpu.VMEM((chunk + 8, 2 * D_M), F32),
        ],
        compiler_params=_cparams(("parallel", "arbitrary")),
        name="mlstm",
    )(qk, v, o, small, smallt, convw, convb, brow, bcol, ghead, c0, n0, m0b, conv0)


def _split3(x):
    hi = x.astype(BF16).astype(F32)
    r = x - hi
    mid = r.astype(BF16).astype(F32)
    lo = (r - mid).astype(BF16).astype(F32)
    return hi, mid, lo


def _fox_prep_kernel(*refs, n_past, n_new, s_valid, blk):
    if n_past:
        past_ref, pre_ref, bias_ref, lf_ref, hi_ref, mid_ref, lo_ref = refs
    else:
        pre_ref, bias_ref, lf_ref, hi_ref, mid_ref, lo_ref = refs
    tri = (lax.broadcasted_iota(jnp.int32, (blk, blk), 1) <= lax.broadcasted_iota(jnp.int32, (blk, blk), 0)).astype(BF16)
    carry = jnp.zeros((1, LANES), F32)
    start = 0
    while start < n_past + n_new:
        if start < n_past:
            r = min(blk, n_past - start)
            lf = jnp.concatenate([jnp.zeros((r, FOX_GATE_LANE), F32), past_ref[0, start:start + r, :],
                                  jnp.zeros((r, LANES - FOX_GATE_LANE - H_FOX), F32)], axis=1)
        else:
            r = min(blk, n_past + n_new - start)
            ns = start - n_past
            row = ns + lax.broadcasted_iota(jnp.int32, (r, LANES), 0)
            lf = jnp.where(row < s_valid, _log_sigmoid(pre_ref[0, ns:ns + r, :] + bias_ref[...]), 0.0)
            lf_ref[0, ns:ns + r, :] = lf
        cum = carry
        for piece in _split3(lf):
            cum = cum + jnp.dot(tri[0:r, 0:r], piece.astype(BF16), preferred_element_type=F32)
        carry = cum[r - 1:r, :]
        hi, mid, lo = _split3(cum)
        hi_ref[0, start:start + r, :] = hi
        mid_ref[0, start:start + r, :] = mid
        lo_ref[0, start:start + r, :] = lo
        start += r


def _fox_prep(past, pre, bias_row, s_valid):
    b, n_new = pre.shape[:2]
    n_past = 0 if past is None else past.shape[1]
    n = n_past + n_new
    full = lambda rows: pl.BlockSpec((1, rows, LANES), lambda bi: (bi, 0, 0))
    past_spec = pl.BlockSpec((1, n_past, H_FOX), lambda bi: (bi, 0, 0))
    in_specs = ([past_spec] if n_past else []) + [full(n_new), pl.BlockSpec((1, LANES), lambda bi: (0, 0))]
    args = ([past] if n_past else []) + [pre, bias_row]
    cum_shape = jax.ShapeDtypeStruct((b, n, LANES), F32)
    return pl.pallas_call(
        functools.partial(_fox_prep_kernel, n_past=n_past, n_new=n_new, s_valid=s_valid, blk=256),
        out_shape=(jax.ShapeDtypeStruct((b, n_new, LANES), F32), cum_shape, cum_shape, cum_shape),
        grid=(b,),
        in_specs=in_specs,
        out_specs=(full(n_new), full(n), full(n), full(n)),
        compiler_params=_cparams(("parallel",)),
        name="fox_prep",
    )(*args)


def _fox_pack_kernel(*refs, q_side, with_v):
    if with_v:
        x_ref, hi_ref, mid_ref, lo_ref, v_ref, out_ref, vt_ref = refs
        vt_ref[0] = v_ref[0].T.astype(BF16)
    else:
        x_ref, hi_ref, mid_ref, lo_ref, out_ref = refs
    tm = x_ref.shape[1]
    lane = lax.broadcasted_iota(jnp.int32, (tm, HD_FOX), 1)
    hi, mid, lo = hi_ref[0], mid_ref[0], lo_ref[0]
    for h in range(H_FOX):
        c = FOX_GATE_LANE + h
        pieces = (hi[:, c:c + 1], mid[:, c:c + 1], lo[:, c:c + 1])
        first = 0 if q_side else 3
        bias = jnp.where(jnp.logical_and(lane >= 3 - first, lane < 6 - first), -1.0 if q_side else 1.0, 0.0)
        for j, p in enumerate(pieces):
            bias = jnp.where(lane == first + j, p, bias)
        out_ref[0, h] = jnp.concatenate(
            [x_ref[0, :, h * HD_FOX:(h + 1) * HD_FOX].astype(BF16), bias.astype(BF16)], axis=1)


def _fox_pack(x, hi, mid, lo, row_offset, q_side, v=None):
    b, n, _ = x.shape
    tm = next(c for c in (512, 384, 256, LANES) if n % c == 0 and row_offset % c == 0)
    off = row_offset // tm
    tile = lambda w: pl.BlockSpec((1, tm, w), lambda bi, i: (bi, i, 0))
    piece = pl.BlockSpec((1, tm, LANES), lambda bi, i: (bi, off + i, 0))
    out_shape = [jax.ShapeDtypeStruct((b, H_FOX, n, 2 * HD_FOX), BF16)]
    out_specs = [pl.BlockSpec((1, H_FOX, tm, 2 * HD_FOX), lambda bi, i: (bi, 0, i, 0))]
    in_specs = [tile(D_FOX), piece, piece, piece]
    args = [x, hi, mid, lo]
    if v is not None:
        in_specs.append(tile(D_FOX))
        args.append(v)
        out_shape.append(jax.ShapeDtypeStruct((b, D_FOX, n), BF16))
        out_specs.append(pl.BlockSpec((1, D_FOX, tm), lambda bi, i: (bi, 0, i)))
    return pl.pallas_call(
        functools.partial(_fox_pack_kernel, q_side=q_side, with_v=v is not None),
        out_shape=tuple(out_shape),
        grid=(b, n // tm),
        in_specs=in_specs,
        out_specs=tuple(out_specs),
        compiler_params=_cparams(("parallel", "parallel")),
        name="fox_pack_q" if q_side else "fox_pack_kv",
    )(*args)


def _fox_attn_kernel(qa_ref, ka_ref, vt_ref, out_ref, m_scr, l_scr, acc_scr, *, p0, tq, tk, nk):
    qi, kj = pl.program_id(1), pl.program_id(2)

    @pl.when(kj == 0)
    def _():
        m_scr[...] = jnp.full(m_scr.shape, NEG_BIG, F32)
        l_scr[...] = jnp.zeros(l_scr.shape, F32)
        acc_scr[...] = jnp.zeros(acc_scr.shape, F32)

    def step(masked):
        col = lax.broadcasted_iota(jnp.int32, (1, 2 * HD_FOX), 1)
        scale = jnp.where(col < HD_FOX, HD_FOX ** -0.5, 1.0).astype(BF16)
        if masked:
            kpos = kj * tk + lax.broadcasted_iota(jnp.int32, (tk, tq), 0)
            qpos = p0 + qi * tq + lax.broadcasted_iota(jnp.int32, (tk, tq), 1)
            mask = kpos <= qpos
        for h in range(H_FOX):
            s = _nt_dot(ka_ref[0, h], qa_ref[0, h] * scale)
            if masked:
                s = jnp.where(mask, s, NEG_BIG)
            m_prev = m_scr[h]
            m_new = jnp.maximum(m_prev, jnp.max(s, axis=0, keepdims=True))
            p = jnp.exp(s - m_new)
            alpha = jnp.exp(m_prev - m_new)
            l_scr[h] = alpha * l_scr[h] + jnp.sum(p, axis=0, keepdims=True)
            acc_scr[h] = alpha * acc_scr[h] + jnp.dot(vt_ref[0, h], p.astype(BF16), preferred_element_type=F32)
            m_scr[h] = m_new

    first_q = p0 + qi * tq
    last_q = first_q + tq - 1
    unmasked = (kj + 1) * tk - 1 <= first_q

    @pl.when(unmasked)
    def _():
        step(False)

    @pl.when(jnp.logical_and(jnp.logical_not(unmasked), kj * tk <= last_q))
    def _():
        step(True)

    @pl.when(kj == nk - 1)
    def _():
        o_t = jnp.concatenate([acc_scr[h] / l_scr[h] for h in range(H_FOX)], axis=0)
        out_ref[0] = o_t.T.astype(BF16)


def _fox_attn(qa, ka, vt, p0):
    b, _, sq, _ = qa.shape
    sk = ka.shape[2]
    tq = min(sq, 512)
    tk = 512 if sk % 512 == 0 else 384
    nq, nk = sq // tq, sk // tk
    last_tile = lambda qi: (p0 + (qi + 1) * tq - 1) // tk
    return pl.pallas_call(
        functools.partial(_fox_attn_kernel, p0=p0, tq=tq, tk=tk, nk=nk),
        out_shape=jax.ShapeDtypeStruct((b, sq, D_FOX), BF16),
        grid=(b, nq, nk),
        in_specs=[
            pl.BlockSpec((1, H_FOX, tq, 2 * HD_FOX), lambda bi, qi, kj: (bi, 0, qi, 0)),
            pl.BlockSpec((1, H_FOX, tk, 2 * HD_FOX), lambda bi, qi, kj: (bi, 0, jnp.minimum(kj, last_tile(qi)), 0)),
            pl.BlockSpec((1, H_FOX, HD_FOX, tk), lambda bi, qi, kj: (bi, 0, 0, jnp.minimum(kj, last_tile(qi)))),
        ],
        out_specs=pl.BlockSpec((1, tq, D_FOX), lambda bi, qi, kj: (bi, qi, 0)),
        scratch_shapes=[
            pltpu.VMEM((H_FOX, 1, tq), F32),
            pltpu.VMEM((H_FOX, 1, tq), F32),
            pltpu.VMEM((H_FOX, HD_FOX, tq), F32),
        ],
        compiler_params=_cparams(("parallel", "parallel", "arbitrary")),
        name="fox_attn",
    )(qa, ka, vt)


def _merge_kernel(x_ref, hm_ref, of_ref, cq_ref, gates_ref, mk_ref, mv_ref, wm_ref, wf_ref, wc_ref,
                  wo_ref, gffn_ref, xnew_ref, hn_ref):
    cq = cq_ref[0]
    heads = []
    for h in range(H_MEM):
        sl = slice(h * HD_MEM, (h + 1) * HD_MEM)
        s = _nt_dot(cq[:, sl], mk_ref[0, :, sl]) * (HD_MEM ** -0.5)
        p = jnp.exp(s - jnp.max(s, axis=1, keepdims=True))
        o = jnp.dot(p.astype(BF16), mv_ref[0, :, sl], preferred_element_type=F32)
        heads.append(o / jnp.sum(p, axis=1, keepdims=True))
    oc = jnp.concatenate(heads, axis=1).astype(BF16)
    a_m = jnp.dot(hm_ref[0], wm_ref[...], preferred_element_type=F32)
    a_f = jnp.dot(of_ref[0], wf_ref[...], preferred_element_type=F32)
    a_c = jnp.dot(oc, wc_ref[...], preferred_element_type=F32)
    g = gates_ref[0].astype(F32)
    merged = (_sigmoid(g[:, 0:D_MODEL]) * a_m + _sigmoid(g[:, D_MODEL:2 * D_MODEL]) * a_f
              + _sigmoid(g[:, 2 * D_MODEL:3 * D_MODEL]) * a_c)
    xn = x_ref[0] + jnp.dot(merged.astype(BF16), wo_ref[...], preferred_element_type=F32)
    xnew_ref[0] = xn
    hn_ref[0] = _rms(xn, gffn_ref[...]).astype(BF16)


def _merge(x, hm, of, cq, gates, mk, mv, wm, wf, wc, wo, gffn):
    b, s, _ = x.shape
    ts = min(s, 256)
    tile = lambda w: pl.BlockSpec((1, ts, w), lambda bi, si: (bi, si, 0))
    const = lambda shape: pl.BlockSpec(shape, lambda bi, si: (0, 0))
    mem = pl.BlockSpec((1, mk.shape[1], D_MEM), lambda bi, si: (bi, 0, 0))
    return pl.pallas_call(
        _merge_kernel,
        out_shape=(jax.ShapeDtypeStruct((b, s, D_MODEL), F32), jax.ShapeDtypeStruct((b, s, D_MODEL), BF16)),
        grid=(b, s // ts),
        in_specs=[tile(D_MODEL), tile(D_M), tile(D_FOX), tile(D_MEM), tile(N_BRANCH * D_MODEL), mem, mem,
                  const((D_M, D_MODEL)), const((D_FOX, D_MODEL)), const((D_MEM, D_MODEL)),
                  const((D_MODEL, D_MODEL)), const((1, D_MODEL))],
        out_specs=(tile(D_MODEL), tile(D_MODEL)),
        compiler_params=_cparams(("parallel", "parallel")),
        name="merge",
    )(x, hm, of, cq, gates, mk, mv, wm, wf, wc, wo, gffn)


def _extract_top16(s, key_io, val_scr, idx_scr, h, want_rank):
    rank = jnp.full(s.shape, float(PEER_TOPK), F32) if want_rank else None
    for r in range(PEER_TOPK):
        m = jnp.max(s, axis=0, keepdims=True)
        first = jnp.min(jnp.where(s == m, key_io, float(N_KEYS)), axis=0, keepdims=True)
        hit = key_io == first
        s = jnp.where(hit, -jnp.inf, s)
        val_scr[r, pl.ds(h, 1), :] = m
        if want_rank:
            rank = jnp.where(hit, float(r), rank)
        else:
            idx_scr[r, pl.ds(h, 1), :] = first
    return rank


def _router_kernel(hn_ref, wqt_ref, keys_ref, a0_ref, cnt_ref, r1_ref, b1_ref,
                   qr_scr, e1_scr, top0_scr, top1_scr, idx0_scr, cnt_scr, invz_scr):
    tp = hn_ref.shape[0]
    qr_scr[...] = _nt_dot(wqt_ref[...], hn_ref[...])
    key_io = lax.broadcasted_iota(jnp.int32, (N_KEYS, tp), 0).astype(F32)

    heads_per_trip = 8

    def scores_and_top16_group(hg, carry):
        for k in range(heads_per_trip):
            scores_and_top16(heads_per_trip * hg + k)
        return carry

    def scores_and_top16(h):
        base = pl.multiple_of(h * 2 * N_KEYS, 2 * N_KEYS)
        s0 = jnp.dot(keys_ref[2 * h], qr_scr[pl.ds(base, N_KEYS), :].astype(BF16),
                     preferred_element_type=F32)
        s1 = jnp.dot(keys_ref[2 * h + 1], qr_scr[pl.ds(base + N_KEYS, N_KEYS), :].astype(BF16),
                     preferred_element_type=F32)
        _extract_top16(s0, key_io, top0_scr, idx0_scr, h, False)
        rank1 = _extract_top16(s1, key_io, top1_scr, None, h, True)
        a0_ref[h] = jnp.exp(s0 - top0_scr[0, pl.ds(h, 1), :])
        e1_scr[h] = jnp.exp(s1 - top1_scr[0, pl.ds(h, 1), :])
        r1_ref[h] = rank1.astype(BF16)

    lax.fori_loop(0, PEER_HEADS // heads_per_trip, scores_and_top16_group, 0)

    top0 = [top0_scr[a] for a in range(PEER_TOPK)]
    top1 = [top1_scr[b] for b in range(PEER_TOPK)]
    cnt = [jnp.zeros((PEER_HEADS, tp), F32) for _ in range(PEER_TOPK)]
    for _ in range(PEER_TOPK):
        front = []
        for a in range(PEER_TOPK):
            nxt = jnp.full((PEER_HEADS, tp), -jnp.inf, F32)
            for bb in range(PEER_TOPK // (a + 1)):
                nxt = jnp.where(cnt[a] == float(bb), top1[bb], nxt)
            front.append(top0[a] + nxt)
        mx = functools.reduce(jnp.maximum, front)
        first = functools.reduce(jnp.minimum,
                                 [jnp.where(front[a] == mx, float(a), float(PEER_TOPK)) for a in range(PEER_TOPK)])
        cnt = [cnt[a] + jnp.where(first == float(a), 1.0, 0.0) for a in range(PEER_TOPK)]
    z = jnp.zeros((PEER_HEADS, tp), F32)
    for a in range(PEER_TOPK):
        za = jnp.zeros((PEER_HEADS, tp), F32)
        for bb in range(PEER_TOPK // (a + 1)):
            za = za + jnp.where(cnt[a] > float(bb), jnp.exp(top1[bb] - top1[0]), 0.0)
        z = z + jnp.exp(top0[a] - top0[0]) * za
        cnt_scr[a] = cnt[a]
    invz_scr[...] = 1.0 / z

    def counts_and_gates(h, carry):
        cnt_i = jnp.zeros((N_KEYS, tp), F32)
        for a in range(PEER_TOPK):
            cnt_i = jnp.where(key_io == idx0_scr[a, pl.ds(h, 1), :], cnt_scr[a, pl.ds(h, 1), :], cnt_i)
        cnt_ref[h] = cnt_i
        b1 = e1_scr[h] * invz_scr[pl.ds(h, 1), :]
        b1_ref[h] = b1.astype(BF16)
        return carry

    lax.fori_loop(0, PEER_HEADS, counts_and_gates, 0)


def _router(hn2d, wqt, keys):
    t = hn2d.shape[0]
    tp = LANES
    shape = jax.ShapeDtypeStruct((PEER_HEADS, N_KEYS, t), F32)
    spec = pl.BlockSpec((PEER_HEADS, N_KEYS, tp), lambda i: (0, 0, i))
    slab_shape = jax.ShapeDtypeStruct((PEER_HEADS, N_KEYS, t), BF16)
    slab_spec = spec
    return pl.pallas_call(
        _router_kernel,
        out_shape=(shape, shape, slab_shape, slab_shape),
        grid=(t // tp,),
        in_specs=[
            pl.BlockSpec((tp, D_MODEL), lambda i: (i, 0)),
            pl.BlockSpec(wqt.shape, lambda i: (0, 0)),
            pl.BlockSpec(keys.shape, lambda i: (0, 0, 0)),
        ],
        out_specs=(spec, spec, slab_spec, slab_spec),
        scratch_shapes=[
            pltpu.VMEM((wqt.shape[0], tp), F32),
            pltpu.VMEM((PEER_HEADS, N_KEYS, tp), F32),
        ] + [pltpu.VMEM((PEER_TOPK, PEER_HEADS, tp), F32) for _ in range(4)] + [pltpu.VMEM((PEER_HEADS, tp), F32)],
        compiler_params=_cparams(("parallel",)),
        name="peer_router",
    )(hn2d, wqt, keys)


def _gelu_tanh(x):
    k = -2.0 * 0.7978845608028654 * 1.4426950408889634
    return x / (1.0 + jnp.exp2(x * (k + (k * 0.044715) * (x * x))))


def _experts_kernel(hn_ref, u_ref, vt_ref, a0_ref, cnt_ref, r1_ref, b1_ref, x_ref, gfin_ref, out_ref,
                    acc_scr, pre0_scr, pre1_scr, w0_scr, w1_scr, *, rows_per_tile, n_tiles, final_norm):
    s = pl.program_id(1)
    tp = hn_ref.shape[0]
    te = u_ref.shape[0] // 2
    slab = 16
    slabs_per_row = N_KEYS // slab
    d_half = acc_scr.shape[0] // 2
    zero = jnp.zeros((), BF16)

    @pl.when(s == 0)
    def _():
        acc_scr[...] = jnp.zeros(acc_scr.shape, F32)
        pre1_scr[...] = jnp.zeros(pre1_scr.shape, F32)
        w0_scr[...] = jnp.zeros(w0_scr.shape, BF16)

    def pipeline_step(parity, pre_w, pre_r, w_w, w_r):
        e = 2 * s + parity
        cols = slice(parity * te, (parity + 1) * te)

        def stage_c(half):
            rows = slice(half * d_half, (half + 1) * d_half)
            acc_scr[rows, :] += jnp.dot(vt_ref[rows, cols], w_r[...], preferred_element_type=F32)

        def stage_a(half):
            rows = slice(half * (te // 2), (half + 1) * (te // 2))
            pre_w[rows, :] = _nt_dot(u_ref[parity * te + half * (te // 2):parity * te + (half + 1) * (te // 2), :],
                                     hn_ref[...])

        mxu_work = (lambda: stage_c(0), lambda: stage_a(0), lambda: stage_c(1), lambda: stage_a(1))

        valid = jnp.logical_and(e >= 1, e <= n_tiles)
        tile_b = jnp.clip(e - 1, 0, n_tiles - 1)
        for ib in range(rows_per_tile):
            mxu_work[ib]()
            i = tile_b * rows_per_tile + ib
            for lanes in (slice(0, tp // 2), slice(tp // 2, tp)):
                gates = [jnp.zeros((slab, tp // 2), BF16) for _ in range(slabs_per_row)]
                for h in range(PEER_HEADS):
                    a_b = jnp.broadcast_to(a0_ref[h, pl.ds(i, 1), lanes], (slab, tp // 2)).astype(BF16)
                    c_b = jnp.broadcast_to(cnt_ref[h, pl.ds(i, 1), lanes], (slab, tp // 2)).astype(BF16)
                    for g in range(slabs_per_row):
                        key_rows = slice(g * slab, (g + 1) * slab)
                        gates[g] = gates[g] + a_b * jnp.where(r1_ref[h, key_rows, lanes] < c_b,
                                                              b1_ref[h, key_rows, lanes], zero)
                for g in range(slabs_per_row):
                    rows = slice(ib * N_KEYS + g * slab, ib * N_KEYS + (g + 1) * slab)
                    w = gates[g] * _gelu_tanh(pre_r[rows, lanes]).astype(BF16)
                    w_w[rows, lanes] = jnp.where(valid, w, zero)

    pipeline_step(0, pre0_scr, pre1_scr, w1_scr, w0_scr)
    pipeline_step(1, pre1_scr, pre0_scr, w0_scr, w1_scr)

    @pl.when(s == n_tiles // 2)
    def _():
        xo = x_ref[...] + acc_scr[...].T
        if final_norm:
            xo = _rms(xo, gfin_ref[...])
        out_ref[...] = xo


def _experts(hn2d, u_bf, vt_bf, layer, a0, cnt, r1, b1, x2d, gfin, final_norm):
    t = hn2d.shape[0]
    tp = 512
    rows_per_tile = 4
    te = rows_per_tile * N_KEYS
    n_tiles = u_bf.shape[1] // te
    rspec = pl.BlockSpec((PEER_HEADS, N_KEYS, tp), lambda ti, e: (0, 0, ti))
    sspec = rspec
    return pl.pallas_call(
        functools.partial(_experts_kernel, rows_per_tile=rows_per_tile, n_tiles=n_tiles, final_norm=final_norm),
        out_shape=jax.ShapeDtypeStruct((t, D_MODEL), F32),
        grid=(t // tp, n_tiles // 2 + 1),
        in_specs=[
            pl.BlockSpec((tp, D_MODEL), lambda ti, s: (ti, 0)),
            pl.BlockSpec((None, 2 * te, D_MODEL), lambda ti, s: (layer, jnp.minimum(s, n_tiles // 2 - 1), 0)),
            pl.BlockSpec((None, D_MODEL, 2 * te), lambda ti, s: (layer, 0, jnp.maximum(s - 1, 0))),
            rspec, rspec, sspec, sspec,
            pl.BlockSpec((tp, D_MODEL), lambda ti, s: (ti, 0)),
            pl.BlockSpec((1, D_MODEL), lambda ti, s: (0, 0)),
        ],
        out_specs=pl.BlockSpec((tp, D_MODEL), lambda ti, s: (ti, 0)),
        scratch_shapes=[
            pltpu.VMEM((D_MODEL, tp), F32),
            pltpu.VMEM((te, tp), F32), pltpu.VMEM((te, tp), F32),
            pltpu.VMEM((te, tp), BF16), pltpu.VMEM((te, tp), BF16),
        ],
        compiler_params=_cparams(("parallel", "arbitrary"), EXPERT_FLAGS),
        name="peer_experts",
    )(hn2d, u_bf, vt_bf, a0, cnt, r1, b1, x2d, gfin)


def _prep_layer_weights(p):
    w_in = p['w_in']
    o_mi = 4 * D_M
    o_fq = o_mi + 2 * H_M
    o_ff = o_fq + 3 * D_FOX
    o_cq = o_ff + H_FOX
    w_big = jnp.concatenate([w_in[:, :o_mi], w_in[:, o_fq:o_ff], w_in[:, o_cq:]], axis=1).astype(BF16)
    w_big = w_big.reshape(D_MODEL, -1, PROJ_TILE).transpose(1, 0, 2)
    w_small = jnp.concatenate([w_in[:, o_mi:o_fq], w_in[:, o_ff:o_cq]], axis=1)
    w_small = jnp.pad(w_small, ((0, 0), (0, LANES - w_small.shape[1])))
    w_small_hi = w_small.astype(BF16)
    w_small = jnp.stack([w_small_hi, (w_small - w_small_hi.astype(F32)).astype(BF16)])
    bias = jnp.concatenate([p['b_m_i'], p['b_m_f'], p['b_fox_f']]).astype(F32)
    return dict(
        norm_mix=p['norm_mix'].reshape(1, D_MODEL), w_big=w_big, w_small=w_small,
        conv_w=p['conv_w'], conv_b=p['conv_b'].reshape(1, 2 * D_M),
        bias_row=jnp.pad(bias, (0, LANES - 16)).reshape(1, LANES), bias_col=bias.reshape(16, 1),
        norm_m_head=p['norm_m_head'].reshape(1, D_M),
        w_up_m=p['w_up_m'].astype(BF16), w_up_f=p['w_up_f'].astype(BF16), w_up_c=p['w_up_c'].astype(BF16),
        w_out=p['w_out'].astype(BF16), norm_ffn=p['norm_ffn'].reshape(1, D_MODEL),
        wqt=p['peer_wq'].T.astype(BF16),
        keys=p['peer_keys'].reshape(2 * PEER_HEADS, N_KEYS, N_KEYS).astype(BF16),
    )


def _layer(x, mem_k, mem_v, conv_prev, c0, n0, m0, fk_past, fv_past, flf_past, w, peer_tables, layer, gfin, final_norm):
    b, s, _ = x.shape
    t = b * s
    p0 = 0 if fk_past is None else fk_past.shape[1]
    qk, v_m, o_m, fq, fk, fv, cq, gates, small = _norm_proj(x.reshape(t, D_MODEL), w['norm_mix'], w['w_big'], w['w_small'])

    small3 = small.reshape(b, s, LANES)
    smallt = small3[:, :, :16].transpose(0, 2, 1)
    m0b = jnp.broadcast_to(m0[:, :, None, None], (b, H_M, 1, LANES))
    hm, c1, n1, m1b, conv_new = _mlstm(
        qk.reshape(b, s, 2 * D_M), v_m.reshape(b, s, D_M), o_m.reshape(b, s, D_M), small3, smallt,
        w['conv_w'], w['conv_b'], w['bias_row'], w['bias_col'], w['norm_m_head'],
        c0, n0[:, :, None, :], m0b, conv_prev)
    n1, m1 = n1[:, :, 0, :], m1b[:, :, 0, 0]

    k_f = fk.reshape(b, s, H_FOX, HD_FOX)
    v_f = fv.reshape(b, s, H_FOX, HD_FOX)
    s_pad = -(-s // LANES) * LANES
    pad_rows = lambda a: jnp.pad(a, ((0, 0), (0, s_pad - s), (0, 0)))
    past = flf_past.astype(F32) if p0 else None
    lf_all, hi, mid, lo = _fox_prep(past, pad_rows(small3), w['bias_row'], s)
    lf_f = lf_all[:, :s, FOX_GATE_LANE:FOX_GATE_LANE + H_FOX]
    sk = p0 + s_pad
    k_all, v_all = pad_rows(fk.reshape(b, s, D_FOX)), pad_rows(fv.reshape(b, s, D_FOX))
    if p0:
        k_all = jnp.concatenate([fk_past.reshape(b, p0, D_FOX).astype(F32), k_all], axis=1)
        v_all = jnp.concatenate([fv_past.reshape(b, p0, D_FOX).astype(F32), v_all], axis=1)
    qa = _fox_pack(pad_rows(fq.reshape(b, s, D_FOX)), hi, mid, lo, p0, True)[0]
    ka, vt_all = _fox_pack(k_all, hi, mid, lo, 0, False, v=v_all)
    o_f = _fox_attn(qa, ka, vt_all.reshape(b, H_FOX, HD_FOX, sk), p0)[:, :s]

    x_new, hn = _merge(x, hm, o_f, cq.reshape(b, s, D_MEM), gates.reshape(b, s, N_BRANCH * D_MODEL),
                       mem_k.reshape(b, -1, D_MEM).astype(BF16), mem_v.reshape(b, -1, D_MEM).astype(BF16),
                       w['w_up_m'], w['w_up_f'], w['w_up_c'], w['w_out'], w['norm_ffn'])

    hn2d = hn.reshape(t, D_MODEL)
    a0, cnt, r1, b1 = _router(hn2d, w['wqt'], w['keys'])
    x_out = _experts(hn2d, peer_tables[0], peer_tables[1], layer, a0, cnt, r1, b1, x_new.reshape(t, D_MODEL),
                     gfin, final_norm)
    return x_out.reshape(b, s, D_MODEL), conv_new, c1, n1, m1, k_f, v_f, lf_f


def kernel(x_prompt, x_sample, mem_prompt, cache_fox_k, cache_fox_v, cache_fox_lf, state_mlstm_c, state_mlstm_n, state_mlstm_m, state_conv, cache_mem_k, cache_mem_v, norm_mix, w_in, conv_w, conv_b, b_m_i, b_m_f, norm_m_head, b_fox_f, norm_mem, w_mem_kv, w_up_m, w_up_f, w_up_c, w_out, norm_ffn, peer_wq, peer_keys, peer_u, peer_v, norm_final):
    depth = w_in.shape[0]
    names = dict(norm_mix=norm_mix, w_in=w_in, conv_w=conv_w, conv_b=conv_b, b_m_i=b_m_i, b_m_f=b_m_f,
                 norm_m_head=norm_m_head, b_fox_f=b_fox_f, w_up_m=w_up_m, w_up_f=w_up_f, w_up_c=w_up_c,
                 w_out=w_out, norm_ffn=norm_ffn, peer_wq=peer_wq, peer_keys=peer_keys, peer_u=peer_u,
                 peer_v=peer_v)
    weights = [_prep_layer_weights({k: a[l] for k, a in names.items()}) for l in range(depth)]
    peer_tables = (peer_u.astype(BF16), jnp.swapaxes(peer_v, 1, 2).astype(BF16))
    gfin = norm_final.reshape(1, D_MODEL)
    bp, n_mem = mem_prompt.shape[0], mem_prompt.shape[1]

    xp = x_prompt
    new_p = [[] for _ in range(9)]
    for l in range(depth):
        mk, mv = _norm_matmul(mem_prompt.reshape(bp * n_mem, D_MODEL), norm_mem[l].reshape(1, D_MODEL),
                              w_mem_kv[l].astype(BF16))
        mk = mk.reshape(bp, n_mem, H_MEM, HD_MEM)
        mv = mv.reshape(bp, n_mem, H_MEM, HD_MEM)
        xp, conv1, c1, n1, m1, kf, vf, lff = _layer(
            xp, mk, mv, jnp.zeros((bp, CONV_W - 1, 2 * D_M), F32),
            jnp.zeros((bp, H_M, HD_M, HD_M), F32), jnp.zeros((bp, H_M, HD_M), F32), jnp.zeros((bp, H_M), F32),
            None, None, None, weights[l], peer_tables, l, gfin, l == depth - 1)
        for lst, a in zip(new_p, (kf, vf, lff, c1, n1, m1, conv1, mk, mv)):
            lst.append(a)
    outs_p = [jnp.stack(a) for a in new_p]

    xs = x_sample
    new_s = [[] for _ in range(7)]
    for l in range(depth):
        xs, conv1, c1, n1, m1, kf, vf, lff = _layer(
            xs, cache_mem_k[l], cache_mem_v[l], state_conv[l], state_mlstm_c[l], state_mlstm_n[l],
            state_mlstm_m[l], cache_fox_k[l], cache_fox_v[l], cache_fox_lf[l], weights[l], peer_tables, l, gfin,
            l == depth - 1)
        for lst, a in zip(new_s, (kf, vf, lff, c1, n1, m1, conv1)):
            lst.append(a)
    outs_s = [jnp.stack(a) for a in new_s]

    return (xp, xs, *outs_p, *outs_s)
```

```python
import functools

import jax
import jax.numpy as jnp
from jax import lax
from jax.experimental import pallas as pl
from jax.experimental.pallas import tpu as pltpu

F32 = jnp.float32
BF16 = jnp.bfloat16
EPS = 1e-6
NEG_BIG = -1e30

D_MODEL = 1024
H_M, HD_M = 4, 128
D_M = H_M * HD_M
CONV_W = 4
H_FOX, HD_FOX = 8, 64
D_FOX = H_FOX * HD_FOX
H_MEM, HD_MEM = 4, 128
D_MEM = H_MEM * HD_MEM
N_BRANCH = 3
PEER_HEADS = 8
N_KEYS = 128
PEER_TOPK = 16
LANES = 128
FOX_GATE_LANE = 2 * H_M
MLSTM_CHUNK = 256
PROJ_TILE = 512
VMEM_LIMIT = 56 * 1024 * 1024
EXPERT_FLAGS = None


def _cparams(sem, flags=None):
    return pltpu.CompilerParams(dimension_semantics=sem, vmem_limit_bytes=VMEM_LIMIT, flags=flags)


def _nt_dot(a, b):
    return lax.dot_general(a, b, (((1,), (1,)), ((), ())), preferred_element_type=F32)


def _rms(x, g):
    return x * lax.rsqrt(jnp.mean(x * x, axis=-1, keepdims=True) + EPS) * g


def _log_sigmoid(x):
    return jnp.minimum(x, 0.0) - jnp.log(1.0 + jnp.exp(-jnp.abs(x)))


def _sigmoid(x):
    return 1.0 / (1.0 + jnp.exp(-x))


def _norm_proj_kernel(x_ref, g_ref, w_ref, ws_ref, qk_ref, v_ref, o_ref, fq_ref, fk_ref, fv_ref,
                      cq_ref, gates_ref, small_ref, h_scr):
    j = pl.program_id(1)

    @pl.when(j == 0)
    def _():
        hf = _rms(x_ref[...], g_ref[...])
        h = hf.astype(BF16)
        h_scr[...] = h
        h_lo = (hf - h.astype(F32)).astype(BF16)
        small_ref[...] = (jnp.dot(h, ws_ref[0], preferred_element_type=F32)
                          + jnp.dot(h_lo, ws_ref[0], preferred_element_type=F32)
                          + jnp.dot(h, ws_ref[1], preferred_element_type=F32))

    def tile():
        return jnp.dot(h_scr[...], w_ref[0], preferred_element_type=F32)

    @pl.when(j == 0)
    def _():
        qk_ref[:, 0:PROJ_TILE] = tile()

    @pl.when(j == 1)
    def _():
        qk_ref[:, PROJ_TILE:2 * PROJ_TILE] = tile()

    @pl.when(j == 2)
    def _():
        v_ref[...] = tile().astype(BF16)

    @pl.when(j == 3)
    def _():
        o_ref[...] = tile()

    @pl.when(j == 4)
    def _():
        fq_ref[...] = tile().astype(BF16)

    @pl.when(j == 5)
    def _():
        fk_ref[...] = tile()

    @pl.when(j == 6)
    def _():
        fv_ref[...] = tile()

    @pl.when(j == 7)
    def _():
        cq_ref[...] = tile().astype(BF16)

    @pl.when(j >= 8)
    def _():
        gates_ref[...] = tile().astype(BF16)


def _norm_proj(x2d, g, w_big, w_small):
    t = x2d.shape[0]
    tm = 1024 if t % 1024 == 0 else 512
    n_tiles = w_big.shape[0]
    n_gate_tiles = n_tiles - 8
    row = lambda i, j: (i, 0)
    out_shape = (
        jax.ShapeDtypeStruct((t, 2 * D_M), F32),
        jax.ShapeDtypeStruct((t, D_M), BF16),
        jax.ShapeDtypeStruct((t, D_M), F32),
        jax.ShapeDtypeStruct((t, D_FOX), BF16),
        jax.ShapeDtypeStruct((t, D_FOX), F32),
        jax.ShapeDtypeStruct((t, D_FOX), F32),
        jax.ShapeDtypeStruct((t, D_MEM), BF16),
        jax.ShapeDtypeStruct((t, N_BRANCH * D_MODEL), BF16),
        jax.ShapeDtypeStruct((t, LANES), F32),
    )
    out_specs = (
        pl.BlockSpec((tm, 2 * D_M), row),
        pl.BlockSpec((tm, D_M), row),
        pl.BlockSpec((tm, D_M), row),
        pl.BlockSpec((tm, D_FOX), row),
        pl.BlockSpec((tm, D_FOX), row),
        pl.BlockSpec((tm, D_FOX), row),
        pl.BlockSpec((tm, D_MEM), row),
        pl.BlockSpec((tm, PROJ_TILE), lambda i, j: (i, jnp.clip(j - 8, 0, n_gate_tiles - 1))),
        pl.BlockSpec((tm, LANES), row),
    )
    return pl.pallas_call(
        _norm_proj_kernel,
        out_shape=out_shape,
        grid=(t // tm, n_tiles),
        in_specs=[
            pl.BlockSpec((tm, D_MODEL), row),
            pl.BlockSpec((1, D_MODEL), lambda i, j: (0, 0)),
            pl.BlockSpec((1, D_MODEL, PROJ_TILE), lambda i, j: (j, 0, 0)),
            pl.BlockSpec((2, D_MODEL, LANES), lambda i, j: (0, 0, 0)),
        ],
        out_specs=out_specs,
        scratch_shapes=[pltpu.VMEM((tm, D_MODEL), BF16)],
        compiler_params=_cparams(("parallel", "arbitrary")),
        name="norm_proj",
    )(x2d, g, w_big, w_small)


def _norm_matmul_kernel(x_ref, g_ref, w_ref, k_ref, v_ref):
    h = _rms(x_ref[...], g_ref[...]).astype(BF16)
    k_ref[...] = jnp.dot(h, w_ref[:, 0:D_MEM], preferred_element_type=F32)
    v_ref[...] = jnp.dot(h, w_ref[:, D_MEM:2 * D_MEM], preferred_element_type=F32)


def _norm_matmul(x2d, g, w):
    t = x2d.shape[0]
    tm = 512
    out = jax.ShapeDtypeStruct((t, D_MEM), F32)
    return pl.pallas_call(
        _norm_matmul_kernel,
        out_shape=(out, out),
        grid=(t // tm,),
        in_specs=[
            pl.BlockSpec((tm, D_MODEL), lambda i: (i, 0)),
            pl.BlockSpec((1, D_MODEL), lambda i: (0, 0)),
            pl.BlockSpec((D_MODEL, 2 * D_MEM), lambda i: (0, 0)),
        ],
        out_specs=(pl.BlockSpec((tm, D_MEM), lambda i: (i, 0)), pl.BlockSpec((tm, D_MEM), lambda i: (i, 0))),
        compiler_params=_cparams(("parallel",)),
        name="norm_matmul",
    )(x2d, g, w)


def _mlstm_kernel(qk_ref, v_ref, o_ref, small_ref, smallt_ref, convw_ref, convb_ref, brow_ref,
                  bcol_ref, ghead_ref, c0_ref, n0_ref, m0_ref, conv0_ref,
                  hm_ref, c_out_ref, n_out_ref, m_out_ref, conv_out_ref,
                  c_scr, n_scr, m_scr, xp_scr, *, chunk):
    L = chunk
    si = pl.program_id(1)

    @pl.when(si == 0)
    def _():
        c_scr[...] = c0_ref[0]
        n_scr[...] = n0_ref[0]
        m_scr[...] = m0_ref[0]
        xp_scr[5:8, :] = conv0_ref[0]

    xp_scr[8:8 + L, :] = qk_ref[0]
    y = convb_ref[...] + convw_ref[0:1, :] * xp_scr[5:5 + L, :]
    for j in range(1, CONV_W):
        y = y + convw_ref[j:j + 1, :] * xp_scr[5 + j:5 + j + L, :]
    y = y * _sigmoid(y)
    tail = xp_scr[5 + L:8 + L, :]
    xp_scr[5:8, :] = tail
    conv_out_ref[0] = tail

    t_io = lax.broadcasted_iota(jnp.int32, (L, L), 0)
    s_io = lax.broadcasted_iota(jnp.int32, (L, L), 1)
    causal = s_io <= t_io
    ones_col = (lax.broadcasted_iota(jnp.int32, (L, HD_M), 1) == 0).astype(BF16)

    small = small_ref[0]
    smallt = smallt_ref[0]
    brow = brow_ref[...]
    bcol = bcol_ref[...]
    outs = []
    for h in range(H_M):
        q = y[:, h * HD_M:(h + 1) * HD_M]
        k = y[:, D_M + h * HD_M:D_M + (h + 1) * HD_M] * (HD_M ** -0.5)
        v_aug = jnp.concatenate([v_ref[0, :, h * HD_M:(h + 1) * HD_M], ones_col], axis=1)
        ig_col = small[:, h:h + 1] + brow[:, h:h + 1]
        ig_row = smallt[h:h + 1, :] + bcol[h:h + 1, :]
        lf_col = _log_sigmoid(small[:, H_M + h:H_M + h + 1] + brow[:, H_M + h:H_M + h + 1])
        lf_row = _log_sigmoid(smallt[H_M + h:H_M + h + 1, :] + bcol[H_M + h:H_M + h + 1, :])
        bcum_col = jnp.sum(jnp.where(causal, lf_row, 0.0), axis=1, keepdims=True)
        bcum_row = jnp.sum(jnp.where(t_io <= s_io, lf_col, 0.0), axis=0, keepdims=True)
        m_prev = m_scr[h][:, 0:1]
        dmat = jnp.where(causal, bcum_col - bcum_row + ig_row, -jnp.inf)
        inter = bcum_col + m_prev
        m_t = jnp.maximum(inter, jnp.max(dmat, axis=1, keepdims=True))
        w_intra = jnp.exp(dmat - m_t)
        w_state = jnp.exp(inter - m_t)
        qb = q.astype(BF16)
        sw = _nt_dot(qb, k.astype(BF16)) * w_intra
        c_prev = c_scr[h]
        n_prev = n_scr[h]
        intra = jnp.dot(sw.astype(BF16), v_aug, preferred_element_type=F32)
        num = intra[:, 0:HD_M] + w_state * jnp.dot(qb, c_prev.astype(BF16), preferred_element_type=F32)
        den = intra[:, HD_M:HD_M + 1] + w_state * jnp.sum(q * n_prev, axis=1, keepdims=True)
        hh = num / jnp.maximum(jnp.abs(den), jnp.exp(-m_t))
        hh = _rms(hh, ghead_ref[:, h * HD_M:(h + 1) * HD_M])
        outs.append(hh * _sigmoid(o_ref[0, :, h * HD_M:(h + 1) * HD_M]))
        bl = bcum_row[:, L - 1:L]
        m_new = jnp.maximum(bl + m_prev, jnp.max(bl - bcum_row + ig_row, axis=1, keepdims=True))
        decay = jnp.exp(bl + m_prev - m_new)
        ws_col = jnp.exp(bl - bcum_col + ig_col - m_new)
        kw = k * ws_col
        c_scr[h] = decay * c_prev + jnp.dot(kw.T.astype(BF16), v_ref[0, :, h * HD_M:(h + 1) * HD_M],
                                            preferred_element_type=F32)
        n_scr[h] = decay * n_prev + jnp.sum(kw, axis=0, keepdims=True)
        m_scr[h] = jnp.broadcast_to(m_new, (1, LANES))
    hm_ref[0] = jnp.concatenate(outs, axis=1).astype(BF16)
    c_out_ref[0] = c_scr[...]
    n_out_ref[0] = n_scr[...]
    m_out_ref[0] = m_scr[...]


def _mlstm(qk, v, o, small, smallt, convw, convb, brow, bcol, ghead, c0, n0, m0b, conv0):
    b, s, _ = qk.shape
    chunk = min(s, MLSTM_CHUNK)
    tile = lambda w: pl.BlockSpec((1, chunk, w), lambda bi, si: (bi, si, 0))
    per_b = lambda shape: pl.BlockSpec((1,) + shape, lambda bi, si: (bi,) + (0,) * len(shape))
    const = lambda shape: pl.BlockSpec(shape, lambda bi, si: (0, 0))
    state_shapes = ((H_M, HD_M, HD_M), (H_M, 1, HD_M), (H_M, 1, LANES), (CONV_W - 1, 2 * D_M))
    return pl.pallas_call(
        functools.partial(_mlstm_kernel, chunk=chunk),
        out_shape=(jax.ShapeDtypeStruct((b, s, D_M), BF16),)
        + tuple(jax.ShapeDtypeStruct((b,) + sh, F32) for sh in state_shapes),
        grid=(b, s // chunk),
        in_specs=[
            tile(2 * D_M), tile(D_M), tile(D_M), tile(LANES),
            pl.BlockSpec((1, 16, chunk), lambda bi, si: (bi, 0, si)),
            const((CONV_W, 2 * D_M)), const((1, 2 * D_M)), const((1, LANES)), const((16, 1)), const((1, D_M)),
        ] + [per_b(sh) for sh in state_shapes],
        out_specs=(tile(D_M),) + tuple(per_b(sh) for sh in state_shapes),
        scratch_shapes=[
            pltpu.VMEM((H_M, HD_M, HD_M), F32),
            pltpu.VMEM((H_M, 1, HD_M), F32),
            pltpu.VMEM((H_M, 1, LANES), F32),
            pltpu.VMEM((chunk + 8, 2 * D_M), F32),
        ],
        compiler_params=_cparams(("parallel", "arbitrary")),
        name="mlstm",
    )(qk, v, o, small, smallt, convw, convb, brow, bcol, ghead, c0, n0, m0b, conv0)


def _split3(x):
    hi = x.astype(BF16).astype(F32)
    r = x - hi
    mid = r.astype(BF16).astype(F32)
    lo = (r - mid).astype(BF16).astype(F32)
    return hi, mid, lo


def _fox_prep_kernel(*refs, n_past, n_new, s_valid, blk):
    if n_past:
        past_ref, pre_ref, bias_ref, lf_ref, hi_ref, mid_ref, lo_ref = refs
    else:
        pre_ref, bias_ref, lf_ref, hi_ref, mid_ref, lo_ref = refs
    tri = (lax.broadcasted_iota(jnp.int32, (blk, blk), 1) <= lax.broadcasted_iota(jnp.int32, (blk, blk), 0)).astype(BF16)
    carry = jnp.zeros((1, LANES), F32)
    start = 0
    while start < n_past + n_new:
        if start < n_past:
            r = min(blk, n_past - start)
            lf = jnp.concatenate([jnp.zeros((r, FOX_GATE_LANE), F32), past_ref[0, start:start + r, :],
                                  jnp.zeros((r, LANES - FOX_GATE_LANE - H_FOX), F32)], axis=1)
        else:
            r = min(blk, n_past + n_new - start)
            ns = start - n_past
            row = ns + lax.broadcasted_iota(jnp.int32, (r, LANES), 0)
            lf = jnp.where(row < s_valid, _log_sigmoid(pre_ref[0, ns:ns + r, :] + bias_ref[...]), 0.0)
            lf_ref[0, ns:ns + r, :] = lf
        cum = carry
        for piece in _split3(lf):
            cum = cum + jnp.dot(tri[0:r, 0:r], piece.astype(BF16), preferred_element_type=F32)
        carry = cum[r - 1:r, :]
        hi, mid, lo = _split3(cum)
        hi_ref[0, start:start + r, :] = hi
        mid_ref[0, start:start + r, :] = mid
        lo_ref[0, start:start + r, :] = lo
        start += r


def _fox_prep(past, pre, bias_row, s_valid):
    b, n_new = pre.shape[:2]
    n_past = 0 if past is None else past.shape[1]
    n = n_past + n_new
    full = lambda rows: pl.BlockSpec((1, rows, LANES), lambda bi: (bi, 0, 0))
    past_spec = pl.BlockSpec((1, n_past, H_FOX), lambda bi: (bi, 0, 0))
    in_specs = ([past_spec] if n_past else []) + [full(n_new), pl.BlockSpec((1, LANES), lambda bi: (0, 0))]
    args = ([past] if n_past else []) + [pre, bias_row]
    cum_shape = jax.ShapeDtypeStruct((b, n, LANES), F32)
    return pl.pallas_call(
        functools.partial(_fox_prep_kernel, n_past=n_past, n_new=n_new, s_valid=s_valid, blk=256),
        out_shape=(jax.ShapeDtypeStruct((b, n_new, LANES), F32), cum_shape, cum_shape, cum_shape),
        grid=(b,),
        in_specs=in_specs,
        out_specs=(full(n_new), full(n), full(n), full(n)),
        compiler_params=_cparams(("parallel",)),
        name="fox_prep",
    )(*args)


def _fox_pack_kernel(*refs, q_side, with_v):
    if with_v:
        x_ref, hi_ref, mid_ref, lo_ref, v_ref, out_ref, vt_ref = refs
        vt_ref[0] = v_ref[0].T.astype(BF16)
    else:
        x_ref, hi_ref, mid_ref, lo_ref, out_ref = refs
    tm = x_ref.shape[1]
    lane = lax.broadcasted_iota(jnp.int32, (tm, HD_FOX), 1)
    hi, mid, lo = hi_ref[0], mid_ref[0], lo_ref[0]
    for h in range(H_FOX):
        c = FOX_GATE_LANE + h
        pieces = (hi[:, c:c + 1], mid[:, c:c + 1], lo[:, c:c + 1])
        first = 0 if q_side else 3
        bias = jnp.where(jnp.logical_and(lane >= 3 - first, lane < 6 - first), -1.0 if q_side else 1.0, 0.0)
        for j, p in enumerate(pieces):
            bias = jnp.where(lane == first + j, p, bias)
        out_ref[0, h] = jnp.concatenate(
            [x_ref[0, :, h * HD_FOX:(h + 1) * HD_FOX].astype(BF16), bias.astype(BF16)], axis=1)


def _fox_pack(x, hi, mid, lo, row_offset, q_side, v=None):
    b, n, _ = x.shape
    tm = next(c for c in (512, 384, 256, LANES) if n % c == 0 and row_offset % c == 0)
    off = row_offset // tm
    tile = lambda w: pl.BlockSpec((1, tm, w), lambda bi, i: (bi, i, 0))
    piece = pl.BlockSpec((1, tm, LANES), lambda bi, i: (bi, off + i, 0))
    out_shape = [jax.ShapeDtypeStruct((b, H_FOX, n, 2 * HD_FOX), BF16)]
    out_specs = [pl.BlockSpec((1, H_FOX, tm, 2 * HD_FOX), lambda bi, i: (bi, 0, i, 0))]
    in_specs = [tile(D_FOX), piece, piece, piece]
    args = [x, hi, mid, lo]
    if v is not None:
        in_specs.append(tile(D_FOX))
        args.append(v)
        out_shape.append(jax.ShapeDtypeStruct((b, D_FOX, n), BF16))
        out_specs.append(pl.BlockSpec((1, D_FOX, tm), lambda bi, i: (bi, 0, i)))
    return pl.pallas_call(
        functools.partial(_fox_pack_kernel, q_side=q_side, with_v=v is not None),
        out_shape=tuple(out_shape),
        grid=(b, n // tm),
        in_specs=in_specs,
        out_specs=tuple(out_specs),
        compiler_params=_cparams(("parallel", "parallel")),
        name="fox_pack_q" if q_side else "fox_pack_kv",
    )(*args)


def _fox_attn_kernel(qa_ref, ka_ref, vt_ref, out_ref, m_scr, l_scr, acc_scr, *, p0, tq, tk, nk):
    qi, kj = pl.program_id(1), pl.program_id(2)

    @pl.when(kj == 0)
    def _():
        m_scr[...] = jnp.full(m_scr.shape, NEG_BIG, F32)
        l_scr[...] = jnp.zeros(l_scr.shape, F32)
        acc_scr[...] = jnp.zeros(acc_scr.shape, F32)

    def step(masked):
        col = lax.broadcasted_iota(jnp.int32, (1, 2 * HD_FOX), 1)
        scale = jnp.where(col < HD_FOX, HD_FOX ** -0.5, 1.0).astype(BF16)
        if masked:
            kpos = kj * tk + lax.broadcasted_iota(jnp.int32, (tk, tq), 0)
            qpos = p0 + qi * tq + lax.broadcasted_iota(jnp.int32, (tk, tq), 1)
            mask = kpos <= qpos
        for h in range(H_FOX):
            s = _nt_dot(ka_ref[0, h], qa_ref[0, h] * scale)
            if masked:
                s = jnp.where(mask, s, NEG_BIG)
            m_prev = m_scr[h]
            m_new = jnp.maximum(m_prev, jnp.max(s, axis=0, keepdims=True))
            p = jnp.exp(s - m_new)
            alpha = jnp.exp(m_prev - m_new)
            l_scr[h] = alpha * l_scr[h] + jnp.sum(p, axis=0, keepdims=True)
            acc_scr[h] = alpha * acc_scr[h] + jnp.dot(vt_ref[0, h], p.astype(BF16), preferred_element_type=F32)
            m_scr[h] = m_new

    first_q = p0 + qi * tq
    last_q = first_q + tq - 1
    unmasked = (kj + 1) * tk - 1 <= first_q

    @pl.when(unmasked)
    def _():
        step(False)

    @pl.when(jnp.logical_and(jnp.logical_not(unmasked), kj * tk <= last_q))
    def _():
        step(True)

    @pl.when(kj == nk - 1)
    def _():
        o_t = jnp.concatenate([acc_scr[h] / l_scr[h] for h in range(H_FOX)], axis=0)
        out_ref[0] = o_t.T.astype(BF16)


def _fox_attn(qa, ka, vt, p0):
    b, _, sq, _ = qa.shape
    sk = ka.shape[2]
    tq = min(sq, 512)
    tk = 512 if sk % 512 == 0 else 384
    nq, nk = sq // tq, sk // tk
    last_tile = lambda qi: (p0 + (qi + 1) * tq - 1) // tk
    return pl.pallas_call(
        functools.partial(_fox_attn_kernel, p0=p0, tq=tq, tk=tk, nk=nk),
        out_shape=jax.ShapeDtypeStruct((b, sq, D_FOX), BF16),
        grid=(b, nq, nk),
        in_specs=[
            pl.BlockSpec((1, H_FOX, tq, 2 * HD_FOX), lambda bi, qi, kj: (bi, 0, qi, 0)),
            pl.BlockSpec((1, H_FOX, tk, 2 * HD_FOX), lambda bi, qi, kj: (bi, 0, jnp.minimum(kj, last_tile(qi)), 0)),
            pl.BlockSpec((1, H_FOX, HD_FOX, tk), lambda bi, qi, kj: (bi, 0, 0, jnp.minimum(kj, last_tile(qi)))),
        ],
        out_specs=pl.BlockSpec((1, tq, D_FOX), lambda bi, qi, kj: (bi, qi, 0)),
        scratch_shapes=[
            pltpu.VMEM((H_FOX, 1, tq), F32),
            pltpu.VMEM((H_FOX, 1, tq), F32),
            pltpu.VMEM((H_FOX, HD_FOX, tq), F32),
        ],
        compiler_params=_cparams(("parallel", "parallel", "arbitrary")),
        name="fox_attn",
    )(qa, ka, vt)


def _merge_kernel(x_ref, hm_ref, of_ref, cq_ref, gates_ref, mk_ref, mv_ref, wm_ref, wf_ref, wc_ref,
                  wo_ref, gffn_ref, xnew_ref, hn_ref, hnlo_ref):
    cq = cq_ref[0]
    heads = []
    for h in range(H_MEM):
        sl = slice(h * HD_MEM, (h + 1) * HD_MEM)
        s = _nt_dot(cq[:, sl], mk_ref[0, :, sl]) * (HD_MEM ** -0.5)
        p = jnp.exp(s - jnp.max(s, axis=1, keepdims=True))
        o = jnp.dot(p.astype(BF16), mv_ref[0, :, sl], preferred_element_type=F32)
        heads.append(o / jnp.sum(p, axis=1, keepdims=True))
    oc = jnp.concatenate(heads, axis=1).astype(BF16)
    a_m = jnp.dot(hm_ref[0], wm_ref[...], preferred_element_type=F32)
    a_f = jnp.dot(of_ref[0], wf_ref[...], preferred_element_type=F32)
    a_c = jnp.dot(oc, wc_ref[...], preferred_element_type=F32)
    g = gates_ref[0].astype(F32)
    merged = (_sigmoid(g[:, 0:D_MODEL]) * a_m + _sigmoid(g[:, D_MODEL:2 * D_MODEL]) * a_f
              + _sigmoid(g[:, 2 * D_MODEL:3 * D_MODEL]) * a_c)
    xn = x_ref[0] + jnp.dot(merged.astype(BF16), wo_ref[...], preferred_element_type=F32)
    xnew_ref[0] = xn
    hn = _rms(xn, gffn_ref[...])
    hn_hi = hn.astype(BF16)
    hn_ref[0] = hn_hi
    hnlo_ref[0] = (hn - hn_hi.astype(F32)).astype(BF16)


def _merge(x, hm, of, cq, gates, mk, mv, wm, wf, wc, wo, gffn):
    b, s, _ = x.shape
    ts = min(s, 256)
    tile = lambda w: pl.BlockSpec((1, ts, w), lambda bi, si: (bi, si, 0))
    const = lambda shape: pl.BlockSpec(shape, lambda bi, si: (0, 0))
    mem = pl.BlockSpec((1, mk.shape[1], D_MEM), lambda bi, si: (bi, 0, 0))
    return pl.pallas_call(
        _merge_kernel,
        out_shape=(jax.ShapeDtypeStruct((b, s, D_MODEL), F32), jax.ShapeDtypeStruct((b, s, D_MODEL), BF16),
                   jax.ShapeDtypeStruct((b, s, D_MODEL), BF16)),
        grid=(b, s // ts),
        in_specs=[tile(D_MODEL), tile(D_M), tile(D_FOX), tile(D_MEM), tile(N_BRANCH * D_MODEL), mem, mem,
                  const((D_M, D_MODEL)), const((D_FOX, D_MODEL)), const((D_MEM, D_MODEL)),
                  const((D_MODEL, D_MODEL)), const((1, D_MODEL))],
        out_specs=(tile(D_MODEL), tile(D_MODEL), tile(D_MODEL)),
        compiler_params=_cparams(("parallel", "parallel")),
        name="merge",
    )(x, hm, of, cq, gates, mk, mv, wm, wf, wc, wo, gffn)


def _extract_top16(s, key_io, val_scr, idx_scr, h, want_rank):
    rank = jnp.full(s.shape, float(PEER_TOPK), F32) if want_rank else None
    for r in range(PEER_TOPK):
        m = jnp.max(s, axis=0, keepdims=True)
        first = jnp.min(jnp.where(s == m, key_io, float(N_KEYS)), axis=0, keepdims=True)
        hit = key_io == first
        s = jnp.where(hit, -jnp.inf, s)
        val_scr[r, pl.ds(h, 1), :] = m
        if want_rank:
            rank = jnp.where(hit, float(r), rank)
        else:
            idx_scr[r, pl.ds(h, 1), :] = first
    return rank


def _dot3(a_hi, a_lo, b_hi, b_lo, dot):
    return dot(a_hi, b_hi) + dot(a_lo, b_hi) + dot(a_hi, b_lo)


def _router_kernel(hn_ref, hnlo_ref, wqt_ref, keys_ref, a0_ref, cnt_ref, r1_ref, b1_ref,
                   qr_scr, e1_scr, top0_scr, top1_scr, idx0_scr, cnt_scr, invz_scr):
    tp = hn_ref.shape[0]
    qr_scr[...] = _dot3(wqt_ref[0], wqt_ref[1], hn_ref[...], hnlo_ref[...], _nt_dot)
    key_io = lax.broadcasted_iota(jnp.int32, (N_KEYS, tp), 0).astype(F32)

    heads_per_trip = 8

    def scores_and_top16_group(hg, carry):
        for k in range(heads_per_trip):
            scores_and_top16(heads_per_trip * hg + k)
        return carry

    def scores_and_top16(h):
        base = pl.multiple_of(h * 2 * N_KEYS, 2 * N_KEYS)
        def scores(c):
            q = qr_scr[pl.ds(base + c * N_KEYS, N_KEYS), :]
            q_hi = q.astype(BF16)
            q_lo = (q - q_hi.astype(F32)).astype(BF16)
            return _dot3(keys_ref[0, 2 * h + c], keys_ref[1, 2 * h + c], q_hi, q_lo,
                         lambda a, b_: jnp.dot(a, b_, preferred_element_type=F32))

        s0, s1 = scores(0), scores(1)
        _extract_top16(s0, key_io, top0_scr, idx0_scr, h, False)
        rank1 = _extract_top16(s1, key_io, top1_scr, None, h, True)
        a0_ref[h] = jnp.exp(s0 - top0_scr[0, pl.ds(h, 1), :])
        e1_scr[h] = jnp.exp(s1 - top1_scr[0, pl.ds(h, 1), :])
        r1_ref[h] = rank1.astype(BF16)

    lax.fori_loop(0, PEER_HEADS // heads_per_trip, scores_and_top16_group, 0)

    top0 = [top0_scr[a] for a in range(PEER_TOPK)]
    top1 = [top1_scr[b] for b in range(PEER_TOPK)]
    cnt = [jnp.zeros((PEER_HEADS, tp), F32) for _ in range(PEER_TOPK)]
    for _ in range(PEER_TOPK):
        front = []
        for a in range(PEER_TOPK):
            nxt = jnp.full((PEER_HEADS, tp), -jnp.inf, F32)
            for bb in range(PEER_TOPK // (a + 1)):
                nxt = jnp.where(cnt[a] == float(bb), top1[bb], nxt)
            front.append(top0[a] + nxt)
        mx = functools.reduce(jnp.maximum, front)
        first = functools.reduce(jnp.minimum,
                                 [jnp.where(front[a] == mx, float(a), float(PEER_TOPK)) for a in range(PEER_TOPK)])
        cnt = [cnt[a] + jnp.where(first == float(a), 1.0, 0.0) for a in range(PEER_TOPK)]
    z = jnp.zeros((PEER_HEADS, tp), F32)
    for a in range(PEER_TOPK):
        za = jnp.zeros((PEER_HEADS, tp), F32)
        for bb in range(PEER_TOPK // (a + 1)):
            za = za + jnp.where(cnt[a] > float(bb), jnp.exp(top1[bb] - top1[0]), 0.0)
        z = z + jnp.exp(top0[a] - top0[0]) * za
        cnt_scr[a] = cnt[a]
    invz_scr[...] = 1.0 / z

    def counts_and_gates(h, carry):
        cnt_i = jnp.zeros((N_KEYS, tp), F32)
        for a in range(PEER_TOPK):
            cnt_i = jnp.where(key_io == idx0_scr[a, pl.ds(h, 1), :], cnt_scr[a, pl.ds(h, 1), :], cnt_i)
        cnt_ref[h] = cnt_i
        b1 = e1_scr[h] * invz_scr[pl.ds(h, 1), :]
        b1_ref[h] = b1.astype(BF16)
        return carry

    lax.fori_loop(0, PEER_HEADS, counts_and_gates, 0)


def _router(hn2d, hnlo2d, wqt, keys):
    t = hn2d.shape[0]
    tp = 2 * LANES
    shape = jax.ShapeDtypeStruct((PEER_HEADS, N_KEYS, t), F32)
    spec = pl.BlockSpec((PEER_HEADS, N_KEYS, tp), lambda i: (0, 0, i))
    slab_shape = jax.ShapeDtypeStruct((PEER_HEADS, N_KEYS, t), BF16)
    slab_spec = spec
    return pl.pallas_call(
        _router_kernel,
        out_shape=(shape, shape, slab_shape, slab_shape),
        grid=(t // tp,),
        in_specs=[
            pl.BlockSpec((tp, D_MODEL), lambda i: (i, 0)),
            pl.BlockSpec((tp, D_MODEL), lambda i: (i, 0)),
            pl.BlockSpec(wqt.shape, lambda i: (0, 0, 0)),
            pl.BlockSpec(keys.shape, lambda i: (0, 0, 0, 0)),
        ],
        out_specs=(spec, spec, slab_spec, slab_spec),
        scratch_shapes=[
            pltpu.VMEM((wqt.shape[1], tp), F32),
            pltpu.VMEM((PEER_HEADS, N_KEYS, tp), F32),
        ] + [pltpu.VMEM((PEER_TOPK, PEER_HEADS, tp), F32) for _ in range(4)] + [pltpu.VMEM((PEER_HEADS, tp), F32)],
        compiler_params=_cparams(("parallel",)),
        name="peer_router",
    )(hn2d, hnlo2d, wqt, keys)


def _gelu_tanh(x):
    k = -2.0 * 0.7978845608028654 * 1.4426950408889634
    return x / (1.0 + jnp.exp2(x * (k + (k * 0.044715) * (x * x))))


def _experts_kernel(hn_ref, u_ref, vt_ref, a0_ref, cnt_ref, r1_ref, b1_ref, x_ref, gfin_ref, out_ref,
                    acc_scr, pre0_scr, pre1_scr, w0_scr, w1_scr, *, rows_per_tile, n_tiles, final_norm):
    s = pl.program_id(1)
    tp = hn_ref.shape[0]
    te = u_ref.shape[0] // 2
    slab = 16
    slabs_per_row = N_KEYS // slab
    d_half = acc_scr.shape[0] // 2
    zero = jnp.zeros((), BF16)

    @pl.when(s == 0)
    def _():
        acc_scr[...] = jnp.zeros(acc_scr.shape, F32)
        pre1_scr[...] = jnp.zeros(pre1_scr.shape, F32)
        w0_scr[...] = jnp.zeros(w0_scr.shape, BF16)

    def pipeline_step(parity, pre_w, pre_r, w_w, w_r):
        e = 2 * s + parity
        cols = slice(parity * te, (parity + 1) * te)

        def stage_c(half):
            rows = slice(half * d_half, (half + 1) * d_half)
            acc_scr[rows, :] += jnp.dot(vt_ref[rows, cols], w_r[...], preferred_element_type=F32)

        def stage_a(half):
            rows = slice(half * (te // 2), (half + 1) * (te // 2))
            pre_w[rows, :] = _nt_dot(u_ref[parity * te + half * (te // 2):parity * te + (half + 1) * (te // 2), :],
                                     hn_ref[...])

        mxu_work = (lambda: stage_c(0), lambda: stage_a(0), lambda: stage_c(1), lambda: stage_a(1))

        valid = jnp.logical_and(e >= 1, e <= n_tiles)
        tile_b = jnp.clip(e - 1, 0, n_tiles - 1)
        for ib in range(rows_per_tile):
            mxu_work[ib]()
            i = tile_b * rows_per_tile + ib
            for lanes in (slice(0, tp // 2), slice(tp // 2, tp)):
                gates = [jnp.zeros((slab, tp // 2), BF16) for _ in range(slabs_per_row)]
                for h in range(PEER_HEADS):
                    a_b = jnp.broadcast_to(a0_ref[h, pl.ds(i, 1), lanes], (slab, tp // 2)).astype(BF16)
                    c_b = jnp.broadcast_to(cnt_ref[h, pl.ds(i, 1), lanes], (slab, tp // 2)).astype(BF16)
                    for g in range(slabs_per_row):
                        key_rows = slice(g * slab, (g + 1) * slab)
                        gates[g] = gates[g] + a_b * jnp.where(r1_ref[h, key_rows, lanes] < c_b,
                                                              b1_ref[h, key_rows, lanes], zero)
                for g in range(slabs_per_row):
                    rows = slice(ib * N_KEYS + g * slab, ib * N_KEYS + (g + 1) * slab)
                    w = gates[g] * _gelu_tanh(pre_r[rows, lanes]).astype(BF16)
                    w_w[rows, lanes] = jnp.where(valid, w, zero)

    pipeline_step(0, pre0_scr, pre1_scr, w1_scr, w0_scr)
    pipeline_step(1, pre1_scr, pre0_scr, w0_scr, w1_scr)

    @pl.when(s == n_tiles // 2)
    def _():
        xo = x_ref[...] + acc_scr[...].T
        if final_norm:
            xo = _rms(xo, gfin_ref[...])
        out_ref[...] = xo


def _experts(hn2d, u_bf, vt_bf, layer, a0, cnt, r1, b1, x2d, gfin, final_norm):
    t = hn2d.shape[0]
    tp = 512
    rows_per_tile = 4
    te = rows_per_tile * N_KEYS
    n_tiles = u_bf.shape[1] // te
    rspec = pl.BlockSpec((PEER_HEADS, N_KEYS, tp), lambda ti, e: (0, 0, ti))
    sspec = rspec
    return pl.pallas_call(
        functools.partial(_experts_kernel, rows_per_tile=rows_per_tile, n_tiles=n_tiles, final_norm=final_norm),
        out_shape=jax.ShapeDtypeStruct((t, D_MODEL), F32),
        grid=(t // tp, n_tiles // 2 + 1),
        in_specs=[
            pl.BlockSpec((tp, D_MODEL), lambda ti, s: (ti, 0)),
            pl.BlockSpec((None, 2 * te, D_MODEL), lambda ti, s: (layer, jnp.minimum(s, n_tiles // 2 - 1), 0)),
            pl.BlockSpec((None, D_MODEL, 2 * te), lambda ti, s: (layer, 0, jnp.maximum(s - 1, 0))),
            rspec, rspec, sspec, sspec,
            pl.BlockSpec((tp, D_MODEL), lambda ti, s: (ti, 0)),
            pl.BlockSpec((1, D_MODEL), lambda ti, s: (0, 0)),
        ],
        out_specs=pl.BlockSpec((tp, D_MODEL), lambda ti, s: (ti, 0)),
        scratch_shapes=[
            pltpu.VMEM((D_MODEL, tp), F32),
            pltpu.VMEM((te, tp), F32), pltpu.VMEM((te, tp), F32),
            pltpu.VMEM((te, tp), BF16), pltpu.VMEM((te, tp), BF16),
        ],
        compiler_params=_cparams(("parallel", "arbitrary"), EXPERT_FLAGS),
        name="peer_experts",
    )(hn2d, u_bf, vt_bf, a0, cnt, r1, b1, x2d, gfin)


def _hi_lo(a):
    hi = a.astype(BF16)
    return jnp.stack([hi, (a - hi.astype(F32)).astype(BF16)])


def _prep_layer_weights(p):
    w_in = p['w_in']
    o_mi = 4 * D_M
    o_fq = o_mi + 2 * H_M
    o_ff = o_fq + 3 * D_FOX
    o_cq = o_ff + H_FOX
    w_big = jnp.concatenate([w_in[:, :o_mi], w_in[:, o_fq:o_ff], w_in[:, o_cq:]], axis=1).astype(BF16)
    w_big = w_big.reshape(D_MODEL, -1, PROJ_TILE).transpose(1, 0, 2)
    w_small = jnp.concatenate([w_in[:, o_mi:o_fq], w_in[:, o_ff:o_cq]], axis=1)
    w_small = jnp.pad(w_small, ((0, 0), (0, LANES - w_small.shape[1])))
    w_small_hi = w_small.astype(BF16)
    w_small = jnp.stack([w_small_hi, (w_small - w_small_hi.astype(F32)).astype(BF16)])
    bias = jnp.concatenate([p['b_m_i'], p['b_m_f'], p['b_fox_f']]).astype(F32)
    return dict(
        norm_mix=p['norm_mix'].reshape(1, D_MODEL), w_big=w_big, w_small=w_small,
        conv_w=p['conv_w'], conv_b=p['conv_b'].reshape(1, 2 * D_M),
        bias_row=jnp.pad(bias, (0, LANES - 16)).reshape(1, LANES), bias_col=bias.reshape(16, 1),
        norm_m_head=p['norm_m_head'].reshape(1, D_M),
        w_up_m=p['w_up_m'].astype(BF16), w_up_f=p['w_up_f'].astype(BF16), w_up_c=p['w_up_c'].astype(BF16),
        w_out=p['w_out'].astype(BF16), norm_ffn=p['norm_ffn'].reshape(1, D_MODEL),
        wqt=_hi_lo(p['peer_wq'].T),
        keys=_hi_lo(p['peer_keys'].reshape(2 * PEER_HEADS, N_KEYS, N_KEYS)),
    )


def _layer(x, mem_k, mem_v, conv_prev, c0, n0, m0, fk_past, fv_past, flf_past, w, peer_tables, layer, gfin, final_norm):
    b, s, _ = x.shape
    t = b * s
    p0 = 0 if fk_past is None else fk_past.shape[1]
    qk, v_m, o_m, fq, fk, fv, cq, gates, small = _norm_proj(x.reshape(t, D_MODEL), w['norm_mix'], w['w_big'], w['w_small'])

    small3 = small.reshape(b, s, LANES)
    smallt = small3[:, :, :16].transpose(0, 2, 1)
    m0b = jnp.broadcast_to(m0[:, :, None, None], (b, H_M, 1, LANES))
    hm, c1, n1, m1b, conv_new = _mlstm(
        qk.reshape(b, s, 2 * D_M), v_m.reshape(b, s, D_M), o_m.reshape(b, s, D_M), small3, smallt,
        w['conv_w'], w['conv_b'], w['bias_row'], w['bias_col'], w['norm_m_head'],
        c0, n0[:, :, None, :], m0b, conv_prev)
    n1, m1 = n1[:, :, 0, :], m1b[:, :, 0, 0]

    k_f = fk.reshape(b, s, H_FOX, HD_FOX)
    v_f = fv.reshape(b, s, H_FOX, HD_FOX)
    s_pad = -(-s // LANES) * LANES
    pad_rows = lambda a: jnp.pad(a, ((0, 0), (0, s_pad - s), (0, 0)))
    past = flf_past.astype(F32) if p0 else None
    lf_all, hi, mid, lo = _fox_prep(past, pad_rows(small3), w['bias_row'], s)
    lf_f = lf_all[:, :s, FOX_GATE_LANE:FOX_GATE_LANE + H_FOX]
    sk = p0 + s_pad
    k_all, v_all = pad_rows(fk.reshape(b, s, D_FOX)), pad_rows(fv.reshape(b, s, D_FOX))
    if p0:
        k_all = jnp.concatenate([fk_past.reshape(b, p0, D_FOX).astype(F32), k_all], axis=1)
        v_all = jnp.concatenate([fv_past.reshape(b, p0, D_FOX).astype(F32), v_all], axis=1)
    qa = _fox_pack(pad_rows(fq.reshape(b, s, D_FOX)), hi, mid, lo, p0, True)[0]
    ka, vt_all = _fox_pack(k_all, hi, mid, lo, 0, False, v=v_all)
    o_f = _fox_attn(qa, ka, vt_all.reshape(b, H_FOX, HD_FOX, sk), p0)[:, :s]

    x_new, hn, hn_lo = _merge(x, hm, o_f, cq.reshape(b, s, D_MEM), gates.reshape(b, s, N_BRANCH * D_MODEL),
                       mem_k.reshape(b, -1, D_MEM).astype(BF16), mem_v.reshape(b, -1, D_MEM).astype(BF16),
                       w['w_up_m'], w['w_up_f'], w['w_up_c'], w['w_out'], w['norm_ffn'])

    hn2d = hn.reshape(t, D_MODEL)
    a0, cnt, r1, b1 = _router(hn2d, hn_lo.reshape(t, D_MODEL), w['wqt'], w['keys'])
    x_out = _experts(hn2d, peer_tables[0], peer_tables[1], layer, a0, cnt, r1, b1, x_new.reshape(t, D_MODEL),
                     gfin, final_norm)
    return x_out.reshape(b, s, D_MODEL), conv_new, c1, n1, m1, k_f, v_f, lf_f


def kernel(x_prompt, x_sample, mem_prompt, cache_fox_k, cache_fox_v, cache_fox_lf, state_mlstm_c, state_mlstm_n, state_mlstm_m, state_conv, cache_mem_k, cache_mem_v, norm_mix, w_in, conv_w, conv_b, b_m_i, b_m_f, norm_m_head, b_fox_f, norm_mem, w_mem_kv, w_up_m, w_up_f, w_up_c, w_out, norm_ffn, peer_wq, peer_keys, peer_u, peer_v, norm_final):
    depth = w_in.shape[0]
    names = dict(norm_mix=norm_mix, w_in=w_in, conv_w=conv_w, conv_b=conv_b, b_m_i=b_m_i, b_m_f=b_m_f,
                 norm_m_head=norm_m_head, b_fox_f=b_fox_f, w_up_m=w_up_m, w_up_f=w_up_f, w_up_c=w_up_c,
                 w_out=w_out, norm_ffn=norm_ffn, peer_wq=peer_wq, peer_keys=peer_keys, peer_u=peer_u,
                 peer_v=peer_v)
    weights = [_prep_layer_weights({k: a[l] for k, a in names.items()}) for l in range(depth)]
    peer_tables = (peer_u.astype(BF16), jnp.swapaxes(peer_v, 1, 2).astype(BF16))
    gfin = norm_final.reshape(1, D_MODEL)
    bp, n_mem = mem_prompt.shape[0], mem_prompt.shape[1]

    xp = x_prompt
    new_p = [[] for _ in range(9)]
    for l in range(depth):
        mk, mv = _norm_matmul(mem_prompt.reshape(bp * n_mem, D_MODEL), norm_mem[l].reshape(1, D_MODEL),
                              w_mem_kv[l].astype(BF16))
        mk = mk.reshape(bp, n_mem, H_MEM, HD_MEM)
        mv = mv.reshape(bp, n_mem, H_MEM, HD_MEM)
        xp, conv1, c1, n1, m1, kf, vf, lff = _layer(
            xp, mk, mv, jnp.zeros((bp, CONV_W - 1, 2 * D_M), F32),
            jnp.zeros((bp, H_M, HD_M, HD_M), F32), jnp.zeros((bp, H_M, HD_M), F32), jnp.zeros((bp, H_M), F32),
            None, None, None, weights[l], peer_tables, l, gfin, l == depth - 1)
        for lst, a in zip(new_p, (kf, vf, lff, c1, n1, m1, conv1, mk, mv)):
            lst.append(a)
    outs_p = [jnp.stack(a) for a in new_p]

    xs = x_sample
    new_s = [[] for _ in range(7)]
    for l in range(depth):
        xs, conv1, c1, n1, m1, kf, vf, lff = _layer(
            xs, cache_mem_k[l], cache_mem_v[l], state_conv[l], state_mlstm_c[l], state_mlstm_n[l],
            state_mlstm_m[l], cache_fox_k[l], cache_fox_v[l], cache_fox_lf[l], weights[l], peer_tables, l, gfin,
            l == depth - 1)
        for lst, a in zip(new_s, (kf, vf, lff, c1, n1, m1, conv1)):
            lst.append(a)
    outs_s = [jnp.stack(a) for a in new_s]

    return (xp, xs, *outs_p, *outs_s)
```

```python
import functools

import jax
import jax.numpy as jnp
from jax import lax
from jax.experimental import pallas as pl
from jax.experimental.pallas import tpu as pltpu

F32 = jnp.float32
BF16 = jnp.bfloat16
EPS = 1e-6
NEG_BIG = -1e30

D_MODEL = 1024
H_M, HD_M = 4, 128
D_M = H_M * HD_M
CONV_W = 4
H_FOX, HD_FOX = 8, 64
D_FOX = H_FOX * HD_FOX
H_MEM, HD_MEM = 4, 128
D_MEM = H_MEM * HD_MEM
N_BRANCH = 3
PEER_HEADS = 8
N_KEYS = 128
PEER_TOPK = 16
LANES = 128
FOX_GATE_LANE = 2 * H_M
MLSTM_CHUNK = 256
PROJ_TILE = 512
VMEM_LIMIT = 56 * 1024 * 1024
EXPERT_FLAGS = None


def _cparams(sem, flags=None):
    return pltpu.CompilerParams(dimension_semantics=sem, vmem_limit_bytes=VMEM_LIMIT, flags=flags)


def _nt_dot(a, b):
    return lax.dot_general(a, b, (((1,), (1,)), ((), ())), preferred_element_type=F32)


def _rms(x, g):
    return x * lax.rsqrt(jnp.mean(x * x, axis=-1, keepdims=True) + EPS) * g


def _log_sigmoid(x):
    return jnp.minimum(x, 0.0) - jnp.log(1.0 + jnp.exp(-jnp.abs(x)))


def _sigmoid(x):
    return 1.0 / (1.0 + jnp.exp(-x))


def _norm_proj_kernel(x_ref, g_ref, w_ref, ws_ref, qk_ref, v_ref, o_ref, fq_ref, fk_ref, fv_ref,
                      cq_ref, gates_ref, small_ref, h_scr):
    j = pl.program_id(1)

    @pl.when(j == 0)
    def _():
        hf = _rms(x_ref[...], g_ref[...])
        h = hf.astype(BF16)
        h_scr[...] = h
        h_lo = (hf - h.astype(F32)).astype(BF16)
        small_ref[...] = (jnp.dot(h, ws_ref[0], preferred_element_type=F32)
                          + jnp.dot(h_lo, ws_ref[0], preferred_element_type=F32)
                          + jnp.dot(h, ws_ref[1], preferred_element_type=F32))

    def tile():
        return jnp.dot(h_scr[...], w_ref[0], preferred_element_type=F32)

    @pl.when(j == 0)
    def _():
        qk_ref[:, 0:PROJ_TILE] = tile()

    @pl.when(j == 1)
    def _():
        qk_ref[:, PROJ_TILE:2 * PROJ_TILE] = tile()

    @pl.when(j == 2)
    def _():
        v_ref[...] = tile().astype(BF16)

    @pl.when(j == 3)
    def _():
        o_ref[...] = tile()

    @pl.when(j == 4)
    def _():
        fq_ref[...] = tile().astype(BF16)

    @pl.when(j == 5)
    def _():
        fk_ref[...] = tile()

    @pl.when(j == 6)
    def _():
        fv_ref[...] = tile()

    @pl.when(j == 7)
    def _():
        cq_ref[...] = tile().astype(BF16)

    @pl.when(j >= 8)
    def _():
        gates_ref[...] = tile().astype(BF16)


def _norm_proj(x2d, g, w_big, w_small):
    t = x2d.shape[0]
    tm = 1024 if t % 1024 == 0 else 512
    n_tiles = w_big.shape[0]
    n_gate_tiles = n_tiles - 8
    row = lambda i, j: (i, 0)
    out_shape = (
        jax.ShapeDtypeStruct((t, 2 * D_M), F32),
        jax.ShapeDtypeStruct((t, D_M), BF16),
        jax.ShapeDtypeStruct((t, D_M), F32),
        jax.ShapeDtypeStruct((t, D_FOX), BF16),
        jax.ShapeDtypeStruct((t, D_FOX), F32),
        jax.ShapeDtypeStruct((t, D_FOX), F32),
        jax.ShapeDtypeStruct((t, D_MEM), BF16),
        jax.ShapeDtypeStruct((t, N_BRANCH * D_MODEL), BF16),
        jax.ShapeDtypeStruct((t, LANES), F32),
    )
    out_specs = (
        pl.BlockSpec((tm, 2 * D_M), row),
        pl.BlockSpec((tm, D_M), row),
        pl.BlockSpec((tm, D_M), row),
        pl.BlockSpec((tm, D_FOX), row),
        pl.BlockSpec((tm, D_FOX), row),
        pl.BlockSpec((tm, D_FOX), row),
        pl.BlockSpec((tm, D_MEM), row),
        pl.BlockSpec((tm, PROJ_TILE), lambda i, j: (i, jnp.clip(j - 8, 0, n_gate_tiles - 1))),
        pl.BlockSpec((tm, LANES), row),
    )
    return pl.pallas_call(
        _norm_proj_kernel,
        out_shape=out_shape,
        grid=(t // tm, n_tiles),
        in_specs=[
            pl.BlockSpec((tm, D_MODEL), row),
            pl.BlockSpec((1, D_MODEL), lambda i, j: (0, 0)),
            pl.BlockSpec((1, D_MODEL, PROJ_TILE), lambda i, j: (j, 0, 0)),
            pl.BlockSpec((2, D_MODEL, LANES), lambda i, j: (0, 0, 0)),
        ],
        out_specs=out_specs,
        scratch_shapes=[pltpu.VMEM((tm, D_MODEL), BF16)],
        compiler_params=_cparams(("parallel", "arbitrary")),
        name="norm_proj",
    )(x2d, g, w_big, w_small)


def _norm_matmul_kernel(x_ref, g_ref, w_ref, k_ref, v_ref):
    h = _rms(x_ref[...], g_ref[...]).astype(BF16)
    k_ref[...] = jnp.dot(h, w_ref[:, 0:D_MEM], preferred_element_type=F32)
    v_ref[...] = jnp.dot(h, w_ref[:, D_MEM:2 * D_MEM], preferred_element_type=F32)


def _norm_matmul(x2d, g, w):
    t = x2d.shape[0]
    tm = 512
    out = jax.ShapeDtypeStruct((t, D_MEM), F32)
    return pl.pallas_call(
        _norm_matmul_kernel,
        out_shape=(out, out),
        grid=(t // tm,),
        in_specs=[
            pl.BlockSpec((tm, D_MODEL), lambda i: (i, 0)),
            pl.BlockSpec((1, D_MODEL), lambda i: (0, 0)),
            pl.BlockSpec((D_MODEL, 2 * D_MEM), lambda i: (0, 0)),
        ],
        out_specs=(pl.BlockSpec((tm, D_MEM), lambda i: (i, 0)), pl.BlockSpec((tm, D_MEM), lambda i: (i, 0))),
        compiler_params=_cparams(("parallel",)),
        name="norm_matmul",
    )(x2d, g, w)


def _mlstm_kernel(qk_ref, v_ref, o_ref, small_ref, smallt_ref, convw_ref, convb_ref, brow_ref,
                  bcol_ref, ghead_ref, c0_ref, n0_ref, m0_ref, conv0_ref,
                  hm_ref, c_out_ref, n_out_ref, m_out_ref, conv_out_ref,
                  c_scr, n_scr, m_scr, xp_scr, *, chunk):
    L = chunk
    si = pl.program_id(1)

    @pl.when(si == 0)
    def _():
        c_scr[...] = c0_ref[0]
        n_scr[...] = n0_ref[0]
        m_scr[...] = m0_ref[0]
        xp_scr[5:8, :] = conv0_ref[0]

    xp_scr[8:8 + L, :] = qk_ref[0]
    y = convb_ref[...] + convw_ref[0:1, :] * xp_scr[5:5 + L, :]
    for j in range(1, CONV_W):
        y = y + convw_ref[j:j + 1, :] * xp_scr[5 + j:5 + j + L, :]
    y = y * _sigmoid(y)
    tail = xp_scr[5 + L:8 + L, :]
    xp_scr[5:8, :] = tail
    conv_out_ref[0] = tail

    t_io = lax.broadcasted_iota(jnp.int32, (L, L), 0)
    s_io = lax.broadcasted_iota(jnp.int32, (L, L), 1)
    causal = s_io <= t_io
    ones_col = (lax.broadcasted_iota(jnp.int32, (L, HD_M), 1) == 0).astype(BF16)

    small = small_ref[0]
    smallt = smallt_ref[0]
    brow = brow_ref[...]
    bcol = bcol_ref[...]
    outs = []
    for h in range(H_M):
        q = y[:, h * HD_M:(h + 1) * HD_M]
        k = y[:, D_M + h * HD_M:D_M + (h + 1) * HD_M] * (HD_M ** -0.5)
        v_aug = jnp.concatenate([v_ref[0, :, h * HD_M:(h + 1) * HD_M], ones_col], axis=1)
        ig_col = small[:, h:h + 1] + brow[:, h:h + 1]
        ig_row = smallt[h:h + 1, :] + bcol[h:h + 1, :]
        lf_col = _log_sigmoid(small[:, H_M + h:H_M + h + 1] + brow[:, H_M + h:H_M + h + 1])
        lf_row = _log_sigmoid(smallt[H_M + h:H_M + h + 1, :] + bcol[H_M + h:H_M + h + 1, :])
        bcum_col = jnp.sum(jnp.where(causal, lf_row, 0.0), axis=1, keepdims=True)
        bcum_row = jnp.sum(jnp.where(t_io <= s_io, lf_col, 0.0), axis=0, keepdims=True)
        m_prev = m_scr[h][:, 0:1]
        dmat = jnp.where(causal, bcum_col - bcum_row + ig_row, -jnp.inf)
        inter = bcum_col + m_prev
        m_t = jnp.maximum(inter, jnp.max(dmat, axis=1, keepdims=True))
        w_intra = jnp.exp(dmat - m_t)
        w_state = jnp.exp(inter - m_t)
        qb = q.astype(BF16)
        sw = _nt_dot(qb, k.astype(BF16)) * w_intra
        c_prev = c_scr[h]
        n_prev = n_scr[h]
        intra = jnp.dot(sw.astype(BF16), v_aug, preferred_element_type=F32)
        num = intra[:, 0:HD_M] + w_state * jnp.dot(qb, c_prev.astype(BF16), preferred_element_type=F32)
        den = intra[:, HD_M:HD_M + 1] + w_state * jnp.sum(q * n_prev, axis=1, keepdims=True)
        hh = num / jnp.maximum(jnp.abs(den), jnp.exp(-m_t))
        hh = _rms(hh, ghead_ref[:, h * HD_M:(h + 1) * HD_M])
        outs.append(hh * _sigmoid(o_ref[0, :, h * HD_M:(h + 1) * HD_M]))
        bl = bcum_row[:, L - 1:L]
        m_new = jnp.maximum(bl + m_prev, jnp.max(bl - bcum_row + ig_row, axis=1, keepdims=True))
        decay = jnp.exp(bl + m_prev - m_new)
        ws_col = jnp.exp(bl - bcum_col + ig_col - m_new)
        kw = k * ws_col
        c_scr[h] = decay * c_prev + jnp.dot(kw.T.astype(BF16), v_ref[0, :, h * HD_M:(h + 1) * HD_M],
                                            preferred_element_type=F32)
        n_scr[h] = decay * n_prev + jnp.sum(kw, axis=0, keepdims=True)
        m_scr[h] = jnp.broadcast_to(m_new, (1, LANES))
    hm_ref[0] = jnp.concatenate(outs, axis=1).astype(BF16)
    c_out_ref[0] = c_scr[...]
    n_out_ref[0] = n_scr[...]
    m_out_ref[0] = m_scr[...]


def _mlstm(qk, v, o, small, smallt, convw, convb, brow, bcol, ghead, c0, n0, m0b, conv0):
    b, s, _ = qk.shape
    chunk = min(s, MLSTM_CHUNK)
    tile = lambda w: pl.BlockSpec((1, chunk, w), lambda bi, si: (bi, si, 0))
    per_b = lambda shape: pl.BlockSpec((1,) + shape, lambda bi, si: (bi,) + (0,) * len(shape))
    const = lambda shape: pl.BlockSpec(shape, lambda bi, si: (0, 0))
    state_shapes = ((H_M, HD_M, HD_M), (H_M, 1, HD_M), (H_M, 1, LANES), (CONV_W - 1, 2 * D_M))
    return pl.pallas_call(
        functools.partial(_mlstm_kernel, chunk=chunk),
        out_shape=(jax.ShapeDtypeStruct((b, s, D_M), BF16),)
        + tuple(jax.ShapeDtypeStruct((b,) + sh, F32) for sh in state_shapes),
        grid=(b, s // chunk),
        in_specs=[
            tile(2 * D_M), tile(D_M), tile(D_M), tile(LANES),
            pl.BlockSpec((1, 16, chunk), lambda bi, si: (bi, 0, si)),
            const((CONV_W, 2 * D_M)), const((1, 2 * D_M)), const((1, LANES)), const((16, 1)), const((1, D_M)),
        ] + [per_b(sh) for sh in state_shapes],
        out_specs=(tile(D_M),) + tuple(per_b(sh) for sh in state_shapes),
        scratch_shapes=[
            pltpu.VMEM((H_M, HD_M, HD_M), F32),
            pltpu.VMEM((H_M, 1, HD_M), F32),
            pltpu.VMEM((H_M, 1, LANES), F32),
            pltpu.VMEM((chunk + 8, 2 * D_M), F32),
        ],
        compiler_params=_cparams(("parallel", "arbitrary")),
        name="mlstm",
    )(qk, v, o, small, smallt, convw, convb, brow, bcol, ghead, c0, n0, m0b, conv0)


def _split3(x):
    hi = x.astype(BF16).astype(F32)
    r = x - hi
    mid = r.astype(BF16).astype(F32)
    lo = (r - mid).astype(BF16).astype(F32)
    return hi, mid, lo


def _fox_prep_kernel(*refs, n_past, n_new, s_valid, blk):
    if n_past:
        past_ref, pre_ref, bias_ref, lf_ref, hi_ref, mid_ref, lo_ref = refs
    else:
        pre_ref, bias_ref, lf_ref, hi_ref, mid_ref, lo_ref = refs
    tri = (lax.broadcasted_iota(jnp.int32, (blk, blk), 1) <= lax.broadcasted_iota(jnp.int32, (blk, blk), 0)).astype(BF16)
    carry = jnp.zeros((1, LANES), F32)
    start = 0
    while start < n_past + n_new:
        if start < n_past:
            r = min(blk, n_past - start)
            lf = jnp.concatenate([jnp.zeros((r, FOX_GATE_LANE), F32), past_ref[0, start:start + r, :],
                                  jnp.zeros((r, LANES - FOX_GATE_LANE - H_FOX), F32)], axis=1)
        else:
            r = min(blk, n_past + n_new - start)
            ns = start - n_past
            row = ns + lax.broadcasted_iota(jnp.int32, (r, LANES), 0)
            lf = jnp.where(row < s_valid, _log_sigmoid(pre_ref[0, ns:ns + r, :] + bias_ref[...]), 0.0)
            lf_ref[0, ns:ns + r, :] = lf
        cum = carry
        for piece in _split3(lf):
            cum = cum + jnp.dot(tri[0:r, 0:r], piece.astype(BF16), preferred_element_type=F32)
        carry = cum[r - 1:r, :]
        hi, mid, lo = _split3(cum)
        hi_ref[0, start:start + r, :] = hi
        mid_ref[0, start:start + r, :] = mid
        lo_ref[0, start:start + r, :] = lo
        start += r


def _fox_prep(past, pre, bias_row, s_valid):
    b, n_new = pre.shape[:2]
    n_past = 0 if past is None else past.shape[1]
    n = n_past + n_new
    full = lambda rows: pl.BlockSpec((1, rows, LANES), lambda bi: (bi, 0, 0))
    past_spec = pl.BlockSpec((1, n_past, H_FOX), lambda bi: (bi, 0, 0))
    in_specs = ([past_spec] if n_past else []) + [full(n_new), pl.BlockSpec((1, LANES), lambda bi: (0, 0))]
    args = ([past] if n_past else []) + [pre, bias_row]
    cum_shape = jax.ShapeDtypeStruct((b, n, LANES), F32)
    return pl.pallas_call(
        functools.partial(_fox_prep_kernel, n_past=n_past, n_new=n_new, s_valid=s_valid, blk=256),
        out_shape=(jax.ShapeDtypeStruct((b, n_new, LANES), F32), cum_shape, cum_shape, cum_shape),
        grid=(b,),
        in_specs=in_specs,
        out_specs=(full(n_new), full(n), full(n), full(n)),
        compiler_params=_cparams(("parallel",)),
        name="fox_prep",
    )(*args)


def _fox_pack_kernel(*refs, q_side, with_v):
    if with_v:
        x_ref, hi_ref, mid_ref, lo_ref, v_ref, out_ref, vt_ref = refs
        vt_ref[0] = v_ref[0].T.astype(BF16)
    else:
        x_ref, hi_ref, mid_ref, lo_ref, out_ref = refs
    tm = x_ref.shape[1]
    lane = lax.broadcasted_iota(jnp.int32, (tm, HD_FOX), 1)
    hi, mid, lo = hi_ref[0], mid_ref[0], lo_ref[0]
    for h in range(H_FOX):
        c = FOX_GATE_LANE + h
        pieces = (hi[:, c:c + 1], mid[:, c:c + 1], lo[:, c:c + 1])
        first = 0 if q_side else 3
        bias = jnp.where(jnp.logical_and(lane >= 3 - first, lane < 6 - first), -1.0 if q_side else 1.0, 0.0)
        for j, p in enumerate(pieces):
            bias = jnp.where(lane == first + j, p, bias)
        out_ref[0, h] = jnp.concatenate(
            [x_ref[0, :, h * HD_FOX:(h + 1) * HD_FOX].astype(BF16), bias.astype(BF16)], axis=1)


def _fox_pack(x, hi, mid, lo, row_offset, q_side, v=None):
    b, n, _ = x.shape
    tm = next(c for c in (512, 384, 256, LANES) if n % c == 0 and row_offset % c == 0)
    off = row_offset // tm
    tile = lambda w: pl.BlockSpec((1, tm, w), lambda bi, i: (bi, i, 0))
    piece = pl.BlockSpec((1, tm, LANES), lambda bi, i: (bi, off + i, 0))
    out_shape = [jax.ShapeDtypeStruct((b, H_FOX, n, 2 * HD_FOX), BF16)]
    out_specs = [pl.BlockSpec((1, H_FOX, tm, 2 * HD_FOX), lambda bi, i: (bi, 0, i, 0))]
    in_specs = [tile(D_FOX), piece, piece, piece]
    args = [x, hi, mid, lo]
    if v is not None:
        in_specs.append(tile(D_FOX))
        args.append(v)
        out_shape.append(jax.ShapeDtypeStruct((b, D_FOX, n), BF16))
        out_specs.append(pl.BlockSpec((1, D_FOX, tm), lambda bi, i: (bi, 0, i)))
    return pl.pallas_call(
        functools.partial(_fox_pack_kernel, q_side=q_side, with_v=v is not None),
        out_shape=tuple(out_shape),
        grid=(b, n // tm),
        in_specs=in_specs,
        out_specs=tuple(out_specs),
        compiler_params=_cparams(("parallel", "parallel")),
        name="fox_pack_q" if q_side else "fox_pack_kv",
    )(*args)


def _fox_attn_kernel(qa_ref, ka_ref, vt_ref, out_ref, m_scr, l_scr, acc_scr, *, p0, tq, tk, nk):
    qi, kj = pl.program_id(1), pl.program_id(2)

    @pl.when(kj == 0)
    def _():
        m_scr[...] = jnp.full(m_scr.shape, NEG_BIG, F32)
        l_scr[...] = jnp.zeros(l_scr.shape, F32)
        acc_scr[...] = jnp.zeros(acc_scr.shape, F32)

    def step(masked):
        col = lax.broadcasted_iota(jnp.int32, (1, 2 * HD_FOX), 1)
        scale = jnp.where(col < HD_FOX, HD_FOX ** -0.5, 1.0).astype(BF16)
        ones_rows = jnp.ones((16, tk), BF16)
        if masked:
            kpos = kj * tk + lax.broadcasted_iota(jnp.int32, (tk, tq), 0)
            qpos = p0 + qi * tq + lax.broadcasted_iota(jnp.int32, (tk, tq), 1)
            mask = kpos <= qpos
        for h in range(H_FOX):
            s = _nt_dot(ka_ref[0, h], qa_ref[0, h] * scale)
            if masked:
                s = jnp.where(mask, s, NEG_BIG)
            m_prev = m_scr[h]
            m_new = jnp.maximum(m_prev, jnp.max(s, axis=0, keepdims=True))
            p = jnp.exp(s - m_new)
            alpha = jnp.exp(m_prev - m_new)
            pv = jnp.dot(jnp.concatenate([vt_ref[0, h], ones_rows], axis=0), p.astype(BF16),
                         preferred_element_type=F32)
            l_scr[h] = alpha * l_scr[h] + pv[HD_FOX:HD_FOX + 1, :]
            acc_scr[h] = alpha * acc_scr[h] + pv[0:HD_FOX, :]
            m_scr[h] = m_new

    first_q = p0 + qi * tq
    last_q = first_q + tq - 1
    unmasked = (kj + 1) * tk - 1 <= first_q

    @pl.when(unmasked)
    def _():
        step(False)

    @pl.when(jnp.logical_and(jnp.logical_not(unmasked), kj * tk <= last_q))
    def _():
        step(True)

    @pl.when(kj == nk - 1)
    def _():
        o_t = jnp.concatenate([acc_scr[h] / l_scr[h] for h in range(H_FOX)], axis=0)
        out_ref[0] = o_t.T.astype(BF16)


def _fox_attn(qa, ka, vt, p0):
    b, _, sq, _ = qa.shape
    sk = ka.shape[2]
    tq = min(sq, 512)
    tk = 512 if sk % 512 == 0 else 384
    nq, nk = sq // tq, sk // tk
    last_tile = lambda qi: (p0 + (qi + 1) * tq - 1) // tk
    return pl.pallas_call(
        functools.partial(_fox_attn_kernel, p0=p0, tq=tq, tk=tk, nk=nk),
        out_shape=jax.ShapeDtypeStruct((b, sq, D_FOX), BF16),
        grid=(b, nq, nk),
        in_specs=[
            pl.BlockSpec((1, H_FOX, tq, 2 * HD_FOX), lambda bi, qi, kj: (bi, 0, qi, 0)),
            pl.BlockSpec((1, H_FOX, tk, 2 * HD_FOX), lambda bi, qi, kj: (bi, 0, jnp.minimum(kj, last_tile(qi)), 0)),
            pl.BlockSpec((1, H_FOX, HD_FOX, tk), lambda bi, qi, kj: (bi, 0, 0, jnp.minimum(kj, last_tile(qi)))),
        ],
        out_specs=pl.BlockSpec((1, tq, D_FOX), lambda bi, qi, kj: (bi, qi, 0)),
        scratch_shapes=[
            pltpu.VMEM((H_FOX, 1, tq), F32),
            pltpu.VMEM((H_FOX, 1, tq), F32),
            pltpu.VMEM((H_FOX, HD_FOX, tq), F32),
        ],
        compiler_params=_cparams(("parallel", "parallel", "arbitrary")),
        name="fox_attn",
    )(qa, ka, vt)


def _merge_kernel(x_ref, hm_ref, of_ref, cq_ref, gates_ref, mk_ref, mv_ref, wm_ref, wf_ref, wc_ref,
                  wo_ref, gffn_ref, xnew_ref, hn_ref, hnlo_ref):
    cq = cq_ref[0]
    heads = []
    for h in range(H_MEM):
        sl = slice(h * HD_MEM, (h + 1) * HD_MEM)
        s = _nt_dot(cq[:, sl], mk_ref[0, :, sl]) * (HD_MEM ** -0.5)
        p = jnp.exp(s - jnp.max(s, axis=1, keepdims=True))
        o = jnp.dot(p.astype(BF16), mv_ref[0, :, sl], preferred_element_type=F32)
        heads.append(o / jnp.sum(p, axis=1, keepdims=True))
    oc = jnp.concatenate(heads, axis=1).astype(BF16)
    a_m = jnp.dot(hm_ref[0], wm_ref[...], preferred_element_type=F32)
    a_f = jnp.dot(of_ref[0], wf_ref[...], preferred_element_type=F32)
    a_c = jnp.dot(oc, wc_ref[...], preferred_element_type=F32)
    g = gates_ref[0].astype(F32)
    merged = (_sigmoid(g[:, 0:D_MODEL]) * a_m + _sigmoid(g[:, D_MODEL:2 * D_MODEL]) * a_f
              + _sigmoid(g[:, 2 * D_MODEL:3 * D_MODEL]) * a_c)
    xn = x_ref[0] + jnp.dot(merged.astype(BF16), wo_ref[...], preferred_element_type=F32)
    xnew_ref[0] = xn
    hn = _rms(xn, gffn_ref[...])
    hn_hi = hn.astype(BF16)
    hn_ref[0] = hn_hi
    hnlo_ref[0] = (hn - hn_hi.astype(F32)).astype(BF16)


def _merge(x, hm, of, cq, gates, mk, mv, wm, wf, wc, wo, gffn):
    b, s, _ = x.shape
    ts = min(s, 256)
    tile = lambda w: pl.BlockSpec((1, ts, w), lambda bi, si: (bi, si, 0))
    const = lambda shape: pl.BlockSpec(shape, lambda bi, si: (0, 0))
    mem = pl.BlockSpec((1, mk.shape[1], D_MEM), lambda bi, si: (bi, 0, 0))
    return pl.pallas_call(
        _merge_kernel,
        out_shape=(jax.ShapeDtypeStruct((b, s, D_MODEL), F32), jax.ShapeDtypeStruct((b, s, D_MODEL), BF16),
                   jax.ShapeDtypeStruct((b, s, D_MODEL), BF16)),
        grid=(b, s // ts),
        in_specs=[tile(D_MODEL), tile(D_M), tile(D_FOX), tile(D_MEM), tile(N_BRANCH * D_MODEL), mem, mem,
                  const((D_M, D_MODEL)), const((D_FOX, D_MODEL)), const((D_MEM, D_MODEL)),
                  const((D_MODEL, D_MODEL)), const((1, D_MODEL))],
        out_specs=(tile(D_MODEL), tile(D_MODEL), tile(D_MODEL)),
        compiler_params=_cparams(("parallel", "parallel")),
        name="merge",
    )(x, hm, of, cq, gates, mk, mv, wm, wf, wc, wo, gffn)


def _extract_top16(s, key_io, val_scr, idx_scr, h, want_rank):
    rank = jnp.full(s.shape, float(PEER_TOPK), F32) if want_rank else None
    for r in range(PEER_TOPK):
        m = jnp.max(s, axis=0, keepdims=True)
        first = jnp.min(jnp.where(s == m, key_io, float(N_KEYS)), axis=0, keepdims=True)
        hit = key_io == first
        s = jnp.where(hit, -jnp.inf, s)
        val_scr[r, pl.ds(h, 1), :] = m
        if want_rank:
            rank = jnp.where(hit, float(r), rank)
        else:
            idx_scr[r, pl.ds(h, 1), :] = first
    return rank


def _dot3(a_hi, a_lo, b_hi, b_lo, dot):
    return dot(a_hi, b_hi) + dot(a_lo, b_hi) + dot(a_hi, b_lo)


def _router_kernel(hn_ref, hnlo_ref, wqt_ref, keys_ref, a0_ref, cnt_ref, r1_ref, b1_ref,
                   qr_scr, e1_scr, top0_scr, top1_scr, idx0_scr, cnt_scr, invz_scr):
    tp = hn_ref.shape[0]
    qr_scr[...] = _dot3(wqt_ref[0], wqt_ref[1], hn_ref[...], hnlo_ref[...], _nt_dot)
    key_io = lax.broadcasted_iota(jnp.int32, (N_KEYS, tp), 0).astype(F32)

    heads_per_trip = 8

    def scores_and_top16_group(hg, carry):
        for k in range(heads_per_trip):
            scores_and_top16(heads_per_trip * hg + k)
        return carry

    def scores_and_top16(h):
        base = pl.multiple_of(h * 2 * N_KEYS, 2 * N_KEYS)
        def scores(c):
            q = qr_scr[pl.ds(base + c * N_KEYS, N_KEYS), :]
            q_hi = q.astype(BF16)
            q_lo = (q - q_hi.astype(F32)).astype(BF16)
            return _dot3(keys_ref[0, 2 * h + c], keys_ref[1, 2 * h + c], q_hi, q_lo,
                         lambda a, b_: jnp.dot(a, b_, preferred_element_type=F32))

        s0, s1 = scores(0), scores(1)
        _extract_top16(s0, key_io, top0_scr, idx0_scr, h, False)
        rank1 = _extract_top16(s1, key_io, top1_scr, None, h, True)
        a0_ref[h] = jnp.exp(s0 - top0_scr[0, pl.ds(h, 1), :])
        e1_scr[h] = jnp.exp(s1 - top1_scr[0, pl.ds(h, 1), :])
        r1_ref[h] = rank1.astype(BF16)

    lax.fori_loop(0, PEER_HEADS // heads_per_trip, scores_and_top16_group, 0)

    top0 = [top0_scr[a] for a in range(PEER_TOPK)]
    top1 = [top1_scr[b] for b in range(PEER_TOPK)]
    cnt = [jnp.zeros((PEER_HEADS, tp), F32) for _ in range(PEER_TOPK)]
    for _ in range(PEER_TOPK):
        front = []
        for a in range(PEER_TOPK):
            nxt = jnp.full((PEER_HEADS, tp), -jnp.inf, F32)
            for bb in range(PEER_TOPK // (a + 1)):
                nxt = jnp.where(cnt[a] == float(bb), top1[bb], nxt)
            front.append(top0[a] + nxt)
        mx = functools.reduce(jnp.maximum, front)
        first = functools.reduce(jnp.minimum,
                                 [jnp.where(front[a] == mx, float(a), float(PEER_TOPK)) for a in range(PEER_TOPK)])
        cnt = [cnt[a] + jnp.where(first == float(a), 1.0, 0.0) for a in range(PEER_TOPK)]
    z = jnp.zeros((PEER_HEADS, tp), F32)
    for a in range(PEER_TOPK):
        za = jnp.zeros((PEER_HEADS, tp), F32)
        for bb in range(PEER_TOPK // (a + 1)):
            za = za + jnp.where(cnt[a] > float(bb), jnp.exp(top1[bb] - top1[0]), 0.0)
        z = z + jnp.exp(top0[a] - top0[0]) * za
        cnt_scr[a] = cnt[a]
    invz_scr[...] = 1.0 / z

    def counts_and_gates(h, carry):
        cnt_i = jnp.zeros((N_KEYS, tp), F32)
        for a in range(PEER_TOPK):
            cnt_i = jnp.where(key_io == idx0_scr[a, pl.ds(h, 1), :], cnt_scr[a, pl.ds(h, 1), :], cnt_i)
        cnt_ref[h] = cnt_i
        b1 = e1_scr[h] * invz_scr[pl.ds(h, 1), :]
        b1_ref[h] = b1.astype(BF16)
        return carry

    lax.fori_loop(0, PEER_HEADS, counts_and_gates, 0)


def _router(hn2d, hnlo2d, wqt, keys):
    t = hn2d.shape[0]
    tp = 2 * LANES
    shape = jax.ShapeDtypeStruct((PEER_HEADS, N_KEYS, t), F32)
    spec = pl.BlockSpec((PEER_HEADS, N_KEYS, tp), lambda i: (0, 0, i))
    slab_shape = jax.ShapeDtypeStruct((PEER_HEADS, N_KEYS, t), BF16)
    slab_spec = spec
    return pl.pallas_call(
        _router_kernel,
        out_shape=(shape, shape, slab_shape, slab_shape),
        grid=(t // tp,),
        in_specs=[
            pl.BlockSpec((tp, D_MODEL), lambda i: (i, 0)),
            pl.BlockSpec((tp, D_MODEL), lambda i: (i, 0)),
            pl.BlockSpec(wqt.shape, lambda i: (0, 0, 0)),
            pl.BlockSpec(keys.shape, lambda i: (0, 0, 0, 0)),
        ],
        out_specs=(spec, spec, slab_spec, slab_spec),
        scratch_shapes=[
            pltpu.VMEM((wqt.shape[1], tp), F32),
            pltpu.VMEM((PEER_HEADS, N_KEYS, tp), F32),
        ] + [pltpu.VMEM((PEER_TOPK, PEER_HEADS, tp), F32) for _ in range(4)] + [pltpu.VMEM((PEER_HEADS, tp), F32)],
        compiler_params=_cparams(("parallel",)),
        name="peer_router",
    )(hn2d, hnlo2d, wqt, keys)


def _gelu_tanh(x):
    k = -2.0 * 0.7978845608028654 * 1.4426950408889634
    return x / (1.0 + jnp.exp2(x * (k + (k * 0.044715) * (x * x))))


def _experts_kernel(hn_ref, u_ref, vt_ref, a0_ref, cnt_ref, r1_ref, b1_ref, x_ref, gfin_ref, out_ref,
                    acc_scr, pre0_scr, pre1_scr, w0_scr, w1_scr, *, rows_per_tile, n_tiles, final_norm):
    s = pl.program_id(1)
    tp = hn_ref.shape[0]
    te = u_ref.shape[0] // 2
    slab = 16
    slabs_per_row = N_KEYS // slab
    d_half = acc_scr.shape[0] // 2
    zero = jnp.zeros((), BF16)

    @pl.when(s == 0)
    def _():
        acc_scr[...] = jnp.zeros(acc_scr.shape, F32)
        pre1_scr[...] = jnp.zeros(pre1_scr.shape, F32)
        w0_scr[...] = jnp.zeros(w0_scr.shape, BF16)

    def pipeline_step(parity, pre_w, pre_r, w_w, w_r):
        e = 2 * s + parity
        cols = slice(parity * te, (parity + 1) * te)

        def stage_c(half):
            rows = slice(half * d_half, (half + 1) * d_half)
            acc_scr[rows, :] += jnp.dot(vt_ref[rows, cols], w_r[...], preferred_element_type=F32)

        def stage_a(half):
            rows = slice(half * (te // 2), (half + 1) * (te // 2))
            pre_w[rows, :] = _nt_dot(u_ref[parity * te + half * (te // 2):parity * te + (half + 1) * (te // 2), :],
                                     hn_ref[...])

        mxu_work = (lambda: stage_c(0), lambda: stage_a(0), lambda: stage_c(1), lambda: stage_a(1))

        valid = jnp.logical_and(e >= 1, e <= n_tiles)
        tile_b = jnp.clip(e - 1, 0, n_tiles - 1)
        for ib in range(rows_per_tile):
            mxu_work[ib]()
            i = tile_b * rows_per_tile + ib
            for lanes in (slice(0, tp // 2), slice(tp // 2, tp)):
                gates = [jnp.zeros((slab, tp // 2), BF16) for _ in range(slabs_per_row)]
                for h in range(PEER_HEADS):
                    a_b = jnp.broadcast_to(a0_ref[h, pl.ds(i, 1), lanes], (slab, tp // 2)).astype(BF16)
                    c_b = jnp.broadcast_to(cnt_ref[h, pl.ds(i, 1), lanes], (slab, tp // 2)).astype(BF16)
                    for g in range(slabs_per_row):
                        key_rows = slice(g * slab, (g + 1) * slab)
                        gates[g] = gates[g] + a_b * jnp.where(r1_ref[h, key_rows, lanes] < c_b,
                                                              b1_ref[h, key_rows, lanes], zero)
                for g in range(slabs_per_row):
                    rows = slice(ib * N_KEYS + g * slab, ib * N_KEYS + (g + 1) * slab)
                    w = gates[g] * _gelu_tanh(pre_r[rows, lanes]).astype(BF16)
                    w_w[rows, lanes] = jnp.where(valid, w, zero)

    pipeline_step(0, pre0_scr, pre1_scr, w1_scr, w0_scr)
    pipeline_step(1, pre1_scr, pre0_scr, w0_scr, w1_scr)

    @pl.when(s == n_tiles // 2)
    def _():
        xo = x_ref[...] + acc_scr[...].T
        if final_norm:
            xo = _rms(xo, gfin_ref[...])
        out_ref[...] = xo


def _experts(hn2d, u_bf, vt_bf, layer, a0, cnt, r1, b1, x2d, gfin, final_norm):
    t = hn2d.shape[0]
    tp = 512
    rows_per_tile = 4
    te = rows_per_tile * N_KEYS
    n_tiles = u_bf.shape[1] // te
    rspec = pl.BlockSpec((PEER_HEADS, N_KEYS, tp), lambda ti, e: (0, 0, ti))
    sspec = rspec
    return pl.pallas_call(
        functools.partial(_experts_kernel, rows_per_tile=rows_per_tile, n_tiles=n_tiles, final_norm=final_norm),
        out_shape=jax.ShapeDtypeStruct((t, D_MODEL), F32),
        grid=(t // tp, n_tiles // 2 + 1),
        in_specs=[
            pl.BlockSpec((tp, D_MODEL), lambda ti, s: (ti, 0)),
            pl.BlockSpec((None, 2 * te, D_MODEL), lambda ti, s: (layer, jnp.minimum(s, n_tiles // 2 - 1), 0)),
            pl.BlockSpec((None, D_MODEL, 2 * te), lambda ti, s: (layer, 0, jnp.maximum(s - 1, 0))),
            rspec, rspec, sspec, sspec,
            pl.BlockSpec((tp, D_MODEL), lambda ti, s: (ti, 0)),
            pl.BlockSpec((1, D_MODEL), lambda ti, s: (0, 0)),
        ],
        out_specs=pl.BlockSpec((tp, D_MODEL), lambda ti, s: (ti, 0)),
        scratch_shapes=[
            pltpu.VMEM((D_MODEL, tp), F32),
            pltpu.VMEM((te, tp), F32), pltpu.VMEM((te, tp), F32),
            pltpu.VMEM((te, tp), BF16), pltpu.VMEM((te, tp), BF16),
        ],
        compiler_params=_cparams(("parallel", "arbitrary"), EXPERT_FLAGS),
        name="peer_experts",
    )(hn2d, u_bf, vt_bf, a0, cnt, r1, b1, x2d, gfin)


def _hi_lo(a):
    hi = a.astype(BF16)
    return jnp.stack([hi, (a - hi.astype(F32)).astype(BF16)])


def _prep_layer_weights(p):
    w_in = p['w_in']
    o_mi = 4 * D_M
    o_fq = o_mi + 2 * H_M
    o_ff = o_fq + 3 * D_FOX
    o_cq = o_ff + H_FOX
    w_big = jnp.concatenate([w_in[:, :o_mi], w_in[:, o_fq:o_ff], w_in[:, o_cq:]], axis=1).astype(BF16)
    w_big = w_big.reshape(D_MODEL, -1, PROJ_TILE).transpose(1, 0, 2)
    w_small = jnp.concatenate([w_in[:, o_mi:o_fq], w_in[:, o_ff:o_cq]], axis=1)
    w_small = jnp.pad(w_small, ((0, 0), (0, LANES - w_small.shape[1])))
    w_small_hi = w_small.astype(BF16)
    w_small = jnp.stack([w_small_hi, (w_small - w_small_hi.astype(F32)).astype(BF16)])
    bias = jnp.concatenate([p['b_m_i'], p['b_m_f'], p['b_fox_f']]).astype(F32)
    return dict(
        norm_mix=p['norm_mix'].reshape(1, D_MODEL), w_big=w_big, w_small=w_small,
        conv_w=p['conv_w'], conv_b=p['conv_b'].reshape(1, 2 * D_M),
        bias_row=jnp.pad(bias, (0, LANES - 16)).reshape(1, LANES), bias_col=bias.reshape(16, 1),
        norm_m_head=p['norm_m_head'].reshape(1, D_M),
        w_up_m=p['w_up_m'].astype(BF16), w_up_f=p['w_up_f'].astype(BF16), w_up_c=p['w_up_c'].astype(BF16),
        w_out=p['w_out'].astype(BF16), norm_ffn=p['norm_ffn'].reshape(1, D_MODEL),
        wqt=_hi_lo(p['peer_wq'].T),
        keys=_hi_lo(p['peer_keys'].reshape(2 * PEER_HEADS, N_KEYS, N_KEYS)),
    )


def _layer(x, mem_k, mem_v, conv_prev, c0, n0, m0, fk_past, fv_past, flf_past, w, peer_tables, layer, gfin, final_norm):
    b, s, _ = x.shape
    t = b * s
    p0 = 0 if fk_past is None else fk_past.shape[1]
    qk, v_m, o_m, fq, fk, fv, cq, gates, small = _norm_proj(x.reshape(t, D_MODEL), w['norm_mix'], w['w_big'], w['w_small'])

    small3 = small.reshape(b, s, LANES)
    smallt = small3[:, :, :16].transpose(0, 2, 1)
    m0b = jnp.broadcast_to(m0[:, :, None, None], (b, H_M, 1, LANES))
    hm, c1, n1, m1b, conv_new = _mlstm(
        qk.reshape(b, s, 2 * D_M), v_m.reshape(b, s, D_M), o_m.reshape(b, s, D_M), small3, smallt,
        w['conv_w'], w['conv_b'], w['bias_row'], w['bias_col'], w['norm_m_head'],
        c0, n0[:, :, None, :], m0b, conv_prev)
    n1, m1 = n1[:, :, 0, :], m1b[:, :, 0, 0]

    k_f = fk.reshape(b, s, H_FOX, HD_FOX)
    v_f = fv.reshape(b, s, H_FOX, HD_FOX)
    s_pad = -(-s // LANES) * LANES
    pad_rows = lambda a: jnp.pad(a, ((0, 0), (0, s_pad - s), (0, 0)))
    past = flf_past.astype(F32) if p0 else None
    lf_all, hi, mid, lo = _fox_prep(past, pad_rows(small3), w['bias_row'], s)
    lf_f = lf_all[:, :s, FOX_GATE_LANE:FOX_GATE_LANE + H_FOX]
    sk = p0 + s_pad
    k_all, v_all = pad_rows(fk.reshape(b, s, D_FOX)), pad_rows(fv.reshape(b, s, D_FOX))
    if p0:
        k_all = jnp.concatenate([fk_past.reshape(b, p0, D_FOX).astype(F32), k_all], axis=1)
        v_all = jnp.concatenate([fv_past.reshape(b, p0, D_FOX).astype(F32), v_all], axis=1)
    qa = _fox_pack(pad_rows(fq.reshape(b, s, D_FOX)), hi, mid, lo, p0, True)[0]
    ka, vt_all = _fox_pack(k_all, hi, mid, lo, 0, False, v=v_all)
    o_f = _fox_attn(qa, ka, vt_all.reshape(b, H_FOX, HD_FOX, sk), p0)[:, :s]

    x_new, hn, hn_lo = _merge(x, hm, o_f, cq.reshape(b, s, D_MEM), gates.reshape(b, s, N_BRANCH * D_MODEL),
                       mem_k.reshape(b, -1, D_MEM).astype(BF16), mem_v.reshape(b, -1, D_MEM).astype(BF16),
                       w['w_up_m'], w['w_up_f'], w['w_up_c'], w['w_out'], w['norm_ffn'])

    hn2d = hn.reshape(t, D_MODEL)
    a0, cnt, r1, b1 = _router(hn2d, hn_lo.reshape(t, D_MODEL), w['wqt'], w['keys'])
    x_out = _experts(hn2d, peer_tables[0], peer_tables[1], layer, a0, cnt, r1, b1, x_new.reshape(t, D_MODEL),
                     gfin, final_norm)
    return x_out.reshape(b, s, D_MODEL), conv_new, c1, n1, m1, k_f, v_f, lf_f


def kernel(x_prompt, x_sample, mem_prompt, cache_fox_k, cache_fox_v, cache_fox_lf, state_mlstm_c, state_mlstm_n, state_mlstm_m, state_conv, cache_mem_k, cache_mem_v, norm_mix, w_in, conv_w, conv_b, b_m_i, b_m_f, norm_m_head, b_fox_f, norm_mem, w_mem_kv, w_up_m, w_up_f, w_up_c, w_out, norm_ffn, peer_wq, peer_keys, peer_u, peer_v, norm_final):
    depth = w_in.shape[0]
    names = dict(norm_mix=norm_mix, w_in=w_in, conv_w=conv_w, conv_b=conv_b, b_m_i=b_m_i, b_m_f=b_m_f,
                 norm_m_head=norm_m_head, b_fox_f=b_fox_f, w_up_m=w_up_m, w_up_f=w_up_f, w_up_c=w_up_c,
                 w_out=w_out, norm_ffn=norm_ffn, peer_wq=peer_wq, peer_keys=peer_keys, peer_u=peer_u,
                 peer_v=peer_v)
    weights = [_prep_layer_weights({k: a[l] for k, a in names.items()}) for l in range(depth)]
    peer_tables = (peer_u.astype(BF16), jnp.swapaxes(peer_v, 1, 2).astype(BF16))
    gfin = norm_final.reshape(1, D_MODEL)
    bp, n_mem = mem_prompt.shape[0], mem_prompt.shape[1]

    xp = x_prompt
    new_p = [[] for _ in range(9)]
    for l in range(depth):
        mk, mv = _norm_matmul(mem_prompt.reshape(bp * n_mem, D_MODEL), norm_mem[l].reshape(1, D_MODEL),
                              w_mem_kv[l].astype(BF16))
        mk = mk.reshape(bp, n_mem, H_MEM, HD_MEM)
        mv = mv.reshape(bp, n_mem, H_MEM, HD_MEM)
        xp, conv1, c1, n1, m1, kf, vf, lff = _layer(
            xp, mk, mv, jnp.zeros((bp, CONV_W - 1, 2 * D_M), F32),
            jnp.zeros((bp, H_M, HD_M, HD_M), F32), jnp.zeros((bp, H_M, HD_M), F32), jnp.zeros((bp, H_M), F32),
            None, None, None, weights[l], peer_tables, l, gfin, l == depth - 1)
        for lst, a in zip(new_p, (kf, vf, lff, c1, n1, m1, conv1, mk, mv)):
            lst.append(a)
    outs_p = [jnp.stack(a) for a in new_p]

    xs = x_sample
    new_s = [[] for _ in range(7)]
    for l in range(depth):
        xs, conv1, c1, n1, m1, kf, vf, lff = _layer(
            xs, cache_mem_k[l], cache_mem_v[l], state_conv[l], state_mlstm_c[l], state_mlstm_n[l],
            state_mlstm_m[l], cache_fox_k[l], cache_fox_v[l], cache_fox_lf[l], weights[l], peer_tables, l, gfin,
            l == depth - 1)
        for lst, a in zip(new_s, (kf, vf, lff, c1, n1, m1, conv1)):
            lst.append(a)
    outs_s = [jnp.stack(a) for a in new_s]

    return (xp, xs, *outs_p, *outs_s)
```

```python
import functools

import jax
import jax.numpy as jnp
from jax import lax
from jax.experimental import pallas as pl
from jax.experimental.pallas import tpu as pltpu

F32 = jnp.float32
BF16 = jnp.bfloat16
EPS = 1e-6
NEG_BIG = -1e30

D_MODEL = 1024
H_M, HD_M = 4, 128
D_M = H_M * HD_M
CONV_W = 4
H_FOX, HD_FOX = 8, 64
D_FOX = H_FOX * HD_FOX
H_MEM, HD_MEM = 4, 128
D_MEM = H_MEM * HD_MEM
N_BRANCH = 3
PEER_HEADS = 8
N_KEYS = 128
PEER_TOPK = 16
LANES = 128
FOX_GATE_LANE = 2 * H_M
MLSTM_CHUNK = 256
PROJ_TILE = 512
VMEM_LIMIT = 56 * 1024 * 1024


def _cparams(sem):
    return pltpu.CompilerParams(dimension_semantics=sem, vmem_limit_bytes=VMEM_LIMIT)


def _nt_dot(a, b):
    return lax.dot_general(a, b, (((1,), (1,)), ((), ())), preferred_element_type=F32)


def _rms(x, g):
    return x * lax.rsqrt(jnp.mean(x * x, axis=-1, keepdims=True) + EPS) * g


def _log_sigmoid(x):
    return jnp.minimum(x, 0.0) - jnp.log(1.0 + jnp.exp(-jnp.abs(x)))


def _sigmoid(x):
    return 1.0 / (1.0 + jnp.exp(-x))


def _norm_proj_kernel(x_ref, g_ref, w_ref, ws_ref, qk_ref, v_ref, o_ref, fq_ref, fk_ref, fv_ref,
                      cq_ref, gates_ref, small_ref, h_scr):
    j = pl.program_id(1)

    @pl.when(j == 0)
    def _():
        hf = _rms(x_ref[...], g_ref[...])
        h = hf.astype(BF16)
        h_scr[...] = h
        h_lo = (hf - h.astype(F32)).astype(BF16)
        small_ref[...] = (jnp.dot(h, ws_ref[0], preferred_element_type=F32)
                          + jnp.dot(h_lo, ws_ref[0], preferred_element_type=F32)
                          + jnp.dot(h, ws_ref[1], preferred_element_type=F32))

    def tile():
        return jnp.dot(h_scr[...], w_ref[0], preferred_element_type=F32)

    @pl.when(j == 0)
    def _():
        qk_ref[:, 0:PROJ_TILE] = tile()

    @pl.when(j == 1)
    def _():
        qk_ref[:, PROJ_TILE:2 * PROJ_TILE] = tile()

    @pl.when(j == 2)
    def _():
        v_ref[...] = tile().astype(BF16)

    @pl.when(j == 3)
    def _():
        o_ref[...] = tile().astype(BF16)

    @pl.when(j == 4)
    def _():
        fq_ref[...] = tile().astype(BF16)

    @pl.when(j == 5)
    def _():
        fk_ref[...] = tile()

    @pl.when(j == 6)
    def _():
        fv_ref[...] = tile()

    @pl.when(j == 7)
    def _():
        cq_ref[...] = tile().astype(BF16)

    @pl.when(j >= 8)
    def _():
        gates_ref[...] = tile().astype(BF16)


def _norm_proj(x2d, g, w_big, w_small):
    t = x2d.shape[0]
    tm = 1024 if t % 1024 == 0 else 512
    n_tiles = w_big.shape[0]
    n_gate_tiles = n_tiles - 8
    row = lambda i, j: (i, 0)
    out_shape = (
        jax.ShapeDtypeStruct((t, 2 * D_M), F32),
        jax.ShapeDtypeStruct((t, D_M), BF16),
        jax.ShapeDtypeStruct((t, D_M), BF16),
        jax.ShapeDtypeStruct((t, D_FOX), BF16),
        jax.ShapeDtypeStruct((t, D_FOX), F32),
        jax.ShapeDtypeStruct((t, D_FOX), F32),
        jax.ShapeDtypeStruct((t, D_MEM), BF16),
        jax.ShapeDtypeStruct((t, N_BRANCH * D_MODEL), BF16),
        jax.ShapeDtypeStruct((t, LANES), F32),
    )
    out_specs = (
        pl.BlockSpec((tm, 2 * D_M), row),
        pl.BlockSpec((tm, D_M), row),
        pl.BlockSpec((tm, D_M), row),
        pl.BlockSpec((tm, D_FOX), row),
        pl.BlockSpec((tm, D_FOX), row),
        pl.BlockSpec((tm, D_FOX), row),
        pl.BlockSpec((tm, D_MEM), row),
        pl.BlockSpec((tm, PROJ_TILE), lambda i, j: (i, jnp.clip(j - 8, 0, n_gate_tiles - 1))),
        pl.BlockSpec((tm, LANES), row),
    )
    return pl.pallas_call(
        _norm_proj_kernel,
        out_shape=out_shape,
        grid=(t // tm, n_tiles),
        in_specs=[
            pl.BlockSpec((tm, D_MODEL), row),
            pl.BlockSpec((1, D_MODEL), lambda i, j: (0, 0)),
            pl.BlockSpec((1, D_MODEL, PROJ_TILE), lambda i, j: (j, 0, 0)),
            pl.BlockSpec((2, D_MODEL, LANES), lambda i, j: (0, 0, 0)),
        ],
        out_specs=out_specs,
        scratch_shapes=[pltpu.VMEM((tm, D_MODEL), BF16)],
        compiler_params=_cparams(("parallel", "arbitrary")),
        name="norm_proj",
    )(x2d, g, w_big, w_small)


def _norm_matmul_kernel(x_ref, g_ref, w_ref, k_ref, v_ref):
    h = _rms(x_ref[...], g_ref[...]).astype(BF16)
    k_ref[...] = jnp.dot(h, w_ref[:, 0:D_MEM], preferred_element_type=F32)
    v_ref[...] = jnp.dot(h, w_ref[:, D_MEM:2 * D_MEM], preferred_element_type=F32)


def _norm_matmul(x2d, g, w):
    t = x2d.shape[0]
    tm = 512
    out = jax.ShapeDtypeStruct((t, D_MEM), F32)
    return pl.pallas_call(
        _norm_matmul_kernel,
        out_shape=(out, out),
        grid=(t // tm,),
        in_specs=[
            pl.BlockSpec((tm, D_MODEL), lambda i: (i, 0)),
            pl.BlockSpec((1, D_MODEL), lambda i: (0, 0)),
            pl.BlockSpec((D_MODEL, 2 * D_MEM), lambda i: (0, 0)),
        ],
        out_specs=(pl.BlockSpec((tm, D_MEM), lambda i: (i, 0)), pl.BlockSpec((tm, D_MEM), lambda i: (i, 0))),
        compiler_params=_cparams(("parallel",)),
        name="norm_matmul",
    )(x2d, g, w)


def _mlstm_kernel(qk_ref, v_ref, o_ref, small_ref, smallt_ref, convw_ref, convb_ref, brow_ref,
                  bcol_ref, ghead_ref, c0_ref, n0_ref, m0_ref, conv0_ref,
                  hm_ref, c_out_ref, n_out_ref, m_out_ref, conv_out_ref,
                  c_scr, n_scr, m_scr, xp_scr, *, chunk):
    L = chunk
    si = pl.program_id(1)

    @pl.when(si == 0)
    def _():
        c_scr[...] = c0_ref[0]
        n_scr[...] = n0_ref[0]
        m_scr[...] = m0_ref[0]
        xp_scr[5:8, :] = conv0_ref[0]

    xp_scr[8:8 + L, :] = qk_ref[0]
    y = convb_ref[...] + convw_ref[0:1, :] * xp_scr[5:5 + L, :]
    for j in range(1, CONV_W):
        y = y + convw_ref[j:j + 1, :] * xp_scr[5 + j:5 + j + L, :]
    y = y * _sigmoid(y)
    tail = xp_scr[5 + L:8 + L, :]
    xp_scr[5:8, :] = tail
    conv_out_ref[0] = tail

    t_io = lax.broadcasted_iota(jnp.int32, (L, L), 0)
    s_io = lax.broadcasted_iota(jnp.int32, (L, L), 1)
    causal = s_io <= t_io
    ones_col = (lax.broadcasted_iota(jnp.int32, (L, HD_M), 1) == 0).astype(BF16)

    small = small_ref[0]
    smallt = smallt_ref[0]
    brow = brow_ref[...]
    bcol = bcol_ref[...]
    outs = []
    for h in range(H_M):
        q = y[:, h * HD_M:(h + 1) * HD_M]
        k = y[:, D_M + h * HD_M:D_M + (h + 1) * HD_M] * (HD_M ** -0.5)
        v_aug = jnp.concatenate([v_ref[0, :, h * HD_M:(h + 1) * HD_M], ones_col], axis=1)
        ig_col = small[:, h:h + 1] + brow[:, h:h + 1]
        ig_row = smallt[h:h + 1, :] + bcol[h:h + 1, :]
        lf_col = _log_sigmoid(small[:, H_M + h:H_M + h + 1] + brow[:, H_M + h:H_M + h + 1])
        lf_row = _log_sigmoid(smallt[H_M + h:H_M + h + 1, :] + bcol[H_M + h:H_M + h + 1, :])
        bcum_col = jnp.sum(jnp.where(causal, lf_row, 0.0), axis=1, keepdims=True)
        bcum_row = jnp.sum(jnp.where(t_io <= s_io, lf_col, 0.0), axis=0, keepdims=True)
        m_prev = m_scr[h][:, 0:1]
        dmat = jnp.where(causal, bcum_col - bcum_row + ig_row, -jnp.inf)
        inter = bcum_col + m_prev
        m_t = jnp.maximum(inter, jnp.max(dmat, axis=1, keepdims=True))
        w_intra = jnp.exp(dmat - m_t)
        w_state = jnp.exp(inter - m_t)
        qb = q.astype(BF16)
        sw = _nt_dot(qb, k.astype(BF16)) * w_intra
        c_prev = c_scr[h]
        n_prev = n_scr[h]
        intra = jnp.dot(sw.astype(BF16), v_aug, preferred_element_type=F32)
        num = intra[:, 0:HD_M] + w_state * jnp.dot(qb, c_prev.astype(BF16), preferred_element_type=F32)
        den = intra[:, HD_M:HD_M + 1] + w_state * jnp.sum(q * n_prev, axis=1, keepdims=True)
        hh = num / jnp.maximum(jnp.abs(den), jnp.exp(-m_t))
        hh = _rms(hh, ghead_ref[:, h * HD_M:(h + 1) * HD_M])
        outs.append(hh * _sigmoid(o_ref[0, :, h * HD_M:(h + 1) * HD_M].astype(F32)))
        bl = bcum_row[:, L - 1:L]
        m_new = jnp.maximum(bl + m_prev, jnp.max(bl - bcum_row + ig_row, axis=1, keepdims=True))
        decay = jnp.exp(bl + m_prev - m_new)
        ws_col = jnp.exp(bl - bcum_col + ig_col - m_new)
        kw = k * ws_col
        c_scr[h] = decay * c_prev + jnp.dot(kw.T.astype(BF16), v_ref[0, :, h * HD_M:(h + 1) * HD_M],
                                            preferred_element_type=F32)
        n_scr[h] = decay * n_prev + jnp.sum(kw, axis=0, keepdims=True)
        m_scr[h] = jnp.broadcast_to(m_new, (1, LANES))
    hm_ref[0] = jnp.concatenate(outs, axis=1).astype(BF16)
    c_out_ref[0] = c_scr[...]
    n_out_ref[0] = n_scr[...]
    m_out_ref[0] = m_scr[...]


def _mlstm(qk, v, o, small, smallt, convw, convb, brow, bcol, ghead, c0, n0, m0b, conv0):
    b, s, _ = qk.shape
    chunk = min(s, MLSTM_CHUNK)
    tile = lambda w: pl.BlockSpec((1, chunk, w), lambda bi, si: (bi, si, 0))
    per_b = lambda shape: pl.BlockSpec((1,) + shape, lambda bi, si: (bi,) + (0,) * len(shape))
    const = lambda shape: pl.BlockSpec(shape, lambda bi, si: (0, 0))
    state_shapes = ((H_M, HD_M, HD_M), (H_M, 1, HD_M), (H_M, 1, LANES), (CONV_W - 1, 2 * D_M))
    return pl.pallas_call(
        functools.partial(_mlstm_kernel, chunk=chunk),
        out_shape=(jax.ShapeDtypeStruct((b, s, D_M), BF16),)
        + tuple(jax.ShapeDtypeStruct((b,) + sh, F32) for sh in state_shapes),
        grid=(b, s // chunk),
        in_specs=[
            tile(2 * D_M), tile(D_M), tile(D_M), tile(LANES),
            pl.BlockSpec((1, 16, chunk), lambda bi, si: (bi, 0, si)),
            const((CONV_W, 2 * D_M)), const((1, 2 * D_M)), const((1, LANES)), const((16, 1)), const((1, D_M)),
        ] + [per_b(sh) for sh in state_shapes],
        out_specs=(tile(D_M),) + tuple(per_b(sh) for sh in state_shapes),
        scratch_shapes=[
            pltpu.VMEM((H_M, HD_M, HD_M), F32),
            pltpu.VMEM((H_M, 1, HD_M), F32),
            pltpu.VMEM((H_M, 1, LANES), F32),
            pltpu.VMEM((chunk + 8, 2 * D_M), F32),
        ],
        compiler_params=_cparams(("parallel", "arbitrary")),
        name="mlstm",
    )(qk, v, o, small, smallt, convw, convb, brow, bcol, ghead, c0, n0, m0b, conv0)


def _split3(x):
    hi = x.astype(BF16).astype(F32)
    r = x - hi
    mid = r.astype(BF16).astype(F32)
    lo = (r - mid).astype(BF16).astype(F32)
    return hi, mid, lo


def _fox_prep_kernel(*refs, n_past, n_new, s_valid, blk):
    if n_past:
        past_ref, pre_ref, bias_ref, lf_ref, hi_ref, mid_ref, lo_ref = refs
    else:
        pre_ref, bias_ref, lf_ref, hi_ref, mid_ref, lo_ref = refs
    tri = (lax.broadcasted_iota(jnp.int32, (blk, blk), 1) <= lax.broadcasted_iota(jnp.int32, (blk, blk), 0)).astype(BF16)
    carry = jnp.zeros((1, LANES), F32)
    start = 0
    while start < n_past + n_new:
        if start < n_past:
            r = min(blk, n_past - start)
            lf = jnp.concatenate([jnp.zeros((r, FOX_GATE_LANE), F32), past_ref[0, start:start + r, :],
                                  jnp.zeros((r, LANES - FOX_GATE_LANE - H_FOX), F32)], axis=1)
        else:
            r = min(blk, n_past + n_new - start)
            ns = start - n_past
            row = ns + lax.broadcasted_iota(jnp.int32, (r, LANES), 0)
            lf = jnp.where(row < s_valid, _log_sigmoid(pre_ref[0, ns:ns + r, :] + bias_ref[...]), 0.0)
            lf_ref[0, ns:ns + r, :] = lf
        cum = carry
        for piece in _split3(lf):
            cum = cum + jnp.dot(tri[0:r, 0:r], piece.astype(BF16), preferred_element_type=F32)
        carry = cum[r - 1:r, :]
        hi, mid, lo = _split3(cum)
        hi_ref[0, start:start + r, :] = hi
        mid_ref[0, start:start + r, :] = mid
        lo_ref[0, start:start + r, :] = lo
        start += r


def _fox_prep(past, pre, bias_row, s_valid):
    b, n_new = pre.shape[:2]
    n_past = 0 if past is None else past.shape[1]
    n = n_past + n_new
    full = lambda rows: pl.BlockSpec((1, rows, LANES), lambda bi: (bi, 0, 0))
    past_spec = pl.BlockSpec((1, n_past, H_FOX), lambda bi: (bi, 0, 0))
    in_specs = ([past_spec] if n_past else []) + [full(n_new), pl.BlockSpec((1, LANES), lambda bi: (0, 0))]
    args = ([past] if n_past else []) + [pre, bias_row]
    cum_shape = jax.ShapeDtypeStruct((b, n, LANES), F32)
    return pl.pallas_call(
        functools.partial(_fox_prep_kernel, n_past=n_past, n_new=n_new, s_valid=s_valid, blk=256),
        out_shape=(jax.ShapeDtypeStruct((b, n_new, LANES), F32), cum_shape, cum_shape, cum_shape),
        grid=(b,),
        in_specs=in_specs,
        out_specs=(full(n_new), full(n), full(n), full(n)),
        compiler_params=_cparams(("parallel",)),
        name="fox_prep",
    )(*args)


def _fox_pack_kernel(*refs, q_side, with_v):
    if with_v:
        x_ref, hi_ref, mid_ref, lo_ref, v_ref, out_ref, vt_ref = refs
        vt_ref[0] = v_ref[0].T.astype(BF16)
    else:
        x_ref, hi_ref, mid_ref, lo_ref, out_ref = refs
    tm = x_ref.shape[1]
    lane = lax.broadcasted_iota(jnp.int32, (tm, HD_FOX), 1)
    hi, mid, lo = hi_ref[0], mid_ref[0], lo_ref[0]
    for h in range(H_FOX):
        c = FOX_GATE_LANE + h
        pieces = (hi[:, c:c + 1], mid[:, c:c + 1], lo[:, c:c + 1])
        first = 0 if q_side else 3
        bias = jnp.where(jnp.logical_and(lane >= 3 - first, lane < 6 - first), -1.0 if q_side else 1.0, 0.0)
        for j, p in enumerate(pieces):
            bias = jnp.where(lane == first + j, p, bias)
        out_ref[0, h] = jnp.concatenate(
            [x_ref[0, :, h * HD_FOX:(h + 1) * HD_FOX].astype(BF16), bias.astype(BF16)], axis=1)


def _fox_pack(x, hi, mid, lo, row_offset, q_side, v=None):
    b, n, _ = x.shape
    tm = next(c for c in (512, 384, 256, LANES) if n % c == 0 and row_offset % c == 0)
    off = row_offset // tm
    tile = lambda w: pl.BlockSpec((1, tm, w), lambda bi, i: (bi, i, 0))
    piece = pl.BlockSpec((1, tm, LANES), lambda bi, i: (bi, off + i, 0))
    out_shape = [jax.ShapeDtypeStruct((b, H_FOX, n, 2 * HD_FOX), BF16)]
    out_specs = [pl.BlockSpec((1, H_FOX, tm, 2 * HD_FOX), lambda bi, i: (bi, 0, i, 0))]
    in_specs = [tile(D_FOX), piece, piece, piece]
    args = [x, hi, mid, lo]
    if v is not None:
        in_specs.append(tile(D_FOX))
        args.append(v)
        out_shape.append(jax.ShapeDtypeStruct((b, D_FOX, n), BF16))
        out_specs.append(pl.BlockSpec((1, D_FOX, tm), lambda bi, i: (bi, 0, i)))
    return pl.pallas_call(
        functools.partial(_fox_pack_kernel, q_side=q_side, with_v=v is not None),
        out_shape=tuple(out_shape),
        grid=(b, n // tm),
        in_specs=in_specs,
        out_specs=tuple(out_specs),
        compiler_params=_cparams(("parallel", "parallel")),
        name="fox_pack_q" if q_side else "fox_pack_kv",
    )(*args)


def _fox_attn_kernel(qa_ref, ka_ref, vt_ref, out_ref, m_scr, l_scr, acc_scr, *, p0, tq, tk, nk):
    qi, kj = pl.program_id(1), pl.program_id(2)

    @pl.when(kj == 0)
    def _():
        m_scr[...] = jnp.full(m_scr.shape, NEG_BIG, F32)
        l_scr[...] = jnp.zeros(l_scr.shape, F32)
        acc_scr[...] = jnp.zeros(acc_scr.shape, F32)

    def step(masked):
        col = lax.broadcasted_iota(jnp.int32, (1, 2 * HD_FOX), 1)
        scale = jnp.where(col < HD_FOX, HD_FOX ** -0.5, 1.0).astype(BF16)
        ones_rows = jnp.ones((16, tk), BF16)
        if masked:
            kpos = kj * tk + lax.broadcasted_iota(jnp.int32, (tk, tq), 0)
            qpos = p0 + qi * tq + lax.broadcasted_iota(jnp.int32, (tk, tq), 1)
            mask = kpos <= qpos
        for h in range(H_FOX):
            s = _nt_dot(ka_ref[0, h], qa_ref[0, h] * scale)
            if masked:
                s = jnp.where(mask, s, NEG_BIG)
            m_prev = m_scr[h]
            m_new = jnp.maximum(m_prev, jnp.max(s, axis=0, keepdims=True))
            p = jnp.exp(s - m_new)
            alpha = jnp.exp(m_prev - m_new)
            pv = jnp.dot(jnp.concatenate([vt_ref[0, h], ones_rows], axis=0), p.astype(BF16),
                         preferred_element_type=F32)
            l_scr[h] = alpha * l_scr[h] + pv[HD_FOX:HD_FOX + 1, :]
            acc_scr[h] = alpha * acc_scr[h] + pv[0:HD_FOX, :]
            m_scr[h] = m_new

    first_q = p0 + qi * tq
    last_q = first_q + tq - 1
    unmasked = (kj + 1) * tk - 1 <= first_q

    @pl.when(unmasked)
    def _():
        step(False)

    @pl.when(jnp.logical_and(jnp.logical_not(unmasked), kj * tk <= last_q))
    def _():
        step(True)

    @pl.when(kj == nk - 1)
    def _():
        o_t = jnp.concatenate([acc_scr[h] / l_scr[h] for h in range(H_FOX)], axis=0)
        out_ref[0] = o_t.T.astype(BF16)


def _fox_attn(qa, ka, vt, p0):
    b, _, sq, _ = qa.shape
    sk = ka.shape[2]
    tq = min(sq, 512)
    tk = 512 if sk % 512 == 0 else 384
    nq, nk = sq // tq, sk // tk
    last_tile = lambda qi: (p0 + (qi + 1) * tq - 1) // tk
    return pl.pallas_call(
        functools.partial(_fox_attn_kernel, p0=p0, tq=tq, tk=tk, nk=nk),
        out_shape=jax.ShapeDtypeStruct((b, sq, D_FOX), BF16),
        grid=(b, nq, nk),
        in_specs=[
            pl.BlockSpec((1, H_FOX, tq, 2 * HD_FOX), lambda bi, qi, kj: (bi, 0, qi, 0)),
            pl.BlockSpec((1, H_FOX, tk, 2 * HD_FOX), lambda bi, qi, kj: (bi, 0, jnp.minimum(kj, last_tile(qi)), 0)),
            pl.BlockSpec((1, H_FOX, HD_FOX, tk), lambda bi, qi, kj: (bi, 0, 0, jnp.minimum(kj, last_tile(qi)))),
        ],
        out_specs=pl.BlockSpec((1, tq, D_FOX), lambda bi, qi, kj: (bi, qi, 0)),
        scratch_shapes=[
            pltpu.VMEM((H_FOX, 1, tq), F32),
            pltpu.VMEM((H_FOX, 1, tq), F32),
            pltpu.VMEM((H_FOX, HD_FOX, tq), F32),
        ],
        compiler_params=_cparams(("parallel", "parallel", "arbitrary")),
        name="fox_attn",
    )(qa, ka, vt)


def _merge_kernel(x_ref, hm_ref, of_ref, cq_ref, gates_ref, mk_ref, mv_ref, wm_ref, wf_ref, wc_ref,
                  wo_ref, gffn_ref, xnew_ref, hn_ref, hnlo_ref):
    cq = cq_ref[0]
    heads = []
    for h in range(H_MEM):
        sl = slice(h * HD_MEM, (h + 1) * HD_MEM)
        s = _nt_dot(cq[:, sl], mk_ref[0, :, sl]) * (HD_MEM ** -0.5)
        p = jnp.exp(s - jnp.max(s, axis=1, keepdims=True))
        o = jnp.dot(p.astype(BF16), mv_ref[0, :, sl], preferred_element_type=F32)
        heads.append(o / jnp.sum(p, axis=1, keepdims=True))
    oc = jnp.concatenate(heads, axis=1).astype(BF16)
    a_m = jnp.dot(hm_ref[0], wm_ref[...], preferred_element_type=F32)
    a_f = jnp.dot(of_ref[0], wf_ref[...], preferred_element_type=F32)
    a_c = jnp.dot(oc, wc_ref[...], preferred_element_type=F32)
    g = gates_ref[0].astype(F32)
    merged = (_sigmoid(g[:, 0:D_MODEL]) * a_m + _sigmoid(g[:, D_MODEL:2 * D_MODEL]) * a_f
              + _sigmoid(g[:, 2 * D_MODEL:3 * D_MODEL]) * a_c)
    xn = x_ref[0] + jnp.dot(merged.astype(BF16), wo_ref[...], preferred_element_type=F32)
    xnew_ref[0] = xn
    hn = _rms(xn, gffn_ref[...])
    hn_hi = hn.astype(BF16)
    hn_ref[0] = hn_hi
    hnlo_ref[0] = (hn - hn_hi.astype(F32)).astype(BF16)


def _merge(x, hm, of, cq, gates, mk, mv, wm, wf, wc, wo, gffn):
    b, s, _ = x.shape
    ts = min(s, 512)
    tile = lambda w: pl.BlockSpec((1, ts, w), lambda bi, si: (bi, si, 0))
    const = lambda shape: pl.BlockSpec(shape, lambda bi, si: (0, 0))
    mem = pl.BlockSpec((1, mk.shape[1], D_MEM), lambda bi, si: (bi, 0, 0))
    return pl.pallas_call(
        _merge_kernel,
        out_shape=(jax.ShapeDtypeStruct((b, s, D_MODEL), F32), jax.ShapeDtypeStruct((b, s, D_MODEL), BF16),
                   jax.ShapeDtypeStruct((b, s, D_MODEL), BF16)),
        grid=(b, s // ts),
        in_specs=[tile(D_MODEL), tile(D_M), tile(D_FOX), tile(D_MEM), tile(N_BRANCH * D_MODEL), mem, mem,
                  const((D_M, D_MODEL)), const((D_FOX, D_MODEL)), const((D_MEM, D_MODEL)),
                  const((D_MODEL, D_MODEL)), const((1, D_MODEL))],
        out_specs=(tile(D_MODEL), tile(D_MODEL), tile(D_MODEL)),
        compiler_params=_cparams(("parallel", "parallel")),
        name="merge",
    )(x, hm, of, cq, gates, mk, mv, wm, wf, wc, wo, gffn)


def _extract_top16(s, key_io, val_scr, idx_scr, h, want_rank):
    rank = jnp.full(s.shape, float(PEER_TOPK), F32) if want_rank else None
    for r in range(PEER_TOPK):
        m = jnp.max(s, axis=0, keepdims=True)
        first = jnp.min(jnp.where(s == m, key_io, float(N_KEYS)), axis=0, keepdims=True)
        hit = key_io == first
        s = jnp.where(hit, -jnp.inf, s)
        val_scr[r, pl.ds(h, 1), :] = m
        if want_rank:
            rank = jnp.where(hit, float(r), rank)
        else:
            idx_scr[r, pl.ds(h, 1), :] = first
    return rank


def _dot3(a_hi, a_lo, b_hi, b_lo, dot):
    return dot(a_hi, b_hi) + dot(a_lo, b_hi) + dot(a_hi, b_lo)


def _router_kernel(hn_ref, hnlo_ref, wqt_ref, keys_ref, a0_ref, cnt_ref, r1_ref, b1_ref,
                   qr_scr, e1_scr, top0_scr, top1_scr, idx0_scr, cnt_scr, invz_scr):
    tp = hn_ref.shape[0]
    qr_scr[...] = _dot3(wqt_ref[0], wqt_ref[1], hn_ref[...], hnlo_ref[...], _nt_dot)
    key_io = lax.broadcasted_iota(jnp.int32, (N_KEYS, tp), 0).astype(F32)

    heads_per_trip = 8

    def scores_and_top16_group(hg, carry):
        for k in range(heads_per_trip):
            scores_and_top16(heads_per_trip * hg + k)
        return carry

    def scores_and_top16(h):
        base = pl.multiple_of(h * 2 * N_KEYS, 2 * N_KEYS)
        def scores(c):
            q = qr_scr[pl.ds(base + c * N_KEYS, N_KEYS), :]
            q_hi = q.astype(BF16)
            q_lo = (q - q_hi.astype(F32)).astype(BF16)
            return _dot3(keys_ref[0, 2 * h + c], keys_ref[1, 2 * h + c], q_hi, q_lo,
                         lambda a, b_: jnp.dot(a, b_, preferred_element_type=F32))

        s0, s1 = scores(0), scores(1)
        _extract_top16(s0, key_io, top0_scr, idx0_scr, h, False)
        rank1 = _extract_top16(s1, key_io, top1_scr, None, h, True)
        a0_ref[h] = jnp.exp(s0 - top0_scr[0, pl.ds(h, 1), :])
        e1_scr[h] = jnp.exp(s1 - top1_scr[0, pl.ds(h, 1), :])
        r1_ref[h] = rank1.astype(BF16)

    lax.fori_loop(0, PEER_HEADS // heads_per_trip, scores_and_top16_group, 0)

    top0 = [top0_scr[a] for a in range(PEER_TOPK)]
    top1 = [top1_scr[b] for b in range(PEER_TOPK)]
    cnt = [jnp.zeros((PEER_HEADS, tp), F32) for _ in range(PEER_TOPK)]
    for _ in range(PEER_TOPK):
        front = []
        for a in range(PEER_TOPK):
            nxt = jnp.full((PEER_HEADS, tp), -jnp.inf, F32)
            for bb in range(PEER_TOPK // (a + 1)):
                nxt = jnp.where(cnt[a] == float(bb), top1[bb], nxt)
            front.append(top0[a] + nxt)
        mx = functools.reduce(jnp.maximum, front)
        first = functools.reduce(jnp.minimum,
                                 [jnp.where(front[a] == mx, float(a), float(PEER_TOPK)) for a in range(PEER_TOPK)])
        cnt = [cnt[a] + jnp.where(first == float(a), 1.0, 0.0) for a in range(PEER_TOPK)]
    z = jnp.zeros((PEER_HEADS, tp), F32)
    for a in range(PEER_TOPK):
        za = jnp.zeros((PEER_HEADS, tp), F32)
        for bb in range(PEER_TOPK // (a + 1)):
            za = za + jnp.where(cnt[a] > float(bb), jnp.exp(top1[bb] - top1[0]), 0.0)
        z = z + jnp.exp(top0[a] - top0[0]) * za
        cnt_scr[a] = cnt[a]
    invz_scr[...] = 1.0 / z

    def counts_and_gates(h, carry):
        cnt_i = jnp.zeros((N_KEYS, tp), F32)
        for a in range(PEER_TOPK):
            cnt_i = jnp.where(key_io == idx0_scr[a, pl.ds(h, 1), :], cnt_scr[a, pl.ds(h, 1), :], cnt_i)
        cnt_ref[h] = cnt_i
        b1 = e1_scr[h] * invz_scr[pl.ds(h, 1), :]
        b1_ref[h] = b1.astype(BF16)
        return carry

    lax.fori_loop(0, PEER_HEADS, counts_and_gates, 0)


def _router(hn2d, hnlo2d, wqt, keys):
    t = hn2d.shape[0]
    tp = 2 * LANES
    shape = jax.ShapeDtypeStruct((PEER_HEADS, N_KEYS, t), F32)
    spec = pl.BlockSpec((PEER_HEADS, N_KEYS, tp), lambda i: (0, 0, i))
    slab_shape = jax.ShapeDtypeStruct((PEER_HEADS, N_KEYS, t), BF16)
    slab_spec = spec
    return pl.pallas_call(
        _router_kernel,
        out_shape=(shape, shape, slab_shape, slab_shape),
        grid=(t // tp,),
        in_specs=[
            pl.BlockSpec((tp, D_MODEL), lambda i: (i, 0)),
            pl.BlockSpec((tp, D_MODEL), lambda i: (i, 0)),
            pl.BlockSpec(wqt.shape, lambda i: (0, 0, 0)),
            pl.BlockSpec(keys.shape, lambda i: (0, 0, 0, 0)),
        ],
        out_specs=(spec, spec, slab_spec, slab_spec),
        scratch_shapes=[
            pltpu.VMEM((wqt.shape[1], tp), F32),
            pltpu.VMEM((PEER_HEADS, N_KEYS, tp), F32),
        ] + [pltpu.VMEM((PEER_TOPK, PEER_HEADS, tp), F32) for _ in range(4)] + [pltpu.VMEM((PEER_HEADS, tp), F32)],
        compiler_params=_cparams(("parallel",)),
        name="peer_router",
    )(hn2d, hnlo2d, wqt, keys)


def _gelu_tanh(x):
    k = -2.0 * 0.7978845608028654 * 1.4426950408889634
    return x / (1.0 + jnp.exp2(x * (k + (k * 0.044715) * (x * x))))


def _experts_kernel(hn_ref, u_ref, vt_ref, a0_ref, cnt_ref, r1_ref, b1_ref, x_ref, gfin_ref, out_ref,
                    acc_scr, pre0_scr, pre1_scr, w0_scr, w1_scr, *, rows_per_tile, n_tiles, final_norm):
    s = pl.program_id(1)
    tp = hn_ref.shape[0]
    te = u_ref.shape[0] // 2
    slab = 16
    slabs_per_row = N_KEYS // slab
    d_half = acc_scr.shape[0] // 2
    zero = jnp.zeros((), BF16)

    @pl.when(s == 0)
    def _():
        acc_scr[...] = jnp.zeros(acc_scr.shape, F32)
        pre1_scr[...] = jnp.zeros(pre1_scr.shape, F32)
        w0_scr[...] = jnp.zeros(w0_scr.shape, BF16)

    def pipeline_step(parity, pre_w, pre_r, w_w, w_r):
        e = 2 * s + parity
        cols = slice(parity * te, (parity + 1) * te)

        def stage_c(half):
            rows = slice(half * d_half, (half + 1) * d_half)
            acc_scr[rows, :] += jnp.dot(vt_ref[rows, cols], w_r[...], preferred_element_type=F32)

        def stage_a(half):
            rows = slice(half * (te // 2), (half + 1) * (te // 2))
            pre_w[rows, :] = _nt_dot(u_ref[parity * te + half * (te // 2):parity * te + (half + 1) * (te // 2), :],
                                     hn_ref[...])

        mxu_work = (lambda: stage_c(0), lambda: stage_a(0), lambda: stage_c(1), lambda: stage_a(1))

        valid = jnp.logical_and(e >= 1, e <= n_tiles)
        tile_b = jnp.clip(e - 1, 0, n_tiles - 1)
        for ib in range(rows_per_tile):
            for piece in mxu_work[ib * len(mxu_work) // rows_per_tile:(ib + 1) * len(mxu_work) // rows_per_tile]:
                piece()
            i = tile_b * rows_per_tile + ib
            for lanes in (slice(0, tp // 2), slice(tp // 2, tp)):
                gates = [jnp.zeros((slab, tp // 2), BF16) for _ in range(slabs_per_row)]
                for h in range(PEER_HEADS):
                    a_b = jnp.broadcast_to(a0_ref[h, pl.ds(i, 1), lanes], (slab, tp // 2)).astype(BF16)
                    c_b = jnp.broadcast_to(cnt_ref[h, pl.ds(i, 1), lanes], (slab, tp // 2)).astype(BF16)
                    for g in range(slabs_per_row):
                        key_rows = slice(g * slab, (g + 1) * slab)
                        gates[g] = gates[g] + a_b * jnp.where(r1_ref[h, key_rows, lanes] < c_b,
                                                              b1_ref[h, key_rows, lanes], zero)
                for g in range(slabs_per_row):
                    rows = slice(ib * N_KEYS + g * slab, ib * N_KEYS + (g + 1) * slab)
                    w = gates[g] * _gelu_tanh(pre_r[rows, lanes]).astype(BF16)
                    w_w[rows, lanes] = jnp.where(valid, w, zero)

    pipeline_step(0, pre0_scr, pre1_scr, w1_scr, w0_scr)
    pipeline_step(1, pre1_scr, pre0_scr, w0_scr, w1_scr)

    @pl.when(s == n_tiles // 2)
    def _():
        xo = x_ref[...] + acc_scr[...].T
        if final_norm:
            xo = _rms(xo, gfin_ref[...])
        out_ref[...] = xo


def _experts(hn2d, u_bf, vt_bf, layer, a0, cnt, r1, b1, x2d, gfin, final_norm):
    t = hn2d.shape[0]
    tp = 512
    rows_per_tile = 4
    te = rows_per_tile * N_KEYS
    n_tiles = u_bf.shape[1] // te
    rspec = pl.BlockSpec((PEER_HEADS, N_KEYS, tp), lambda ti, e: (0, 0, ti))
    sspec = rspec
    return pl.pallas_call(
        functools.partial(_experts_kernel, rows_per_tile=rows_per_tile, n_tiles=n_tiles, final_norm=final_norm),
        out_shape=jax.ShapeDtypeStruct((t, D_MODEL), F32),
        grid=(t // tp, n_tiles // 2 + 1),
        in_specs=[
            pl.BlockSpec((tp, D_MODEL), lambda ti, s: (ti, 0)),
            pl.BlockSpec((None, 2 * te, D_MODEL), lambda ti, s: (layer, jnp.minimum(s, n_tiles // 2 - 1), 0)),
            pl.BlockSpec((None, D_MODEL, 2 * te), lambda ti, s: (layer, 0, jnp.maximum(s - 1, 0))),
            rspec, rspec, sspec, sspec,
            pl.BlockSpec((tp, D_MODEL), lambda ti, s: (ti, 0)),
            pl.BlockSpec((1, D_MODEL), lambda ti, s: (0, 0)),
        ],
        out_specs=pl.BlockSpec((tp, D_MODEL), lambda ti, s: (ti, 0)),
        scratch_shapes=[
            pltpu.VMEM((D_MODEL, tp), F32),
            pltpu.VMEM((te, tp), F32), pltpu.VMEM((te, tp), F32),
            pltpu.VMEM((te, tp), BF16), pltpu.VMEM((te, tp), BF16),
        ],
        compiler_params=_cparams(("parallel", "arbitrary")),
        name="peer_experts",
    )(hn2d, u_bf, vt_bf, a0, cnt, r1, b1, x2d, gfin)


def _hi_lo(a):
    hi = a.astype(BF16)
    return jnp.stack([hi, (a - hi.astype(F32)).astype(BF16)])


def _prep_layer_weights(p):
    w_in = p['w_in']
    o_mi = 4 * D_M
    o_fq = o_mi + 2 * H_M
    o_ff = o_fq + 3 * D_FOX
    o_cq = o_ff + H_FOX
    w_big = jnp.concatenate([w_in[:, :o_mi], w_in[:, o_fq:o_ff], w_in[:, o_cq:]], axis=1).astype(BF16)
    w_big = w_big.reshape(D_MODEL, -1, PROJ_TILE).transpose(1, 0, 2)
    w_small = jnp.concatenate([w_in[:, o_mi:o_fq], w_in[:, o_ff:o_cq]], axis=1)
    w_small = jnp.pad(w_small, ((0, 0), (0, LANES - w_small.shape[1])))
    w_small_hi = w_small.astype(BF16)
    w_small = jnp.stack([w_small_hi, (w_small - w_small_hi.astype(F32)).astype(BF16)])
    bias = jnp.concatenate([p['b_m_i'], p['b_m_f'], p['b_fox_f']]).astype(F32)
    return dict(
        norm_mix=p['norm_mix'].reshape(1, D_MODEL), w_big=w_big, w_small=w_small,
        conv_w=p['conv_w'], conv_b=p['conv_b'].reshape(1, 2 * D_M),
        bias_row=jnp.pad(bias, (0, LANES - 16)).reshape(1, LANES), bias_col=bias.reshape(16, 1),
        norm_m_head=p['norm_m_head'].reshape(1, D_M),
        w_up_m=p['w_up_m'].astype(BF16), w_up_f=p['w_up_f'].astype(BF16), w_up_c=p['w_up_c'].astype(BF16),
        w_out=p['w_out'].astype(BF16), norm_ffn=p['norm_ffn'].reshape(1, D_MODEL),
        wqt=_hi_lo(p['peer_wq'].T),
        keys=_hi_lo(p['peer_keys'].reshape(2 * PEER_HEADS, N_KEYS, N_KEYS)),
    )


def _layer(x, mem_k, mem_v, conv_prev, c0, n0, m0, fk_past, fv_past, flf_past, w, peer_tables, layer, gfin, final_norm):
    b, s, _ = x.shape
    t = b * s
    p0 = 0 if fk_past is None else fk_past.shape[1]
    qk, v_m, o_m, fq, fk, fv, cq, gates, small = _norm_proj(x.reshape(t, D_MODEL), w['norm_mix'], w['w_big'], w['w_small'])

    small3 = small.reshape(b, s, LANES)
    smallt = small3[:, :, :16].transpose(0, 2, 1)
    m0b = jnp.broadcast_to(m0[:, :, None, None], (b, H_M, 1, LANES))
    hm, c1, n1, m1b, conv_new = _mlstm(
        qk.reshape(b, s, 2 * D_M), v_m.reshape(b, s, D_M), o_m.reshape(b, s, D_M), small3, smallt,
        w['conv_w'], w['conv_b'], w['bias_row'], w['bias_col'], w['norm_m_head'],
        c0, n0[:, :, None, :], m0b, conv_prev)
    n1, m1 = n1[:, :, 0, :], m1b[:, :, 0, 0]

    k_f = fk.reshape(b, s, H_FOX, HD_FOX)
    v_f = fv.reshape(b, s, H_FOX, HD_FOX)
    s_pad = -(-s // LANES) * LANES
    pad_rows = lambda a: jnp.pad(a, ((0, 0), (0, s_pad - s), (0, 0)))
    past = flf_past.astype(F32) if p0 else None
    lf_all, hi, mid, lo = _fox_prep(past, pad_rows(small3), w['bias_row'], s)
    lf_f = lf_all[:, :s, FOX_GATE_LANE:FOX_GATE_LANE + H_FOX]
    sk = p0 + s_pad
    k_all, v_all = pad_rows(fk.reshape(b, s, D_FOX)), pad_rows(fv.reshape(b, s, D_FOX))
    if p0:
        k_all = jnp.concatenate([fk_past.reshape(b, p0, D_FOX).astype(F32), k_all], axis=1)
        v_all = jnp.concatenate([fv_past.reshape(b, p0, D_FOX).astype(F32), v_all], axis=1)
    qa = _fox_pack(pad_rows(fq.reshape(b, s, D_FOX)), hi, mid, lo, p0, True)[0]
    ka, vt_all = _fox_pack(k_all, hi, mid, lo, 0, False, v=v_all)
    o_f = _fox_attn(qa, ka, vt_all.reshape(b, H_FOX, HD_FOX, sk), p0)[:, :s]

    x_new, hn, hn_lo = _merge(x, hm, o_f, cq.reshape(b, s, D_MEM), gates.reshape(b, s, N_BRANCH * D_MODEL),
                       mem_k.reshape(b, -1, D_MEM).astype(BF16), mem_v.reshape(b, -1, D_MEM).astype(BF16),
                       w['w_up_m'], w['w_up_f'], w['w_up_c'], w['w_out'], w['norm_ffn'])

    hn2d = hn.reshape(t, D_MODEL)
    a0, cnt, r1, b1 = _router(hn2d, hn_lo.reshape(t, D_MODEL), w['wqt'], w['keys'])
    x_out = _experts(hn2d, peer_tables[0], peer_tables[1], layer, a0, cnt, r1, b1, x_new.reshape(t, D_MODEL),
                     gfin, final_norm)
    return x_out.reshape(b, s, D_MODEL), conv_new, c1, n1, m1, k_f, v_f, lf_f


def kernel(x_prompt, x_sample, mem_prompt, cache_fox_k, cache_fox_v, cache_fox_lf, state_mlstm_c, state_mlstm_n, state_mlstm_m, state_conv, cache_mem_k, cache_mem_v, norm_mix, w_in, conv_w, conv_b, b_m_i, b_m_f, norm_m_head, b_fox_f, norm_mem, w_mem_kv, w_up_m, w_up_f, w_up_c, w_out, norm_ffn, peer_wq, peer_keys, peer_u, peer_v, norm_final):
    depth = w_in.shape[0]
    names = dict(norm_mix=norm_mix, w_in=w_in, conv_w=conv_w, conv_b=conv_b, b_m_i=b_m_i, b_m_f=b_m_f,
                 norm_m_head=norm_m_head, b_fox_f=b_fox_f, w_up_m=w_up_m, w_up_f=w_up_f, w_up_c=w_up_c,
                 w_out=w_out, norm_ffn=norm_ffn, peer_wq=peer_wq, peer_keys=peer_keys, peer_u=peer_u,
                 peer_v=peer_v)
    weights = [_prep_layer_weights({k: a[l] for k, a in names.items()}) for l in range(depth)]
    peer_tables = (peer_u.astype(BF16), jnp.swapaxes(peer_v, 1, 2).astype(BF16))
    gfin = norm_final.reshape(1, D_MODEL)
    bp, n_mem = mem_prompt.shape[0], mem_prompt.shape[1]

    xp = x_prompt
    new_p = [[] for _ in range(9)]
    for l in range(depth):
        mk, mv = _norm_matmul(mem_prompt.reshape(bp * n_mem, D_MODEL), norm_mem[l].reshape(1, D_MODEL),
                              w_mem_kv[l].astype(BF16))
        mk = mk.reshape(bp, n_mem, H_MEM, HD_MEM)
        mv = mv.reshape(bp, n_mem, H_MEM, HD_MEM)
        xp, conv1, c1, n1, m1, kf, vf, lff = _layer(
            xp, mk, mv, jnp.zeros((bp, CONV_W - 1, 2 * D_M), F32),
            jnp.zeros((bp, H_M, HD_M, HD_M), F32), jnp.zeros((bp, H_M, HD_M), F32), jnp.zeros((bp, H_M), F32),
            None, None, None, weights[l], peer_tables, l, gfin, l == depth - 1)
        for lst, a in zip(new_p, (kf, vf, lff, c1, n1, m1, conv1, mk, mv)):
            lst.append(a)
    outs_p = [jnp.stack(a) for a in new_p]

    xs = x_sample
    new_s = [[] for _ in range(7)]
    for l in range(depth):
        xs, conv1, c1, n1, m1, kf, vf, lff = _layer(
            xs, cache_mem_k[l], cache_mem_v[l], state_conv[l], state_mlstm_c[l], state_mlstm_n[l],
            state_mlstm_m[l], cache_fox_k[l], cache_fox_v[l], cache_fox_lf[l], weights[l], peer_tables, l, gfin,
            l == depth - 1)
        for lst, a in zip(new_s, (kf, vf, lff, c1, n1, m1, conv1)):
            lst.append(a)
    outs_s = [jnp.stack(a) for a in new_s]

    return (xp, xs, *outs_p, *outs_s)
```

```python
import functools

import jax
import jax.numpy as jnp
from jax import lax
from jax.experimental import pallas as pl
from jax.experimental.pallas import tpu as pltpu

F32 = jnp.float32
BF16 = jnp.bfloat16
EPS = 1e-6
NEG_BIG = -1e30

D_MODEL = 1024
H_M, HD_M = 4, 128
D_M = H_M * HD_M
CONV_W = 4
H_FOX, HD_FOX = 8, 64
D_FOX = H_FOX * HD_FOX
H_MEM, HD_MEM = 4, 128
D_MEM = H_MEM * HD_MEM
N_BRANCH = 3
PEER_HEADS = 8
N_KEYS = 128
PEER_TOPK = 16
LANES = 128
FOX_GATE_LANE = 2 * H_M
MLSTM_CHUNK = 256
PROJ_TILE = 512
VMEM_LIMIT = 56 * 1024 * 1024


def _cparams(sem):
    return pltpu.CompilerParams(dimension_semantics=sem, vmem_limit_bytes=VMEM_LIMIT)


def _nt_dot(a, b):
    return lax.dot_general(a, b, (((1,), (1,)), ((), ())), preferred_element_type=F32)


def _rms(x, g):
    return x * lax.rsqrt(jnp.mean(x * x, axis=-1, keepdims=True) + EPS) * g


def _log_sigmoid(x):
    return jnp.minimum(x, 0.0) - jnp.log(1.0 + jnp.exp(-jnp.abs(x)))


def _sigmoid(x):
    return 1.0 / (1.0 + jnp.exp(-x))


def _norm_proj_kernel(x_ref, g_ref, w_ref, ws_ref, qk_ref, v_ref, o_ref, fq_ref, fk_ref, fv_ref,
                      cq_ref, gates_ref, small_ref):
    hf = _rms(x_ref[...], g_ref[...])
    h = hf.astype(BF16)
    h_lo = (hf - h.astype(F32)).astype(BF16)
    small_ref[...] = (jnp.dot(h, ws_ref[0], preferred_element_type=F32)
                      + jnp.dot(h_lo, ws_ref[0], preferred_element_type=F32)
                      + jnp.dot(h, ws_ref[1], preferred_element_type=F32))

    def tile(j):
        return jnp.dot(h, w_ref[j], preferred_element_type=F32)

    qk_ref[:, 0:PROJ_TILE] = tile(0)
    qk_ref[:, PROJ_TILE:2 * PROJ_TILE] = tile(1)
    v_ref[...] = tile(2).astype(BF16)
    o_ref[...] = tile(3).astype(BF16)
    fq_ref[...] = tile(4).astype(BF16)
    fk_ref[...] = tile(5)
    fv_ref[...] = tile(6)
    cq_ref[...] = tile(7).astype(BF16)
    for j in range(8, w_ref.shape[0]):
        gates_ref[:, (j - 8) * PROJ_TILE:(j - 7) * PROJ_TILE] = tile(j).astype(BF16)


def _norm_proj(x2d, g, w_big, w_small):
    t = x2d.shape[0]
    tm = 512
    row = lambda i: (i, 0)
    out_widths = ((2 * D_M, F32),
                  (D_M, BF16),
                  (D_M, BF16),
                  (D_FOX, BF16),
                  (D_FOX, F32),
                  (D_FOX, F32),
                  (D_MEM, BF16),
                  (N_BRANCH * D_MODEL, BF16),
                  (LANES, F32))
    resident = lambda shape: pl.BlockSpec(shape, lambda i: (0,) * len(shape), pipeline_mode=pl.Buffered(1))
    return pl.pallas_call(
        _norm_proj_kernel,
        out_shape=tuple(jax.ShapeDtypeStruct((t, wd), dt) for wd, dt in out_widths),
        grid=(t // tm,),
        in_specs=[
            pl.BlockSpec((tm, D_MODEL), row),
            resident((1, D_MODEL)),
            resident(w_big.shape),
            resident(w_small.shape),
        ],
        out_specs=tuple(pl.BlockSpec((tm, wd), row) for wd, _ in out_widths),
        compiler_params=_cparams(("parallel",)),
        name="norm_proj",
    )(x2d, g, w_big, w_small)


def _norm_matmul_kernel(x_ref, g_ref, w_ref, k_ref, v_ref):
    h = _rms(x_ref[...], g_ref[...]).astype(BF16)
    k_ref[...] = jnp.dot(h, w_ref[:, 0:D_MEM], preferred_element_type=F32)
    v_ref[...] = jnp.dot(h, w_ref[:, D_MEM:2 * D_MEM], preferred_element_type=F32)


def _norm_matmul(x2d, g, w):
    t = x2d.shape[0]
    tm = 512
    out = jax.ShapeDtypeStruct((t, D_MEM), F32)
    return pl.pallas_call(
        _norm_matmul_kernel,
        out_shape=(out, out),
        grid=(t // tm,),
        in_specs=[
            pl.BlockSpec((tm, D_MODEL), lambda i: (i, 0)),
            pl.BlockSpec((1, D_MODEL), lambda i: (0, 0)),
            pl.BlockSpec((D_MODEL, 2 * D_MEM), lambda i: (0, 0)),
        ],
        out_specs=(pl.BlockSpec((tm, D_MEM), lambda i: (i, 0)), pl.BlockSpec((tm, D_MEM), lambda i: (i, 0))),
        compiler_params=_cparams(("parallel",)),
        name="norm_matmul",
    )(x2d, g, w)


def _mlstm_kernel(qk_ref, v_ref, o_ref, small_ref, smallt_ref, convw_ref, convb_ref, brow_ref,
                  bcol_ref, ghead_ref, c0_ref, n0_ref, m0_ref, conv0_ref,
                  hm_ref, c_out_ref, n_out_ref, m_out_ref, conv_out_ref,
                  c_scr, n_scr, m_scr, xp_scr, *, chunk):
    L = chunk
    si = pl.program_id(1)

    @pl.when(si == 0)
    def _():
        c_scr[...] = c0_ref[0]
        n_scr[...] = n0_ref[0]
        m_scr[...] = m0_ref[0]
        xp_scr[5:8, :] = conv0_ref[0]

    xp_scr[8:8 + L, :] = qk_ref[0]
    y = convb_ref[...] + convw_ref[0:1, :] * xp_scr[5:5 + L, :]
    for j in range(1, CONV_W):
        y = y + convw_ref[j:j + 1, :] * xp_scr[5 + j:5 + j + L, :]
    y = y * _sigmoid(y)
    tail = xp_scr[5 + L:8 + L, :]
    xp_scr[5:8, :] = tail
    conv_out_ref[0] = tail

    t_io = lax.broadcasted_iota(jnp.int32, (L, L), 0)
    s_io = lax.broadcasted_iota(jnp.int32, (L, L), 1)
    causal = s_io <= t_io
    ones_col = (lax.broadcasted_iota(jnp.int32, (L, HD_M), 1) == 0).astype(BF16)

    small = small_ref[0]
    smallt = smallt_ref[0]
    brow = brow_ref[...]
    bcol = bcol_ref[...]
    outs = []
    for h in range(H_M):
        q = y[:, h * HD_M:(h + 1) * HD_M]
        k = y[:, D_M + h * HD_M:D_M + (h + 1) * HD_M] * (HD_M ** -0.5)
        v_aug = jnp.concatenate([v_ref[0, :, h * HD_M:(h + 1) * HD_M], ones_col], axis=1)
        ig_col = small[:, h:h + 1] + brow[:, h:h + 1]
        ig_row = smallt[h:h + 1, :] + bcol[h:h + 1, :]
        lf_col = _log_sigmoid(small[:, H_M + h:H_M + h + 1] + brow[:, H_M + h:H_M + h + 1])
        lf_row = _log_sigmoid(smallt[H_M + h:H_M + h + 1, :] + bcol[H_M + h:H_M + h + 1, :])
        bcum_col = jnp.sum(jnp.where(causal, lf_row, 0.0), axis=1, keepdims=True)
        bcum_row = jnp.sum(jnp.where(t_io <= s_io, lf_col, 0.0), axis=0, keepdims=True)
        m_prev = m_scr[h][:, 0:1]
        dmat = jnp.where(causal, bcum_col - bcum_row + ig_row, -jnp.inf)
        inter = bcum_col + m_prev
        m_t = jnp.maximum(inter, jnp.max(dmat, axis=1, keepdims=True))
        w_intra = jnp.exp(dmat - m_t)
        w_state = jnp.exp(inter - m_t)
        qb = q.astype(BF16)
        sw = _nt_dot(qb, k.astype(BF16)) * w_intra
        c_prev = c_scr[h]
        n_prev = n_scr[h]
        intra = jnp.dot(sw.astype(BF16), v_aug, preferred_element_type=F32)
        num = intra[:, 0:HD_M] + w_state * jnp.dot(qb, c_prev.astype(BF16), preferred_element_type=F32)
        den = intra[:, HD_M:HD_M + 1] + w_state * jnp.sum(q * n_prev, axis=1, keepdims=True)
        hh = num / jnp.maximum(jnp.abs(den), jnp.exp(-m_t))
        hh = _rms(hh, ghead_ref[:, h * HD_M:(h + 1) * HD_M])
        outs.append(hh * _sigmoid(o_ref[0, :, h * HD_M:(h + 1) * HD_M].astype(F32)))
        bl = bcum_row[:, L - 1:L]
        m_new = jnp.maximum(bl + m_prev, jnp.max(bl - bcum_row + ig_row, axis=1, keepdims=True))
        decay = jnp.exp(bl + m_prev - m_new)
        ws_col = jnp.exp(bl - bcum_col + ig_col - m_new)
        kw = k * ws_col
        c_scr[h] = decay * c_prev + jnp.dot(kw.T.astype(BF16), v_ref[0, :, h * HD_M:(h + 1) * HD_M],
                                            preferred_element_type=F32)
        n_scr[h] = decay * n_prev + jnp.sum(kw, axis=0, keepdims=True)
        m_scr[h] = jnp.broadcast_to(m_new, (1, LANES))
    hm_ref[0] = jnp.concatenate(outs, axis=1).astype(BF16)
    c_out_ref[0] = c_scr[...]
    n_out_ref[0] = n_scr[...]
    m_out_ref[0] = m_scr[...]


def _mlstm(qk, v, o, small, smallt, convw, convb, brow, bcol, ghead, c0, n0, m0b, conv0):
    b, s, _ = qk.shape
    chunk = min(s, MLSTM_CHUNK)
    tile = lambda w: pl.BlockSpec((1, chunk, w), lambda bi, si: (bi, si, 0))
    per_b = lambda shape: pl.BlockSpec((1,) + shape, lambda bi, si: (bi,) + (0,) * len(shape))
    const = lambda shape: pl.BlockSpec(shape, lambda bi, si: (0, 0))
    state_shapes = ((H_M, HD_M, HD_M), (H_M, 1, HD_M), (H_M, 1, LANES), (CONV_W - 1, 2 * D_M))
    return pl.pallas_call(
        functools.partial(_mlstm_kernel, chunk=chunk),
        out_shape=(jax.ShapeDtypeStruct((b, s, D_M), BF16),)
        + tuple(jax.ShapeDtypeStruct((b,) + sh, F32) for sh in state_shapes),
        grid=(b, s // chunk),
        in_specs=[
            tile(2 * D_M), tile(D_M), tile(D_M), tile(LANES),
            pl.BlockSpec((1, 16, chunk), lambda bi, si: (bi, 0, si)),
            const((CONV_W, 2 * D_M)), const((1, 2 * D_M)), const((1, LANES)), const((16, 1)), const((1, D_M)),
        ] + [per_b(sh) for sh in state_shapes],
        out_specs=(tile(D_M),) + tuple(per_b(sh) for sh in state_shapes),
        scratch_shapes=[
            pltpu.VMEM((H_M, HD_M, HD_M), F32),
            pltpu.VMEM((H_M, 1, HD_M), F32),
            pltpu.VMEM((H_M, 1, LANES), F32),
            pltpu.VMEM((chunk + 8, 2 * D_M), F32),
        ],
        compiler_params=_cparams(("parallel", "arbitrary")),
        name="mlstm",
    )(qk, v, o, small, smallt, convw, convb, brow, bcol, ghead, c0, n0, m0b, conv0)


def _split3(x):
    hi = x.astype(BF16).astype(F32)
    r = x - hi
    mid = r.astype(BF16).astype(F32)
    lo = (r - mid).astype(BF16).astype(F32)
    return hi, mid, lo


def _fox_prep_kernel(*refs, n_past, n_new, s_valid, blk):
    if n_past:
        past_ref, pre_ref, bias_ref, lf_ref, hi_ref, mid_ref, lo_ref = refs
    else:
        pre_ref, bias_ref, lf_ref, hi_ref, mid_ref, lo_ref = refs
    tri = (lax.broadcasted_iota(jnp.int32, (blk, blk), 1) <= lax.broadcasted_iota(jnp.int32, (blk, blk), 0)).astype(BF16)
    carry = jnp.zeros((1, LANES), F32)
    start = 0
    while start < n_past + n_new:
        if start < n_past:
            r = min(blk, n_past - start)
            lf = jnp.concatenate([jnp.zeros((r, FOX_GATE_LANE), F32), past_ref[0, start:start + r, :],
                                  jnp.zeros((r, LANES - FOX_GATE_LANE - H_FOX), F32)], axis=1)
        else:
            r = min(blk, n_past + n_new - start)
            ns = start - n_past
            row = ns + lax.broadcasted_iota(jnp.int32, (r, LANES), 0)
            lf = jnp.where(row < s_valid, _log_sigmoid(pre_ref[0, ns:ns + r, :] + bias_ref[...]), 0.0)
            lf_ref[0, ns:ns + r, :] = lf
        cum = carry
        for piece in _split3(lf):
            cum = cum + jnp.dot(tri[0:r, 0:r], piece.astype(BF16), preferred_element_type=F32)
        carry = cum[r - 1:r, :]
        hi, mid, lo = _split3(cum)
        hi_ref[0, start:start + r, :] = hi
        mid_ref[0, start:start + r, :] = mid
        lo_ref[0, start:start + r, :] = lo
        start += r


def _fox_prep(past, pre, bias_row, s_valid):
    b, n_new = pre.shape[:2]
    n_past = 0 if past is None else past.shape[1]
    n = n_past + n_new
    full = lambda rows: pl.BlockSpec((1, rows, LANES), lambda bi: (bi, 0, 0))
    past_spec = pl.BlockSpec((1, n_past, H_FOX), lambda bi: (bi, 0, 0))
    in_specs = ([past_spec] if n_past else []) + [full(n_new), pl.BlockSpec((1, LANES), lambda bi: (0, 0))]
    args = ([past] if n_past else []) + [pre, bias_row]
    cum_shape = jax.ShapeDtypeStruct((b, n, LANES), F32)
    return pl.pallas_call(
        functools.partial(_fox_prep_kernel, n_past=n_past, n_new=n_new, s_valid=s_valid, blk=256),
        out_shape=(jax.ShapeDtypeStruct((b, n_new, LANES), F32), cum_shape, cum_shape, cum_shape),
        grid=(b,),
        in_specs=in_specs,
        out_specs=(full(n_new), full(n), full(n), full(n)),
        compiler_params=_cparams(("parallel",)),
        name="fox_prep",
    )(*args)


def _fox_pack_kernel(*refs, q_side, with_v):
    if with_v:
        x_ref, hi_ref, mid_ref, lo_ref, v_ref, out_ref, vt_ref = refs
        vt_ref[0] = v_ref[0].T.astype(BF16)
    else:
        x_ref, hi_ref, mid_ref, lo_ref, out_ref = refs
    tm = x_ref.shape[1]
    lane = lax.broadcasted_iota(jnp.int32, (tm, HD_FOX), 1)
    hi, mid, lo = hi_ref[0], mid_ref[0], lo_ref[0]
    for h in range(H_FOX):
        c = FOX_GATE_LANE + h
        pieces = (hi[:, c:c + 1], mid[:, c:c + 1], lo[:, c:c + 1])
        first = 0 if q_side else 3
        bias = jnp.where(jnp.logical_and(lane >= 3 - first, lane < 6 - first), -1.0 if q_side else 1.0, 0.0)
        for j, p in enumerate(pieces):
            bias = jnp.where(lane == first + j, p, bias)
        out_ref[0, h] = jnp.concatenate(
            [x_ref[0, :, h * HD_FOX:(h + 1) * HD_FOX].astype(BF16), bias.astype(BF16)], axis=1)


def _fox_pack(x, hi, mid, lo, row_offset, q_side, v=None):
    b, n, _ = x.shape
    tm = next(c for c in (512, 384, 256, LANES) if n % c == 0 and row_offset % c == 0)
    off = row_offset // tm
    tile = lambda w: pl.BlockSpec((1, tm, w), lambda bi, i: (bi, i, 0))
    piece = pl.BlockSpec((1, tm, LANES), lambda bi, i: (bi, off + i, 0))
    out_shape = [jax.ShapeDtypeStruct((b, H_FOX, n, 2 * HD_FOX), BF16)]
    out_specs = [pl.BlockSpec((1, H_FOX, tm, 2 * HD_FOX), lambda bi, i: (bi, 0, i, 0))]
    in_specs = [tile(D_FOX), piece, piece, piece]
    args = [x, hi, mid, lo]
    if v is not None:
        in_specs.append(tile(D_FOX))
        args.append(v)
        out_shape.append(jax.ShapeDtypeStruct((b, D_FOX, n), BF16))
        out_specs.append(pl.BlockSpec((1, D_FOX, tm), lambda bi, i: (bi, 0, i)))
    return pl.pallas_call(
        functools.partial(_fox_pack_kernel, q_side=q_side, with_v=v is not None),
        out_shape=tuple(out_shape),
        grid=(b, n // tm),
        in_specs=in_specs,
        out_specs=tuple(out_specs),
        compiler_params=_cparams(("parallel", "parallel")),
        name="fox_pack_q" if q_side else "fox_pack_kv",
    )(*args)


def _fox_attn_kernel(qa_ref, ka_ref, vt_ref, out_ref, m_scr, l_scr, acc_scr, *, p0, tq, tk, nk):
    qi, kj = pl.program_id(1), pl.program_id(2)

    @pl.when(kj == 0)
    def _():
        m_scr[...] = jnp.full(m_scr.shape, NEG_BIG, F32)
        l_scr[...] = jnp.zeros(l_scr.shape, F32)
        acc_scr[...] = jnp.zeros(acc_scr.shape, F32)

    def step(masked):
        col = lax.broadcasted_iota(jnp.int32, (1, 2 * HD_FOX), 1)
        scale = jnp.where(col < HD_FOX, HD_FOX ** -0.5, 1.0).astype(BF16)
        ones_rows = jnp.ones((16, tk), BF16)
        if masked:
            kpos = kj * tk + lax.broadcasted_iota(jnp.int32, (tk, tq), 0)
            qpos = p0 + qi * tq + lax.broadcasted_iota(jnp.int32, (tk, tq), 1)
            mask = kpos <= qpos
        for h in range(H_FOX):
            s = _nt_dot(ka_ref[0, h], qa_ref[0, h] * scale)
            if masked:
                s = jnp.where(mask, s, NEG_BIG)
            m_prev = m_scr[h]
            m_new = jnp.maximum(m_prev, jnp.max(s, axis=0, keepdims=True))
            p = jnp.exp(s - m_new)
            alpha = jnp.exp(m_prev - m_new)
            pv = jnp.dot(jnp.concatenate([vt_ref[0, h], ones_rows], axis=0), p.astype(BF16),
                         preferred_element_type=F32)
            l_scr[h] = alpha * l_scr[h] + pv[HD_FOX:HD_FOX + 1, :]
            acc_scr[h] = alpha * acc_scr[h] + pv[0:HD_FOX, :]
            m_scr[h] = m_new

    first_q = p0 + qi * tq
    last_q = first_q + tq - 1
    unmasked = (kj + 1) * tk - 1 <= first_q

    @pl.when(unmasked)
    def _():
        step(False)

    @pl.when(jnp.logical_and(jnp.logical_not(unmasked), kj * tk <= last_q))
    def _():
        step(True)

    @pl.when(kj == nk - 1)
    def _():
        o_t = jnp.concatenate([acc_scr[h] / l_scr[h] for h in range(H_FOX)], axis=0)
        out_ref[0] = o_t.T.astype(BF16)


def _fox_attn(qa, ka, vt, p0):
    b, _, sq, _ = qa.shape
    sk = ka.shape[2]
    tq = min(sq, 512)
    tk = 512 if sk % 512 == 0 else 384
    nq, nk = sq // tq, sk // tk
    last_tile = lambda qi: (p0 + (qi + 1) * tq - 1) // tk
    return pl.pallas_call(
        functools.partial(_fox_attn_kernel, p0=p0, tq=tq, tk=tk, nk=nk),
        out_shape=jax.ShapeDtypeStruct((b, sq, D_FOX), BF16),
        grid=(b, nq, nk),
        in_specs=[
            pl.BlockSpec((1, H_FOX, tq, 2 * HD_FOX), lambda bi, qi, kj: (bi, 0, qi, 0)),
            pl.BlockSpec((1, H_FOX, tk, 2 * HD_FOX), lambda bi, qi, kj: (bi, 0, jnp.minimum(kj, last_tile(qi)), 0)),
            pl.BlockSpec((1, H_FOX, HD_FOX, tk), lambda bi, qi, kj: (bi, 0, 0, jnp.minimum(kj, last_tile(qi)))),
        ],
        out_specs=pl.BlockSpec((1, tq, D_FOX), lambda bi, qi, kj: (bi, qi, 0)),
        scratch_shapes=[
            pltpu.VMEM((H_FOX, 1, tq), F32),
            pltpu.VMEM((H_FOX, 1, tq), F32),
            pltpu.VMEM((H_FOX, HD_FOX, tq), F32),
        ],
        compiler_params=_cparams(("parallel", "parallel", "arbitrary")),
        name="fox_attn",
    )(qa, ka, vt)


def _merge_kernel(x_ref, hm_ref, of_ref, cq_ref, gates_ref, mk_ref, mv_ref, wm_ref, wf_ref, wc_ref,
                  wo_ref, gffn_ref, xnew_ref, hn_ref, hnlo_ref):
    cq = cq_ref[0]
    heads = []
    for h in range(H_MEM):
        sl = slice(h * HD_MEM, (h + 1) * HD_MEM)
        s = _nt_dot(cq[:, sl], mk_ref[0, :, sl]) * (HD_MEM ** -0.5)
        p = jnp.exp(s - jnp.max(s, axis=1, keepdims=True))
        o = jnp.dot(p.astype(BF16), mv_ref[0, :, sl], preferred_element_type=F32)
        heads.append(o / jnp.sum(p, axis=1, keepdims=True))
    oc = jnp.concatenate(heads, axis=1).astype(BF16)
    a_m = jnp.dot(hm_ref[0], wm_ref[...], preferred_element_type=F32)
    a_f = jnp.dot(of_ref[0], wf_ref[...], preferred_element_type=F32)
    a_c = jnp.dot(oc, wc_ref[...], preferred_element_type=F32)
    g = gates_ref[0].astype(F32)
    merged = (_sigmoid(g[:, 0:D_MODEL]) * a_m + _sigmoid(g[:, D_MODEL:2 * D_MODEL]) * a_f
              + _sigmoid(g[:, 2 * D_MODEL:3 * D_MODEL]) * a_c)
    xn = x_ref[0] + jnp.dot(merged.astype(BF16), wo_ref[...], preferred_element_type=F32)
    xnew_ref[0] = xn
    hn = _rms(xn, gffn_ref[...])
    hn_hi = hn.astype(BF16)
    hn_ref[0] = hn_hi
    hnlo_ref[0] = (hn - hn_hi.astype(F32)).astype(BF16)


def _merge(x, hm, of, cq, gates, mk, mv, wm, wf, wc, wo, gffn):
    b, s, _ = x.shape
    ts = min(s, 512)
    tile = lambda w: pl.BlockSpec((1, ts, w), lambda bi, si: (bi, si, 0))
    const = lambda shape: pl.BlockSpec(shape, lambda bi, si: (0, 0))
    mem = pl.BlockSpec((1, mk.shape[1], D_MEM), lambda bi, si: (bi, 0, 0))
    return pl.pallas_call(
        _merge_kernel,
        out_shape=(jax.ShapeDtypeStruct((b, s, D_MODEL), F32), jax.ShapeDtypeStruct((b, s, D_MODEL), BF16),
                   jax.ShapeDtypeStruct((b, s, D_MODEL), BF16)),
        grid=(b, s // ts),
        in_specs=[tile(D_MODEL), tile(D_M), tile(D_FOX), tile(D_MEM), tile(N_BRANCH * D_MODEL), mem, mem,
                  const((D_M, D_MODEL)), const((D_FOX, D_MODEL)), const((D_MEM, D_MODEL)),
                  const((D_MODEL, D_MODEL)), const((1, D_MODEL))],
        out_specs=(tile(D_MODEL), tile(D_MODEL), tile(D_MODEL)),
        compiler_params=_cparams(("parallel", "parallel")),
        name="merge",
    )(x, hm, of, cq, gates, mk, mv, wm, wf, wc, wo, gffn)


def _extract_top16(s, key_io, val_scr, idx_scr, h, want_rank):
    rank = jnp.full(s.shape, float(PEER_TOPK), F32) if want_rank else None
    for r in range(PEER_TOPK):
        m = jnp.max(s, axis=0, keepdims=True)
        first = jnp.min(jnp.where(s == m, key_io, float(N_KEYS)), axis=0, keepdims=True)
        hit = key_io == first
        s = jnp.where(hit, -jnp.inf, s)
        val_scr[r, pl.ds(h, 1), :] = m
        if want_rank:
            rank = jnp.where(hit, float(r), rank)
        else:
            idx_scr[r, pl.ds(h, 1), :] = first
    return rank


def _dot3(a_hi, a_lo, b_hi, b_lo, dot):
    return dot(a_hi, b_hi) + dot(a_lo, b_hi) + dot(a_hi, b_lo)


def _router_kernel(hn_ref, hnlo_ref, wqt_ref, keys_ref, a0_ref, cnt_ref, r1_ref, b1_ref,
                   qr_scr, e1_scr, top0_scr, top1_scr, idx0_scr, cnt_scr, invz_scr):
    tp = hn_ref.shape[0]
    qr_scr[...] = _dot3(wqt_ref[0], wqt_ref[1], hn_ref[...], hnlo_ref[...], _nt_dot)
    key_io = lax.broadcasted_iota(jnp.int32, (N_KEYS, tp), 0).astype(F32)

    heads_per_trip = 8

    def scores_and_top16_group(hg, carry):
        for k in range(heads_per_trip):
            scores_and_top16(heads_per_trip * hg + k)
        return carry

    def scores_and_top16(h):
        base = pl.multiple_of(h * 2 * N_KEYS, 2 * N_KEYS)
        def scores(c):
            q = qr_scr[pl.ds(base + c * N_KEYS, N_KEYS), :]
            q_hi = q.astype(BF16)
            q_lo = (q - q_hi.astype(F32)).astype(BF16)
            return _dot3(keys_ref[0, 2 * h + c], keys_ref[1, 2 * h + c], q_hi, q_lo,
                         lambda a, b_: jnp.dot(a, b_, preferred_element_type=F32))

        s0, s1 = scores(0), scores(1)
        _extract_top16(s0, key_io, top0_scr, idx0_scr, h, False)
        rank1 = _extract_top16(s1, key_io, top1_scr, None, h, True)
        a0_ref[h] = jnp.exp(s0 - top0_scr[0, pl.ds(h, 1), :])
        e1_scr[h] = jnp.exp(s1 - top1_scr[0, pl.ds(h, 1), :])
        r1_ref[h] = rank1.astype(BF16)

    lax.fori_loop(0, PEER_HEADS // heads_per_trip, scores_and_top16_group, 0)

    top0 = [top0_scr[a] for a in range(PEER_TOPK)]
    top1 = [top1_scr[b] for b in range(PEER_TOPK)]
    cnt = [jnp.zeros((PEER_HEADS, tp), F32) for _ in range(PEER_TOPK)]
    for _ in range(PEER_TOPK):
        front = []
        for a in range(PEER_TOPK):
            nxt = jnp.full((PEER_HEADS, tp), -jnp.inf, F32)
            for bb in range(PEER_TOPK // (a + 1)):
                nxt = jnp.where(cnt[a] == float(bb), top1[bb], nxt)
            front.append(top0[a] + nxt)
        mx = functools.reduce(jnp.maximum, front)
        first = functools.reduce(jnp.minimum,
                                 [jnp.where(front[a] == mx, float(a), float(PEER_TOPK)) for a in range(PEER_TOPK)])
        cnt = [cnt[a] + jnp.where(first == float(a), 1.0, 0.0) for a in range(PEER_TOPK)]
    z = jnp.zeros((PEER_HEADS, tp), F32)
    for a in range(PEER_TOPK):
        za = jnp.zeros((PEER_HEADS, tp), F32)
        for bb in range(PEER_TOPK // (a + 1)):
            za = za + jnp.where(cnt[a] > float(bb), jnp.exp(top1[bb] - top1[0]), 0.0)
        z = z + jnp.exp(top0[a] - top0[0]) * za
        cnt_scr[a] = cnt[a]
    invz_scr[...] = 1.0 / z

    def counts_and_gates(h, carry):
        cnt_i = jnp.zeros((N_KEYS, tp), F32)
        for a in range(PEER_TOPK):
            cnt_i = jnp.where(key_io == idx0_scr[a, pl.ds(h, 1), :], cnt_scr[a, pl.ds(h, 1), :], cnt_i)
        cnt_ref[h] = cnt_i
        b1 = e1_scr[h] * invz_scr[pl.ds(h, 1), :]
        b1_ref[h] = b1.astype(BF16)
        return carry

    lax.fori_loop(0, PEER_HEADS, counts_and_gates, 0)


def _router(hn2d, hnlo2d, wqt, keys):
    t = hn2d.shape[0]
    tp = 2 * LANES
    shape = jax.ShapeDtypeStruct((PEER_HEADS, N_KEYS, t), F32)
    spec = pl.BlockSpec((PEER_HEADS, N_KEYS, tp), lambda i: (0, 0, i))
    slab_shape = jax.ShapeDtypeStruct((PEER_HEADS, N_KEYS, t), BF16)
    slab_spec = spec
    return pl.pallas_call(
        _router_kernel,
        out_shape=(shape, shape, slab_shape, slab_shape),
        grid=(t // tp,),
        in_specs=[
            pl.BlockSpec((tp, D_MODEL), lambda i: (i, 0)),
            pl.BlockSpec((tp, D_MODEL), lambda i: (i, 0)),
            pl.BlockSpec(wqt.shape, lambda i: (0, 0, 0)),
            pl.BlockSpec(keys.shape, lambda i: (0, 0, 0, 0)),
        ],
        out_specs=(spec, spec, slab_spec, slab_spec),
        scratch_shapes=[
            pltpu.VMEM((wqt.shape[1], tp), F32),
            pltpu.VMEM((PEER_HEADS, N_KEYS, tp), F32),
        ] + [pltpu.VMEM((PEER_TOPK, PEER_HEADS, tp), F32) for _ in range(4)] + [pltpu.VMEM((PEER_HEADS, tp), F32)],
        compiler_params=_cparams(("parallel",)),
        name="peer_router",
    )(hn2d, hnlo2d, wqt, keys)


def _gelu_tanh(x):
    k = -2.0 * 0.7978845608028654 * 1.4426950408889634
    return x / (1.0 + jnp.exp2(x * (k + (k * 0.044715) * (x * x))))


def _experts_kernel(hn_ref, u_ref, vt_ref, a0_ref, cnt_ref, r1_ref, b1_ref, x_ref, gfin_ref, out_ref,
                    acc_scr, pre0_scr, pre1_scr, w0_scr, w1_scr, *, rows_per_tile, n_tiles, final_norm):
    s = pl.program_id(1)
    tp = hn_ref.shape[0]
    te = u_ref.shape[0] // 2
    slab = 16
    slabs_per_row = N_KEYS // slab
    d_half = acc_scr.shape[0] // 2
    zero = jnp.zeros((), BF16)

    @pl.when(s == 0)
    def _():
        acc_scr[...] = jnp.zeros(acc_scr.shape, F32)
        pre1_scr[...] = jnp.zeros(pre1_scr.shape, F32)
        w0_scr[...] = jnp.zeros(w0_scr.shape, BF16)

    def pipeline_step(parity, pre_w, pre_r, w_w, w_r):
        e = 2 * s + parity
        cols = slice(parity * te, (parity + 1) * te)

        def stage_c(half):
            rows = slice(half * d_half, (half + 1) * d_half)
            acc_scr[rows, :] += jnp.dot(vt_ref[rows, cols], w_r[...], preferred_element_type=F32)

        def stage_a(half):
            rows = slice(half * (te // 2), (half + 1) * (te // 2))
            pre_w[rows, :] = _nt_dot(u_ref[parity * te + half * (te // 2):parity * te + (half + 1) * (te // 2), :],
                                     hn_ref[...])

        mxu_work = (lambda: stage_c(0), lambda: stage_a(0), lambda: stage_c(1), lambda: stage_a(1))

        valid = jnp.logical_and(e >= 1, e <= n_tiles)
        tile_b = jnp.clip(e - 1, 0, n_tiles - 1)
        for ib in range(rows_per_tile):
            for piece in mxu_work[ib * len(mxu_work) // rows_per_tile:(ib + 1) * len(mxu_work) // rows_per_tile]:
                piece()
            i = tile_b * rows_per_tile + ib
            for lanes in (slice(0, tp // 2), slice(tp // 2, tp)):
                gates = [jnp.zeros((slab, tp // 2), BF16) for _ in range(slabs_per_row)]
                for h in range(PEER_HEADS):
                    a_b = jnp.broadcast_to(a0_ref[h, pl.ds(i, 1), lanes], (slab, tp // 2)).astype(BF16)
                    c_b = jnp.broadcast_to(cnt_ref[h, pl.ds(i, 1), lanes], (slab, tp // 2)).astype(BF16)
                    for g in range(slabs_per_row):
                        key_rows = slice(g * slab, (g + 1) * slab)
                        gates[g] = gates[g] + a_b * jnp.where(r1_ref[h, key_rows, lanes] < c_b,
                                                              b1_ref[h, key_rows, lanes], zero)
                for g in range(slabs_per_row):
                    rows = slice(ib * N_KEYS + g * slab, ib * N_KEYS + (g + 1) * slab)
                    w = gates[g] * _gelu_tanh(pre_r[rows, lanes]).astype(BF16)
                    w_w[rows, lanes] = jnp.where(valid, w, zero)

    pipeline_step(0, pre0_scr, pre1_scr, w1_scr, w0_scr)
    pipeline_step(1, pre1_scr, pre0_scr, w0_scr, w1_scr)

    @pl.when(s == n_tiles // 2)
    def _():
        xo = x_ref[...] + acc_scr[...].T
        if final_norm:
            xo = _rms(xo, gfin_ref[...])
        out_ref[...] = xo


def _experts(hn2d, u_bf, vt_bf, layer, a0, cnt, r1, b1, x2d, gfin, final_norm):
    t = hn2d.shape[0]
    tp = 512
    rows_per_tile = 4
    te = rows_per_tile * N_KEYS
    n_tiles = u_bf.shape[1] // te
    rspec = pl.BlockSpec((PEER_HEADS, N_KEYS, tp), lambda ti, e: (0, 0, ti))
    sspec = rspec
    return pl.pallas_call(
        functools.partial(_experts_kernel, rows_per_tile=rows_per_tile, n_tiles=n_tiles, final_norm=final_norm),
        out_shape=jax.ShapeDtypeStruct((t, D_MODEL), F32),
        grid=(t // tp, n_tiles // 2 + 1),
        in_specs=[
            pl.BlockSpec((tp, D_MODEL), lambda ti, s: (ti, 0)),
            pl.BlockSpec((None, 2 * te, D_MODEL), lambda ti, s: (layer, jnp.minimum(s, n_tiles // 2 - 1), 0)),
            pl.BlockSpec((None, D_MODEL, 2 * te), lambda ti, s: (layer, 0, jnp.maximum(s - 1, 0))),
            rspec, rspec, sspec, sspec,
            pl.BlockSpec((tp, D_MODEL), lambda ti, s: (ti, 0)),
            pl.BlockSpec((1, D_MODEL), lambda ti, s: (0, 0)),
        ],
        out_specs=pl.BlockSpec((tp, D_MODEL), lambda ti, s: (ti, 0)),
        scratch_shapes=[
            pltpu.VMEM((D_MODEL, tp), F32),
            pltpu.VMEM((te, tp), F32), pltpu.VMEM((te, tp), F32),
            pltpu.VMEM((te, tp), BF16), pltpu.VMEM((te, tp), BF16),
        ],
        compiler_params=_cparams(("parallel", "arbitrary")),
        name="peer_experts",
    )(hn2d, u_bf, vt_bf, a0, cnt, r1, b1, x2d, gfin)


def _hi_lo(a):
    hi = a.astype(BF16)
    return jnp.stack([hi, (a - hi.astype(F32)).astype(BF16)])


def _prep_layer_weights(p):
    w_in = p['w_in']
    o_mi = 4 * D_M
    o_fq = o_mi + 2 * H_M
    o_ff = o_fq + 3 * D_FOX
    o_cq = o_ff + H_FOX
    w_big = jnp.concatenate([w_in[:, :o_mi], w_in[:, o_fq:o_ff], w_in[:, o_cq:]], axis=1).astype(BF16)
    w_big = w_big.reshape(D_MODEL, -1, PROJ_TILE).transpose(1, 0, 2)
    w_small = jnp.concatenate([w_in[:, o_mi:o_fq], w_in[:, o_ff:o_cq]], axis=1)
    w_small = jnp.pad(w_small, ((0, 0), (0, LANES - w_small.shape[1])))
    w_small_hi = w_small.astype(BF16)
    w_small = jnp.stack([w_small_hi, (w_small - w_small_hi.astype(F32)).astype(BF16)])
    bias = jnp.concatenate([p['b_m_i'], p['b_m_f'], p['b_fox_f']]).astype(F32)
    return dict(
        norm_mix=p['norm_mix'].reshape(1, D_MODEL), w_big=w_big, w_small=w_small,
        conv_w=p['conv_w'], conv_b=p['conv_b'].reshape(1, 2 * D_M),
        bias_row=jnp.pad(bias, (0, LANES - 16)).reshape(1, LANES), bias_col=bias.reshape(16, 1),
        norm_m_head=p['norm_m_head'].reshape(1, D_M),
        w_up_m=p['w_up_m'].astype(BF16), w_up_f=p['w_up_f'].astype(BF16), w_up_c=p['w_up_c'].astype(BF16),
        w_out=p['w_out'].astype(BF16), norm_ffn=p['norm_ffn'].reshape(1, D_MODEL),
        wqt=_hi_lo(p['peer_wq'].T),
        keys=_hi_lo(p['peer_keys'].reshape(2 * PEER_HEADS, N_KEYS, N_KEYS)),
    )


def _layer(x, mem_k, mem_v, conv_prev, c0, n0, m0, fk_past, fv_past, flf_past, w, peer_tables, layer, gfin, final_norm):
    b, s, _ = x.shape
    t = b * s
    p0 = 0 if fk_past is None else fk_past.shape[1]
    qk, v_m, o_m, fq, fk, fv, cq, gates, small = _norm_proj(x.reshape(t, D_MODEL), w['norm_mix'], w['w_big'], w['w_small'])

    small3 = small.reshape(b, s, LANES)
    smallt = small3[:, :, :16].transpose(0, 2, 1)
    m0b = jnp.broadcast_to(m0[:, :, None, None], (b, H_M, 1, LANES))
    hm, c1, n1, m1b, conv_new = _mlstm(
        qk.reshape(b, s, 2 * D_M), v_m.reshape(b, s, D_M), o_m.reshape(b, s, D_M), small3, smallt,
        w['conv_w'], w['conv_b'], w['bias_row'], w['bias_col'], w['norm_m_head'],
        c0, n0[:, :, None, :], m0b, conv_prev)
    n1, m1 = n1[:, :, 0, :], m1b[:, :, 0, 0]

    k_f = fk.reshape(b, s, H_FOX, HD_FOX)
    v_f = fv.reshape(b, s, H_FOX, HD_FOX)
    s_pad = -(-s // LANES) * LANES
    pad_rows = lambda a: jnp.pad(a, ((0, 0), (0, s_pad - s), (0, 0)))
    past = flf_past.astype(F32) if p0 else None
    lf_all, hi, mid, lo = _fox_prep(past, pad_rows(small3), w['bias_row'], s)
    lf_f = lf_all[:, :s, FOX_GATE_LANE:FOX_GATE_LANE + H_FOX]
    sk = p0 + s_pad
    k_all, v_all = pad_rows(fk.reshape(b, s, D_FOX)), pad_rows(fv.reshape(b, s, D_FOX))
    if p0:
        k_all = jnp.concatenate([fk_past.reshape(b, p0, D_FOX).astype(F32), k_all], axis=1)
        v_all = jnp.concatenate([fv_past.reshape(b, p0, D_FOX).astype(F32), v_all], axis=1)
    qa = _fox_pack(pad_rows(fq.reshape(b, s, D_FOX)), hi, mid, lo, p0, True)[0]
    ka, vt_all = _fox_pack(k_all, hi, mid, lo, 0, False, v=v_all)
    o_f = _fox_attn(qa, ka, vt_all.reshape(b, H_FOX, HD_FOX, sk), p0)[:, :s]

    x_new, hn, hn_lo = _merge(x, hm, o_f, cq.reshape(b, s, D_MEM), gates.reshape(b, s, N_BRANCH * D_MODEL),
                       mem_k.reshape(b, -1, D_MEM).astype(BF16), mem_v.reshape(b, -1, D_MEM).astype(BF16),
                       w['w_up_m'], w['w_up_f'], w['w_up_c'], w['w_out'], w['norm_ffn'])

    hn2d = hn.reshape(t, D_MODEL)
    a0, cnt, r1, b1 = _router(hn2d, hn_lo.reshape(t, D_MODEL), w['wqt'], w['keys'])
    x_out = _experts(hn2d, peer_tables[0], peer_tables[1], layer, a0, cnt, r1, b1, x_new.reshape(t, D_MODEL),
                     gfin, final_norm)
    return x_out.reshape(b, s, D_MODEL), conv_new, c1, n1, m1, k_f, v_f, lf_f


def kernel(x_prompt, x_sample, mem_prompt, cache_fox_k, cache_fox_v, cache_fox_lf, state_mlstm_c, state_mlstm_n, state_mlstm_m, state_conv, cache_mem_k, cache_mem_v, norm_mix, w_in, conv_w, conv_b, b_m_i, b_m_f, norm_m_head, b_fox_f, norm_mem, w_mem_kv, w_up_m, w_up_f, w_up_c, w_out, norm_ffn, peer_wq, peer_keys, peer_u, peer_v, norm_final):
    depth = w_in.shape[0]
    names = dict(norm_mix=norm_mix, w_in=w_in, conv_w=conv_w, conv_b=conv_b, b_m_i=b_m_i, b_m_f=b_m_f,
                 norm_m_head=norm_m_head, b_fox_f=b_fox_f, w_up_m=w_up_m, w_up_f=w_up_f, w_up_c=w_up_c,
                 w_out=w_out, norm_ffn=norm_ffn, peer_wq=peer_wq, peer_keys=peer_keys, peer_u=peer_u,
                 peer_v=peer_v)
    weights = [_prep_layer_weights({k: a[l] for k, a in names.items()}) for l in range(depth)]
    peer_tables = (peer_u.astype(BF16), jnp.swapaxes(peer_v, 1, 2).astype(BF16))
    gfin = norm_final.reshape(1, D_MODEL)
    bp, n_mem = mem_prompt.shape[0], mem_prompt.shape[1]

    xp = x_prompt
    new_p = [[] for _ in range(9)]
    for l in range(depth):
        mk, mv = _norm_matmul(mem_prompt.reshape(bp * n_mem, D_MODEL), norm_mem[l].reshape(1, D_MODEL),
                              w_mem_kv[l].astype(BF16))
        mk = mk.reshape(bp, n_mem, H_MEM, HD_MEM)
        mv = mv.reshape(bp, n_mem, H_MEM, HD_MEM)
        xp, conv1, c1, n1, m1, kf, vf, lff = _layer(
            xp, mk, mv, jnp.zeros((bp, CONV_W - 1, 2 * D_M), F32),
            jnp.zeros((bp, H_M, HD_M, HD_M), F32), jnp.zeros((bp, H_M, HD_M), F32), jnp.zeros((bp, H_M), F32),
            None, None, None, weights[l], peer_tables, l, gfin, l == depth - 1)
        for lst, a in zip(new_p, (kf, vf, lff, c1, n1, m1, conv1, mk, mv)):
            lst.append(a)
    outs_p = [jnp.stack(a) for a in new_p]

    xs = x_sample
    new_s = [[] for _ in range(7)]
    for l in range(depth):
        xs, conv1, c1, n1, m1, kf, vf, lff = _layer(
            xs, cache_mem_k[l], cache_mem_v[l], state_conv[l], state_mlstm_c[l], state_mlstm_n[l],
            state_mlstm_m[l], cache_fox_k[l], cache_fox_v[l], cache_fox_lf[l], weights[l], peer_tables, l, gfin,
            l == depth - 1)
        for lst, a in zip(new_s, (kf, vf, lff, c1, n1, m1, conv1)):
            lst.append(a)
    outs_s = [jnp.stack(a) for a in new_s]

    return (xp, xs, *outs_p, *outs_s)
```

```python
import functools

import jax
import jax.numpy as jnp
from jax import lax
from jax.experimental import pallas as pl
from jax.experimental.pallas import tpu as pltpu

F32 = jnp.float32
BF16 = jnp.bfloat16
EPS = 1e-6
NEG_BIG = -1e30

D_MODEL = 1024
H_M, HD_M = 4, 128
D_M = H_M * HD_M
CONV_W = 4
H_FOX, HD_FOX = 8, 64
D_FOX = H_FOX * HD_FOX
H_MEM, HD_MEM = 4, 128
D_MEM = H_MEM * HD_MEM
N_BRANCH = 3
PEER_HEADS = 8
N_KEYS = 128
PEER_TOPK = 16
LANES = 128
FOX_GATE_LANE = 2 * H_M
MLSTM_CHUNK = 256
PROJ_TILE = 512
ROW_TILE = 512
KEY_TILE_SHORT = 384
ROUTER_TOKENS = 2 * LANES
EXPERT_KEY_ROWS = 4
VMEM_LIMIT = 56 * 1024 * 1024


def _cparams(sem):
    return pltpu.CompilerParams(dimension_semantics=sem, vmem_limit_bytes=VMEM_LIMIT)


def _nt_dot(a, b):
    return lax.dot_general(a, b, (((1,), (1,)), ((), ())), preferred_element_type=F32)


def _rms(x, g):
    return x * lax.rsqrt(jnp.mean(x * x, axis=-1, keepdims=True) + EPS) * g


def _log_sigmoid(x):
    return jnp.minimum(x, 0.0) - jnp.log(1.0 + jnp.exp(-jnp.abs(x)))


def _sigmoid(x):
    return 1.0 / (1.0 + jnp.exp(-x))


def _norm_proj_kernel(x_ref, g_ref, w_ref, ws_ref, qk_ref, v_ref, o_ref, fq_ref, fk_ref, fv_ref,
                      cq_ref, gates_ref, small_ref):
    hf = _rms(x_ref[...], g_ref[...])
    h = hf.astype(BF16)
    h_lo = (hf - h.astype(F32)).astype(BF16)
    small_ref[...] = (jnp.dot(h, ws_ref[0], preferred_element_type=F32)
                      + jnp.dot(h_lo, ws_ref[0], preferred_element_type=F32)
                      + jnp.dot(h, ws_ref[1], preferred_element_type=F32))

    def tile(j):
        return jnp.dot(h, w_ref[j], preferred_element_type=F32)

    qk_ref[:, 0:PROJ_TILE] = tile(0)
    qk_ref[:, PROJ_TILE:2 * PROJ_TILE] = tile(1)
    v_ref[...] = tile(2).astype(BF16)
    o_ref[...] = tile(3).astype(BF16)
    fq_ref[...] = tile(4).astype(BF16)
    fk_ref[...] = tile(5)
    fv_ref[...] = tile(6)
    cq_ref[...] = tile(7).astype(BF16)
    for j in range(8, w_ref.shape[0]):
        gates_ref[:, (j - 8) * PROJ_TILE:(j - 7) * PROJ_TILE] = tile(j).astype(BF16)


def _norm_proj(x2d, g, w_big, w_small):
    t = x2d.shape[0]
    tm = ROW_TILE
    row = lambda i: (i, 0)
    out_widths = ((2 * D_M, F32),
                  (D_M, BF16),
                  (D_M, BF16),
                  (D_FOX, BF16),
                  (D_FOX, F32),
                  (D_FOX, F32),
                  (D_MEM, BF16),
                  (N_BRANCH * D_MODEL, BF16),
                  (LANES, F32))
    resident = lambda shape: pl.BlockSpec(shape, lambda i: (0,) * len(shape), pipeline_mode=pl.Buffered(1))
    return pl.pallas_call(
        _norm_proj_kernel,
        out_shape=tuple(jax.ShapeDtypeStruct((t, wd), dt) for wd, dt in out_widths),
        grid=(t // tm,),
        in_specs=[
            pl.BlockSpec((tm, D_MODEL), row),
            resident((1, D_MODEL)),
            resident(w_big.shape),
            resident(w_small.shape),
        ],
        out_specs=tuple(pl.BlockSpec((tm, wd), row) for wd, _ in out_widths),
        compiler_params=_cparams(("parallel",)),
        name="norm_proj",
    )(x2d, g, w_big, w_small)


def _norm_matmul_kernel(x_ref, g_ref, w_ref, k_ref, v_ref):
    h = _rms(x_ref[...], g_ref[...]).astype(BF16)
    k_ref[...] = jnp.dot(h, w_ref[:, 0:D_MEM], preferred_element_type=F32)
    v_ref[...] = jnp.dot(h, w_ref[:, D_MEM:2 * D_MEM], preferred_element_type=F32)


def _norm_matmul(x2d, g, w):
    t = x2d.shape[0]
    tm = ROW_TILE
    out = jax.ShapeDtypeStruct((t, D_MEM), F32)
    return pl.pallas_call(
        _norm_matmul_kernel,
        out_shape=(out, out),
        grid=(t // tm,),
        in_specs=[
            pl.BlockSpec((tm, D_MODEL), lambda i: (i, 0)),
            pl.BlockSpec((1, D_MODEL), lambda i: (0, 0)),
            pl.BlockSpec((D_MODEL, 2 * D_MEM), lambda i: (0, 0)),
        ],
        out_specs=(pl.BlockSpec((tm, D_MEM), lambda i: (i, 0)), pl.BlockSpec((tm, D_MEM), lambda i: (i, 0))),
        compiler_params=_cparams(("parallel",)),
        name="norm_matmul",
    )(x2d, g, w)


def _mlstm_kernel(qk_ref, v_ref, o_ref, small_ref, smallt_ref, convw_ref, convb_ref, brow_ref,
                  bcol_ref, ghead_ref, c0_ref, n0_ref, m0_ref, conv0_ref,
                  hm_ref, c_out_ref, n_out_ref, m_out_ref, conv_out_ref,
                  c_scr, n_scr, m_scr, xp_scr, *, chunk):
    L = chunk
    si = pl.program_id(1)

    @pl.when(si == 0)
    def _():
        c_scr[...] = c0_ref[0]
        n_scr[...] = n0_ref[0]
        m_scr[...] = m0_ref[0]
        xp_scr[5:8, :] = conv0_ref[0]

    xp_scr[8:8 + L, :] = qk_ref[0]
    y = convb_ref[...] + convw_ref[0:1, :] * xp_scr[5:5 + L, :]
    for j in range(1, CONV_W):
        y = y + convw_ref[j:j + 1, :] * xp_scr[5 + j:5 + j + L, :]
    y = y * _sigmoid(y)
    tail = xp_scr[5 + L:8 + L, :]
    xp_scr[5:8, :] = tail
    conv_out_ref[0] = tail

    t_io = lax.broadcasted_iota(jnp.int32, (L, L), 0)
    s_io = lax.broadcasted_iota(jnp.int32, (L, L), 1)
    causal = s_io <= t_io
    ones_col = (lax.broadcasted_iota(jnp.int32, (L, HD_M), 1) == 0).astype(BF16)

    small = small_ref[0]
    smallt = smallt_ref[0]
    brow = brow_ref[...]
    bcol = bcol_ref[...]
    outs = []
    for h in range(H_M):
        q = y[:, h * HD_M:(h + 1) * HD_M]
        k = y[:, D_M + h * HD_M:D_M + (h + 1) * HD_M] * (HD_M ** -0.5)
        v_aug = jnp.concatenate([v_ref[0, :, h * HD_M:(h + 1) * HD_M], ones_col], axis=1)
        ig_col = small[:, h:h + 1] + brow[:, h:h + 1]
        ig_row = smallt[h:h + 1, :] + bcol[h:h + 1, :]
        lf_col = _log_sigmoid(small[:, H_M + h:H_M + h + 1] + brow[:, H_M + h:H_M + h + 1])
        lf_row = _log_sigmoid(smallt[H_M + h:H_M + h + 1, :] + bcol[H_M + h:H_M + h + 1, :])
        bcum_col = jnp.sum(jnp.where(causal, lf_row, 0.0), axis=1, keepdims=True)
        bcum_row = jnp.sum(jnp.where(t_io <= s_io, lf_col, 0.0), axis=0, keepdims=True)
        m_prev = m_scr[h][:, 0:1]
        dmat = jnp.where(causal, bcum_col - bcum_row + ig_row, -jnp.inf)
        inter = bcum_col + m_prev
        m_t = jnp.maximum(inter, jnp.max(dmat, axis=1, keepdims=True))
        w_intra = jnp.exp(dmat - m_t)
        w_state = jnp.exp(inter - m_t)
        qb = q.astype(BF16)
        sw = _nt_dot(qb, k.astype(BF16)) * w_intra
        c_prev = c_scr[h]
        n_prev = n_scr[h]
        intra = jnp.dot(sw.astype(BF16), v_aug, preferred_element_type=F32)
        num = intra[:, 0:HD_M] + w_state * jnp.dot(qb, c_prev.astype(BF16), preferred_element_type=F32)
        den = intra[:, HD_M:HD_M + 1] + w_state * jnp.sum(q * n_prev, axis=1, keepdims=True)
        hh = num / jnp.maximum(jnp.abs(den), jnp.exp(-m_t))
        hh = _rms(hh, ghead_ref[:, h * HD_M:(h + 1) * HD_M])
        outs.append(hh * _sigmoid(o_ref[0, :, h * HD_M:(h + 1) * HD_M].astype(F32)))
        bl = bcum_row[:, L - 1:L]
        m_new = jnp.maximum(bl + m_prev, jnp.max(bl - bcum_row + ig_row, axis=1, keepdims=True))
        decay = jnp.exp(bl + m_prev - m_new)
        ws_col = jnp.exp(bl - bcum_col + ig_col - m_new)
        kw = k * ws_col
        c_scr[h] = decay * c_prev + jnp.dot(kw.T.astype(BF16), v_ref[0, :, h * HD_M:(h + 1) * HD_M],
                                            preferred_element_type=F32)
        n_scr[h] = decay * n_prev + jnp.sum(kw, axis=0, keepdims=True)
        m_scr[h] = jnp.broadcast_to(m_new, (1, LANES))
    hm_ref[0] = jnp.concatenate(outs, axis=1).astype(BF16)
    c_out_ref[0] = c_scr[...]
    n_out_ref[0] = n_scr[...]
    m_out_ref[0] = m_scr[...]


def _mlstm(qk, v, o, small, smallt, convw, convb, brow, bcol, ghead, c0, n0, m0b, conv0):
    b, s, _ = qk.shape
    chunk = min(s, MLSTM_CHUNK)
    tile = lambda w: pl.BlockSpec((1, chunk, w), lambda bi, si: (bi, si, 0))
    per_b = lambda shape: pl.BlockSpec((1,) + shape, lambda bi, si: (bi,) + (0,) * len(shape))
    const = lambda shape: pl.BlockSpec(shape, lambda bi, si: (0, 0))
    state_shapes = ((H_M, HD_M, HD_M), (H_M, 1, HD_M), (H_M, 1, LANES), (CONV_W - 1, 2 * D_M))
    return pl.pallas_call(
        functools.partial(_mlstm_kernel, chunk=chunk),
        out_shape=(jax.ShapeDtypeStruct((b, s, D_M), BF16),)
        + tuple(jax.ShapeDtypeStruct((b,) + sh, F32) for sh in state_shapes),
        grid=(b, s // chunk),
        in_specs=[
            tile(2 * D_M), tile(D_M), tile(D_M), tile(LANES),
            pl.BlockSpec((1, 16, chunk), lambda bi, si: (bi, 0, si)),
            const((CONV_W, 2 * D_M)), const((1, 2 * D_M)), const((1, LANES)), const((16, 1)), const((1, D_M)),
        ] + [per_b(sh) for sh in state_shapes],
        out_specs=(tile(D_M),) + tuple(per_b(sh) for sh in state_shapes),
        scratch_shapes=[
            pltpu.VMEM((H_M, HD_M, HD_M), F32),
            pltpu.VMEM((H_M, 1, HD_M), F32),
            pltpu.VMEM((H_M, 1, LANES), F32),
            pltpu.VMEM((chunk + 8, 2 * D_M), F32),
        ],
        compiler_params=_cparams(("parallel", "arbitrary")),
        name="mlstm",
    )(qk, v, o, small, smallt, convw, convb, brow, bcol, ghead, c0, n0, m0b, conv0)


def _split3(x):
    hi = x.astype(BF16).astype(F32)
    r = x - hi
    mid = r.astype(BF16).astype(F32)
    lo = (r - mid).astype(BF16).astype(F32)
    return hi, mid, lo


def _fox_prep_kernel(*refs, n_past, n_new, s_valid, blk):
    if n_past:
        past_ref, pre_ref, bias_ref, lf_ref, hi_ref, mid_ref, lo_ref = refs
    else:
        pre_ref, bias_ref, lf_ref, hi_ref, mid_ref, lo_ref = refs
    tri = (lax.broadcasted_iota(jnp.int32, (blk, blk), 1) <= lax.broadcasted_iota(jnp.int32, (blk, blk), 0)).astype(BF16)
    carry = jnp.zeros((1, LANES), F32)
    start = 0
    while start < n_past + n_new:
        if start < n_past:
            r = min(blk, n_past - start)
            lf = jnp.concatenate([jnp.zeros((r, FOX_GATE_LANE), F32), past_ref[0, start:start + r, :],
                                  jnp.zeros((r, LANES - FOX_GATE_LANE - H_FOX), F32)], axis=1)
        else:
            r = min(blk, n_past + n_new - start)
            ns = start - n_past
            row = ns + lax.broadcasted_iota(jnp.int32, (r, LANES), 0)
            lf = jnp.where(row < s_valid, _log_sigmoid(pre_ref[0, ns:ns + r, :] + bias_ref[...]), 0.0)
            lf_ref[0, ns:ns + r, :] = lf
        cum = carry
        for piece in _split3(lf):
            cum = cum + jnp.dot(tri[0:r, 0:r], piece.astype(BF16), preferred_element_type=F32)
        carry = cum[r - 1:r, :]
        hi, mid, lo = _split3(cum)
        hi_ref[0, start:start + r, :] = hi
        mid_ref[0, start:start + r, :] = mid
        lo_ref[0, start:start + r, :] = lo
        start += r


def _fox_prep(past, pre, bias_row, s_valid):
    b, n_new = pre.shape[:2]
    n_past = 0 if past is None else past.shape[1]
    n = n_past + n_new
    full = lambda rows: pl.BlockSpec((1, rows, LANES), lambda bi: (bi, 0, 0))
    past_spec = pl.BlockSpec((1, n_past, H_FOX), lambda bi: (bi, 0, 0))
    in_specs = ([past_spec] if n_past else []) + [full(n_new), pl.BlockSpec((1, LANES), lambda bi: (0, 0))]
    args = ([past] if n_past else []) + [pre, bias_row]
    cum_shape = jax.ShapeDtypeStruct((b, n, LANES), F32)
    return pl.pallas_call(
        functools.partial(_fox_prep_kernel, n_past=n_past, n_new=n_new, s_valid=s_valid, blk=256),
        out_shape=(jax.ShapeDtypeStruct((b, n_new, LANES), F32), cum_shape, cum_shape, cum_shape),
        grid=(b,),
        in_specs=in_specs,
        out_specs=(full(n_new), full(n), full(n), full(n)),
        compiler_params=_cparams(("parallel",)),
        name="fox_prep",
    )(*args)


def _fox_pack_kernel(*refs, q_side, with_v):
    if with_v:
        x_ref, hi_ref, mid_ref, lo_ref, v_ref, out_ref, vt_ref = refs
        vt_ref[0] = v_ref[0].T.astype(BF16)
    else:
        x_ref, hi_ref, mid_ref, lo_ref, out_ref = refs
    tm = x_ref.shape[1]
    lane = lax.broadcasted_iota(jnp.int32, (tm, HD_FOX), 1)
    hi, mid, lo = hi_ref[0], mid_ref[0], lo_ref[0]
    for h in range(H_FOX):
        c = FOX_GATE_LANE + h
        pieces = (hi[:, c:c + 1], mid[:, c:c + 1], lo[:, c:c + 1])
        first = 0 if q_side else 3
        bias = jnp.where(jnp.logical_and(lane >= 3 - first, lane < 6 - first), -1.0 if q_side else 1.0, 0.0)
        for j, p in enumerate(pieces):
            bias = jnp.where(lane == first + j, p, bias)
        out_ref[0, h] = jnp.concatenate(
            [x_ref[0, :, h * HD_FOX:(h + 1) * HD_FOX].astype(BF16), bias.astype(BF16)], axis=1)


def _fox_pack(x, hi, mid, lo, row_offset, q_side, v=None):
    b, n, _ = x.shape
    tm = next(c for c in (ROW_TILE, KEY_TILE_SHORT, 2 * LANES, LANES) if n % c == 0 and row_offset % c == 0)
    off = row_offset // tm
    tile = lambda w: pl.BlockSpec((1, tm, w), lambda bi, i: (bi, i, 0))
    piece = pl.BlockSpec((1, tm, LANES), lambda bi, i: (bi, off + i, 0))
    out_shape = [jax.ShapeDtypeStruct((b, H_FOX, n, 2 * HD_FOX), BF16)]
    out_specs = [pl.BlockSpec((1, H_FOX, tm, 2 * HD_FOX), lambda bi, i: (bi, 0, i, 0))]
    in_specs = [tile(D_FOX), piece, piece, piece]
    args = [x, hi, mid, lo]
    if v is not None:
        in_specs.append(tile(D_FOX))
        args.append(v)
        out_shape.append(jax.ShapeDtypeStruct((b, D_FOX, n), BF16))
        out_specs.append(pl.BlockSpec((1, D_FOX, tm), lambda bi, i: (bi, 0, i)))
    return pl.pallas_call(
        functools.partial(_fox_pack_kernel, q_side=q_side, with_v=v is not None),
        out_shape=tuple(out_shape),
        grid=(b, n // tm),
        in_specs=in_specs,
        out_specs=tuple(out_specs),
        compiler_params=_cparams(("parallel", "parallel")),
        name="fox_pack_q" if q_side else "fox_pack_kv",
    )(*args)


def _paired_tiles(pair, step, nq):
    first = step <= pair
    return jnp.where(first, pair, nq - 1 - pair), jnp.where(first, step, step - pair - 1)


def _fox_attn_kernel(qa_ref, ka_ref, vt_ref, out_ref, m_scr, l_scr, acc_scr, *, p0, tq, tk, nq, nk, paired):
    qi, kj = pl.program_id(1), pl.program_id(2)
    if paired:
        qi, kj = _paired_tiles(qi, kj, nq)

    @pl.when(kj == 0)
    def _():
        m_scr[...] = jnp.full(m_scr.shape, NEG_BIG, F32)
        l_scr[...] = jnp.zeros(l_scr.shape, F32)
        acc_scr[...] = jnp.zeros(acc_scr.shape, F32)

    def step(masked):
        col = lax.broadcasted_iota(jnp.int32, (1, 2 * HD_FOX), 1)
        scale = jnp.where(col < HD_FOX, HD_FOX ** -0.5, 1.0).astype(BF16)
        ones_rows = jnp.ones((16, tk), BF16)
        if masked:
            kpos = kj * tk + lax.broadcasted_iota(jnp.int32, (tk, tq), 0)
            qpos = p0 + qi * tq + lax.broadcasted_iota(jnp.int32, (tk, tq), 1)
            mask = kpos <= qpos
        for h in range(H_FOX):
            s = _nt_dot(ka_ref[0, h], qa_ref[0, h] * scale)
            if masked:
                s = jnp.where(mask, s, NEG_BIG)
            m_prev = m_scr[h]
            m_new = jnp.maximum(m_prev, jnp.max(s, axis=0, keepdims=True))
            p = jnp.exp(s - m_new)
            alpha = jnp.exp(m_prev - m_new)
            pv = jnp.dot(jnp.concatenate([vt_ref[0, h], ones_rows], axis=0), p.astype(BF16),
                         preferred_element_type=F32)
            l_scr[h] = alpha * l_scr[h] + pv[HD_FOX:HD_FOX + 1, :]
            acc_scr[h] = alpha * acc_scr[h] + pv[0:HD_FOX, :]
            m_scr[h] = m_new

    first_q = p0 + qi * tq
    last_q = first_q + tq - 1
    unmasked = (kj + 1) * tk - 1 <= first_q

    @pl.when(unmasked)
    def _():
        step(False)

    @pl.when(jnp.logical_and(jnp.logical_not(unmasked), kj * tk <= last_q))
    def _():
        step(True)

    @pl.when(kj == jnp.minimum(last_q // tk, nk - 1))
    def _():
        o_t = jnp.concatenate([acc_scr[h] / l_scr[h] for h in range(H_FOX)], axis=0)
        out_ref[0] = o_t.T.astype(BF16)


def _fox_attn(qa, ka, vt, p0):
    b, _, sq, _ = qa.shape
    sk = ka.shape[2]
    tq = min(sq, ROW_TILE)
    tk = ROW_TILE if sk % ROW_TILE == 0 else KEY_TILE_SHORT
    nq, nk = sq // tq, sk // tk
    paired = p0 == 0 and tq == tk and nq == nk and nq % 2 == 0
    if paired:
        grid = (b, nq // 2, nq + 1)
        tiles = lambda qi, kj: _paired_tiles(qi, kj, nq)
    else:
        grid = (b, nq, nk)
        tiles = lambda qi, kj: (qi, jnp.minimum(kj, (p0 + (qi + 1) * tq - 1) // tk))
    return pl.pallas_call(
        functools.partial(_fox_attn_kernel, p0=p0, tq=tq, tk=tk, nq=nq, nk=nk, paired=paired),
        out_shape=jax.ShapeDtypeStruct((b, sq, D_FOX), BF16),
        grid=grid,
        in_specs=[
            pl.BlockSpec((1, H_FOX, tq, 2 * HD_FOX), lambda bi, qi, kj: (bi, 0, tiles(qi, kj)[0], 0)),
            pl.BlockSpec((1, H_FOX, tk, 2 * HD_FOX), lambda bi, qi, kj: (bi, 0, tiles(qi, kj)[1], 0)),
            pl.BlockSpec((1, H_FOX, HD_FOX, tk), lambda bi, qi, kj: (bi, 0, 0, tiles(qi, kj)[1])),
        ],
        out_specs=pl.BlockSpec((1, tq, D_FOX), lambda bi, qi, kj: (bi, tiles(qi, kj)[0], 0)),
        scratch_shapes=[
            pltpu.VMEM((H_FOX, 1, tq), F32),
            pltpu.VMEM((H_FOX, 1, tq), F32),
            pltpu.VMEM((H_FOX, HD_FOX, tq), F32),
        ],
        compiler_params=_cparams(("parallel", "parallel", "arbitrary")),
        name="fox_attn",
    )(qa, ka, vt)


def _merge_kernel(x_ref, hm_ref, of_ref, cq_ref, gates_ref, mk_ref, mv_ref, wm_ref, wf_ref, wc_ref,
                  wo_ref, gffn_ref, xnew_ref, hn_ref, hnlo_ref):
    cq = cq_ref[0]
    heads = []
    for h in range(H_MEM):
        sl = slice(h * HD_MEM, (h + 1) * HD_MEM)
        s = _nt_dot(cq[:, sl], mk_ref[0, :, sl]) * (HD_MEM ** -0.5)
        p = jnp.exp(s - jnp.max(s, axis=1, keepdims=True))
        o = jnp.dot(p.astype(BF16), mv_ref[0, :, sl], preferred_element_type=F32)
        heads.append(o / jnp.sum(p, axis=1, keepdims=True))
    oc = jnp.concatenate(heads, axis=1).astype(BF16)
    a_m = jnp.dot(hm_ref[0], wm_ref[...], preferred_element_type=F32)
    a_f = jnp.dot(of_ref[0], wf_ref[...], preferred_element_type=F32)
    a_c = jnp.dot(oc, wc_ref[...], preferred_element_type=F32)
    g = gates_ref[0].astype(F32)
    merged = (_sigmoid(g[:, 0:D_MODEL]) * a_m + _sigmoid(g[:, D_MODEL:2 * D_MODEL]) * a_f
              + _sigmoid(g[:, 2 * D_MODEL:3 * D_MODEL]) * a_c)
    xn = x_ref[0] + jnp.dot(merged.astype(BF16), wo_ref[...], preferred_element_type=F32)
    xnew_ref[0] = xn
    hn = _rms(xn, gffn_ref[...])
    hn_hi = hn.astype(BF16)
    hn_ref[0] = hn_hi
    hnlo_ref[0] = (hn - hn_hi.astype(F32)).astype(BF16)


def _merge(x, hm, of, cq, gates, mk, mv, wm, wf, wc, wo, gffn):
    b, s, _ = x.shape
    ts = min(s, ROW_TILE)
    tile = lambda w: pl.BlockSpec((1, ts, w), lambda bi, si: (bi, si, 0))
    const = lambda shape: pl.BlockSpec(shape, lambda bi, si: (0, 0))
    mem = pl.BlockSpec((1, mk.shape[1], D_MEM), lambda bi, si: (bi, 0, 0))
    return pl.pallas_call(
        _merge_kernel,
        out_shape=(jax.ShapeDtypeStruct((b, s, D_MODEL), F32), jax.ShapeDtypeStruct((b, s, D_MODEL), BF16),
                   jax.ShapeDtypeStruct((b, s, D_MODEL), BF16)),
        grid=(b, s // ts),
        in_specs=[tile(D_MODEL), tile(D_M), tile(D_FOX), tile(D_MEM), tile(N_BRANCH * D_MODEL), mem, mem,
                  const((D_M, D_MODEL)), const((D_FOX, D_MODEL)), const((D_MEM, D_MODEL)),
                  const((D_MODEL, D_MODEL)), const((1, D_MODEL))],
        out_specs=(tile(D_MODEL), tile(D_MODEL), tile(D_MODEL)),
        compiler_params=_cparams(("parallel", "parallel")),
        name="merge",
    )(x, hm, of, cq, gates, mk, mv, wm, wf, wc, wo, gffn)


def _extract_top16(s, key_io, val_scr, idx_scr, h, want_rank):
    rank = jnp.full(s.shape, float(PEER_TOPK), F32) if want_rank else None
    for r in range(PEER_TOPK):
        m = jnp.max(s, axis=0, keepdims=True)
        first = jnp.min(jnp.where(s == m, key_io, float(N_KEYS)), axis=0, keepdims=True)
        hit = key_io == first
        s = jnp.where(hit, -jnp.inf, s)
        val_scr[r, pl.ds(h, 1), :] = m
        if want_rank:
            rank = jnp.where(hit, float(r), rank)
        else:
            idx_scr[r, pl.ds(h, 1), :] = first
    return rank


def _dot3(a_hi, a_lo, b_hi, b_lo, dot):
    return dot(a_hi, b_hi) + dot(a_lo, b_hi) + dot(a_hi, b_lo)


def _router_kernel(hn_ref, hnlo_ref, wqt_ref, keys_ref, a0_ref, cnt_ref, r1_ref, b1_ref,
                   qr_scr, e1_scr, top0_scr, top1_scr, idx0_scr, cnt_scr, invz_scr):
    tp = hn_ref.shape[0]
    qr_scr[...] = _dot3(wqt_ref[0], wqt_ref[1], hn_ref[...], hnlo_ref[...], _nt_dot)
    key_io = lax.broadcasted_iota(jnp.int32, (N_KEYS, tp), 0).astype(F32)

    heads_per_trip = 8

    def scores_and_top16_group(hg, carry):
        for k in range(heads_per_trip):
            scores_and_top16(heads_per_trip * hg + k)
        return carry

    def scores_and_top16(h):
        base = pl.multiple_of(h * 2 * N_KEYS, 2 * N_KEYS)
        def scores(c):
            q = qr_scr[pl.ds(base + c * N_KEYS, N_KEYS), :]
            q_hi = q.astype(BF16)
            q_lo = (q - q_hi.astype(F32)).astype(BF16)
            return _dot3(keys_ref[0, 2 * h + c], keys_ref[1, 2 * h + c], q_hi, q_lo,
                         lambda a, b_: jnp.dot(a, b_, preferred_element_type=F32))

        s0, s1 = scores(0), scores(1)
        _extract_top16(s0, key_io, top0_scr, idx0_scr, h, False)
        rank1 = _extract_top16(s1, key_io, top1_scr, None, h, True)
        a0_ref[h] = jnp.exp(s0 - top0_scr[0, pl.ds(h, 1), :])
        e1_scr[h] = jnp.exp(s1 - top1_scr[0, pl.ds(h, 1), :])
        r1_ref[h] = rank1.astype(BF16)

    lax.fori_loop(0, PEER_HEADS // heads_per_trip, scores_and_top16_group, 0)

    top0 = [top0_scr[a] for a in range(PEER_TOPK)]
    top1 = [top1_scr[b] for b in range(PEER_TOPK)]
    cnt = [jnp.zeros((PEER_HEADS, tp), F32) for _ in range(PEER_TOPK)]
    for _ in range(PEER_TOPK):
        front = []
        for a in range(PEER_TOPK):
            nxt = jnp.full((PEER_HEADS, tp), -jnp.inf, F32)
            for bb in range(PEER_TOPK // (a + 1)):
                nxt = jnp.where(cnt[a] == float(bb), top1[bb], nxt)
            front.append(top0[a] + nxt)
        mx = functools.reduce(jnp.maximum, front)
        first = functools.reduce(jnp.minimum,
                                 [jnp.where(front[a] == mx, float(a), float(PEER_TOPK)) for a in range(PEER_TOPK)])
        cnt = [cnt[a] + jnp.where(first == float(a), 1.0, 0.0) for a in range(PEER_TOPK)]
    z = jnp.zeros((PEER_HEADS, tp), F32)
    for a in range(PEER_TOPK):
        za = jnp.zeros((PEER_HEADS, tp), F32)
        for bb in range(PEER_TOPK // (a + 1)):
            za = za + jnp.where(cnt[a] > float(bb), jnp.exp(top1[bb] - top1[0]), 0.0)
        z = z + jnp.exp(top0[a] - top0[0]) * za
        cnt_scr[a] = cnt[a]
    invz_scr[...] = 1.0 / z

    def counts_and_gates(h, carry):
        cnt_i = jnp.zeros((N_KEYS, tp), F32)
        for a in range(PEER_TOPK):
            cnt_i = jnp.where(key_io == idx0_scr[a, pl.ds(h, 1), :], cnt_scr[a, pl.ds(h, 1), :], cnt_i)
        cnt_ref[h] = cnt_i
        b1 = e1_scr[h] * invz_scr[pl.ds(h, 1), :]
        b1_ref[h] = b1.astype(BF16)
        return carry

    lax.fori_loop(0, PEER_HEADS, counts_and_gates, 0)


def _router(hn2d, hnlo2d, wqt, keys):
    t = hn2d.shape[0]
    tp = ROUTER_TOKENS
    shape = jax.ShapeDtypeStruct((PEER_HEADS, N_KEYS, t), F32)
    spec = pl.BlockSpec((PEER_HEADS, N_KEYS, tp), lambda i: (0, 0, i))
    slab_shape = jax.ShapeDtypeStruct((PEER_HEADS, N_KEYS, t), BF16)
    slab_spec = spec
    return pl.pallas_call(
        _router_kernel,
        out_shape=(shape, shape, slab_shape, slab_shape),
        grid=(t // tp,),
        in_specs=[
            pl.BlockSpec((tp, D_MODEL), lambda i: (i, 0)),
            pl.BlockSpec((tp, D_MODEL), lambda i: (i, 0)),
            pl.BlockSpec(wqt.shape, lambda i: (0, 0, 0)),
            pl.BlockSpec(keys.shape, lambda i: (0, 0, 0, 0)),
        ],
        out_specs=(spec, spec, slab_spec, slab_spec),
        scratch_shapes=[
            pltpu.VMEM((wqt.shape[1], tp), F32),
            pltpu.VMEM((PEER_HEADS, N_KEYS, tp), F32),
        ] + [pltpu.VMEM((PEER_TOPK, PEER_HEADS, tp), F32) for _ in range(4)] + [pltpu.VMEM((PEER_HEADS, tp), F32)],
        compiler_params=_cparams(("parallel",)),
        name="peer_router",
    )(hn2d, hnlo2d, wqt, keys)


def _gelu_tanh(x):
    k = -2.0 * 0.7978845608028654 * 1.4426950408889634
    return x / (1.0 + jnp.exp2(x * (k + (k * 0.044715) * (x * x))))


def _experts_kernel(hn_ref, u_ref, vt_ref, a0_ref, cnt_ref, r1_ref, b1_ref, x_ref, gfin_ref, out_ref,
                    acc_scr, pre0_scr, pre1_scr, w0_scr, w1_scr, *, rows_per_tile, n_tiles, final_norm):
    s = pl.program_id(1)
    tp = hn_ref.shape[0]
    te = u_ref.shape[0] // 2
    slab = 16
    slabs_per_row = N_KEYS // slab
    d_half = acc_scr.shape[0] // 2
    zero = jnp.zeros((), BF16)

    @pl.when(s == 0)
    def _():
        acc_scr[...] = jnp.zeros(acc_scr.shape, F32)
        pre1_scr[...] = jnp.zeros(pre1_scr.shape, F32)
        w0_scr[...] = jnp.zeros(w0_scr.shape, BF16)

    def pipeline_step(parity, pre_w, pre_r, w_w, w_r):
        e = 2 * s + parity
        cols = slice(parity * te, (parity + 1) * te)

        def stage_c(half):
            rows = slice(half * d_half, (half + 1) * d_half)
            acc_scr[rows, :] += jnp.dot(vt_ref[rows, cols], w_r[...], preferred_element_type=F32)

        def stage_a(half):
            rows = slice(half * (te // 2), (half + 1) * (te // 2))
            pre_w[rows, :] = _nt_dot(u_ref[parity * te + half * (te // 2):parity * te + (half + 1) * (te // 2), :],
                                     hn_ref[...])

        mxu_work = (lambda: stage_c(0), lambda: stage_a(0), lambda: stage_c(1), lambda: stage_a(1))

        valid = jnp.logical_and(e >= 1, e <= n_tiles)
        tile_b = jnp.clip(e - 1, 0, n_tiles - 1)
        for ib in range(rows_per_tile):
            for piece in mxu_work[ib * len(mxu_work) // rows_per_tile:(ib + 1) * len(mxu_work) // rows_per_tile]:
                piece()
            i = tile_b * rows_per_tile + ib
            for lanes in (slice(0, tp // 2), slice(tp // 2, tp)):
                gates = [jnp.zeros((slab, tp // 2), BF16) for _ in range(slabs_per_row)]
                for h in range(PEER_HEADS):
                    a_b = jnp.broadcast_to(a0_ref[h, pl.ds(i, 1), lanes], (slab, tp // 2)).astype(BF16)
                    c_b = jnp.broadcast_to(cnt_ref[h, pl.ds(i, 1), lanes], (slab, tp // 2)).astype(BF16)
                    for g in range(slabs_per_row):
                        key_rows = slice(g * slab, (g + 1) * slab)
                        gates[g] = gates[g] + a_b * jnp.where(r1_ref[h, key_rows, lanes] < c_b,
                                                              b1_ref[h, key_rows, lanes], zero)
                for g in range(slabs_per_row):
                    rows = slice(ib * N_KEYS + g * slab, ib * N_KEYS + (g + 1) * slab)
                    w = gates[g] * _gelu_tanh(pre_r[rows, lanes]).astype(BF16)
                    w_w[rows, lanes] = jnp.where(valid, w, zero)

    pipeline_step(0, pre0_scr, pre1_scr, w1_scr, w0_scr)
    pipeline_step(1, pre1_scr, pre0_scr, w0_scr, w1_scr)

    @pl.when(s == n_tiles // 2)
    def _():
        xo = x_ref[...] + acc_scr[...].T
        if final_norm:
            xo = _rms(xo, gfin_ref[...])
        out_ref[...] = xo


def _experts(hn2d, u_bf, vt_bf, layer, a0, cnt, r1, b1, x2d, gfin, final_norm):
    t = hn2d.shape[0]
    tp = ROW_TILE
    rows_per_tile = EXPERT_KEY_ROWS
    te = rows_per_tile * N_KEYS
    n_tiles = u_bf.shape[1] // te
    rspec = pl.BlockSpec((PEER_HEADS, N_KEYS, tp), lambda ti, e: (0, 0, ti))
    sspec = rspec
    return pl.pallas_call(
        functools.partial(_experts_kernel, rows_per_tile=rows_per_tile, n_tiles=n_tiles, final_norm=final_norm),
        out_shape=jax.ShapeDtypeStruct((t, D_MODEL), F32),
        grid=(t // tp, n_tiles // 2 + 1),
        in_specs=[
            pl.BlockSpec((tp, D_MODEL), lambda ti, s: (ti, 0)),
            pl.BlockSpec((None, 2 * te, D_MODEL), lambda ti, s: (layer, jnp.minimum(s, n_tiles // 2 - 1), 0)),
            pl.BlockSpec((None, D_MODEL, 2 * te), lambda ti, s: (layer, 0, jnp.maximum(s - 1, 0))),
            rspec, rspec, sspec, sspec,
            pl.BlockSpec((tp, D_MODEL), lambda ti, s: (ti, 0)),
            pl.BlockSpec((1, D_MODEL), lambda ti, s: (0, 0)),
        ],
        out_specs=pl.BlockSpec((tp, D_MODEL), lambda ti, s: (ti, 0)),
        scratch_shapes=[
            pltpu.VMEM((D_MODEL, tp), F32),
            pltpu.VMEM((te, tp), F32), pltpu.VMEM((te, tp), F32),
            pltpu.VMEM((te, tp), BF16), pltpu.VMEM((te, tp), BF16),
        ],
        compiler_params=_cparams(("parallel", "arbitrary")),
        name="peer_experts",
    )(hn2d, u_bf, vt_bf, a0, cnt, r1, b1, x2d, gfin)


def _hi_lo(a):
    hi = a.astype(BF16)
    return jnp.stack([hi, (a - hi.astype(F32)).astype(BF16)])


def _prep_layer_weights(p):
    w_in = p['w_in']
    o_mi = 4 * D_M
    o_fq = o_mi + 2 * H_M
    o_ff = o_fq + 3 * D_FOX
    o_cq = o_ff + H_FOX
    w_big = jnp.concatenate([w_in[:, :o_mi], w_in[:, o_fq:o_ff], w_in[:, o_cq:]], axis=1).astype(BF16)
    w_big = w_big.reshape(D_MODEL, -1, PROJ_TILE).transpose(1, 0, 2)
    w_small = jnp.concatenate([w_in[:, o_mi:o_fq], w_in[:, o_ff:o_cq]], axis=1)
    w_small = jnp.pad(w_small, ((0, 0), (0, LANES - w_small.shape[1])))
    w_small_hi = w_small.astype(BF16)
    w_small = jnp.stack([w_small_hi, (w_small - w_small_hi.astype(F32)).astype(BF16)])
    bias = jnp.concatenate([p['b_m_i'], p['b_m_f'], p['b_fox_f']]).astype(F32)
    return dict(
        norm_mix=p['norm_mix'].reshape(1, D_MODEL), w_big=w_big, w_small=w_small,
        conv_w=p['conv_w'], conv_b=p['conv_b'].reshape(1, 2 * D_M),
        bias_row=jnp.pad(bias, (0, LANES - 16)).reshape(1, LANES), bias_col=bias.reshape(16, 1),
        norm_m_head=p['norm_m_head'].reshape(1, D_M),
        w_up_m=p['w_up_m'].astype(BF16), w_up_f=p['w_up_f'].astype(BF16), w_up_c=p['w_up_c'].astype(BF16),
        w_out=p['w_out'].astype(BF16), norm_ffn=p['norm_ffn'].reshape(1, D_MODEL),
        wqt=_hi_lo(p['peer_wq'].T),
        keys=_hi_lo(p['peer_keys'].reshape(2 * PEER_HEADS, N_KEYS, N_KEYS)),
    )


def _layer(x, mem_k, mem_v, conv_prev, c0, n0, m0, fk_past, fv_past, flf_past, w, peer_tables, layer, gfin, final_norm):
    b, s, _ = x.shape
    t = b * s
    p0 = 0 if fk_past is None else fk_past.shape[1]
    qk, v_m, o_m, fq, fk, fv, cq, gates, small = _norm_proj(x.reshape(t, D_MODEL), w['norm_mix'], w['w_big'], w['w_small'])

    small3 = small.reshape(b, s, LANES)
    smallt = small3[:, :, :16].transpose(0, 2, 1)
    m0b = jnp.broadcast_to(m0[:, :, None, None], (b, H_M, 1, LANES))
    hm, c1, n1, m1b, conv_new = _mlstm(
        qk.reshape(b, s, 2 * D_M), v_m.reshape(b, s, D_M), o_m.reshape(b, s, D_M), small3, smallt,
        w['conv_w'], w['conv_b'], w['bias_row'], w['bias_col'], w['norm_m_head'],
        c0, n0[:, :, None, :], m0b, conv_prev)
    n1, m1 = n1[:, :, 0, :], m1b[:, :, 0, 0]

    k_f = fk.reshape(b, s, H_FOX, HD_FOX)
    v_f = fv.reshape(b, s, H_FOX, HD_FOX)
    s_pad = -(-s // LANES) * LANES
    pad_rows = lambda a: jnp.pad(a, ((0, 0), (0, s_pad - s), (0, 0)))
    past = flf_past.astype(F32) if p0 else None
    lf_all, hi, mid, lo = _fox_prep(past, pad_rows(small3), w['bias_row'], s)
    lf_f = lf_all[:, :s, FOX_GATE_LANE:FOX_GATE_LANE + H_FOX]
    sk = p0 + s_pad
    k_all, v_all = pad_rows(fk.reshape(b, s, D_FOX)), pad_rows(fv.reshape(b, s, D_FOX))
    if p0:
        k_all = jnp.concatenate([fk_past.reshape(b, p0, D_FOX).astype(F32), k_all], axis=1)
        v_all = jnp.concatenate([fv_past.reshape(b, p0, D_FOX).astype(F32), v_all], axis=1)
    qa = _fox_pack(pad_rows(fq.reshape(b, s, D_FOX)), hi, mid, lo, p0, True)[0]
    ka, vt_all = _fox_pack(k_all, hi, mid, lo, 0, False, v=v_all)
    o_f = _fox_attn(qa, ka, vt_all.reshape(b, H_FOX, HD_FOX, sk), p0)[:, :s]

    x_new, hn, hn_lo = _merge(x, hm, o_f, cq.reshape(b, s, D_MEM), gates.reshape(b, s, N_BRANCH * D_MODEL),
                       mem_k.reshape(b, -1, D_MEM).astype(BF16), mem_v.reshape(b, -1, D_MEM).astype(BF16),
                       w['w_up_m'], w['w_up_f'], w['w_up_c'], w['w_out'], w['norm_ffn'])

    hn2d = hn.reshape(t, D_MODEL)
    a0, cnt, r1, b1 = _router(hn2d, hn_lo.reshape(t, D_MODEL), w['wqt'], w['keys'])
    x_out = _experts(hn2d, peer_tables[0], peer_tables[1], layer, a0, cnt, r1, b1, x_new.reshape(t, D_MODEL),
                     gfin, final_norm)
    return x_out.reshape(b, s, D_MODEL), conv_new, c1, n1, m1, k_f, v_f, lf_f


def kernel(x_prompt, x_sample, mem_prompt, cache_fox_k, cache_fox_v, cache_fox_lf, state_mlstm_c, state_mlstm_n, state_mlstm_m, state_conv, cache_mem_k, cache_mem_v, norm_mix, w_in, conv_w, conv_b, b_m_i, b_m_f, norm_m_head, b_fox_f, norm_mem, w_mem_kv, w_up_m, w_up_f, w_up_c, w_out, norm_ffn, peer_wq, peer_keys, peer_u, peer_v, norm_final):
    depth = w_in.shape[0]
    names = dict(norm_mix=norm_mix, w_in=w_in, conv_w=conv_w, conv_b=conv_b, b_m_i=b_m_i, b_m_f=b_m_f,
                 norm_m_head=norm_m_head, b_fox_f=b_fox_f, w_up_m=w_up_m, w_up_f=w_up_f, w_up_c=w_up_c,
                 w_out=w_out, norm_ffn=norm_ffn, peer_wq=peer_wq, peer_keys=peer_keys, peer_u=peer_u,
                 peer_v=peer_v)
    weights = [_prep_layer_weights({k: a[l] for k, a in names.items()}) for l in range(depth)]
    peer_tables = (peer_u.astype(BF16), jnp.swapaxes(peer_v, 1, 2).astype(BF16))
    gfin = norm_final.reshape(1, D_MODEL)
    bp, n_mem = mem_prompt.shape[0], mem_prompt.shape[1]

    xp = x_prompt
    new_p = [[] for _ in range(9)]
    for l in range(depth):
        mk, mv = _norm_matmul(mem_prompt.reshape(bp * n_mem, D_MODEL), norm_mem[l].reshape(1, D_MODEL),
                              w_mem_kv[l].astype(BF16))
        mk = mk.reshape(bp, n_mem, H_MEM, HD_MEM)
        mv = mv.reshape(bp, n_mem, H_MEM, HD_MEM)
        xp, conv1, c1, n1, m1, kf, vf, lff = _layer(
            xp, mk, mv, jnp.zeros((bp, CONV_W - 1, 2 * D_M), F32),
            jnp.zeros((bp, H_M, HD_M, HD_M), F32), jnp.zeros((bp, H_M, HD_M), F32), jnp.zeros((bp, H_M), F32),
            None, None, None, weights[l], peer_tables, l, gfin, l == depth - 1)
        for lst, a in zip(new_p, (kf, vf, lff, c1, n1, m1, conv1, mk, mv)):
            lst.append(a)
    outs_p = [jnp.stack(a) for a in new_p]

    xs = x_sample
    new_s = [[] for _ in range(7)]
    for l in range(depth):
        xs, conv1, c1, n1, m1, kf, vf, lff = _layer(
            xs, cache_mem_k[l], cache_mem_v[l], state_conv[l], state_mlstm_c[l], state_mlstm_n[l],
            state_mlstm_m[l], cache_fox_k[l], cache_fox_v[l], cache_fox_lf[l], weights[l], peer_tables, l, gfin,
            l == depth - 1)
        for lst, a in zip(new_s, (kf, vf, lff, c1, n1, m1, conv1)):
            lst.append(a)
    outs_s = [jnp.stack(a) for a in new_s]

    return (xp, xs, *outs_p, *outs_s)
```

```python
import functools

import jax
import jax.numpy as jnp
from jax import lax
from jax.experimental import pallas as pl
from jax.experimental.pallas import tpu as pltpu

F32 = jnp.float32
BF16 = jnp.bfloat16
EPS = 1e-6
NEG_BIG = -1e30

D_MODEL = 1024
H_M, HD_M = 4, 128
D_M = H_M * HD_M
CONV_W = 4
H_FOX, HD_FOX = 8, 64
D_FOX = H_FOX * HD_FOX
H_MEM, HD_MEM = 4, 128
D_MEM = H_MEM * HD_MEM
N_BRANCH = 3
PEER_HEADS = 8
N_KEYS = 128
PEER_TOPK = 16
LANES = 128
FOX_GATE_LANE = 2 * H_M
MLSTM_CHUNK = 256
PROJ_TILE = 512
ROW_TILE = 512
KEY_TILE_SHORT = 384
ROUTER_TOKENS = 4 * LANES
EXPERT_KEY_ROWS = 4
VMEM_LIMIT = 56 * 1024 * 1024


def _cparams(sem):
    return pltpu.CompilerParams(dimension_semantics=sem, vmem_limit_bytes=VMEM_LIMIT)


def _nt_dot(a, b):
    return lax.dot_general(a, b, (((1,), (1,)), ((), ())), preferred_element_type=F32)


def _rms(x, g):
    return x * lax.rsqrt(jnp.mean(x * x, axis=-1, keepdims=True) + EPS) * g


def _log_sigmoid(x):
    return jnp.minimum(x, 0.0) - jnp.log(1.0 + jnp.exp(-jnp.abs(x)))


def _sigmoid(x):
    return 1.0 / (1.0 + jnp.exp(-x))


def _norm_proj_kernel(x_ref, g_ref, w_ref, ws_ref, qk_ref, v_ref, o_ref, fq_ref, fk_ref, fv_ref,
                      cq_ref, gates_ref, small_ref):
    hf = _rms(x_ref[...], g_ref[...])
    h = hf.astype(BF16)
    h_lo = (hf - h.astype(F32)).astype(BF16)
    small_ref[...] = (jnp.dot(h, ws_ref[0], preferred_element_type=F32)
                      + jnp.dot(h_lo, ws_ref[0], preferred_element_type=F32)
                      + jnp.dot(h, ws_ref[1], preferred_element_type=F32))

    def tile(j):
        return jnp.dot(h, w_ref[j], preferred_element_type=F32)

    qk_ref[:, 0:PROJ_TILE] = tile(0)
    qk_ref[:, PROJ_TILE:2 * PROJ_TILE] = tile(1)
    v_ref[...] = tile(2).astype(BF16)
    o_ref[...] = tile(3).astype(BF16)
    fq_ref[...] = tile(4).astype(BF16)
    fk_ref[...] = tile(5)
    fv_ref[...] = tile(6)
    cq_ref[...] = tile(7).astype(BF16)
    for j in range(8, w_ref.shape[0]):
        gates_ref[:, (j - 8) * PROJ_TILE:(j - 7) * PROJ_TILE] = tile(j).astype(BF16)


def _norm_proj(x2d, g, w_big, w_small):
    t = x2d.shape[0]
    tm = ROW_TILE
    row = lambda i: (i, 0)
    out_widths = ((2 * D_M, F32),
                  (D_M, BF16),
                  (D_M, BF16),
                  (D_FOX, BF16),
                  (D_FOX, F32),
                  (D_FOX, F32),
                  (D_MEM, BF16),
                  (N_BRANCH * D_MODEL, BF16),
                  (LANES, F32))
    resident = lambda shape: pl.BlockSpec(shape, lambda i: (0,) * len(shape), pipeline_mode=pl.Buffered(1))
    return pl.pallas_call(
        _norm_proj_kernel,
        out_shape=tuple(jax.ShapeDtypeStruct((t, wd), dt) for wd, dt in out_widths),
        grid=(t // tm,),
        in_specs=[
            pl.BlockSpec((tm, D_MODEL), row),
            resident((1, D_MODEL)),
            resident(w_big.shape),
            resident(w_small.shape),
        ],
        out_specs=tuple(pl.BlockSpec((tm, wd), row) for wd, _ in out_widths),
        compiler_params=_cparams(("parallel",)),
        name="norm_proj",
    )(x2d, g, w_big, w_small)


def _norm_matmul_kernel(x_ref, g_ref, w_ref, k_ref, v_ref):
    h = _rms(x_ref[...], g_ref[...]).astype(BF16)
    k_ref[...] = jnp.dot(h, w_ref[:, 0:D_MEM], preferred_element_type=F32)
    v_ref[...] = jnp.dot(h, w_ref[:, D_MEM:2 * D_MEM], preferred_element_type=F32)


def _norm_matmul(x2d, g, w):
    t = x2d.shape[0]
    tm = ROW_TILE
    out = jax.ShapeDtypeStruct((t, D_MEM), F32)
    return pl.pallas_call(
        _norm_matmul_kernel,
        out_shape=(out, out),
        grid=(t // tm,),
        in_specs=[
            pl.BlockSpec((tm, D_MODEL), lambda i: (i, 0)),
            pl.BlockSpec((1, D_MODEL), lambda i: (0, 0)),
            pl.BlockSpec((D_MODEL, 2 * D_MEM), lambda i: (0, 0)),
        ],
        out_specs=(pl.BlockSpec((tm, D_MEM), lambda i: (i, 0)), pl.BlockSpec((tm, D_MEM), lambda i: (i, 0))),
        compiler_params=_cparams(("parallel",)),
        name="norm_matmul",
    )(x2d, g, w)


def _mlstm_kernel(qk_ref, v_ref, o_ref, small_ref, smallt_ref, convw_ref, convb_ref, brow_ref,
                  bcol_ref, ghead_ref, c0_ref, n0_ref, m0_ref, conv0_ref,
                  hm_ref, c_out_ref, n_out_ref, m_out_ref, conv_out_ref,
                  c_scr, n_scr, m_scr, xp_scr, *, chunk):
    L = chunk
    si = pl.program_id(1)

    @pl.when(si == 0)
    def _():
        c_scr[...] = c0_ref[0]
        n_scr[...] = n0_ref[0]
        m_scr[...] = m0_ref[0]
        xp_scr[5:8, :] = conv0_ref[0]

    xp_scr[8:8 + L, :] = qk_ref[0]
    y = convb_ref[...] + convw_ref[0:1, :] * xp_scr[5:5 + L, :]
    for j in range(1, CONV_W):
        y = y + convw_ref[j:j + 1, :] * xp_scr[5 + j:5 + j + L, :]
    y = y * _sigmoid(y)
    tail = xp_scr[5 + L:8 + L, :]
    xp_scr[5:8, :] = tail
    conv_out_ref[0] = tail

    t_io = lax.broadcasted_iota(jnp.int32, (L, L), 0)
    s_io = lax.broadcasted_iota(jnp.int32, (L, L), 1)
    causal = s_io <= t_io
    ones_col = (lax.broadcasted_iota(jnp.int32, (L, HD_M), 1) == 0).astype(BF16)

    gate_c = small_ref[0] + brow_ref[...]
    gate_r = smallt_ref[0] + bcol_ref[...]
    lsig_c = _log_sigmoid(gate_c)
    lsig_r = _log_sigmoid(gate_r)
    outs = []
    for h in range(H_M):
        q = y[:, h * HD_M:(h + 1) * HD_M]
        k = y[:, D_M + h * HD_M:D_M + (h + 1) * HD_M] * (HD_M ** -0.5)
        v_aug = jnp.concatenate([v_ref[0, :, h * HD_M:(h + 1) * HD_M], ones_col], axis=1)
        ig_col = gate_c[:, h:h + 1]
        ig_row = gate_r[h:h + 1, :]
        lf_col = lsig_c[:, H_M + h:H_M + h + 1]
        lf_row = lsig_r[H_M + h:H_M + h + 1, :]
        bcum_col = jnp.sum(jnp.where(causal, lf_row, 0.0), axis=1, keepdims=True)
        bcum_row = jnp.sum(jnp.where(t_io <= s_io, lf_col, 0.0), axis=0, keepdims=True)
        m_prev = m_scr[h][:, 0:1]
        dmat = jnp.where(causal, bcum_col - bcum_row + ig_row, -jnp.inf)
        inter = bcum_col + m_prev
        m_t = jnp.maximum(inter, jnp.max(dmat, axis=1, keepdims=True))
        w_intra = jnp.exp(dmat - m_t)
        w_state = jnp.exp(inter - m_t)
        qb = q.astype(BF16)
        sw = _nt_dot(qb, k.astype(BF16)) * w_intra
        c_prev = c_scr[h]
        n_prev = n_scr[h]
        intra = jnp.dot(sw.astype(BF16), v_aug, preferred_element_type=F32)
        num = intra[:, 0:HD_M] + w_state * jnp.dot(qb, c_prev.astype(BF16), preferred_element_type=F32)
        den = intra[:, HD_M:HD_M + 1] + w_state * jnp.sum(q * n_prev, axis=1, keepdims=True)
        hh = num / jnp.maximum(jnp.abs(den), jnp.exp(-m_t))
        hh = _rms(hh, ghead_ref[:, h * HD_M:(h + 1) * HD_M])
        outs.append(hh * _sigmoid(o_ref[0, :, h * HD_M:(h + 1) * HD_M].astype(F32)))
        bl = bcum_row[:, L - 1:L]
        m_new = jnp.maximum(bl + m_prev, jnp.max(bl - bcum_row + ig_row, axis=1, keepdims=True))
        decay = jnp.exp(bl + m_prev - m_new)
        ws_col = jnp.exp(bl - bcum_col + ig_col - m_new)
        kw = k * ws_col
        c_scr[h] = decay * c_prev + jnp.dot(kw.T.astype(BF16), v_ref[0, :, h * HD_M:(h + 1) * HD_M],
                                            preferred_element_type=F32)
        n_scr[h] = decay * n_prev + jnp.sum(kw, axis=0, keepdims=True)
        m_scr[h] = jnp.broadcast_to(m_new, (1, LANES))
    hm_ref[0] = jnp.concatenate(outs, axis=1).astype(BF16)
    c_out_ref[0] = c_scr[...]
    n_out_ref[0] = n_scr[...]
    m_out_ref[0] = m_scr[...]


def _mlstm(qk, v, o, small, smallt, convw, convb, brow, bcol, ghead, c0, n0, m0b, conv0):
    b, s, _ = qk.shape
    chunk = min(s, MLSTM_CHUNK)
    tile = lambda w: pl.BlockSpec((1, chunk, w), lambda bi, si: (bi, si, 0))
    per_b = lambda shape: pl.BlockSpec((1,) + shape, lambda bi, si: (bi,) + (0,) * len(shape))
    const = lambda shape: pl.BlockSpec(shape, lambda bi, si: (0, 0))
    state_shapes = ((H_M, HD_M, HD_M), (H_M, 1, HD_M), (H_M, 1, LANES), (CONV_W - 1, 2 * D_M))
    return pl.pallas_call(
        functools.partial(_mlstm_kernel, chunk=chunk),
        out_shape=(jax.ShapeDtypeStruct((b, s, D_M), BF16),)
        + tuple(jax.ShapeDtypeStruct((b,) + sh, F32) for sh in state_shapes),
        grid=(b, s // chunk),
        in_specs=[
            tile(2 * D_M), tile(D_M), tile(D_M), tile(LANES),
            pl.BlockSpec((1, 16, chunk), lambda bi, si: (bi, 0, si)),
            const((CONV_W, 2 * D_M)), const((1, 2 * D_M)), const((1, LANES)), const((16, 1)), const((1, D_M)),
        ] + [per_b(sh) for sh in state_shapes],
        out_specs=(tile(D_M),) + tuple(per_b(sh) for sh in state_shapes),
        scratch_shapes=[
            pltpu.VMEM((H_M, HD_M, HD_M), F32),
            pltpu.VMEM((H_M, 1, HD_M), F32),
            pltpu.VMEM((H_M, 1, LANES), F32),
            pltpu.VMEM((chunk + 8, 2 * D_M), F32),
        ],
        compiler_params=_cparams(("parallel", "arbitrary")),
        name="mlstm",
    )(qk, v, o, small, smallt, convw, convb, brow, bcol, ghead, c0, n0, m0b, conv0)


def _split3(x):
    hi = x.astype(BF16).astype(F32)
    r = x - hi
    mid = r.astype(BF16).astype(F32)
    lo = (r - mid).astype(BF16).astype(F32)
    return hi, mid, lo


def _fox_prep_kernel(*refs, n_past, n_new, s_valid, blk):
    if n_past:
        past_ref, pre_ref, bias_ref, lf_ref, hi_ref, mid_ref, lo_ref = refs
    else:
        pre_ref, bias_ref, lf_ref, hi_ref, mid_ref, lo_ref = refs
    tri = (lax.broadcasted_iota(jnp.int32, (blk, blk), 1) <= lax.broadcasted_iota(jnp.int32, (blk, blk), 0)).astype(BF16)
    carry = jnp.zeros((1, LANES), F32)
    start = 0
    while start < n_past + n_new:
        if start < n_past:
            r = min(blk, n_past - start)
            lf = jnp.concatenate([jnp.zeros((r, FOX_GATE_LANE), F32), past_ref[0, start:start + r, :],
                                  jnp.zeros((r, LANES - FOX_GATE_LANE - H_FOX), F32)], axis=1)
        else:
            r = min(blk, n_past + n_new - start)
            ns = start - n_past
            row = ns + lax.broadcasted_iota(jnp.int32, (r, LANES), 0)
            lf = jnp.where(row < s_valid, _log_sigmoid(pre_ref[0, ns:ns + r, :] + bias_ref[...]), 0.0)
            lf_ref[0, ns:ns + r, :] = lf
        cum = carry
        for piece in _split3(lf):
            cum = cum + jnp.dot(tri[0:r, 0:r], piece.astype(BF16), preferred_element_type=F32)
        carry = cum[r - 1:r, :]
        hi, mid, lo = _split3(cum)
        hi_ref[0, start:start + r, :] = hi
        mid_ref[0, start:start + r, :] = mid
        lo_ref[0, start:start + r, :] = lo
        start += r


def _fox_prep(past, pre, bias_row, s_valid):
    b, n_new = pre.shape[:2]
    n_past = 0 if past is None else past.shape[1]
    n = n_past + n_new
    full = lambda rows: pl.BlockSpec((1, rows, LANES), lambda bi: (bi, 0, 0))
    past_spec = pl.BlockSpec((1, n_past, H_FOX), lambda bi: (bi, 0, 0))
    in_specs = ([past_spec] if n_past else []) + [full(n_new), pl.BlockSpec((1, LANES), lambda bi: (0, 0))]
    args = ([past] if n_past else []) + [pre, bias_row]
    cum_shape = jax.ShapeDtypeStruct((b, n, LANES), F32)
    return pl.pallas_call(
        functools.partial(_fox_prep_kernel, n_past=n_past, n_new=n_new, s_valid=s_valid, blk=256),
        out_shape=(jax.ShapeDtypeStruct((b, n_new, LANES), F32), cum_shape, cum_shape, cum_shape),
        grid=(b,),
        in_specs=in_specs,
        out_specs=(full(n_new), full(n), full(n), full(n)),
        compiler_params=_cparams(("parallel",)),
        name="fox_prep",
    )(*args)


def _fox_pack_kernel(*refs, q_side, with_v):
    if with_v:
        x_ref, hi_ref, mid_ref, lo_ref, v_ref, out_ref, vt_ref = refs
        vt_ref[0] = v_ref[0].T.astype(BF16)
    else:
        x_ref, hi_ref, mid_ref, lo_ref, out_ref = refs
    tm = x_ref.shape[1]
    lane = lax.broadcasted_iota(jnp.int32, (tm, HD_FOX), 1)
    hi, mid, lo = hi_ref[0], mid_ref[0], lo_ref[0]
    for h in range(H_FOX):
        c = FOX_GATE_LANE + h
        pieces = (hi[:, c:c + 1], mid[:, c:c + 1], lo[:, c:c + 1])
        first = 0 if q_side else 3
        bias = jnp.where(jnp.logical_and(lane >= 3 - first, lane < 6 - first), -1.0 if q_side else 1.0, 0.0)
        for j, p in enumerate(pieces):
            bias = jnp.where(lane == first + j, p, bias)
        out_ref[0, h] = jnp.concatenate(
            [x_ref[0, :, h * HD_FOX:(h + 1) * HD_FOX].astype(BF16), bias.astype(BF16)], axis=1)


def _fox_pack(x, hi, mid, lo, row_offset, q_side, v=None):
    b, n, _ = x.shape
    tm = next(c for c in (ROW_TILE, KEY_TILE_SHORT, 2 * LANES, LANES) if n % c == 0 and row_offset % c == 0)
    off = row_offset // tm
    tile = lambda w: pl.BlockSpec((1, tm, w), lambda bi, i: (bi, i, 0))
    piece = pl.BlockSpec((1, tm, LANES), lambda bi, i: (bi, off + i, 0))
    out_shape = [jax.ShapeDtypeStruct((b, H_FOX, n, 2 * HD_FOX), BF16)]
    out_specs = [pl.BlockSpec((1, H_FOX, tm, 2 * HD_FOX), lambda bi, i: (bi, 0, i, 0))]
    in_specs = [tile(D_FOX), piece, piece, piece]
    args = [x, hi, mid, lo]
    if v is not None:
        in_specs.append(tile(D_FOX))
        args.append(v)
        out_shape.append(jax.ShapeDtypeStruct((b, D_FOX, n), BF16))
        out_specs.append(pl.BlockSpec((1, D_FOX, tm), lambda bi, i: (bi, 0, i)))
    return pl.pallas_call(
        functools.partial(_fox_pack_kernel, q_side=q_side, with_v=v is not None),
        out_shape=tuple(out_shape),
        grid=(b, n // tm),
        in_specs=in_specs,
        out_specs=tuple(out_specs),
        compiler_params=_cparams(("parallel", "parallel")),
        name="fox_pack_q" if q_side else "fox_pack_kv",
    )(*args)


def _paired_tiles(pair, step, nq):
    first = step <= pair
    return jnp.where(first, pair, nq - 1 - pair), jnp.where(first, step, step - pair - 1)


def _fox_attn_kernel(qa_ref, ka_ref, vt_ref, out_ref, m_scr, l_scr, acc_scr, *, p0, tq, tk, nq, nk, paired):
    qi, kj = pl.program_id(1), pl.program_id(2)
    if paired:
        qi, kj = _paired_tiles(qi, kj, nq)

    @pl.when(kj == 0)
    def _():
        m_scr[...] = jnp.full(m_scr.shape, NEG_BIG, F32)
        l_scr[...] = jnp.zeros(l_scr.shape, F32)
        acc_scr[...] = jnp.zeros(acc_scr.shape, F32)

    def step(masked):
        col = lax.broadcasted_iota(jnp.int32, (1, 2 * HD_FOX), 1)
        scale = jnp.where(col < HD_FOX, HD_FOX ** -0.5, 1.0).astype(BF16)
        ones_rows = jnp.ones((16, tk), BF16)
        if masked:
            kpos = kj * tk + lax.broadcasted_iota(jnp.int32, (tk, tq), 0)
            qpos = p0 + qi * tq + lax.broadcasted_iota(jnp.int32, (tk, tq), 1)
            mask = kpos <= qpos
        for h in range(H_FOX):
            s = _nt_dot(ka_ref[0, h], qa_ref[0, h] * scale)
            if masked:
                s = jnp.where(mask, s, NEG_BIG)
            m_prev = m_scr[h]
            m_new = jnp.maximum(m_prev, jnp.max(s, axis=0, keepdims=True))
            p = jnp.exp(s - m_new)
            alpha = jnp.exp(m_prev - m_new)
            pv = jnp.dot(jnp.concatenate([vt_ref[0, h], ones_rows], axis=0), p.astype(BF16),
                         preferred_element_type=F32)
            l_scr[h] = alpha * l_scr[h] + pv[HD_FOX:HD_FOX + 1, :]
            acc_scr[h] = alpha * acc_scr[h] + pv[0:HD_FOX, :]
            m_scr[h] = m_new

    first_q = p0 + qi * tq
    last_q = first_q + tq - 1
    unmasked = (kj + 1) * tk - 1 <= first_q

    @pl.when(unmasked)
    def _():
        step(False)

    @pl.when(jnp.logical_and(jnp.logical_not(unmasked), kj * tk <= last_q))
    def _():
        step(True)

    @pl.when(kj == jnp.minimum(last_q // tk, nk - 1))
    def _():
        o_t = jnp.concatenate([acc_scr[h] / l_scr[h] for h in range(H_FOX)], axis=0)
        out_ref[0] = o_t.T.astype(BF16)


def _fox_attn(qa, ka, vt, p0):
    b, _, sq, _ = qa.shape
    sk = ka.shape[2]
    tq = min(sq, ROW_TILE)
    tk = ROW_TILE if sk % ROW_TILE == 0 else KEY_TILE_SHORT
    nq, nk = sq // tq, sk // tk
    paired = p0 == 0 and tq == tk and nq == nk and nq % 2 == 0
    if paired:
        grid = (b, nq // 2, nq + 1)
        tiles = lambda qi, kj: _paired_tiles(qi, kj, nq)
    else:
        grid = (b, nq, nk)
        tiles = lambda qi, kj: (qi, jnp.minimum(kj, (p0 + (qi + 1) * tq - 1) // tk))
    return pl.pallas_call(
        functools.partial(_fox_attn_kernel, p0=p0, tq=tq, tk=tk, nq=nq, nk=nk, paired=paired),
        out_shape=jax.ShapeDtypeStruct((b, sq, D_FOX), BF16),
        grid=grid,
        in_specs=[
            pl.BlockSpec((1, H_FOX, tq, 2 * HD_FOX), lambda bi, qi, kj: (bi, 0, tiles(qi, kj)[0], 0)),
            pl.BlockSpec((1, H_FOX, tk, 2 * HD_FOX), lambda bi, qi, kj: (bi, 0, tiles(qi, kj)[1], 0)),
            pl.BlockSpec((1, H_FOX, HD_FOX, tk), lambda bi, qi, kj: (bi, 0, 0, tiles(qi, kj)[1])),
        ],
        out_specs=pl.BlockSpec((1, tq, D_FOX), lambda bi, qi, kj: (bi, tiles(qi, kj)[0], 0)),
        scratch_shapes=[
            pltpu.VMEM((H_FOX, 1, tq), F32),
            pltpu.VMEM((H_FOX, 1, tq), F32),
            pltpu.VMEM((H_FOX, HD_FOX, tq), F32),
        ],
        compiler_params=_cparams(("parallel", "parallel", "arbitrary")),
        name="fox_attn",
    )(qa, ka, vt)


def _merge_kernel(x_ref, hm_ref, of_ref, cq_ref, gates_ref, mk_ref, mv_ref, wm_ref, wf_ref, wc_ref,
                  wo_ref, gffn_ref, xnew_ref, hn_ref, hnlo_ref):
    cq = cq_ref[0]
    heads = []
    for h in range(H_MEM):
        sl = slice(h * HD_MEM, (h + 1) * HD_MEM)
        s = _nt_dot(cq[:, sl], mk_ref[0, :, sl]) * (HD_MEM ** -0.5)
        p = jnp.exp(s - jnp.max(s, axis=1, keepdims=True))
        o = jnp.dot(p.astype(BF16), mv_ref[0, :, sl], preferred_element_type=F32)
        heads.append(o / jnp.sum(p, axis=1, keepdims=True))
    oc = jnp.concatenate(heads, axis=1).astype(BF16)
    a_m = jnp.dot(hm_ref[0], wm_ref[...], preferred_element_type=F32)
    a_f = jnp.dot(of_ref[0], wf_ref[...], preferred_element_type=F32)
    a_c = jnp.dot(oc, wc_ref[...], preferred_element_type=F32)
    g = gates_ref[0].astype(F32)
    merged = (_sigmoid(g[:, 0:D_MODEL]) * a_m + _sigmoid(g[:, D_MODEL:2 * D_MODEL]) * a_f
              + _sigmoid(g[:, 2 * D_MODEL:3 * D_MODEL]) * a_c)
    xn = x_ref[0] + jnp.dot(merged.astype(BF16), wo_ref[...], preferred_element_type=F32)
    xnew_ref[0] = xn
    hn = _rms(xn, gffn_ref[...])
    hn_hi = hn.astype(BF16)
    hn_ref[0] = hn_hi
    hnlo_ref[0] = (hn - hn_hi.astype(F32)).astype(BF16)


def _merge(x, hm, of, cq, gates, mk, mv, wm, wf, wc, wo, gffn):
    b, s, _ = x.shape
    ts = min(s, ROW_TILE)
    tile = lambda w: pl.BlockSpec((1, ts, w), lambda bi, si: (bi, si, 0))
    const = lambda shape: pl.BlockSpec(shape, lambda bi, si: (0, 0))
    mem = pl.BlockSpec((1, mk.shape[1], D_MEM), lambda bi, si: (bi, 0, 0))
    return pl.pallas_call(
        _merge_kernel,
        out_shape=(jax.ShapeDtypeStruct((b, s, D_MODEL), F32), jax.ShapeDtypeStruct((b, s, D_MODEL), BF16),
                   jax.ShapeDtypeStruct((b, s, D_MODEL), BF16)),
        grid=(b, s // ts),
        in_specs=[tile(D_MODEL), tile(D_M), tile(D_FOX), tile(D_MEM), tile(N_BRANCH * D_MODEL), mem, mem,
                  const((D_M, D_MODEL)), const((D_FOX, D_MODEL)), const((D_MEM, D_MODEL)),
                  const((D_MODEL, D_MODEL)), const((1, D_MODEL))],
        out_specs=(tile(D_MODEL), tile(D_MODEL), tile(D_MODEL)),
        compiler_params=_cparams(("parallel", "parallel")),
        name="merge",
    )(x, hm, of, cq, gates, mk, mv, wm, wf, wc, wo, gffn)


def _extract_top16(s, key_io, val_scr, idx_scr, h, want_rank):
    rank = jnp.full(s.shape, float(PEER_TOPK), F32) if want_rank else None
    for r in range(PEER_TOPK):
        m = jnp.max(s, axis=0, keepdims=True)
        first = jnp.min(jnp.where(s == m, key_io, float(N_KEYS)), axis=0, keepdims=True)
        hit = key_io == first
        s = jnp.where(hit, -jnp.inf, s)
        val_scr[r, pl.ds(h, 1), :] = m
        if want_rank:
            rank = jnp.where(hit, float(r), rank)
        else:
            idx_scr[r, pl.ds(h, 1), :] = first
    return rank


def _dot3(a_hi, a_lo, b_hi, b_lo, dot):
    return dot(a_hi, b_hi) + dot(a_lo, b_hi) + dot(a_hi, b_lo)


def _router_kernel(hn_ref, hnlo_ref, wqt_ref, keys_ref, a0_ref, cnt_ref, r1_ref, b1_ref,
                   qr_scr, e1_scr, top0_scr, top1_scr, idx0_scr, cnt_scr, invz_scr):
    tp = hn_ref.shape[0]
    qr_scr[...] = _dot3(wqt_ref[0], wqt_ref[1], hn_ref[...], hnlo_ref[...], _nt_dot)
    key_io = lax.broadcasted_iota(jnp.int32, (N_KEYS, tp), 0).astype(F32)

    heads_per_trip = 8

    def scores_and_top16_group(hg, carry):
        for k in range(heads_per_trip):
            scores_and_top16(heads_per_trip * hg + k)
        return carry

    def scores_and_top16(h):
        base = pl.multiple_of(h * 2 * N_KEYS, 2 * N_KEYS)
        def scores(c):
            q = qr_scr[pl.ds(base + c * N_KEYS, N_KEYS), :]
            q_hi = q.astype(BF16)
            q_lo = (q - q_hi.astype(F32)).astype(BF16)
            return _dot3(keys_ref[0, 2 * h + c], keys_ref[1, 2 * h + c], q_hi, q_lo,
                         lambda a, b_: jnp.dot(a, b_, preferred_element_type=F32))

        s0, s1 = scores(0), scores(1)
        _extract_top16(s0, key_io, top0_scr, idx0_scr, h, False)
        rank1 = _extract_top16(s1, key_io, top1_scr, None, h, True)
        a0_ref[h] = jnp.exp(s0 - top0_scr[0, pl.ds(h, 1), :])
        e1_scr[h] = jnp.exp(s1 - top1_scr[0, pl.ds(h, 1), :])
        r1_ref[h] = rank1.astype(BF16)

    lax.fori_loop(0, PEER_HEADS // heads_per_trip, scores_and_top16_group, 0)

    top0 = [top0_scr[a] for a in range(PEER_TOPK)]
    top1 = [top1_scr[b] for b in range(PEER_TOPK)]
    cnt = [jnp.zeros((PEER_HEADS, tp), F32) for _ in range(PEER_TOPK)]
    for _ in range(PEER_TOPK):
        front = []
        for a in range(PEER_TOPK):
            nxt = jnp.full((PEER_HEADS, tp), -jnp.inf, F32)
            for bb in range(PEER_TOPK // (a + 1)):
                nxt = jnp.where(cnt[a] == float(bb), top1[bb], nxt)
            front.append(top0[a] + nxt)
        mx = functools.reduce(jnp.maximum, front)
        first = functools.reduce(jnp.minimum,
                                 [jnp.where(front[a] == mx, float(a), float(PEER_TOPK)) for a in range(PEER_TOPK)])
        cnt = [cnt[a] + jnp.where(first == float(a), 1.0, 0.0) for a in range(PEER_TOPK)]
    z = jnp.zeros((PEER_HEADS, tp), F32)
    for a in range(PEER_TOPK):
        za = jnp.zeros((PEER_HEADS, tp), F32)
        for bb in range(PEER_TOPK // (a + 1)):
            za = za + jnp.where(cnt[a] > float(bb), jnp.exp(top1[bb] - top1[0]), 0.0)
        z = z + jnp.exp(top0[a] - top0[0]) * za
        cnt_scr[a] = cnt[a]
    invz_scr[...] = 1.0 / z

    def counts_and_gates(h, carry):
        cnt_i = jnp.zeros((N_KEYS, tp), F32)
        for a in range(PEER_TOPK):
            cnt_i = jnp.where(key_io == idx0_scr[a, pl.ds(h, 1), :], cnt_scr[a, pl.ds(h, 1), :], cnt_i)
        cnt_ref[h] = cnt_i
        b1 = e1_scr[h] * invz_scr[pl.ds(h, 1), :]
        b1_ref[h] = b1.astype(BF16)
        return carry

    lax.fori_loop(0, PEER_HEADS, counts_and_gates, 0)


def _router(hn2d, hnlo2d, wqt, keys):
    t = hn2d.shape[0]
    tp = ROUTER_TOKENS
    shape = jax.ShapeDtypeStruct((PEER_HEADS, N_KEYS, t), F32)
    spec = pl.BlockSpec((PEER_HEADS, N_KEYS, tp), lambda i: (0, 0, i))
    slab_shape = jax.ShapeDtypeStruct((PEER_HEADS, N_KEYS, t), BF16)
    slab_spec = spec
    return pl.pallas_call(
        _router_kernel,
        out_shape=(shape, shape, slab_shape, slab_shape),
        grid=(t // tp,),
        in_specs=[
            pl.BlockSpec((tp, D_MODEL), lambda i: (i, 0)),
            pl.BlockSpec((tp, D_MODEL), lambda i: (i, 0)),
            pl.BlockSpec(wqt.shape, lambda i: (0, 0, 0)),
            pl.BlockSpec(keys.shape, lambda i: (0, 0, 0, 0)),
        ],
        out_specs=(spec, spec, slab_spec, slab_spec),
        scratch_shapes=[
            pltpu.VMEM((wqt.shape[1], tp), F32),
            pltpu.VMEM((PEER_HEADS, N_KEYS, tp), F32),
        ] + [pltpu.VMEM((PEER_TOPK, PEER_HEADS, tp), F32) for _ in range(4)] + [pltpu.VMEM((PEER_HEADS, tp), F32)],
        compiler_params=_cparams(("parallel",)),
        name="peer_router",
    )(hn2d, hnlo2d, wqt, keys)


def _gelu_tanh(x):
    k = -2.0 * 0.7978845608028654 * 1.4426950408889634
    return x / (1.0 + jnp.exp2(x * (k + (k * 0.044715) * (x * x))))


def _experts_kernel(hn_ref, u_ref, vt_ref, a0_ref, cnt_ref, r1_ref, b1_ref, x_ref, gfin_ref, out_ref,
                    acc_scr, pre0_scr, pre1_scr, w0_scr, w1_scr, *, rows_per_tile, n_tiles, final_norm):
    s = pl.program_id(1)
    tp = hn_ref.shape[0]
    te = u_ref.shape[0] // 2
    slab = 16
    slabs_per_row = N_KEYS // slab
    d_half = acc_scr.shape[0] // 2
    zero = jnp.zeros((), BF16)

    @pl.when(s == 0)
    def _():
        acc_scr[...] = jnp.zeros(acc_scr.shape, F32)
        pre1_scr[...] = jnp.zeros(pre1_scr.shape, F32)
        w0_scr[...] = jnp.zeros(w0_scr.shape, BF16)

    def pipeline_step(parity, pre_w, pre_r, w_w, w_r):
        e = 2 * s + parity
        cols = slice(parity * te, (parity + 1) * te)

        def stage_c(half):
            rows = slice(half * d_half, (half + 1) * d_half)
            acc_scr[rows, :] += jnp.dot(vt_ref[rows, cols], w_r[...], preferred_element_type=F32)

        def stage_a(half):
            rows = slice(half * (te // 2), (half + 1) * (te // 2))
            pre_w[rows, :] = _nt_dot(u_ref[parity * te + half * (te // 2):parity * te + (half + 1) * (te // 2), :],
                                     hn_ref[...])

        mxu_work = (lambda: stage_c(0), lambda: stage_a(0), lambda: stage_c(1), lambda: stage_a(1))

        valid = jnp.logical_and(e >= 1, e <= n_tiles)
        tile_b = jnp.clip(e - 1, 0, n_tiles - 1)
        for ib in range(rows_per_tile):
            for piece in mxu_work[ib * len(mxu_work) // rows_per_tile:(ib + 1) * len(mxu_work) // rows_per_tile]:
                piece()
            i = tile_b * rows_per_tile + ib
            for lanes in (slice(0, tp // 2), slice(tp // 2, tp)):
                gates = [None] * slabs_per_row
                for h in range(PEER_HEADS):
                    a_b = jnp.broadcast_to(a0_ref[h, pl.ds(i, 1), lanes], (slab, tp // 2)).astype(BF16)
                    c_b = jnp.broadcast_to(cnt_ref[h, pl.ds(i, 1), lanes], (slab, tp // 2)).astype(BF16)
                    for g in range(slabs_per_row):
                        key_rows = slice(g * slab, (g + 1) * slab)
                        term = a_b * jnp.where(r1_ref[h, key_rows, lanes] < c_b, b1_ref[h, key_rows, lanes], zero)
                        gates[g] = term if h == 0 else gates[g] + term
                for g in range(slabs_per_row):
                    rows = slice(ib * N_KEYS + g * slab, ib * N_KEYS + (g + 1) * slab)
                    w = gates[g] * _gelu_tanh(pre_r[rows, lanes]).astype(BF16)
                    w_w[rows, lanes] = jnp.where(valid, w, zero)

    pipeline_step(0, pre0_scr, pre1_scr, w1_scr, w0_scr)
    pipeline_step(1, pre1_scr, pre0_scr, w0_scr, w1_scr)

    @pl.when(s == n_tiles // 2)
    def _():
        xo = x_ref[...] + acc_scr[...].T
        if final_norm:
            xo = _rms(xo, gfin_ref[...])
        out_ref[...] = xo


def _experts(hn2d, u_bf, vt_bf, layer, a0, cnt, r1, b1, x2d, gfin, final_norm):
    t = hn2d.shape[0]
    tp = ROW_TILE
    rows_per_tile = EXPERT_KEY_ROWS
    te = rows_per_tile * N_KEYS
    n_tiles = u_bf.shape[1] // te
    rspec = pl.BlockSpec((PEER_HEADS, N_KEYS, tp), lambda ti, e: (0, 0, ti))
    sspec = rspec
    return pl.pallas_call(
        functools.partial(_experts_kernel, rows_per_tile=rows_per_tile, n_tiles=n_tiles, final_norm=final_norm),
        out_shape=jax.ShapeDtypeStruct((t, D_MODEL), F32),
        grid=(t // tp, n_tiles // 2 + 1),
        in_specs=[
            pl.BlockSpec((tp, D_MODEL), lambda ti, s: (ti, 0)),
            pl.BlockSpec((None, 2 * te, D_MODEL), lambda ti, s: (layer, jnp.minimum(s, n_tiles // 2 - 1), 0)),
            pl.BlockSpec((None, D_MODEL, 2 * te), lambda ti, s: (layer, 0, jnp.maximum(s - 1, 0))),
            rspec, rspec, sspec, sspec,
            pl.BlockSpec((tp, D_MODEL), lambda ti, s: (ti, 0)),
            pl.BlockSpec((1, D_MODEL), lambda ti, s: (0, 0)),
        ],
        out_specs=pl.BlockSpec((tp, D_MODEL), lambda ti, s: (ti, 0)),
        scratch_shapes=[
            pltpu.VMEM((D_MODEL, tp), F32),
            pltpu.VMEM((te, tp), F32), pltpu.VMEM((te, tp), F32),
            pltpu.VMEM((te, tp), BF16), pltpu.VMEM((te, tp), BF16),
        ],
        compiler_params=_cparams(("parallel", "arbitrary")),
        name="peer_experts",
    )(hn2d, u_bf, vt_bf, a0, cnt, r1, b1, x2d, gfin)


def _hi_lo(a):
    hi = a.astype(BF16)
    return jnp.stack([hi, (a - hi.astype(F32)).astype(BF16)])


def _prep_layer_weights(p):
    w_in = p['w_in']
    o_mi = 4 * D_M
    o_fq = o_mi + 2 * H_M
    o_ff = o_fq + 3 * D_FOX
    o_cq = o_ff + H_FOX
    w_big = jnp.concatenate([w_in[:, :o_mi], w_in[:, o_fq:o_ff], w_in[:, o_cq:]], axis=1).astype(BF16)
    w_big = w_big.reshape(D_MODEL, -1, PROJ_TILE).transpose(1, 0, 2)
    w_small = jnp.concatenate([w_in[:, o_mi:o_fq], w_in[:, o_ff:o_cq]], axis=1)
    w_small = jnp.pad(w_small, ((0, 0), (0, LANES - w_small.shape[1])))
    w_small_hi = w_small.astype(BF16)
    w_small = jnp.stack([w_small_hi, (w_small - w_small_hi.astype(F32)).astype(BF16)])
    bias = jnp.concatenate([p['b_m_i'], p['b_m_f'], p['b_fox_f']]).astype(F32)
    return dict(
        norm_mix=p['norm_mix'].reshape(1, D_MODEL), w_big=w_big, w_small=w_small,
        conv_w=p['conv_w'], conv_b=p['conv_b'].reshape(1, 2 * D_M),
        bias_row=jnp.pad(bias, (0, LANES - 16)).reshape(1, LANES), bias_col=bias.reshape(16, 1),
        norm_m_head=p['norm_m_head'].reshape(1, D_M),
        w_up_m=p['w_up_m'].astype(BF16), w_up_f=p['w_up_f'].astype(BF16), w_up_c=p['w_up_c'].astype(BF16),
        w_out=p['w_out'].astype(BF16), norm_ffn=p['norm_ffn'].reshape(1, D_MODEL),
        wqt=_hi_lo(p['peer_wq'].T),
        keys=_hi_lo(p['peer_keys'].reshape(2 * PEER_HEADS, N_KEYS, N_KEYS)),
    )


def _layer(x, mem_k, mem_v, conv_prev, c0, n0, m0, fk_past, fv_past, flf_past, w, peer_tables, layer, gfin, final_norm):
    b, s, _ = x.shape
    t = b * s
    p0 = 0 if fk_past is None else fk_past.shape[1]
    qk, v_m, o_m, fq, fk, fv, cq, gates, small = _norm_proj(x.reshape(t, D_MODEL), w['norm_mix'], w['w_big'], w['w_small'])

    small3 = small.reshape(b, s, LANES)
    smallt = small3[:, :, :16].transpose(0, 2, 1)
    m0b = jnp.broadcast_to(m0[:, :, None, None], (b, H_M, 1, LANES))
    hm, c1, n1, m1b, conv_new = _mlstm(
        qk.reshape(b, s, 2 * D_M), v_m.reshape(b, s, D_M), o_m.reshape(b, s, D_M), small3, smallt,
        w['conv_w'], w['conv_b'], w['bias_row'], w['bias_col'], w['norm_m_head'],
        c0, n0[:, :, None, :], m0b, conv_prev)
    n1, m1 = n1[:, :, 0, :], m1b[:, :, 0, 0]

    k_f = fk.reshape(b, s, H_FOX, HD_FOX)
    v_f = fv.reshape(b, s, H_FOX, HD_FOX)
    s_pad = -(-s // LANES) * LANES
    pad_rows = lambda a: jnp.pad(a, ((0, 0), (0, s_pad - s), (0, 0)))
    past = flf_past.astype(F32) if p0 else None
    lf_all, hi, mid, lo = _fox_prep(past, pad_rows(small3), w['bias_row'], s)
    lf_f = lf_all[:, :s, FOX_GATE_LANE:FOX_GATE_LANE + H_FOX]
    sk = p0 + s_pad
    k_all, v_all = pad_rows(fk.reshape(b, s, D_FOX)), pad_rows(fv.reshape(b, s, D_FOX))
    if p0:
        k_all = jnp.concatenate([fk_past.reshape(b, p0, D_FOX).astype(F32), k_all], axis=1)
        v_all = jnp.concatenate([fv_past.reshape(b, p0, D_FOX).astype(F32), v_all], axis=1)
    qa = _fox_pack(pad_rows(fq.reshape(b, s, D_FOX)), hi, mid, lo, p0, True)[0]
    ka, vt_all = _fox_pack(k_all, hi, mid, lo, 0, False, v=v_all)
    o_f = _fox_attn(qa, ka, vt_all.reshape(b, H_FOX, HD_FOX, sk), p0)[:, :s]

    x_new, hn, hn_lo = _merge(x, hm, o_f, cq.reshape(b, s, D_MEM), gates.reshape(b, s, N_BRANCH * D_MODEL),
                       mem_k.reshape(b, -1, D_MEM).astype(BF16), mem_v.reshape(b, -1, D_MEM).astype(BF16),
                       w['w_up_m'], w['w_up_f'], w['w_up_c'], w['w_out'], w['norm_ffn'])

    hn2d = hn.reshape(t, D_MODEL)
    a0, cnt, r1, b1 = _router(hn2d, hn_lo.reshape(t, D_MODEL), w['wqt'], w['keys'])
    x_out = _experts(hn2d, peer_tables[0], peer_tables[1], layer, a0, cnt, r1, b1, x_new.reshape(t, D_MODEL),
                     gfin, final_norm)
    return x_out.reshape(b, s, D_MODEL), conv_new, c1, n1, m1, k_f, v_f, lf_f


def kernel(x_prompt, x_sample, mem_prompt, cache_fox_k, cache_fox_v, cache_fox_lf, state_mlstm_c, state_mlstm_n, state_mlstm_m, state_conv, cache_mem_k, cache_mem_v, norm_mix, w_in, conv_w, conv_b, b_m_i, b_m_f, norm_m_head, b_fox_f, norm_mem, w_mem_kv, w_up_m, w_up_f, w_up_c, w_out, norm_ffn, peer_wq, peer_keys, peer_u, peer_v, norm_final):
    depth = w_in.shape[0]
    names = dict(norm_mix=norm_mix, w_in=w_in, conv_w=conv_w, conv_b=conv_b, b_m_i=b_m_i, b_m_f=b_m_f,
                 norm_m_head=norm_m_head, b_fox_f=b_fox_f, w_up_m=w_up_m, w_up_f=w_up_f, w_up_c=w_up_c,
                 w_out=w_out, norm_ffn=norm_ffn, peer_wq=peer_wq, peer_keys=peer_keys, peer_u=peer_u,
                 peer_v=peer_v)
    weights = [_prep_layer_weights({k: a[l] for k, a in names.items()}) for l in range(depth)]
    peer_tables = (peer_u.astype(BF16), jnp.swapaxes(peer_v, 1, 2).astype(BF16))
    gfin = norm_final.reshape(1, D_MODEL)
    bp, n_mem = mem_prompt.shape[0], mem_prompt.shape[1]

    xp = x_prompt
    new_p = [[] for _ in range(9)]
    for l in range(depth):
        mk, mv = _norm_matmul(mem_prompt.reshape(bp * n_mem, D_MODEL), norm_mem[l].reshape(1, D_MODEL),
                              w_mem_kv[l].astype(BF16))
        mk = mk.reshape(bp, n_mem, H_MEM, HD_MEM)
        mv = mv.reshape(bp, n_mem, H_MEM, HD_MEM)
        xp, conv1, c1, n1, m1, kf, vf, lff = _layer(
            xp, mk, mv, jnp.zeros((bp, CONV_W - 1, 2 * D_M), F32),
            jnp.zeros((bp, H_M, HD_M, HD_M), F32), jnp.zeros((bp, H_M, HD_M), F32), jnp.zeros((bp, H_M), F32),
            None, None, None, weights[l], peer_tables, l, gfin, l == depth - 1)
        for lst, a in zip(new_p, (kf, vf, lff, c1, n1, m1, conv1, mk, mv)):
            lst.append(a)
    outs_p = [jnp.stack(a) for a in new_p]

    xs = x_sample
    new_s = [[] for _ in range(7)]
    for l in range(depth):
        xs, conv1, c1, n1, m1, kf, vf, lff = _layer(
            xs, cache_mem_k[l], cache_mem_v[l], state_conv[l], state_mlstm_c[l], state_mlstm_n[l],
            state_mlstm_m[l], cache_fox_k[l], cache_fox_v[l], cache_fox_lf[l], weights[l], peer_tables, l, gfin,
            l == depth - 1)
        for lst, a in zip(new_s, (kf, vf, lff, c1, n1, m1, conv1)):
            lst.append(a)
    outs_s = [jnp.stack(a) for a in new_s]

    return (xp, xs, *outs_p, *outs_s)
```

```python
import functools

import jax
import jax.numpy as jnp
from jax import lax
from jax.experimental import pallas as pl
from jax.experimental.pallas import tpu as pltpu

F32 = jnp.float32
BF16 = jnp.bfloat16
EPS = 1e-6
NEG_BIG = -1e30

D_MODEL = 1024
H_M, HD_M = 4, 128
D_M = H_M * HD_M
CONV_W = 4
H_FOX, HD_FOX = 8, 64
D_FOX = H_FOX * HD_FOX
H_MEM, HD_MEM = 4, 128
D_MEM = H_MEM * HD_MEM
N_BRANCH = 3
PEER_HEADS = 8
N_KEYS = 128
PEER_TOPK = 16
LANES = 128
FOX_GATE_LANE = 2 * H_M
MLSTM_CHUNK = 256
PROJ_TILE = 512
ROW_TILE = 512
KEY_TILE_SHORT = 384
ROUTER_TOKENS = 4 * LANES
EXPERT_KEY_ROWS = 4
VMEM_LIMIT = 56 * 1024 * 1024


def _cparams(sem):
    return pltpu.CompilerParams(dimension_semantics=sem, vmem_limit_bytes=VMEM_LIMIT)


def _nt_dot(a, b):
    return lax.dot_general(a, b, (((1,), (1,)), ((), ())), preferred_element_type=F32)


def _rms(x, g):
    return x * lax.rsqrt(jnp.mean(x * x, axis=-1, keepdims=True) + EPS) * g


def _log_sigmoid(x):
    return jnp.minimum(x, 0.0) - jnp.log(1.0 + jnp.exp(-jnp.abs(x)))


def _sigmoid(x):
    return 1.0 / (1.0 + jnp.exp(-x))


def _norm_proj_kernel(x_ref, g_ref, w_ref, ws_ref, qk_ref, v_ref, o_ref, fq_ref, fk_ref, fv_ref,
                      cq_ref, gates_ref, small_ref):
    hf = _rms(x_ref[...], g_ref[...])
    h = hf.astype(BF16)
    h_lo = (hf - h.astype(F32)).astype(BF16)
    small_ref[...] = (jnp.dot(h, ws_ref[0], preferred_element_type=F32)
                      + jnp.dot(h_lo, ws_ref[0], preferred_element_type=F32)
                      + jnp.dot(h, ws_ref[1], preferred_element_type=F32))

    def tile(j):
        return jnp.dot(h, w_ref[j], preferred_element_type=F32)

    qk_ref[:, 0:PROJ_TILE] = tile(0)
    qk_ref[:, PROJ_TILE:2 * PROJ_TILE] = tile(1)
    v_ref[...] = tile(2).astype(BF16)
    o_ref[...] = tile(3).astype(BF16)
    fq_ref[...] = tile(4).astype(BF16)
    fk_ref[...] = tile(5)
    fv_ref[...] = tile(6)
    cq_ref[...] = tile(7).astype(BF16)
    for j in range(8, w_ref.shape[0]):
        gates_ref[:, (j - 8) * PROJ_TILE:(j - 7) * PROJ_TILE] = tile(j).astype(BF16)


def _norm_proj(x2d, g, w_big, w_small):
    t = x2d.shape[0]
    tm = ROW_TILE
    row = lambda i: (i, 0)
    out_widths = ((2 * D_M, F32),
                  (D_M, BF16),
                  (D_M, BF16),
                  (D_FOX, BF16),
                  (D_FOX, F32),
                  (D_FOX, F32),
                  (D_MEM, BF16),
                  (N_BRANCH * D_MODEL, BF16),
                  (LANES, F32))
    resident = lambda shape: pl.BlockSpec(shape, lambda i: (0,) * len(shape), pipeline_mode=pl.Buffered(1))
    return pl.pallas_call(
        _norm_proj_kernel,
        out_shape=tuple(jax.ShapeDtypeStruct((t, wd), dt) for wd, dt in out_widths),
        grid=(t // tm,),
        in_specs=[
            pl.BlockSpec((tm, D_MODEL), row),
            resident((1, D_MODEL)),
            resident(w_big.shape),
            resident(w_small.shape),
        ],
        out_specs=tuple(pl.BlockSpec((tm, wd), row) for wd, _ in out_widths),
        compiler_params=_cparams(("parallel",)),
        name="norm_proj",
    )(x2d, g, w_big, w_small)


def _norm_matmul_kernel(x_ref, g_ref, w_ref, k_ref, v_ref):
    h = _rms(x_ref[...], g_ref[...]).astype(BF16)
    k_ref[...] = jnp.dot(h, w_ref[:, 0:D_MEM], preferred_element_type=F32)
    v_ref[...] = jnp.dot(h, w_ref[:, D_MEM:2 * D_MEM], preferred_element_type=F32)


def _norm_matmul(x2d, g, w):
    t = x2d.shape[0]
    tm = ROW_TILE
    out = jax.ShapeDtypeStruct((t, D_MEM), F32)
    return pl.pallas_call(
        _norm_matmul_kernel,
        out_shape=(out, out),
        grid=(t // tm,),
        in_specs=[
            pl.BlockSpec((tm, D_MODEL), lambda i: (i, 0)),
            pl.BlockSpec((1, D_MODEL), lambda i: (0, 0)),
            pl.BlockSpec((D_MODEL, 2 * D_MEM), lambda i: (0, 0)),
        ],
        out_specs=(pl.BlockSpec((tm, D_MEM), lambda i: (i, 0)), pl.BlockSpec((tm, D_MEM), lambda i: (i, 0))),
        compiler_params=_cparams(("parallel",)),
        name="norm_matmul",
    )(x2d, g, w)


def _mlstm_kernel(qk_ref, v_ref, o_ref, small_ref, smallt_ref, convw_ref, convb_ref, brow_ref,
                  bcol_ref, ghead_ref, c0_ref, n0_ref, m0_ref, conv0_ref,
                  hm_ref, c_out_ref, n_out_ref, m_out_ref, conv_out_ref,
                  c_scr, n_scr, m_scr, xp_scr, *, chunk):
    L = chunk
    si = pl.program_id(1)

    @pl.when(si == 0)
    def _():
        c_scr[...] = c0_ref[0]
        n_scr[...] = n0_ref[0]
        m_scr[...] = m0_ref[0]
        xp_scr[5:8, :] = conv0_ref[0]

    xp_scr[8:8 + L, :] = qk_ref[0]
    y = convb_ref[...] + convw_ref[0:1, :] * xp_scr[5:5 + L, :]
    for j in range(1, CONV_W):
        y = y + convw_ref[j:j + 1, :] * xp_scr[5 + j:5 + j + L, :]
    y = y * _sigmoid(y)
    tail = xp_scr[5 + L:8 + L, :]
    xp_scr[5:8, :] = tail
    conv_out_ref[0] = tail

    t_io = lax.broadcasted_iota(jnp.int32, (L, L), 0)
    s_io = lax.broadcasted_iota(jnp.int32, (L, L), 1)
    causal = s_io <= t_io
    ones_col = (lax.broadcasted_iota(jnp.int32, (L, HD_M), 1) == 0).astype(BF16)

    gate_c = small_ref[0] + brow_ref[...]
    gate_r = smallt_ref[0] + bcol_ref[...]
    lsig_c = _log_sigmoid(gate_c)
    lsig_r = _log_sigmoid(gate_r)
    outs = []
    for h in range(H_M):
        q = y[:, h * HD_M:(h + 1) * HD_M]
        k = y[:, D_M + h * HD_M:D_M + (h + 1) * HD_M] * (HD_M ** -0.5)
        v_aug = jnp.concatenate([v_ref[0, :, h * HD_M:(h + 1) * HD_M], ones_col], axis=1)
        ig_col = gate_c[:, h:h + 1]
        ig_row = gate_r[h:h + 1, :]
        lf_col = lsig_c[:, H_M + h:H_M + h + 1]
        lf_row = lsig_r[H_M + h:H_M + h + 1, :]
        bcum_col = jnp.sum(jnp.where(causal, lf_row, 0.0), axis=1, keepdims=True)
        bcum_row = jnp.sum(jnp.where(t_io <= s_io, lf_col, 0.0), axis=0, keepdims=True)
        m_prev = m_scr[h][:, 0:1]
        dmat = jnp.where(causal, bcum_col - bcum_row + ig_row, -jnp.inf)
        inter = bcum_col + m_prev
        m_t = jnp.maximum(inter, jnp.max(dmat, axis=1, keepdims=True))
        w_intra = jnp.exp(dmat - m_t)
        w_state = jnp.exp(inter - m_t)
        qb = q.astype(BF16)
        sw = _nt_dot(qb, k.astype(BF16)) * w_intra
        c_prev = c_scr[h]
        n_prev = n_scr[h]
        intra = jnp.dot(sw.astype(BF16), v_aug, preferred_element_type=F32)
        num = intra[:, 0:HD_M] + w_state * jnp.dot(qb, c_prev.astype(BF16), preferred_element_type=F32)
        den = intra[:, HD_M:HD_M + 1] + w_state * jnp.sum(q * n_prev, axis=1, keepdims=True)
        hh = num / jnp.maximum(jnp.abs(den), jnp.exp(-m_t))
        hh = _rms(hh, ghead_ref[:, h * HD_M:(h + 1) * HD_M])
        outs.append(hh * _sigmoid(o_ref[0, :, h * HD_M:(h + 1) * HD_M].astype(F32)))
        bl = bcum_row[:, L - 1:L]
        m_new = jnp.maximum(bl + m_prev, jnp.max(bl - bcum_row + ig_row, axis=1, keepdims=True))
        decay = jnp.exp(bl + m_prev - m_new)
        ws_col = jnp.exp(bl - bcum_col + ig_col - m_new)
        kw = k * ws_col
        c_scr[h] = decay * c_prev + jnp.dot(kw.T.astype(BF16), v_ref[0, :, h * HD_M:(h + 1) * HD_M],
                                            preferred_element_type=F32)
        n_scr[h] = decay * n_prev + jnp.sum(kw, axis=0, keepdims=True)
        m_scr[h] = jnp.broadcast_to(m_new, (1, LANES))
    hm_ref[0] = jnp.concatenate(outs, axis=1).astype(BF16)
    c_out_ref[0] = c_scr[...]
    n_out_ref[0] = n_scr[...]
    m_out_ref[0] = m_scr[...]


def _mlstm(qk, v, o, small, smallt, convw, convb, brow, bcol, ghead, c0, n0, m0b, conv0):
    b, s, _ = qk.shape
    chunk = min(s, MLSTM_CHUNK)
    tile = lambda w: pl.BlockSpec((1, chunk, w), lambda bi, si: (bi, si, 0))
    per_b = lambda shape: pl.BlockSpec((1,) + shape, lambda bi, si: (bi,) + (0,) * len(shape))
    const = lambda shape: pl.BlockSpec(shape, lambda bi, si: (0, 0))
    state_shapes = ((H_M, HD_M, HD_M), (H_M, 1, HD_M), (H_M, 1, LANES), (CONV_W - 1, 2 * D_M))
    return pl.pallas_call(
        functools.partial(_mlstm_kernel, chunk=chunk),
        out_shape=(jax.ShapeDtypeStruct((b, s, D_M), BF16),)
        + tuple(jax.ShapeDtypeStruct((b,) + sh, F32) for sh in state_shapes),
        grid=(b, s // chunk),
        in_specs=[
            tile(2 * D_M), tile(D_M), tile(D_M), tile(LANES),
            pl.BlockSpec((1, 16, chunk), lambda bi, si: (bi, 0, si)),
            const((CONV_W, 2 * D_M)), const((1, 2 * D_M)), const((1, LANES)), const((16, 1)), const((1, D_M)),
        ] + [per_b(sh) for sh in state_shapes],
        out_specs=(tile(D_M),) + tuple(per_b(sh) for sh in state_shapes),
        scratch_shapes=[
            pltpu.VMEM((H_M, HD_M, HD_M), F32),
            pltpu.VMEM((H_M, 1, HD_M), F32),
            pltpu.VMEM((H_M, 1, LANES), F32),
            pltpu.VMEM((chunk + 8, 2 * D_M), F32),
        ],
        compiler_params=_cparams(("parallel", "arbitrary")),
        name="mlstm",
    )(qk, v, o, small, smallt, convw, convb, brow, bcol, ghead, c0, n0, m0b, conv0)


def _split3(x):
    hi = x.astype(BF16).astype(F32)
    r = x - hi
    mid = r.astype(BF16).astype(F32)
    lo = (r - mid).astype(BF16).astype(F32)
    return hi, mid, lo


def _fox_prep_kernel(*refs, n_past, n_new, s_valid, blk):
    if n_past:
        past_ref, pre_ref, bias_ref, lf_ref, hi_ref, mid_ref, lo_ref = refs
    else:
        pre_ref, bias_ref, lf_ref, hi_ref, mid_ref, lo_ref = refs
    tri = (lax.broadcasted_iota(jnp.int32, (blk, blk), 1) <= lax.broadcasted_iota(jnp.int32, (blk, blk), 0)).astype(BF16)
    carry = jnp.zeros((1, LANES), F32)
    start = 0
    while start < n_past + n_new:
        if start < n_past:
            r = min(blk, n_past - start)
            lf = jnp.concatenate([jnp.zeros((r, FOX_GATE_LANE), F32), past_ref[0, start:start + r, :],
                                  jnp.zeros((r, LANES - FOX_GATE_LANE - H_FOX), F32)], axis=1)
        else:
            r = min(blk, n_past + n_new - start)
            ns = start - n_past
            row = ns + lax.broadcasted_iota(jnp.int32, (r, LANES), 0)
            lf = jnp.where(row < s_valid, _log_sigmoid(pre_ref[0, ns:ns + r, :] + bias_ref[...]), 0.0)
            lf_ref[0, ns:ns + r, :] = lf
        cum = carry
        for piece in _split3(lf):
            cum = cum + jnp.dot(tri[0:r, 0:r], piece.astype(BF16), preferred_element_type=F32)
        carry = cum[r - 1:r, :]
        hi, mid, lo = _split3(cum)
        hi_ref[0, start:start + r, :] = hi
        mid_ref[0, start:start + r, :] = mid
        lo_ref[0, start:start + r, :] = lo
        start += r


def _fox_prep(past, pre, bias_row, s_valid):
    b, n_new = pre.shape[:2]
    n_past = 0 if past is None else past.shape[1]
    n = n_past + n_new
    full = lambda rows: pl.BlockSpec((1, rows, LANES), lambda bi: (bi, 0, 0))
    past_spec = pl.BlockSpec((1, n_past, H_FOX), lambda bi: (bi, 0, 0))
    in_specs = ([past_spec] if n_past else []) + [full(n_new), pl.BlockSpec((1, LANES), lambda bi: (0, 0))]
    args = ([past] if n_past else []) + [pre, bias_row]
    cum_shape = jax.ShapeDtypeStruct((b, n, LANES), F32)
    return pl.pallas_call(
        functools.partial(_fox_prep_kernel, n_past=n_past, n_new=n_new, s_valid=s_valid, blk=256),
        out_shape=(jax.ShapeDtypeStruct((b, n_new, LANES), F32), cum_shape, cum_shape, cum_shape),
        grid=(b,),
        in_specs=in_specs,
        out_specs=(full(n_new), full(n), full(n), full(n)),
        compiler_params=_cparams(("parallel",)),
        name="fox_prep",
    )(*args)


def _fox_pack_kernel(*refs, q_side, with_v):
    if with_v:
        x_ref, hi_ref, mid_ref, lo_ref, v_ref, out_ref, vt_ref = refs
        vt_ref[0] = v_ref[0].T.astype(BF16)
    else:
        x_ref, hi_ref, mid_ref, lo_ref, out_ref = refs
    tm = x_ref.shape[1]
    lane = lax.broadcasted_iota(jnp.int32, (tm, HD_FOX), 1)
    hi, mid, lo = hi_ref[0], mid_ref[0], lo_ref[0]
    for h in range(H_FOX):
        c = FOX_GATE_LANE + h
        pieces = (hi[:, c:c + 1], mid[:, c:c + 1], lo[:, c:c + 1])
        first = 0 if q_side else 3
        bias = jnp.where(jnp.logical_and(lane >= 3 - first, lane < 6 - first), -1.0 if q_side else 1.0, 0.0)
        for j, p in enumerate(pieces):
            bias = jnp.where(lane == first + j, p, bias)
        out_ref[0, h] = jnp.concatenate(
            [x_ref[0, :, h * HD_FOX:(h + 1) * HD_FOX].astype(BF16), bias.astype(BF16)], axis=1)


def _fox_pack(x, hi, mid, lo, row_offset, q_side, v=None):
    b, n, _ = x.shape
    tm = next(c for c in (ROW_TILE, KEY_TILE_SHORT, 2 * LANES, LANES) if n % c == 0 and row_offset % c == 0)
    off = row_offset // tm
    tile = lambda w: pl.BlockSpec((1, tm, w), lambda bi, i: (bi, i, 0))
    piece = pl.BlockSpec((1, tm, LANES), lambda bi, i: (bi, off + i, 0))
    out_shape = [jax.ShapeDtypeStruct((b, H_FOX, n, 2 * HD_FOX), BF16)]
    out_specs = [pl.BlockSpec((1, H_FOX, tm, 2 * HD_FOX), lambda bi, i: (bi, 0, i, 0))]
    in_specs = [tile(D_FOX), piece, piece, piece]
    args = [x, hi, mid, lo]
    if v is not None:
        in_specs.append(tile(D_FOX))
        args.append(v)
        out_shape.append(jax.ShapeDtypeStruct((b, D_FOX, n), BF16))
        out_specs.append(pl.BlockSpec((1, D_FOX, tm), lambda bi, i: (bi, 0, i)))
    return pl.pallas_call(
        functools.partial(_fox_pack_kernel, q_side=q_side, with_v=v is not None),
        out_shape=tuple(out_shape),
        grid=(b, n // tm),
        in_specs=in_specs,
        out_specs=tuple(out_specs),
        compiler_params=_cparams(("parallel", "parallel")),
        name="fox_pack_q" if q_side else "fox_pack_kv",
    )(*args)


def _paired_tiles(pair, step, nq):
    first = step <= pair
    return jnp.where(first, pair, nq - 1 - pair), jnp.where(first, step, step - pair - 1)


def _fox_attn_kernel(qa_ref, ka_ref, vt_ref, out_ref, m_scr, l_scr, acc_scr, *, p0, tq, tk, nq, nk, paired):
    qi, kj = pl.program_id(1), pl.program_id(2)
    if paired:
        qi, kj = _paired_tiles(qi, kj, nq)

    @pl.when(kj == 0)
    def _():
        m_scr[...] = jnp.full(m_scr.shape, NEG_BIG, F32)
        l_scr[...] = jnp.zeros(l_scr.shape, F32)
        acc_scr[...] = jnp.zeros(acc_scr.shape, F32)

    def step(masked):
        col = lax.broadcasted_iota(jnp.int32, (1, 2 * HD_FOX), 1)
        scale = jnp.where(col < HD_FOX, HD_FOX ** -0.5, 1.0).astype(BF16)
        ones_rows = jnp.ones((16, tk), BF16)
        if masked:
            kpos = kj * tk + lax.broadcasted_iota(jnp.int32, (tk, tq), 0)
            qpos = p0 + qi * tq + lax.broadcasted_iota(jnp.int32, (tk, tq), 1)
            mask = kpos <= qpos
        for h in range(H_FOX):
            s = _nt_dot(ka_ref[0, h], qa_ref[0, h] * scale)
            if masked:
                s = jnp.where(mask, s, NEG_BIG)
            m_prev = m_scr[h]
            m_new = jnp.maximum(m_prev, jnp.max(s, axis=0, keepdims=True))
            p = jnp.exp(s - m_new)
            alpha = jnp.exp(m_prev - m_new)
            pv = jnp.dot(jnp.concatenate([vt_ref[0, h], ones_rows], axis=0), p.astype(BF16),
                         preferred_element_type=F32)
            l_scr[h] = alpha * l_scr[h] + pv[HD_FOX:HD_FOX + 1, :]
            acc_scr[h] = alpha * acc_scr[h] + pv[0:HD_FOX, :]
            m_scr[h] = m_new

    first_q = p0 + qi * tq
    last_q = first_q + tq - 1
    unmasked = (kj + 1) * tk - 1 <= first_q

    @pl.when(unmasked)
    def _():
        step(False)

    @pl.when(jnp.logical_and(jnp.logical_not(unmasked), kj * tk <= last_q))
    def _():
        step(True)

    @pl.when(kj == jnp.minimum(last_q // tk, nk - 1))
    def _():
        o_t = jnp.concatenate([acc_scr[h] / l_scr[h] for h in range(H_FOX)], axis=0)
        out_ref[0] = o_t.T.astype(BF16)


def _fox_attn(qa, ka, vt, p0):
    b, _, sq, _ = qa.shape
    sk = ka.shape[2]
    tq = min(sq, 2 * ROW_TILE)
    tk = 2 * ROW_TILE if sk % (2 * ROW_TILE) == 0 else KEY_TILE_SHORT
    nq, nk = sq // tq, sk // tk
    paired = p0 == 0 and tq == tk and nq == nk and nq % 2 == 0
    if paired:
        grid = (b, nq // 2, nq + 1)
        tiles = lambda qi, kj: _paired_tiles(qi, kj, nq)
    else:
        grid = (b, nq, nk)
        tiles = lambda qi, kj: (qi, jnp.minimum(kj, (p0 + (qi + 1) * tq - 1) // tk))
    return pl.pallas_call(
        functools.partial(_fox_attn_kernel, p0=p0, tq=tq, tk=tk, nq=nq, nk=nk, paired=paired),
        out_shape=jax.ShapeDtypeStruct((b, sq, D_FOX), BF16),
        grid=grid,
        in_specs=[
            pl.BlockSpec((1, H_FOX, tq, 2 * HD_FOX), lambda bi, qi, kj: (bi, 0, tiles(qi, kj)[0], 0)),
            pl.BlockSpec((1, H_FOX, tk, 2 * HD_FOX), lambda bi, qi, kj: (bi, 0, tiles(qi, kj)[1], 0)),
            pl.BlockSpec((1, H_FOX, HD_FOX, tk), lambda bi, qi, kj: (bi, 0, 0, tiles(qi, kj)[1])),
        ],
        out_specs=pl.BlockSpec((1, tq, D_FOX), lambda bi, qi, kj: (bi, tiles(qi, kj)[0], 0)),
        scratch_shapes=[
            pltpu.VMEM((H_FOX, 1, tq), F32),
            pltpu.VMEM((H_FOX, 1, tq), F32),
            pltpu.VMEM((H_FOX, HD_FOX, tq), F32),
        ],
        compiler_params=_cparams(("parallel", "parallel", "arbitrary")),
        name="fox_attn",
    )(qa, ka, vt)


def _merge_kernel(x_ref, hm_ref, of_ref, cq_ref, gates_ref, mk_ref, mv_ref, wm_ref, wf_ref, wc_ref,
                  wo_ref, gffn_ref, xnew_ref, hn_ref, hnlo_ref):
    cq = cq_ref[0]
    heads = []
    for h in range(H_MEM):
        sl = slice(h * HD_MEM, (h + 1) * HD_MEM)
        s = _nt_dot(cq[:, sl], mk_ref[0, :, sl]) * (HD_MEM ** -0.5)
        p = jnp.exp(s - jnp.max(s, axis=1, keepdims=True))
        o = jnp.dot(p.astype(BF16), mv_ref[0, :, sl], preferred_element_type=F32)
        heads.append(o / jnp.sum(p, axis=1, keepdims=True))
    oc = jnp.concatenate(heads, axis=1).astype(BF16)
    a_m = jnp.dot(hm_ref[0], wm_ref[...], preferred_element_type=F32)
    a_f = jnp.dot(of_ref[0], wf_ref[...], preferred_element_type=F32)
    a_c = jnp.dot(oc, wc_ref[...], preferred_element_type=F32)
    g = gates_ref[0].astype(F32)
    merged = (_sigmoid(g[:, 0:D_MODEL]) * a_m + _sigmoid(g[:, D_MODEL:2 * D_MODEL]) * a_f
              + _sigmoid(g[:, 2 * D_MODEL:3 * D_MODEL]) * a_c)
    xn = x_ref[0] + jnp.dot(merged.astype(BF16), wo_ref[...], preferred_element_type=F32)
    xnew_ref[0] = xn
    hn = _rms(xn, gffn_ref[...])
    hn_hi = hn.astype(BF16)
    hn_ref[0] = hn_hi
    hnlo_ref[0] = (hn - hn_hi.astype(F32)).astype(BF16)


def _merge(x, hm, of, cq, gates, mk, mv, wm, wf, wc, wo, gffn):
    b, s, _ = x.shape
    ts = min(s, ROW_TILE)
    tile = lambda w: pl.BlockSpec((1, ts, w), lambda bi, si: (bi, si, 0))
    const = lambda shape: pl.BlockSpec(shape, lambda bi, si: (0, 0))
    mem = pl.BlockSpec((1, mk.shape[1], D_MEM), lambda bi, si: (bi, 0, 0))
    return pl.pallas_call(
        _merge_kernel,
        out_shape=(jax.ShapeDtypeStruct((b, s, D_MODEL), F32), jax.ShapeDtypeStruct((b, s, D_MODEL), BF16),
                   jax.ShapeDtypeStruct((b, s, D_MODEL), BF16)),
        grid=(b, s // ts),
        in_specs=[tile(D_MODEL), tile(D_M), tile(D_FOX), tile(D_MEM), tile(N_BRANCH * D_MODEL), mem, mem,
                  const((D_M, D_MODEL)), const((D_FOX, D_MODEL)), const((D_MEM, D_MODEL)),
                  const((D_MODEL, D_MODEL)), const((1, D_MODEL))],
        out_specs=(tile(D_MODEL), tile(D_MODEL), tile(D_MODEL)),
        compiler_params=_cparams(("parallel", "parallel")),
        name="merge",
    )(x, hm, of, cq, gates, mk, mv, wm, wf, wc, wo, gffn)


def _extract_top16(s, key_io, val_scr, idx_scr, h, want_rank):
    rank = jnp.full(s.shape, float(PEER_TOPK), F32) if want_rank else None
    for r in range(PEER_TOPK):
        m = jnp.max(s, axis=0, keepdims=True)
        first = jnp.min(jnp.where(s == m, key_io, float(N_KEYS)), axis=0, keepdims=True)
        hit = key_io == first
        s = jnp.where(hit, -jnp.inf, s)
        val_scr[r, pl.ds(h, 1), :] = m
        if want_rank:
            rank = jnp.where(hit, float(r), rank)
        else:
            idx_scr[r, pl.ds(h, 1), :] = first
    return rank


def _dot3(a_hi, a_lo, b_hi, b_lo, dot):
    return dot(a_hi, b_hi) + dot(a_lo, b_hi) + dot(a_hi, b_lo)


def _router_kernel(hn_ref, hnlo_ref, wqt_ref, keys_ref, a0_ref, cnt_ref, r1_ref, b1_ref,
                   qr_scr, e1_scr, top0_scr, top1_scr, idx0_scr, cnt_scr, invz_scr):
    tp = hn_ref.shape[0]
    qr_scr[...] = _dot3(wqt_ref[0], wqt_ref[1], hn_ref[...], hnlo_ref[...], _nt_dot)
    key_io = lax.broadcasted_iota(jnp.int32, (N_KEYS, tp), 0).astype(F32)

    heads_per_trip = 8

    def scores_and_top16_group(hg, carry):
        for k in range(heads_per_trip):
            scores_and_top16(heads_per_trip * hg + k)
        return carry

    def scores_and_top16(h):
        base = pl.multiple_of(h * 2 * N_KEYS, 2 * N_KEYS)
        def scores(c):
            q = qr_scr[pl.ds(base + c * N_KEYS, N_KEYS), :]
            q_hi = q.astype(BF16)
            q_lo = (q - q_hi.astype(F32)).astype(BF16)
            return _dot3(keys_ref[0, 2 * h + c], keys_ref[1, 2 * h + c], q_hi, q_lo,
                         lambda a, b_: jnp.dot(a, b_, preferred_element_type=F32))

        s0, s1 = scores(0), scores(1)
        _extract_top16(s0, key_io, top0_scr, idx0_scr, h, False)
        rank1 = _extract_top16(s1, key_io, top1_scr, None, h, True)
        a0_ref[h] = jnp.exp(s0 - top0_scr[0, pl.ds(h, 1), :])
        e1_scr[h] = jnp.exp(s1 - top1_scr[0, pl.ds(h, 1), :])
        r1_ref[h] = rank1.astype(BF16)

    lax.fori_loop(0, PEER_HEADS // heads_per_trip, scores_and_top16_group, 0)

    top0 = [top0_scr[a] for a in range(PEER_TOPK)]
    top1 = [top1_scr[b] for b in range(PEER_TOPK)]
    cnt = [jnp.zeros((PEER_HEADS, tp), F32) for _ in range(PEER_TOPK)]
    for _ in range(PEER_TOPK):
        front = []
        for a in range(PEER_TOPK):
            nxt = jnp.full((PEER_HEADS, tp), -jnp.inf, F32)
            for bb in range(PEER_TOPK // (a + 1)):
                nxt = jnp.where(cnt[a] == float(bb), top1[bb], nxt)
            front.append(top0[a] + nxt)
        mx = functools.reduce(jnp.maximum, front)
        first = functools.reduce(jnp.minimum,
                                 [jnp.where(front[a] == mx, float(a), float(PEER_TOPK)) for a in range(PEER_TOPK)])
        cnt = [cnt[a] + jnp.where(first == float(a), 1.0, 0.0) for a in range(PEER_TOPK)]
    z = jnp.zeros((PEER_HEADS, tp), F32)
    for a in range(PEER_TOPK):
        za = jnp.zeros((PEER_HEADS, tp), F32)
        for bb in range(PEER_TOPK // (a + 1)):
            za = za + jnp.where(cnt[a] > float(bb), jnp.exp(top1[bb] - top1[0]), 0.0)
        z = z + jnp.exp(top0[a] - top0[0]) * za
        cnt_scr[a] = cnt[a]
    invz_scr[...] = 1.0 / z

    def counts_and_gates(h, carry):
        cnt_i = jnp.zeros((N_KEYS, tp), F32)
        for a in range(PEER_TOPK):
            cnt_i = jnp.where(key_io == idx0_scr[a, pl.ds(h, 1), :], cnt_scr[a, pl.ds(h, 1), :], cnt_i)
        cnt_ref[h] = cnt_i
        b1 = e1_scr[h] * invz_scr[pl.ds(h, 1), :]
        b1_ref[h] = b1.astype(BF16)
        return carry

    lax.fori_loop(0, PEER_HEADS, counts_and_gates, 0)


def _router(hn2d, hnlo2d, wqt, keys):
    t = hn2d.shape[0]
    tp = ROUTER_TOKENS
    shape = jax.ShapeDtypeStruct((PEER_HEADS, N_KEYS, t), F32)
    spec = pl.BlockSpec((PEER_HEADS, N_KEYS, tp), lambda i: (0, 0, i))
    slab_shape = jax.ShapeDtypeStruct((PEER_HEADS, N_KEYS, t), BF16)
    slab_spec = spec
    return pl.pallas_call(
        _router_kernel,
        out_shape=(shape, shape, slab_shape, slab_shape),
        grid=(t // tp,),
        in_specs=[
            pl.BlockSpec((tp, D_MODEL), lambda i: (i, 0)),
            pl.BlockSpec((tp, D_MODEL), lambda i: (i, 0)),
            pl.BlockSpec(wqt.shape, lambda i: (0, 0, 0)),
            pl.BlockSpec(keys.shape, lambda i: (0, 0, 0, 0)),
        ],
        out_specs=(spec, spec, slab_spec, slab_spec),
        scratch_shapes=[
            pltpu.VMEM((wqt.shape[1], tp), F32),
            pltpu.VMEM((PEER_HEADS, N_KEYS, tp), F32),
        ] + [pltpu.VMEM((PEER_TOPK, PEER_HEADS, tp), F32) for _ in range(4)] + [pltpu.VMEM((PEER_HEADS, tp), F32)],
        compiler_params=_cparams(("parallel",)),
        name="peer_router",
    )(hn2d, hnlo2d, wqt, keys)


def _gelu_tanh(x):
    k = -2.0 * 0.7978845608028654 * 1.4426950408889634
    return x / (1.0 + jnp.exp2(x * (k + (k * 0.044715) * (x * x))))


def _experts_kernel(hn_ref, u_ref, vt_ref, a0_ref, cnt_ref, r1_ref, b1_ref, x_ref, gfin_ref, out_ref,
                    acc_scr, pre0_scr, pre1_scr, w0_scr, w1_scr, *, rows_per_tile, n_tiles, final_norm):
    s = pl.program_id(1)
    tp = hn_ref.shape[0]
    te = u_ref.shape[0] // 2
    slab = 16
    slabs_per_row = N_KEYS // slab
    d_half = acc_scr.shape[0] // 2
    zero = jnp.zeros((), BF16)

    @pl.when(s == 0)
    def _():
        acc_scr[...] = jnp.zeros(acc_scr.shape, F32)
        pre1_scr[...] = jnp.zeros(pre1_scr.shape, F32)
        w0_scr[...] = jnp.zeros(w0_scr.shape, BF16)

    def pipeline_step(parity, pre_w, pre_r, w_w, w_r):
        e = 2 * s + parity
        cols = slice(parity * te, (parity + 1) * te)

        def stage_c(half):
            rows = slice(half * d_half, (half + 1) * d_half)
            acc_scr[rows, :] += jnp.dot(vt_ref[rows, cols], w_r[...], preferred_element_type=F32)

        def stage_a(half):
            rows = slice(half * (te // 2), (half + 1) * (te // 2))
            pre_w[rows, :] = _nt_dot(u_ref[parity * te + half * (te // 2):parity * te + (half + 1) * (te // 2), :],
                                     hn_ref[...])

        mxu_work = (lambda: stage_c(0), lambda: stage_a(0), lambda: stage_c(1), lambda: stage_a(1))

        valid = jnp.logical_and(e >= 1, e <= n_tiles)
        tile_b = jnp.clip(e - 1, 0, n_tiles - 1)
        for ib in range(rows_per_tile):
            for piece in mxu_work[ib * len(mxu_work) // rows_per_tile:(ib + 1) * len(mxu_work) // rows_per_tile]:
                piece()
            i = tile_b * rows_per_tile + ib
            for lanes in (slice(0, tp // 2), slice(tp // 2, tp)):
                gates = [None] * slabs_per_row
                for h in range(PEER_HEADS):
                    a_b = jnp.broadcast_to(a0_ref[h, pl.ds(i, 1), lanes], (slab, tp // 2)).astype(BF16)
                    c_b = jnp.broadcast_to(cnt_ref[h, pl.ds(i, 1), lanes], (slab, tp // 2)).astype(BF16)
                    for g in range(slabs_per_row):
                        key_rows = slice(g * slab, (g + 1) * slab)
                        term = a_b * jnp.where(r1_ref[h, key_rows, lanes] < c_b, b1_ref[h, key_rows, lanes], zero)
                        gates[g] = term if h == 0 else gates[g] + term
                for g in range(slabs_per_row):
                    rows = slice(ib * N_KEYS + g * slab, ib * N_KEYS + (g + 1) * slab)
                    w = gates[g] * _gelu_tanh(pre_r[rows, lanes]).astype(BF16)
                    w_w[rows, lanes] = jnp.where(valid, w, zero)

    pipeline_step(0, pre0_scr, pre1_scr, w1_scr, w0_scr)
    pipeline_step(1, pre1_scr, pre0_scr, w0_scr, w1_scr)

    @pl.when(s == n_tiles // 2)
    def _():
        xo = x_ref[...] + acc_scr[...].T
        if final_norm:
            xo = _rms(xo, gfin_ref[...])
        out_ref[...] = xo


def _experts(hn2d, u_bf, vt_bf, layer, a0, cnt, r1, b1, x2d, gfin, final_norm):
    t = hn2d.shape[0]
    tp = ROW_TILE
    rows_per_tile = EXPERT_KEY_ROWS
    te = rows_per_tile * N_KEYS
    n_tiles = u_bf.shape[1] // te
    rspec = pl.BlockSpec((PEER_HEADS, N_KEYS, tp), lambda ti, e: (0, 0, ti))
    sspec = rspec
    return pl.pallas_call(
        functools.partial(_experts_kernel, rows_per_tile=rows_per_tile, n_tiles=n_tiles, final_norm=final_norm),
        out_shape=jax.ShapeDtypeStruct((t, D_MODEL), F32),
        grid=(t // tp, n_tiles // 2 + 1),
        in_specs=[
            pl.BlockSpec((tp, D_MODEL), lambda ti, s: (ti, 0)),
            pl.BlockSpec((None, 2 * te, D_MODEL), lambda ti, s: (layer, jnp.minimum(s, n_tiles // 2 - 1), 0)),
            pl.BlockSpec((None, D_MODEL, 2 * te), lambda ti, s: (layer, 0, jnp.maximum(s - 1, 0))),
            rspec, rspec, sspec, sspec,
            pl.BlockSpec((tp, D_MODEL), lambda ti, s: (ti, 0)),
            pl.BlockSpec((1, D_MODEL), lambda ti, s: (0, 0)),
        ],
        out_specs=pl.BlockSpec((tp, D_MODEL), lambda ti, s: (ti, 0)),
        scratch_shapes=[
            pltpu.VMEM((D_MODEL, tp), F32),
            pltpu.VMEM((te, tp), F32), pltpu.VMEM((te, tp), F32),
            pltpu.VMEM((te, tp), BF16), pltpu.VMEM((te, tp), BF16),
        ],
        compiler_params=_cparams(("parallel", "arbitrary")),
        name="peer_experts",
    )(hn2d, u_bf, vt_bf, a0, cnt, r1, b1, x2d, gfin)


def _hi_lo(a):
    hi = a.astype(BF16)
    return jnp.stack([hi, (a - hi.astype(F32)).astype(BF16)])


def _prep_layer_weights(p):
    w_in = p['w_in']
    o_mi = 4 * D_M
    o_fq = o_mi + 2 * H_M
    o_ff = o_fq + 3 * D_FOX
    o_cq = o_ff + H_FOX
    w_big = jnp.concatenate([w_in[:, :o_mi], w_in[:, o_fq:o_ff], w_in[:, o_cq:]], axis=1).astype(BF16)
    w_big = w_big.reshape(D_MODEL, -1, PROJ_TILE).transpose(1, 0, 2)
    w_small = jnp.concatenate([w_in[:, o_mi:o_fq], w_in[:, o_ff:o_cq]], axis=1)
    w_small = jnp.pad(w_small, ((0, 0), (0, LANES - w_small.shape[1])))
    w_small_hi = w_small.astype(BF16)
    w_small = jnp.stack([w_small_hi, (w_small - w_small_hi.astype(F32)).astype(BF16)])
    bias = jnp.concatenate([p['b_m_i'], p['b_m_f'], p['b_fox_f']]).astype(F32)
    return dict(
        norm_mix=p['norm_mix'].reshape(1, D_MODEL), w_big=w_big, w_small=w_small,
        conv_w=p['conv_w'], conv_b=p['conv_b'].reshape(1, 2 * D_M),
        bias_row=jnp.pad(bias, (0, LANES - 16)).reshape(1, LANES), bias_col=bias.reshape(16, 1),
        norm_m_head=p['norm_m_head'].reshape(1, D_M),
        w_up_m=p['w_up_m'].astype(BF16), w_up_f=p['w_up_f'].astype(BF16), w_up_c=p['w_up_c'].astype(BF16),
        w_out=p['w_out'].astype(BF16), norm_ffn=p['norm_ffn'].reshape(1, D_MODEL),
        wqt=_hi_lo(p['peer_wq'].T),
        keys=_hi_lo(p['peer_keys'].reshape(2 * PEER_HEADS, N_KEYS, N_KEYS)),
    )


def _layer(x, mem_k, mem_v, conv_prev, c0, n0, m0, fk_past, fv_past, flf_past, w, peer_tables, layer, gfin, final_norm):
    b, s, _ = x.shape
    t = b * s
    p0 = 0 if fk_past is None else fk_past.shape[1]
    qk, v_m, o_m, fq, fk, fv, cq, gates, small = _norm_proj(x.reshape(t, D_MODEL), w['norm_mix'], w['w_big'], w['w_small'])

    small3 = small.reshape(b, s, LANES)
    smallt = small3[:, :, :16].transpose(0, 2, 1)
    m0b = jnp.broadcast_to(m0[:, :, None, None], (b, H_M, 1, LANES))
    hm, c1, n1, m1b, conv_new = _mlstm(
        qk.reshape(b, s, 2 * D_M), v_m.reshape(b, s, D_M), o_m.reshape(b, s, D_M), small3, smallt,
        w['conv_w'], w['conv_b'], w['bias_row'], w['bias_col'], w['norm_m_head'],
        c0, n0[:, :, None, :], m0b, conv_prev)
    n1, m1 = n1[:, :, 0, :], m1b[:, :, 0, 0]

    k_f = fk.reshape(b, s, H_FOX, HD_FOX)
    v_f = fv.reshape(b, s, H_FOX, HD_FOX)
    s_pad = -(-s // LANES) * LANES
    pad_rows = lambda a: jnp.pad(a, ((0, 0), (0, s_pad - s), (0, 0)))
    past = flf_past.astype(F32) if p0 else None
    lf_all, hi, mid, lo = _fox_prep(past, pad_rows(small3), w['bias_row'], s)
    lf_f = lf_all[:, :s, FOX_GATE_LANE:FOX_GATE_LANE + H_FOX]
    sk = p0 + s_pad
    k_all, v_all = pad_rows(fk.reshape(b, s, D_FOX)), pad_rows(fv.reshape(b, s, D_FOX))
    if p0:
        k_all = jnp.concatenate([fk_past.reshape(b, p0, D_FOX).astype(F32), k_all], axis=1)
        v_all = jnp.concatenate([fv_past.reshape(b, p0, D_FOX).astype(F32), v_all], axis=1)
    qa = _fox_pack(pad_rows(fq.reshape(b, s, D_FOX)), hi, mid, lo, p0, True)[0]
    ka, vt_all = _fox_pack(k_all, hi, mid, lo, 0, False, v=v_all)
    o_f = _fox_attn(qa, ka, vt_all.reshape(b, H_FOX, HD_FOX, sk), p0)[:, :s]

    x_new, hn, hn_lo = _merge(x, hm, o_f, cq.reshape(b, s, D_MEM), gates.reshape(b, s, N_BRANCH * D_MODEL),
                       mem_k.reshape(b, -1, D_MEM).astype(BF16), mem_v.reshape(b, -1, D_MEM).astype(BF16),
                       w['w_up_m'], w['w_up_f'], w['w_up_c'], w['w_out'], w['norm_ffn'])

    hn2d = hn.reshape(t, D_MODEL)
    a0, cnt, r1, b1 = _router(hn2d, hn_lo.reshape(t, D_MODEL), w['wqt'], w['keys'])
    x_out = _experts(hn2d, peer_tables[0], peer_tables[1], layer, a0, cnt, r1, b1, x_new.reshape(t, D_MODEL),
                     gfin, final_norm)
    return x_out.reshape(b, s, D_MODEL), conv_new, c1, n1, m1, k_f, v_f, lf_f


def kernel(x_prompt, x_sample, mem_prompt, cache_fox_k, cache_fox_v, cache_fox_lf, state_mlstm_c, state_mlstm_n, state_mlstm_m, state_conv, cache_mem_k, cache_mem_v, norm_mix, w_in, conv_w, conv_b, b_m_i, b_m_f, norm_m_head, b_fox_f, norm_mem, w_mem_kv, w_up_m, w_up_f, w_up_c, w_out, norm_ffn, peer_wq, peer_keys, peer_u, peer_v, norm_final):
    depth = w_in.shape[0]
    names = dict(norm_mix=norm_mix, w_in=w_in, conv_w=conv_w, conv_b=conv_b, b_m_i=b_m_i, b_m_f=b_m_f,
                 norm_m_head=norm_m_head, b_fox_f=b_fox_f, w_up_m=w_up_m, w_up_f=w_up_f, w_up_c=w_up_c,
                 w_out=w_out, norm_ffn=norm_ffn, peer_wq=peer_wq, peer_keys=peer_keys, peer_u=peer_u,
                 peer_v=peer_v)
    weights = [_prep_layer_weights({k: a[l] for k, a in names.items()}) for l in range(depth)]
    peer_tables = (peer_u.astype(BF16), jnp.swapaxes(peer_v, 1, 2).astype(BF16))
    gfin = norm_final.reshape(1, D_MODEL)
    bp, n_mem = mem_prompt.shape[0], mem_prompt.shape[1]

    xp = x_prompt
    new_p = [[] for _ in range(9)]
    for l in range(depth):
        mk, mv = _norm_matmul(mem_prompt.reshape(bp * n_mem, D_MODEL), norm_mem[l].reshape(1, D_MODEL),
                              w_mem_kv[l].astype(BF16))
        mk = mk.reshape(bp, n_mem, H_MEM, HD_MEM)
        mv = mv.reshape(bp, n_mem, H_MEM, HD_MEM)
        xp, conv1, c1, n1, m1, kf, vf, lff = _layer(
            xp, mk, mv, jnp.zeros((bp, CONV_W - 1, 2 * D_M), F32),
            jnp.zeros((bp, H_M, HD_M, HD_M), F32), jnp.zeros((bp, H_M, HD_M), F32), jnp.zeros((bp, H_M), F32),
            None, None, None, weights[l], peer_tables, l, gfin, l == depth - 1)
        for lst, a in zip(new_p, (kf, vf, lff, c1, n1, m1, conv1, mk, mv)):
            lst.append(a)
    outs_p = [jnp.stack(a) for a in new_p]

    xs = x_sample
    new_s = [[] for _ in range(7)]
    for l in range(depth):
        xs, conv1, c1, n1, m1, kf, vf, lff = _layer(
            xs, cache_mem_k[l], cache_mem_v[l], state_conv[l], state_mlstm_c[l], state_mlstm_n[l],
            state_mlstm_m[l], cache_fox_k[l], cache_fox_v[l], cache_fox_lf[l], weights[l], peer_tables, l, gfin,
            l == depth - 1)
        for lst, a in zip(new_s, (kf, vf, lff, c1, n1, m1, conv1)):
            lst.append(a)
    outs_s = [jnp.stack(a) for a in new_s]

    return (xp, xs, *outs_p, *outs_s)
```

```python
import functools

import jax
import jax.numpy as jnp
from jax import lax
from jax.experimental import pallas as pl
from jax.experimental.pallas import tpu as pltpu

F32 = jnp.float32
BF16 = jnp.bfloat16
EPS = 1e-6
NEG_BIG = -1e30

D_MODEL = 1024
H_M, HD_M = 4, 128
D_M = H_M * HD_M
CONV_W = 4
H_FOX, HD_FOX = 8, 64
D_FOX = H_FOX * HD_FOX
H_MEM, HD_MEM = 4, 128
D_MEM = H_MEM * HD_MEM
N_BRANCH = 3
PEER_HEADS = 8
N_KEYS = 128
PEER_TOPK = 16
LANES = 128
FOX_GATE_LANE = 2 * H_M
MLSTM_CHUNK = 256
PROJ_TILE = 512
ROW_TILE = 512
KEY_TILE_SHORT = 384
ROUTER_TOKENS = 4 * LANES
EXPERT_KEY_ROWS = 4
VMEM_LIMIT = 56 * 1024 * 1024


def _cparams(sem):
    return pltpu.CompilerParams(dimension_semantics=sem, vmem_limit_bytes=VMEM_LIMIT)


def _nt_dot(a, b):
    return lax.dot_general(a, b, (((1,), (1,)), ((), ())), preferred_element_type=F32)


def _rms(x, g):
    return x * lax.rsqrt(jnp.mean(x * x, axis=-1, keepdims=True) + EPS) * g


def _log_sigmoid(x):
    return jnp.minimum(x, 0.0) - jnp.log(1.0 + jnp.exp(-jnp.abs(x)))


def _sigmoid(x):
    return 1.0 / (1.0 + jnp.exp(-x))


def _norm_proj_kernel(x_ref, g_ref, w_ref, ws_ref, qk_ref, v_ref, o_ref, fq_ref, fk_ref, fv_ref,
                      cq_ref, gates_ref, small_ref):
    hf = _rms(x_ref[...], g_ref[...])
    h = hf.astype(BF16)
    h_lo = (hf - h.astype(F32)).astype(BF16)
    small_ref[...] = (jnp.dot(h, ws_ref[0], preferred_element_type=F32)
                      + jnp.dot(h_lo, ws_ref[0], preferred_element_type=F32)
                      + jnp.dot(h, ws_ref[1], preferred_element_type=F32))

    def tile(j):
        return jnp.dot(h, w_ref[j], preferred_element_type=F32)

    qk_ref[:, 0:PROJ_TILE] = tile(0)
    qk_ref[:, PROJ_TILE:2 * PROJ_TILE] = tile(1)
    v_ref[...] = tile(2).astype(BF16)
    o_ref[...] = tile(3).astype(BF16)
    fq_ref[...] = tile(4).astype(BF16)
    fk_ref[...] = tile(5)
    fv_ref[...] = tile(6)
    cq_ref[...] = tile(7).astype(BF16)
    for j in range(8, w_ref.shape[0]):
        gates_ref[:, (j - 8) * PROJ_TILE:(j - 7) * PROJ_TILE] = tile(j).astype(BF16)


def _norm_proj(x2d, g, w_big, w_small):
    t = x2d.shape[0]
    tm = ROW_TILE
    row = lambda i: (i, 0)
    out_widths = ((2 * D_M, F32),
                  (D_M, BF16),
                  (D_M, BF16),
                  (D_FOX, BF16),
                  (D_FOX, F32),
                  (D_FOX, F32),
                  (D_MEM, BF16),
                  (N_BRANCH * D_MODEL, BF16),
                  (LANES, F32))
    resident = lambda shape: pl.BlockSpec(shape, lambda i: (0,) * len(shape), pipeline_mode=pl.Buffered(1))
    return pl.pallas_call(
        _norm_proj_kernel,
        out_shape=tuple(jax.ShapeDtypeStruct((t, wd), dt) for wd, dt in out_widths),
        grid=(t // tm,),
        in_specs=[
            pl.BlockSpec((tm, D_MODEL), row),
            resident((1, D_MODEL)),
            resident(w_big.shape),
            resident(w_small.shape),
        ],
        out_specs=tuple(pl.BlockSpec((tm, wd), row) for wd, _ in out_widths),
        compiler_params=_cparams(("parallel",)),
        name="norm_proj",
    )(x2d, g, w_big, w_small)


def _norm_matmul_kernel(x_ref, g_ref, w_ref, k_ref, v_ref):
    h = _rms(x_ref[...], g_ref[...]).astype(BF16)
    k_ref[...] = jnp.dot(h, w_ref[:, 0:D_MEM], preferred_element_type=F32)
    v_ref[...] = jnp.dot(h, w_ref[:, D_MEM:2 * D_MEM], preferred_element_type=F32)


def _norm_matmul(x2d, g, w):
    t = x2d.shape[0]
    tm = ROW_TILE
    out = jax.ShapeDtypeStruct((t, D_MEM), F32)
    return pl.pallas_call(
        _norm_matmul_kernel,
        out_shape=(out, out),
        grid=(t // tm,),
        in_specs=[
            pl.BlockSpec((tm, D_MODEL), lambda i: (i, 0)),
            pl.BlockSpec((1, D_MODEL), lambda i: (0, 0)),
            pl.BlockSpec((D_MODEL, 2 * D_MEM), lambda i: (0, 0)),
        ],
        out_specs=(pl.BlockSpec((tm, D_MEM), lambda i: (i, 0)), pl.BlockSpec((tm, D_MEM), lambda i: (i, 0))),
        compiler_params=_cparams(("parallel",)),
        name="norm_matmul",
    )(x2d, g, w)


def _mlstm_kernel(qk_ref, v_ref, o_ref, small_ref, smallt_ref, convw_ref, convb_ref, brow_ref,
                  bcol_ref, ghead_ref, c0_ref, n0_ref, m0_ref, conv0_ref,
                  hm_ref, c_out_ref, n_out_ref, m_out_ref, conv_out_ref,
                  c_scr, n_scr, m_scr, xp_scr, *, chunk):
    L = chunk
    si = pl.program_id(1)

    @pl.when(si == 0)
    def _():
        c_scr[...] = c0_ref[0]
        n_scr[...] = n0_ref[0]
        m_scr[...] = m0_ref[0]
        xp_scr[5:8, :] = conv0_ref[0]

    xp_scr[8:8 + L, :] = qk_ref[0]
    y = convb_ref[...] + convw_ref[0:1, :] * xp_scr[5:5 + L, :]
    for j in range(1, CONV_W):
        y = y + convw_ref[j:j + 1, :] * xp_scr[5 + j:5 + j + L, :]
    y = y * _sigmoid(y)
    tail = xp_scr[5 + L:8 + L, :]
    xp_scr[5:8, :] = tail
    conv_out_ref[0] = tail

    t_io = lax.broadcasted_iota(jnp.int32, (L, L), 0)
    s_io = lax.broadcasted_iota(jnp.int32, (L, L), 1)
    causal = s_io <= t_io
    ones_col = (lax.broadcasted_iota(jnp.int32, (L, HD_M), 1) == 0).astype(BF16)

    gate_c = small_ref[0] + brow_ref[...]
    gate_r = smallt_ref[0] + bcol_ref[...]
    lsig_c = _log_sigmoid(gate_c)
    lsig_r = _log_sigmoid(gate_r)
    tri_c = causal.astype(BF16)
    tri_r = (t_io <= s_io).astype(BF16)
    bcum_c = sum(jnp.dot(tri_c, p.astype(BF16), preferred_element_type=F32) for p in _split3(lsig_c))
    bcum_r = sum(jnp.dot(p.astype(BF16), tri_r, preferred_element_type=F32) for p in _split3(lsig_r))
    outs = []
    for h in range(H_M):
        q = y[:, h * HD_M:(h + 1) * HD_M]
        k = y[:, D_M + h * HD_M:D_M + (h + 1) * HD_M] * (HD_M ** -0.5)
        v_aug = jnp.concatenate([v_ref[0, :, h * HD_M:(h + 1) * HD_M], ones_col], axis=1)
        ig_col = gate_c[:, h:h + 1]
        ig_row = gate_r[h:h + 1, :]
        lf_col = lsig_c[:, H_M + h:H_M + h + 1]
        lf_row = lsig_r[H_M + h:H_M + h + 1, :]
        bcum_col = bcum_c[:, H_M + h:H_M + h + 1]
        bcum_row = bcum_r[H_M + h:H_M + h + 1, :]
        m_prev = m_scr[h][:, 0:1]
        dmat = jnp.where(causal, bcum_col - bcum_row + ig_row, -jnp.inf)
        inter = bcum_col + m_prev
        m_t = jnp.maximum(inter, jnp.max(dmat, axis=1, keepdims=True))
        w_intra = jnp.exp(dmat - m_t)
        w_state = jnp.exp(inter - m_t)
        qb = q.astype(BF16)
        sw = _nt_dot(qb, k.astype(BF16)) * w_intra
        c_prev = c_scr[h]
        n_prev = n_scr[h]
        intra = jnp.dot(sw.astype(BF16), v_aug, preferred_element_type=F32)
        num = intra[:, 0:HD_M] + w_state * jnp.dot(qb, c_prev.astype(BF16), preferred_element_type=F32)
        den = intra[:, HD_M:HD_M + 1] + w_state * jnp.sum(q * n_prev, axis=1, keepdims=True)
        hh = num / jnp.maximum(jnp.abs(den), jnp.exp(-m_t))
        hh = _rms(hh, ghead_ref[:, h * HD_M:(h + 1) * HD_M])
        outs.append(hh * _sigmoid(o_ref[0, :, h * HD_M:(h + 1) * HD_M].astype(F32)))
        bl = bcum_row[:, L - 1:L]
        m_new = jnp.maximum(bl + m_prev, jnp.max(bl - bcum_row + ig_row, axis=1, keepdims=True))
        decay = jnp.exp(bl + m_prev - m_new)
        ws_col = jnp.exp(bl - bcum_col + ig_col - m_new)
        kw = k * ws_col
        c_scr[h] = decay * c_prev + jnp.dot(kw.T.astype(BF16), v_ref[0, :, h * HD_M:(h + 1) * HD_M],
                                            preferred_element_type=F32)
        n_scr[h] = decay * n_prev + jnp.sum(kw, axis=0, keepdims=True)
        m_scr[h] = jnp.broadcast_to(m_new, (1, LANES))
    hm_ref[0] = jnp.concatenate(outs, axis=1).astype(BF16)
    c_out_ref[0] = c_scr[...]
    n_out_ref[0] = n_scr[...]
    m_out_ref[0] = m_scr[...]


def _mlstm(qk, v, o, small, smallt, convw, convb, brow, bcol, ghead, c0, n0, m0b, conv0):
    b, s, _ = qk.shape
    chunk = min(s, MLSTM_CHUNK)
    tile = lambda w: pl.BlockSpec((1, chunk, w), lambda bi, si: (bi, si, 0))
    per_b = lambda shape: pl.BlockSpec((1,) + shape, lambda bi, si: (bi,) + (0,) * len(shape))
    const = lambda shape: pl.BlockSpec(shape, lambda bi, si: (0, 0))
    state_shapes = ((H_M, HD_M, HD_M), (H_M, 1, HD_M), (H_M, 1, LANES), (CONV_W - 1, 2 * D_M))
    return pl.pallas_call(
        functools.partial(_mlstm_kernel, chunk=chunk),
        out_shape=(jax.ShapeDtypeStruct((b, s, D_M), BF16),)
        + tuple(jax.ShapeDtypeStruct((b,) + sh, F32) for sh in state_shapes),
        grid=(b, s // chunk),
        in_specs=[
            tile(2 * D_M), tile(D_M), tile(D_M), tile(LANES),
            pl.BlockSpec((1, 16, chunk), lambda bi, si: (bi, 0, si)),
            const((CONV_W, 2 * D_M)), const((1, 2 * D_M)), const((1, LANES)), const((16, 1)), const((1, D_M)),
        ] + [per_b(sh) for sh in state_shapes],
        out_specs=(tile(D_M),) + tuple(per_b(sh) for sh in state_shapes),
        scratch_shapes=[
            pltpu.VMEM((H_M, HD_M, HD_M), F32),
            pltpu.VMEM((H_M, 1, HD_M), F32),
            pltpu.VMEM((H_M, 1, LANES), F32),
            pltpu.VMEM((chunk + 8, 2 * D_M), F32),
        ],
        compiler_params=_cparams(("parallel", "arbitrary")),
        name="mlstm",
    )(qk, v, o, small, smallt, convw, convb, brow, bcol, ghead, c0, n0, m0b, conv0)


def _split3(x):
    hi = x.astype(BF16).astype(F32)
    r = x - hi
    mid = r.astype(BF16).astype(F32)
    lo = (r - mid).astype(BF16).astype(F32)
    return hi, mid, lo


def _fox_prep_kernel(*refs, n_past, n_new, s_valid, blk):
    if n_past:
        past_ref, pre_ref, bias_ref, lf_ref, hi_ref, mid_ref, lo_ref = refs
    else:
        pre_ref, bias_ref, lf_ref, hi_ref, mid_ref, lo_ref = refs
    tri = (lax.broadcasted_iota(jnp.int32, (blk, blk), 1) <= lax.broadcasted_iota(jnp.int32, (blk, blk), 0)).astype(BF16)
    carry = jnp.zeros((1, LANES), F32)
    start = 0
    while start < n_past + n_new:
        if start < n_past:
            r = min(blk, n_past - start)
            lf = jnp.concatenate([jnp.zeros((r, FOX_GATE_LANE), F32), past_ref[0, start:start + r, :],
                                  jnp.zeros((r, LANES - FOX_GATE_LANE - H_FOX), F32)], axis=1)
        else:
            r = min(blk, n_past + n_new - start)
            ns = start - n_past
            row = ns + lax.broadcasted_iota(jnp.int32, (r, LANES), 0)
            lf = jnp.where(row < s_valid, _log_sigmoid(pre_ref[0, ns:ns + r, :] + bias_ref[...]), 0.0)
            lf_ref[0, ns:ns + r, :] = lf
        cum = carry
        for piece in _split3(lf):
            cum = cum + jnp.dot(tri[0:r, 0:r], piece.astype(BF16), preferred_element_type=F32)
        carry = cum[r - 1:r, :]
        hi, mid, lo = _split3(cum)
        hi_ref[0, start:start + r, :] = hi
        mid_ref[0, start:start + r, :] = mid
        lo_ref[0, start:start + r, :] = lo
        start += r


def _fox_prep(past, pre, bias_row, s_valid):
    b, n_new = pre.shape[:2]
    n_past = 0 if past is None else past.shape[1]
    n = n_past + n_new
    full = lambda rows: pl.BlockSpec((1, rows, LANES), lambda bi: (bi, 0, 0))
    past_spec = pl.BlockSpec((1, n_past, H_FOX), lambda bi: (bi, 0, 0))
    in_specs = ([past_spec] if n_past else []) + [full(n_new), pl.BlockSpec((1, LANES), lambda bi: (0, 0))]
    args = ([past] if n_past else []) + [pre, bias_row]
    cum_shape = jax.ShapeDtypeStruct((b, n, LANES), F32)
    return pl.pallas_call(
        functools.partial(_fox_prep_kernel, n_past=n_past, n_new=n_new, s_valid=s_valid, blk=256),
        out_shape=(jax.ShapeDtypeStruct((b, n_new, LANES), F32), cum_shape, cum_shape, cum_shape),
        grid=(b,),
        in_specs=in_specs,
        out_specs=(full(n_new), full(n), full(n), full(n)),
        compiler_params=_cparams(("parallel",)),
        name="fox_prep",
    )(*args)


def _fox_pack_kernel(*refs, q_side, with_v):
    if with_v:
        x_ref, hi_ref, mid_ref, lo_ref, v_ref, out_ref, vt_ref = refs
        vt_ref[0] = v_ref[0].T.astype(BF16)
    else:
        x_ref, hi_ref, mid_ref, lo_ref, out_ref = refs
    tm = x_ref.shape[1]
    lane = lax.broadcasted_iota(jnp.int32, (tm, HD_FOX), 1)
    hi, mid, lo = hi_ref[0], mid_ref[0], lo_ref[0]
    for h in range(H_FOX):
        c = FOX_GATE_LANE + h
        pieces = (hi[:, c:c + 1], mid[:, c:c + 1], lo[:, c:c + 1])
        first = 0 if q_side else 3
        bias = jnp.where(jnp.logical_and(lane >= 3 - first, lane < 6 - first), -1.0 if q_side else 1.0, 0.0)
        for j, p in enumerate(pieces):
            bias = jnp.where(lane == first + j, p, bias)
        out_ref[0, h] = jnp.concatenate(
            [x_ref[0, :, h * HD_FOX:(h + 1) * HD_FOX].astype(BF16), bias.astype(BF16)], axis=1)


def _fox_pack(x, hi, mid, lo, row_offset, q_side, v=None):
    b, n, _ = x.shape
    tm = next(c for c in (ROW_TILE, KEY_TILE_SHORT, 2 * LANES, LANES) if n % c == 0 and row_offset % c == 0)
    off = row_offset // tm
    tile = lambda w: pl.BlockSpec((1, tm, w), lambda bi, i: (bi, i, 0))
    piece = pl.BlockSpec((1, tm, LANES), lambda bi, i: (bi, off + i, 0))
    out_shape = [jax.ShapeDtypeStruct((b, H_FOX, n, 2 * HD_FOX), BF16)]
    out_specs = [pl.BlockSpec((1, H_FOX, tm, 2 * HD_FOX), lambda bi, i: (bi, 0, i, 0))]
    in_specs = [tile(D_FOX), piece, piece, piece]
    args = [x, hi, mid, lo]
    if v is not None:
        in_specs.append(tile(D_FOX))
        args.append(v)
        out_shape.append(jax.ShapeDtypeStruct((b, D_FOX, n), BF16))
        out_specs.append(pl.BlockSpec((1, D_FOX, tm), lambda bi, i: (bi, 0, i)))
    return pl.pallas_call(
        functools.partial(_fox_pack_kernel, q_side=q_side, with_v=v is not None),
        out_shape=tuple(out_shape),
        grid=(b, n // tm),
        in_specs=in_specs,
        out_specs=tuple(out_specs),
        compiler_params=_cparams(("parallel", "parallel")),
        name="fox_pack_q" if q_side else "fox_pack_kv",
    )(*args)


def _paired_tiles(pair, step, nq):
    first = step <= pair
    return jnp.where(first, pair, nq - 1 - pair), jnp.where(first, step, step - pair - 1)


def _fox_attn_kernel(qa_ref, ka_ref, vt_ref, out_ref, m_scr, l_scr, acc_scr, *, p0, tq, tk, nq, nk, paired):
    qi, kj = pl.program_id(1), pl.program_id(2)
    if paired:
        qi, kj = _paired_tiles(qi, kj, nq)

    @pl.when(kj == 0)
    def _():
        m_scr[...] = jnp.full(m_scr.shape, NEG_BIG, F32)
        l_scr[...] = jnp.zeros(l_scr.shape, F32)
        acc_scr[...] = jnp.zeros(acc_scr.shape, F32)

    def step(masked):
        col = lax.broadcasted_iota(jnp.int32, (1, 2 * HD_FOX), 1)
        scale = jnp.where(col < HD_FOX, HD_FOX ** -0.5, 1.0).astype(BF16)
        ones_rows = jnp.ones((16, tk), BF16)
        if masked:
            kpos = kj * tk + lax.broadcasted_iota(jnp.int32, (tk, tq), 0)
            qpos = p0 + qi * tq + lax.broadcasted_iota(jnp.int32, (tk, tq), 1)
            mask = kpos <= qpos
        for h in range(H_FOX):
            s = _nt_dot(ka_ref[0, h], qa_ref[0, h] * scale)
            if masked:
                s = jnp.where(mask, s, NEG_BIG)
            m_prev = m_scr[h]
            m_new = jnp.maximum(m_prev, jnp.max(s, axis=0, keepdims=True))
            p = jnp.exp(s - m_new)
            alpha = jnp.exp(m_prev - m_new)
            pv = jnp.dot(jnp.concatenate([vt_ref[0, h], ones_rows], axis=0), p.astype(BF16),
                         preferred_element_type=F32)
            l_scr[h] = alpha * l_scr[h] + pv[HD_FOX:HD_FOX + 1, :]
            acc_scr[h] = alpha * acc_scr[h] + pv[0:HD_FOX, :]
            m_scr[h] = m_new

    first_q = p0 + qi * tq
    last_q = first_q + tq - 1
    unmasked = (kj + 1) * tk - 1 <= first_q

    @pl.when(unmasked)
    def _():
        step(False)

    @pl.when(jnp.logical_and(jnp.logical_not(unmasked), kj * tk <= last_q))
    def _():
        step(True)

    @pl.when(kj == jnp.minimum(last_q // tk, nk - 1))
    def _():
        o_t = jnp.concatenate([acc_scr[h] / l_scr[h] for h in range(H_FOX)], axis=0)
        out_ref[0] = o_t.T.astype(BF16)


def _fox_attn(qa, ka, vt, p0):
    b, _, sq, _ = qa.shape
    sk = ka.shape[2]
    tq = min(sq, 2 * ROW_TILE)
    tk = 2 * ROW_TILE if sk % (2 * ROW_TILE) == 0 else KEY_TILE_SHORT
    nq, nk = sq // tq, sk // tk
    paired = p0 == 0 and tq == tk and nq == nk and nq % 2 == 0
    if paired:
        grid = (b, nq // 2, nq + 1)
        tiles = lambda qi, kj: _paired_tiles(qi, kj, nq)
    else:
        grid = (b, nq, nk)
        tiles = lambda qi, kj: (qi, jnp.minimum(kj, (p0 + (qi + 1) * tq - 1) // tk))
    return pl.pallas_call(
        functools.partial(_fox_attn_kernel, p0=p0, tq=tq, tk=tk, nq=nq, nk=nk, paired=paired),
        out_shape=jax.ShapeDtypeStruct((b, sq, D_FOX), BF16),
        grid=grid,
        in_specs=[
            pl.BlockSpec((1, H_FOX, tq, 2 * HD_FOX), lambda bi, qi, kj: (bi, 0, tiles(qi, kj)[0], 0)),
            pl.BlockSpec((1, H_FOX, tk, 2 * HD_FOX), lambda bi, qi, kj: (bi, 0, tiles(qi, kj)[1], 0)),
            pl.BlockSpec((1, H_FOX, HD_FOX, tk), lambda bi, qi, kj: (bi, 0, 0, tiles(qi, kj)[1])),
        ],
        out_specs=pl.BlockSpec((1, tq, D_FOX), lambda bi, qi, kj: (bi, tiles(qi, kj)[0], 0)),
        scratch_shapes=[
            pltpu.VMEM((H_FOX, 1, tq), F32),
            pltpu.VMEM((H_FOX, 1, tq), F32),
            pltpu.VMEM((H_FOX, HD_FOX, tq), F32),
        ],
        compiler_params=_cparams(("parallel", "parallel", "arbitrary")),
        name="fox_attn",
    )(qa, ka, vt)


def _merge_kernel(x_ref, hm_ref, of_ref, cq_ref, gates_ref, mk_ref, mv_ref, wm_ref, wf_ref, wc_ref,
                  wo_ref, gffn_ref, xnew_ref, hn_ref, hnlo_ref):
    cq = cq_ref[0]
    heads = []
    for h in range(H_MEM):
        sl = slice(h * HD_MEM, (h + 1) * HD_MEM)
        s = _nt_dot(cq[:, sl], mk_ref[0, :, sl]) * (HD_MEM ** -0.5)
        p = jnp.exp(s - jnp.max(s, axis=1, keepdims=True))
        o = jnp.dot(p.astype(BF16), mv_ref[0, :, sl], preferred_element_type=F32)
        heads.append(o / jnp.sum(p, axis=1, keepdims=True))
    oc = jnp.concatenate(heads, axis=1).astype(BF16)
    a_m = jnp.dot(hm_ref[0], wm_ref[...], preferred_element_type=F32)
    a_f = jnp.dot(of_ref[0], wf_ref[...], preferred_element_type=F32)
    a_c = jnp.dot(oc, wc_ref[...], preferred_element_type=F32)
    g = gates_ref[0].astype(F32)
    merged = (_sigmoid(g[:, 0:D_MODEL]) * a_m + _sigmoid(g[:, D_MODEL:2 * D_MODEL]) * a_f
              + _sigmoid(g[:, 2 * D_MODEL:3 * D_MODEL]) * a_c)
    xn = x_ref[0] + jnp.dot(merged.astype(BF16), wo_ref[...], preferred_element_type=F32)
    xnew_ref[0] = xn
    hn = _rms(xn, gffn_ref[...])
    hn_hi = hn.astype(BF16)
    hn_ref[0] = hn_hi
    hnlo_ref[0] = (hn - hn_hi.astype(F32)).astype(BF16)


def _merge(x, hm, of, cq, gates, mk, mv, wm, wf, wc, wo, gffn):
    b, s, _ = x.shape
    ts = min(s, ROW_TILE)
    tile = lambda w: pl.BlockSpec((1, ts, w), lambda bi, si: (bi, si, 0))
    const = lambda shape: pl.BlockSpec(shape, lambda bi, si: (0, 0))
    mem = pl.BlockSpec((1, mk.shape[1], D_MEM), lambda bi, si: (bi, 0, 0))
    return pl.pallas_call(
        _merge_kernel,
        out_shape=(jax.ShapeDtypeStruct((b, s, D_MODEL), F32), jax.ShapeDtypeStruct((b, s, D_MODEL), BF16),
                   jax.ShapeDtypeStruct((b, s, D_MODEL), BF16)),
        grid=(b, s // ts),
        in_specs=[tile(D_MODEL), tile(D_M), tile(D_FOX), tile(D_MEM), tile(N_BRANCH * D_MODEL), mem, mem,
                  const((D_M, D_MODEL)), const((D_FOX, D_MODEL)), const((D_MEM, D_MODEL)),
                  const((D_MODEL, D_MODEL)), const((1, D_MODEL))],
        out_specs=(tile(D_MODEL), tile(D_MODEL), tile(D_MODEL)),
        compiler_params=_cparams(("parallel", "parallel")),
        name="merge",
    )(x, hm, of, cq, gates, mk, mv, wm, wf, wc, wo, gffn)


def _extract_top16(s, key_io, val_scr, idx_scr, h, want_rank):
    rank = jnp.full(s.shape, float(PEER_TOPK), F32) if want_rank else None
    for r in range(PEER_TOPK):
        m = jnp.max(s, axis=0, keepdims=True)
        first = jnp.min(jnp.where(s == m, key_io, float(N_KEYS)), axis=0, keepdims=True)
        hit = key_io == first
        s = jnp.where(hit, -jnp.inf, s)
        val_scr[r, pl.ds(h, 1), :] = m
        if want_rank:
            rank = jnp.where(hit, float(r), rank)
        else:
            idx_scr[r, pl.ds(h, 1), :] = first
    return rank


def _dot3(a_hi, a_lo, b_hi, b_lo, dot):
    return dot(a_hi, b_hi) + dot(a_lo, b_hi) + dot(a_hi, b_lo)


def _router_kernel(hn_ref, hnlo_ref, wqt_ref, keys_ref, a0_ref, cnt_ref, r1_ref, b1_ref,
                   qr_scr, e1_scr, top0_scr, top1_scr, idx0_scr, cnt_scr, invz_scr):
    tp = hn_ref.shape[0]
    qr_scr[...] = _dot3(wqt_ref[0], wqt_ref[1], hn_ref[...], hnlo_ref[...], _nt_dot)
    key_io = lax.broadcasted_iota(jnp.int32, (N_KEYS, tp), 0).astype(F32)

    heads_per_trip = 8

    def scores_and_top16_group(hg, carry):
        for k in range(heads_per_trip):
            scores_and_top16(heads_per_trip * hg + k)
        return carry

    def scores_and_top16(h):
        base = pl.multiple_of(h * 2 * N_KEYS, 2 * N_KEYS)
        def scores(c):
            q = qr_scr[pl.ds(base + c * N_KEYS, N_KEYS), :]
            q_hi = q.astype(BF16)
            q_lo = (q - q_hi.astype(F32)).astype(BF16)
            return _dot3(keys_ref[0, 2 * h + c], keys_ref[1, 2 * h + c], q_hi, q_lo,
                         lambda a, b_: jnp.dot(a, b_, preferred_element_type=F32))

        s0, s1 = scores(0), scores(1)
        _extract_top16(s0, key_io, top0_scr, idx0_scr, h, False)
        rank1 = _extract_top16(s1, key_io, top1_scr, None, h, True)
        a0_ref[h] = jnp.exp(s0 - top0_scr[0, pl.ds(h, 1), :])
        e1_scr[h] = jnp.exp(s1 - top1_scr[0, pl.ds(h, 1), :])
        r1_ref[h] = rank1.astype(BF16)

    lax.fori_loop(0, PEER_HEADS // heads_per_trip, scores_and_top16_group, 0)

    top0 = [top0_scr[a] for a in range(PEER_TOPK)]
    top1 = [top1_scr[b] for b in range(PEER_TOPK)]
    cnt = [jnp.zeros((PEER_HEADS, tp), F32) for _ in range(PEER_TOPK)]
    for _ in range(PEER_TOPK):
        front = []
        for a in range(PEER_TOPK):
            nxt = jnp.full((PEER_HEADS, tp), -jnp.inf, F32)
            for bb in range(PEER_TOPK // (a + 1)):
                nxt = jnp.where(cnt[a] == float(bb), top1[bb], nxt)
            front.append(top0[a] + nxt)
        mx = functools.reduce(jnp.maximum, front)
        first = functools.reduce(jnp.minimum,
                                 [jnp.where(front[a] == mx, float(a), float(PEER_TOPK)) for a in range(PEER_TOPK)])
        cnt = [cnt[a] + jnp.where(first == float(a), 1.0, 0.0) for a in range(PEER_TOPK)]
    z = jnp.zeros((PEER_HEADS, tp), F32)
    for a in range(PEER_TOPK):
        za = jnp.zeros((PEER_HEADS, tp), F32)
        for bb in range(PEER_TOPK // (a + 1)):
            za = za + jnp.where(cnt[a] > float(bb), jnp.exp(top1[bb] - top1[0]), 0.0)
        z = z + jnp.exp(top0[a] - top0[0]) * za
        cnt_scr[a] = cnt[a]
    invz_scr[...] = 1.0 / z

    def counts_and_gates(h, carry):
        cnt_i = jnp.zeros((N_KEYS, tp), F32)
        for a in range(PEER_TOPK):
            cnt_i = jnp.where(key_io == idx0_scr[a, pl.ds(h, 1), :], cnt_scr[a, pl.ds(h, 1), :], cnt_i)
        cnt_ref[h] = cnt_i
        b1 = e1_scr[h] * invz_scr[pl.ds(h, 1), :]
        b1_ref[h] = b1.astype(BF16)
        return carry

    lax.fori_loop(0, PEER_HEADS, counts_and_gates, 0)


def _router(hn2d, hnlo2d, wqt, keys):
    t = hn2d.shape[0]
    tp = ROUTER_TOKENS
    shape = jax.ShapeDtypeStruct((PEER_HEADS, N_KEYS, t), F32)
    spec = pl.BlockSpec((PEER_HEADS, N_KEYS, tp), lambda i: (0, 0, i))
    slab_shape = jax.ShapeDtypeStruct((PEER_HEADS, N_KEYS, t), BF16)
    slab_spec = spec
    return pl.pallas_call(
        _router_kernel,
        out_shape=(shape, shape, slab_shape, slab_shape),
        grid=(t // tp,),
        in_specs=[
            pl.BlockSpec((tp, D_MODEL), lambda i: (i, 0)),
            pl.BlockSpec((tp, D_MODEL), lambda i: (i, 0)),
            pl.BlockSpec(wqt.shape, lambda i: (0, 0, 0)),
            pl.BlockSpec(keys.shape, lambda i: (0, 0, 0, 0)),
        ],
        out_specs=(spec, spec, slab_spec, slab_spec),
        scratch_shapes=[
            pltpu.VMEM((wqt.shape[1], tp), F32),
            pltpu.VMEM((PEER_HEADS, N_KEYS, tp), F32),
        ] + [pltpu.VMEM((PEER_TOPK, PEER_HEADS, tp), F32) for _ in range(4)] + [pltpu.VMEM((PEER_HEADS, tp), F32)],
        compiler_params=_cparams(("parallel",)),
        name="peer_router",
    )(hn2d, hnlo2d, wqt, keys)


def _gelu_tanh(x):
    k = -2.0 * 0.7978845608028654 * 1.4426950408889634
    return x / (1.0 + jnp.exp2(x * (k + (k * 0.044715) * (x * x))))


def _experts_kernel(hn_ref, u_ref, vt_ref, a0_ref, cnt_ref, r1_ref, b1_ref, x_ref, gfin_ref, out_ref,
                    acc_scr, pre0_scr, pre1_scr, w0_scr, w1_scr, *, rows_per_tile, n_tiles, final_norm):
    s = pl.program_id(1)
    tp = hn_ref.shape[0]
    te = u_ref.shape[0] // 2
    slab = 16
    slabs_per_row = N_KEYS // slab
    d_half = acc_scr.shape[0] // 2
    zero = jnp.zeros((), BF16)

    @pl.when(s == 0)
    def _():
        acc_scr[...] = jnp.zeros(acc_scr.shape, F32)
        pre1_scr[...] = jnp.zeros(pre1_scr.shape, F32)
        w0_scr[...] = jnp.zeros(w0_scr.shape, BF16)

    def pipeline_step(parity, pre_w, pre_r, w_w, w_r):
        e = 2 * s + parity
        cols = slice(parity * te, (parity + 1) * te)

        def stage_c(half):
            rows = slice(half * d_half, (half + 1) * d_half)
            acc_scr[rows, :] += jnp.dot(vt_ref[rows, cols], w_r[...], preferred_element_type=F32)

        def stage_a(half):
            rows = slice(half * (te // 2), (half + 1) * (te // 2))
            pre_w[rows, :] = _nt_dot(u_ref[parity * te + half * (te // 2):parity * te + (half + 1) * (te // 2), :],
                                     hn_ref[...])

        mxu_work = (lambda: stage_c(0), lambda: stage_a(0), lambda: stage_c(1), lambda: stage_a(1))

        valid = jnp.logical_and(e >= 1, e <= n_tiles)
        tile_b = jnp.clip(e - 1, 0, n_tiles - 1)
        for ib in range(rows_per_tile):
            for piece in mxu_work[ib * len(mxu_work) // rows_per_tile:(ib + 1) * len(mxu_work) // rows_per_tile]:
                piece()
            i = tile_b * rows_per_tile + ib
            for lanes in (slice(0, tp // 2), slice(tp // 2, tp)):
                gates = [None] * slabs_per_row
                for h in range(PEER_HEADS):
                    a_b = jnp.broadcast_to(a0_ref[h, pl.ds(i, 1), lanes], (slab, tp // 2)).astype(BF16)
                    c_b = jnp.broadcast_to(cnt_ref[h, pl.ds(i, 1), lanes], (slab, tp // 2)).astype(BF16)
                    for g in range(slabs_per_row):
                        key_rows = slice(g * slab, (g + 1) * slab)
                        term = a_b * jnp.where(r1_ref[h, key_rows, lanes] < c_b, b1_ref[h, key_rows, lanes], zero)
                        gates[g] = term if h == 0 else gates[g] + term
                for g in range(slabs_per_row):
                    rows = slice(ib * N_KEYS + g * slab, ib * N_KEYS + (g + 1) * slab)
                    w = gates[g] * _gelu_tanh(pre_r[rows, lanes]).astype(BF16)
                    w_w[rows, lanes] = jnp.where(valid, w, zero)

    pipeline_step(0, pre0_scr, pre1_scr, w1_scr, w0_scr)
    pipeline_step(1, pre1_scr, pre0_scr, w0_scr, w1_scr)

    @pl.when(s == n_tiles // 2)
    def _():
        xo = x_ref[...] + acc_scr[...].T
        if final_norm:
            xo = _rms(xo, gfin_ref[...])
        out_ref[...] = xo


def _experts(hn2d, u_bf, vt_bf, layer, a0, cnt, r1, b1, x2d, gfin, final_norm):
    t = hn2d.shape[0]
    tp = ROW_TILE
    rows_per_tile = EXPERT_KEY_ROWS
    te = rows_per_tile * N_KEYS
    n_tiles = u_bf.shape[1] // te
    rspec = pl.BlockSpec((PEER_HEADS, N_KEYS, tp), lambda ti, e: (0, 0, ti))
    sspec = rspec
    return pl.pallas_call(
        functools.partial(_experts_kernel, rows_per_tile=rows_per_tile, n_tiles=n_tiles, final_norm=final_norm),
        out_shape=jax.ShapeDtypeStruct((t, D_MODEL), F32),
        grid=(t // tp, n_tiles // 2 + 1),
        in_specs=[
            pl.BlockSpec((tp, D_MODEL), lambda ti, s: (ti, 0)),
            pl.BlockSpec((None, 2 * te, D_MODEL), lambda ti, s: (layer, jnp.minimum(s, n_tiles // 2 - 1), 0)),
            pl.BlockSpec((None, D_MODEL, 2 * te), lambda ti, s: (layer, 0, jnp.maximum(s - 1, 0))),
            rspec, rspec, sspec, sspec,
            pl.BlockSpec((tp, D_MODEL), lambda ti, s: (ti, 0)),
            pl.BlockSpec((1, D_MODEL), lambda ti, s: (0, 0)),
        ],
        out_specs=pl.BlockSpec((tp, D_MODEL), lambda ti, s: (ti, 0)),
        scratch_shapes=[
            pltpu.VMEM((D_MODEL, tp), F32),
            pltpu.VMEM((te, tp), F32), pltpu.VMEM((te, tp), F32),
            pltpu.VMEM((te, tp), BF16), pltpu.VMEM((te, tp), BF16),
        ],
        compiler_params=_cparams(("parallel", "arbitrary")),
        name="peer_experts",
    )(hn2d, u_bf, vt_bf, a0, cnt, r1, b1, x2d, gfin)


def _hi_lo(a):
    hi = a.astype(BF16)
    return jnp.stack([hi, (a - hi.astype(F32)).astype(BF16)])


def _prep_layer_weights(p):
    w_in = p['w_in']
    o_mi = 4 * D_M
    o_fq = o_mi + 2 * H_M
    o_ff = o_fq + 3 * D_FOX
    o_cq = o_ff + H_FOX
    w_big = jnp.concatenate([w_in[:, :o_mi], w_in[:, o_fq:o_ff], w_in[:, o_cq:]], axis=1).astype(BF16)
    w_big = w_big.reshape(D_MODEL, -1, PROJ_TILE).transpose(1, 0, 2)
    w_small = jnp.concatenate([w_in[:, o_mi:o_fq], w_in[:, o_ff:o_cq]], axis=1)
    w_small = jnp.pad(w_small, ((0, 0), (0, LANES - w_small.shape[1])))
    w_small_hi = w_small.astype(BF16)
    w_small = jnp.stack([w_small_hi, (w_small - w_small_hi.astype(F32)).astype(BF16)])
    bias = jnp.concatenate([p['b_m_i'], p['b_m_f'], p['b_fox_f']]).astype(F32)
    return dict(
        norm_mix=p['norm_mix'].reshape(1, D_MODEL), w_big=w_big, w_small=w_small,
        conv_w=p['conv_w'], conv_b=p['conv_b'].reshape(1, 2 * D_M),
        bias_row=jnp.pad(bias, (0, LANES - 16)).reshape(1, LANES), bias_col=bias.reshape(16, 1),
        norm_m_head=p['norm_m_head'].reshape(1, D_M),
        w_up_m=p['w_up_m'].astype(BF16), w_up_f=p['w_up_f'].astype(BF16), w_up_c=p['w_up_c'].astype(BF16),
        w_out=p['w_out'].astype(BF16), norm_ffn=p['norm_ffn'].reshape(1, D_MODEL),
        wqt=_hi_lo(p['peer_wq'].T),
        keys=_hi_lo(p['peer_keys'].reshape(2 * PEER_HEADS, N_KEYS, N_KEYS)),
    )


def _layer(x, mem_k, mem_v, conv_prev, c0, n0, m0, fk_past, fv_past, flf_past, w, peer_tables, layer, gfin, final_norm):
    b, s, _ = x.shape
    t = b * s
    p0 = 0 if fk_past is None else fk_past.shape[1]
    qk, v_m, o_m, fq, fk, fv, cq, gates, small = _norm_proj(x.reshape(t, D_MODEL), w['norm_mix'], w['w_big'], w['w_small'])

    small3 = small.reshape(b, s, LANES)
    smallt = small3[:, :, :16].transpose(0, 2, 1)
    m0b = jnp.broadcast_to(m0[:, :, None, None], (b, H_M, 1, LANES))
    hm, c1, n1, m1b, conv_new = _mlstm(
        qk.reshape(b, s, 2 * D_M), v_m.reshape(b, s, D_M), o_m.reshape(b, s, D_M), small3, smallt,
        w['conv_w'], w['conv_b'], w['bias_row'], w['bias_col'], w['norm_m_head'],
        c0, n0[:, :, None, :], m0b, conv_prev)
    n1, m1 = n1[:, :, 0, :], m1b[:, :, 0, 0]

    k_f = fk.reshape(b, s, H_FOX, HD_FOX)
    v_f = fv.reshape(b, s, H_FOX, HD_FOX)
    s_pad = -(-s // LANES) * LANES
    pad_rows = lambda a: jnp.pad(a, ((0, 0), (0, s_pad - s), (0, 0)))
    past = flf_past.astype(F32) if p0 else None
    lf_all, hi, mid, lo = _fox_prep(past, pad_rows(small3), w['bias_row'], s)
    lf_f = lf_all[:, :s, FOX_GATE_LANE:FOX_GATE_LANE + H_FOX]
    sk = p0 + s_pad
    k_all, v_all = pad_rows(fk.reshape(b, s, D_FOX)), pad_rows(fv.reshape(b, s, D_FOX))
    if p0:
        k_all = jnp.concatenate([fk_past.reshape(b, p0, D_FOX).astype(F32), k_all], axis=1)
        v_all = jnp.concatenate([fv_past.reshape(b, p0, D_FOX).astype(F32), v_all], axis=1)
    qa = _fox_pack(pad_rows(fq.reshape(b, s, D_FOX)), hi, mid, lo, p0, True)[0]
    ka, vt_all = _fox_pack(k_all, hi, mid, lo, 0, False, v=v_all)
    o_f = _fox_attn(qa, ka, vt_all.reshape(b, H_FOX, HD_FOX, sk), p0)[:, :s]

    x_new, hn, hn_lo = _merge(x, hm, o_f, cq.reshape(b, s, D_MEM), gates.reshape(b, s, N_BRANCH * D_MODEL),
                       mem_k.reshape(b, -1, D_MEM).astype(BF16), mem_v.reshape(b, -1, D_MEM).astype(BF16),
                       w['w_up_m'], w['w_up_f'], w['w_up_c'], w['w_out'], w['norm_ffn'])

    hn2d = hn.reshape(t, D_MODEL)
    a0, cnt, r1, b1 = _router(hn2d, hn_lo.reshape(t, D_MODEL), w['wqt'], w['keys'])
    x_out = _experts(hn2d, peer_tables[0], peer_tables[1], layer, a0, cnt, r1, b1, x_new.reshape(t, D_MODEL),
                     gfin, final_norm)
    return x_out.reshape(b, s, D_MODEL), conv_new, c1, n1, m1, k_f, v_f, lf_f


def kernel(x_prompt, x_sample, mem_prompt, cache_fox_k, cache_fox_v, cache_fox_lf, state_mlstm_c, state_mlstm_n, state_mlstm_m, state_conv, cache_mem_k, cache_mem_v, norm_mix, w_in, conv_w, conv_b, b_m_i, b_m_f, norm_m_head, b_fox_f, norm_mem, w_mem_kv, w_up_m, w_up_f, w_up_c, w_out, norm_ffn, peer_wq, peer_keys, peer_u, peer_v, norm_final):
    depth = w_in.shape[0]
    names = dict(norm_mix=norm_mix, w_in=w_in, conv_w=conv_w, conv_b=conv_b, b_m_i=b_m_i, b_m_f=b_m_f,
                 norm_m_head=norm_m_head, b_fox_f=b_fox_f, w_up_m=w_up_m, w_up_f=w_up_f, w_up_c=w_up_c,
                 w_out=w_out, norm_ffn=norm_ffn, peer_wq=peer_wq, peer_keys=peer_keys, peer_u=peer_u,
                 peer_v=peer_v)
    weights = [_prep_layer_weights({k: a[l] for k, a in names.items()}) for l in range(depth)]
    peer_tables = (peer_u.astype(BF16), jnp.swapaxes(peer_v, 1, 2).astype(BF16))
    gfin = norm_final.reshape(1, D_MODEL)
    bp, n_mem = mem_prompt.shape[0], mem_prompt.shape[1]

    xp = x_prompt
    new_p = [[] for _ in range(9)]
    for l in range(depth):
        mk, mv = _norm_matmul(mem_prompt.reshape(bp * n_mem, D_MODEL), norm_mem[l].reshape(1, D_MODEL),
                              w_mem_kv[l].astype(BF16))
        mk = mk.reshape(bp, n_mem, H_MEM, HD_MEM)
        mv = mv.reshape(bp, n_mem, H_MEM, HD_MEM)
        xp, conv1, c1, n1, m1, kf, vf, lff = _layer(
            xp, mk, mv, jnp.zeros((bp, CONV_W - 1, 2 * D_M), F32),
            jnp.zeros((bp, H_M, HD_M, HD_M), F32), jnp.zeros((bp, H_M, HD_M), F32), jnp.zeros((bp, H_M), F32),
            None, None, None, weights[l], peer_tables, l, gfin, l == depth - 1)
        for lst, a in zip(new_p, (kf, vf, lff, c1, n1, m1, conv1, mk, mv)):
            lst.append(a)
    outs_p = [jnp.stack(a) for a in new_p]

    xs = x_sample
    new_s = [[] for _ in range(7)]
    for l in range(depth):
        xs, conv1, c1, n1, m1, kf, vf, lff = _layer(
            xs, cache_mem_k[l], cache_mem_v[l], state_conv[l], state_mlstm_c[l], state_mlstm_n[l],
            state_mlstm_m[l], cache_fox_k[l], cache_fox_v[l], cache_fox_lf[l], weights[l], peer_tables, l, gfin,
            l == depth - 1)
        for lst, a in zip(new_s, (kf, vf, lff, c1, n1, m1, conv1)):
            lst.append(a)
    outs_s = [jnp.stack(a) for a in new_s]

    return (xp, xs, *outs_p, *outs_s)
```
